```python
import jax, jax.numpy as jnp
from jax import lax
import numpy as np

D_MODEL = 1024
BATCH = 8
SEQ = 4096
DEPTH = 1
DEC_BATCH = 32
DEC_SEQ = 16
PAST_LEN = 1024

CHUNK = 64
HEAD_DIM = 64
N_FOX = 6
N_RWKV = 6
N_MEM = 4
W_FOX = N_FOX * HEAD_DIM
W_RWKV = N_RWKV * HEAD_DIM
W_MEM = N_MEM * HEAD_DIM
D_MIX = W_FOX + W_RWKV + W_MEM
N_MEM_TOK = 256
LORA_W = 32
LORA_A = 32
Q_BLOCK = 128
NORM_EPS = 1e-6
GN_EPS = 64e-5

FOX_COLS = 4 * W_FOX + N_FOX
RWKV_COLS = 4 * W_RWKV + LORA_W + LORA_A
MEM_COLS = 2 * W_MEM
N_IN = FOX_COLS + RWKV_COLS + MEM_COLS

kernel_name = "fox_rwkv7_memxattn_streaming_step"


def rms_norm(x, g, eps=NORM_EPS):
    xf = x.astype(jnp.float32)
    y = xf * lax.rsqrt(jnp.mean(xf * xf, axis=-1, keepdims=True) + eps)
    return (y * g.astype(jnp.float32)).astype(x.dtype)


def fox_prep(cols, q_g, k_g, b_f):
    B, T, _ = cols.shape
    q, k, v, f, g = jnp.split(cols, [W_FOX, 2 * W_FOX, 3 * W_FOX, 3 * W_FOX + N_FOX], axis=-1)
    q = rms_norm(q.reshape(B, T, N_FOX, HEAD_DIM), q_g)
    k = rms_norm(k.reshape(B, T, N_FOX, HEAD_DIM), k_g)
    v = v.reshape(B, T, N_FOX, HEAD_DIM)
    logf = jax.nn.log_sigmoid((f + b_f).astype(jnp.float32))
    return q, k, v, logf, g


def fox_attend_block(q, cq, qpos, k, v, ck, kpos):
    s = jnp.einsum('bqhd,bkhd->bhqk', q, k, preferred_element_type=jnp.float32) * (HEAD_DIM ** -0.5)
    s = s + jnp.transpose(cq, (0, 2, 1))[..., :, None] - jnp.transpose(ck, (0, 2, 1))[..., None, :]
    mask = kpos[None, :] <= qpos[:, None]
    s = jnp.where(mask, s, jnp.finfo(jnp.float32).min)
    p = jax.nn.softmax(s, axis=-1)
    return jnp.einsum('bhqk,bkhd->bqhd', p.astype(v.dtype), v)


def fox_prompt(q, k, v, logf):
    B, T = q.shape[:2]
    c = jnp.cumsum(logf, axis=1)
    kpos = jnp.arange(T)

    def block(i):
        s0 = i * Q_BLOCK
        qb = lax.dynamic_slice_in_dim(q, s0, Q_BLOCK, axis=1)
        cb = lax.dynamic_slice_in_dim(c, s0, Q_BLOCK, axis=1)
        return fox_attend_block(qb, cb, s0 + jnp.arange(Q_BLOCK), k, v, c, kpos)

    o = lax.map(block, jnp.arange(T // Q_BLOCK))
    return jnp.transpose(o, (1, 0, 2, 3, 4)).reshape(B, T, W_FOX)


def fox_sample(q, k, v, logf, k_past, v_past, logf_past):
    B, S = q.shape[:2]
    P = k_past.shape[1]
    keys = jnp.concatenate([k_past.astype(k.dtype), k], axis=1)
    vals = jnp.concatenate([v_past.astype(v.dtype), v], axis=1)
    c = jnp.cumsum(jnp.concatenate([logf_past.astype(jnp.float32), logf], axis=1), axis=1)
    o = fox_attend_block(q, c[:, P:], P + jnp.arange(S), keys, vals, c, jnp.arange(P + S))
    return o.reshape(B, S, W_FOX)


def token_shift(cols, prev, mu):
    shifted = jnp.concatenate([prev.astype(cols.dtype), cols[:, :-1]], axis=1)
    return cols + (shifted - cols) * mu


def rwkv_scan(s0, r, w, k, v, kk, a):
    def step(S, inp):
        r_t, w_t, k_t, v_t, kk_t, a_t = inp
        sa = jnp.einsum('bhvk,bhk->bhv', S, -kk_t)
        S = S * w_t[:, :, None, :] + sa[..., :, None] * (kk_t * a_t)[:, :, None, :] \
            + v_t[..., :, None] * k_t[:, :, None, :]
        y = jnp.einsum('bhvk,bhk->bhv', S, r_t)
        return S, y

    xs = tuple(jnp.moveaxis(t, 1, 0) for t in (r, w, k, v, kk, a))
    S, ys = lax.scan(step, s0.astype(jnp.float32), xs)
    return S, jnp.moveaxis(ys, 0, 1)


def rwkv_branch(xs, s0, w0, w_up, a0, a_up, k_k, k_a, r_k, gn_w, gn_b):
    B, T, _ = xs.shape
    f32 = jnp.float32
    r, k, v, wd, ad, g = jnp.split(
        xs, [W_RWKV, 2 * W_RWKV, 3 * W_RWKV, 3 * W_RWKV + LORA_W, 3 * W_RWKV + LORA_W + LORA_A], axis=-1)
    w_raw = -jax.nn.softplus(-(w0 + jnp.tanh(wd) @ w_up).astype(f32)) - 0.5
    decay = jnp.exp(-jnp.exp(w_raw))
    a = jax.nn.sigmoid((a0 + ad @ a_up).astype(f32))

    def hd(t):
        return t.reshape(B, T, N_RWKV, HEAD_DIM).astype(f32)

    def ph(t):
        return t.reshape(N_RWKV, HEAD_DIM).astype(f32)

    r, k, v, decay, a = hd(r), hd(k), hd(v), hd(decay), hd(a)
    kk = k * ph(k_k)
    kk = kk * lax.rsqrt(jnp.maximum(jnp.sum(kk * kk, axis=-1, keepdims=True), 1e-24))
    k = k * (1.0 + (a - 1.0) * ph(k_a))
    S, y = rwkv_scan(s0, r, decay, k, v, kk, a)
    mu = jnp.mean(y, axis=-1, keepdims=True)
    var = jnp.mean(jnp.square(y - mu), axis=-1, keepdims=True)
    yn = (y - mu) * lax.rsqrt(var + GN_EPS) * ph(gn_w) + ph(gn_b)
    bonus = jnp.sum(r * k * ph(r_k), axis=-1, keepdims=True) * v
    out = (yn + bonus).reshape(B, T, W_RWKV).astype(xs.dtype)
    return out, S, g


def mem_kv(mem, mem_norm_g, w_mem_kv, mem_k_g):
    B, M, _ = mem.shape
    kv = rms_norm(mem, mem_norm_g) @ w_mem_kv
    k, v = jnp.split(kv, [W_MEM], axis=-1)
    k = rms_norm(k.reshape(B, M, N_MEM, HEAD_DIM), mem_k_g)
    return k, v.reshape(B, M, N_MEM, HEAD_DIM)


def mem_attend(q_cols, mem_q_g, mk, mv):
    B, T, _ = q_cols.shape
    q = rms_norm(q_cols.reshape(B, T, N_MEM, HEAD_DIM), mem_q_g)
    s = jnp.einsum('bthd,bmhd->bhtm', q, mk.astype(q.dtype), preferred_element_type=jnp.float32) * (HEAD_DIM ** -0.5)
    p = jax.nn.softmax(s, axis=-1)
    o = jnp.einsum('bhtm,bmhd->bthd', p.astype(q.dtype), mv.astype(q.dtype))
    return o.reshape(B, T, W_MEM)


def mixer_layer(x, mk, mv, shift_prev, rwkv_s0, fox_past, p):
    (norm_g, w_in, fox_q_g, fox_k_g, fox_b_f, rwkv_mu, rwkv_w0, rwkv_w_up, rwkv_a0, rwkv_a_up,
     rwkv_k_k, rwkv_k_a, rwkv_r_k, rwkv_gn_w, rwkv_gn_b, mem_q_g, w_out) = p
    h = rms_norm(x, norm_g) @ w_in
    fox_cols, rwkv_cols, mem_cols = jnp.split(h, [FOX_COLS, FOX_COLS + RWKV_COLS], axis=-1)
    q, k, v, logf, g_f = fox_prep(fox_cols, fox_q_g, fox_k_g, fox_b_f)
    if fox_past is None:
        o_f = fox_prompt(q, k, v, logf)
    else:
        o_f = fox_sample(q, k, v, logf, *fox_past)
    xs = token_shift(rwkv_cols, shift_prev, rwkv_mu)
    o_r, s_new, g_r = rwkv_branch(xs, rwkv_s0, rwkv_w0, rwkv_w_up, rwkv_a0, rwkv_a_up,
                                  rwkv_k_k, rwkv_k_a, rwkv_r_k, rwkv_gn_w, rwkv_gn_b)
    mq, g_m = jnp.split(mem_cols, [W_MEM], axis=-1)
    o_m = mem_attend(mq, mem_q_g, mk, mv)
    o = jnp.concatenate([o_f.astype(x.dtype) * jax.nn.silu(g_f),
                         o_r * jax.nn.silu(g_r),
                         o_m * jax.nn.silu(g_m)], axis=-1)
    y = x + o @ w_out
    return y, k, v, logf, s_new, rwkv_cols[:, -1:]


def setup_inputs(seed: int = 0) -> dict:
    key = jax.random.key(seed)
    ks = iter(jax.random.split(key, 48))

    def nrm(shape, s=1.0):
        return s * jax.random.normal(next(ks), shape, jnp.float32)

    def gain(shape):
        return 1.0 + 0.05 * nrm(shape)

    return {
        "x_prompt": nrm((BATCH, SEQ, D_MODEL)),
        "x_sample": nrm((DEC_BATCH, DEC_SEQ, D_MODEL)),
        "mem_prompt": nrm((BATCH, N_MEM_TOK, D_MODEL)),
        "cache_fox_k": nrm((DEPTH, DEC_BATCH, PAST_LEN, N_FOX, HEAD_DIM)),
        "cache_fox_v": nrm((DEPTH, DEC_BATCH, PAST_LEN, N_FOX, HEAD_DIM)),
        "cache_fox_logf": jax.nn.log_sigmoid(2.0 + nrm((DEPTH, DEC_BATCH, PAST_LEN, N_FOX))),
        "cache_mem_k": nrm((DEPTH, DEC_BATCH, N_MEM_TOK, N_MEM, HEAD_DIM)),
        "cache_mem_v": nrm((DEPTH, DEC_BATCH, N_MEM_TOK, N_MEM, HEAD_DIM)),
        "state_rwkv": nrm((DEPTH, DEC_BATCH, N_RWKV, HEAD_DIM, HEAD_DIM), 0.3),
        "state_rwkv_shift": nrm((DEPTH, DEC_BATCH, 1, RWKV_COLS)),
        "norm_g": gain((DEPTH, D_MODEL)),
        "w_in": nrm((DEPTH, D_MODEL, N_IN), D_MODEL ** -0.5),
        "fox_q_g": gain((DEPTH, HEAD_DIM)),
        "fox_k_g": gain((DEPTH, HEAD_DIM)),
        "fox_b_f": 2.0 + 0.1 * nrm((DEPTH, N_FOX)),
        "rwkv_mu": jax.random.uniform(next(ks), (DEPTH, RWKV_COLS), jnp.float32, 0.1, 0.9),
        "rwkv_w0": -0.5 + 0.5 * nrm((DEPTH, W_RWKV)),
        "rwkv_w_up": nrm((DEPTH, LORA_W, W_RWKV), 0.3 * LORA_W ** -0.5),
        "rwkv_a0": 0.1 * nrm((DEPTH, W_RWKV)),
        "rwkv_a_up": nrm((DEPTH, LORA_A, W_RWKV), 0.5 * LORA_A ** -0.5),
        "rwkv_k_k": 0.85 + 0.05 * nrm((DEPTH, W_RWKV)),
        "rwkv_k_a": gain((DEPTH, W_RWKV)),
        "rwkv_r_k": 0.1 * nrm((DEPTH, W_RWKV)),
        "rwkv_gn_w": gain((DEPTH, W_RWKV)),
        "rwkv_gn_b": 0.01 * nrm((DEPTH, W_RWKV)),
        "mem_norm_g": gain((DEPTH, D_MODEL)),
        "w_mem_kv": nrm((DEPTH, D_MODEL, 2 * W_MEM), D_MODEL ** -0.5),
        "mem_q_g": gain((DEPTH, HEAD_DIM)),
        "mem_k_g": gain((DEPTH, HEAD_DIM)),
        "w_out": nrm((DEPTH, D_MIX, D_MODEL), D_MIX ** -0.5),
    }


def reference(x_prompt, x_sample, mem_prompt, cache_fox_k, cache_fox_v, cache_fox_logf,
              cache_mem_k, cache_mem_v, state_rwkv, state_rwkv_shift,
              norm_g, w_in, fox_q_g, fox_k_g, fox_b_f, rwkv_mu, rwkv_w0, rwkv_w_up, rwkv_a0,
              rwkv_a_up, rwkv_k_k, rwkv_k_a, rwkv_r_k, rwkv_gn_w, rwkv_gn_b,
              mem_norm_g, w_mem_kv, mem_q_g, mem_k_g, w_out):
    B = x_prompt.shape[0]
    yp, ys = x_prompt, x_sample
    fkp, fvp, flp, mkp, mvp, rsp, rhp = [], [], [], [], [], [], []
    fks, fvs, fls, rss, rhs = [], [], [], [], []
    for l in range(DEPTH):
        p = (norm_g[l], w_in[l], fox_q_g[l], fox_k_g[l], fox_b_f[l], rwkv_mu[l], rwkv_w0[l],
             rwkv_w_up[l], rwkv_a0[l], rwkv_a_up[l], rwkv_k_k[l], rwkv_k_a[l], rwkv_r_k[l],
             rwkv_gn_w[l], rwkv_gn_b[l], mem_q_g[l], w_out[l])
        mk, mv = mem_kv(mem_prompt, mem_norm_g[l], w_mem_kv[l], mem_k_g[l])
        shift0 = jnp.zeros((B, 1, RWKV_COLS), x_prompt.dtype)
        s0 = jnp.zeros((B, N_RWKV, HEAD_DIM, HEAD_DIM), jnp.float32)
        yp, k, v, lf, s_new, sh_new = mixer_layer(yp, mk, mv, shift0, s0, None, p)
        fkp.append(k); fvp.append(v); flp.append(lf); mkp.append(mk); mvp.append(mv)
        rsp.append(s_new); rhp.append(sh_new)
        ys, k, v, lf, s_new, sh_new = mixer_layer(
            ys, cache_mem_k[l], cache_mem_v[l], state_rwkv_shift[l], state_rwkv[l],
            (cache_fox_k[l], cache_fox_v[l], cache_fox_logf[l]), p)
        fks.append(k); fvs.append(v); fls.append(lf); rss.append(s_new); rhs.append(sh_new)
    return (yp, ys,
            jnp.stack(fkp), jnp.stack(fvp), jnp.stack(flp), jnp.stack(mkp), jnp.stack(mvp),
            jnp.stack(rsp), jnp.stack(rhp),
            jnp.stack(fks), jnp.stack(fvs), jnp.stack(fls), jnp.stack(rss), jnp.stack(rhs))
```

```python
import functools

import numpy as np
import jax
import jax.numpy as jnp
from jax import lax
from jax.experimental import pallas as pl
from jax.experimental.pallas import tpu as pltpu

F32 = jnp.float32
BF16 = jnp.bfloat16

D_MODEL = 1024
HEAD = 64
N_FOX = 6
N_RWKV = 6
N_MEM = 4
W_FOX = N_FOX * HEAD
W_RWKV = N_RWKV * HEAD
W_MEM = N_MEM * HEAD
N_MEM_TOK = 256
LORA = 32
NORM_EPS = 1e-6
GN_EPS = 64e-5
FOX_COLS = 4 * W_FOX + N_FOX
RWKV_COLS = 4 * W_RWKV + 2 * LORA
MEM_COLS = 2 * W_MEM

LANE = 128
SMALL = LANE
SM_WD = 32
SM_AD = 64
OFF_FQ, OFF_FK, OFF_FV, OFF_FG = 0, 384, 768, 1152
OFF_RR, OFF_RK, OFF_RV, OFF_RG = 1536, 1920, 2304, 2688
OFF_MQ, OFF_MG = 3072, 3328
OFF_SM = 3584
N_PAD = OFF_SM + SMALL
SHIFT_W = 4 * W_RWKV + SMALL
AUG = LANE
RWKV_CHUNK = 64
VMEM_LIMIT = 56 * 1024 * 1024
PROMPT_TM = 256
PROMPT_TQ = 512
PROMPT_TC = 256


def _dot(a, b):
    return jnp.dot(a, b, preferred_element_type=F32)


def _dot_nt(a, b):
    return lax.dot_general(a, b, (((1,), (1,)), ((), ())), preferred_element_type=F32)


def _dot_tn(a, b):
    return lax.dot_general(a, b, (((0,), (0,)), ((), ())), preferred_element_type=F32)


def _split2(x):
    hi = x.astype(BF16)
    lo = (x - hi.astype(F32)).astype(BF16)
    return hi, lo


def _split3(x):
    hi = x.astype(BF16)
    r1 = x - hi.astype(F32)
    mid = r1.astype(BF16)
    lo = (r1 - mid.astype(F32)).astype(BF16)
    return hi, mid, lo


def _dot3(fn, a, b):
    ah, al = _split2(a)
    bh, bl = _split2(b)
    return fn(ah, bh) + (fn(ah, bl) + fn(al, bh))


def _dot1(fn, a, b):
    return fn(a.astype(BF16), b.astype(BF16))


def _exact_lhs_dot(m_bf16, x, parts):
    pieces = _split3(x) if parts == 3 else _split2(x)
    acc = _dot(m_bf16, pieces[0])
    for p in pieces[1:]:
        acc = acc + _dot(m_bf16, p)
    return acc


def _segsum(x, ones_bd):
    hi, lo = _split2(x)
    return _dot(hi, ones_bd) + _dot(lo, ones_bd)


def _head_rms(t, gain, ones_bd):
    msq = _segsum(t * t, ones_bd) * (1.0 / HEAD)
    return t * lax.rsqrt(msq + NORM_EPS) * gain


def _silu(x):
    return x * jax.nn.sigmoid(x)


def _softplus(z):
    return jnp.maximum(z, 0.0) + jnp.log1p(jnp.exp(-jnp.abs(z)))


def _full(shape):
    n = len(shape)
    return pl.BlockSpec(shape, lambda *_: (0,) * n)


def _params(n_axes):
    return pltpu.CompilerParams(dimension_semantics=("arbitrary",) * n_axes, vmem_limit_bytes=VMEM_LIMIT)


def _mem_kv_body(mem_ref, g_ref, w_ref, kg_ref, ones_ref, k_ref, v_ref):
    x = mem_ref[0]
    ms = jnp.mean(x * x, axis=-1, keepdims=True)
    xn = (x * lax.rsqrt(ms + NORM_EPS) * g_ref[...]).astype(BF16)
    kv = _dot(xn, w_ref[...])
    k_ref[0] = _head_rms(kv[:, :W_MEM], kg_ref[...], ones_ref[...])
    v_ref[0] = kv[:, W_MEM:]


def _mem_kv(mem, g, w_bf16, kg4, ones256):
    b = mem.shape[0]
    blk = pl.BlockSpec((1, N_MEM_TOK, W_MEM), lambda i: (i, 0, 0))
    return pl.pallas_call(
        _mem_kv_body,
        grid=(b,),
        in_specs=[pl.BlockSpec((1, N_MEM_TOK, D_MODEL), lambda i: (i, 0, 0)),
                  _full((1, D_MODEL)), _full((D_MODEL, 2 * W_MEM)), _full((1, W_MEM)), _full((W_MEM, W_MEM))],
        out_specs=[blk, blk],
        out_shape=[jax.ShapeDtypeStruct((b, N_MEM_TOK, W_MEM), F32)] * 2,
        compiler_params=_params(1),
        name="mem_kv",
    )(mem, g, w_bf16, kg4, ones256)


def _proj_body(chunk,
               x_ref, shift0_ref, mkt_ref, mvb_ref, ng_ref, w_ref,
               fqg_ref, fkg_ref, bf_ref, mu_ref, w0_ref, a0_ref, wup_ref, aup_ref,
               kk_ref, ka_ref, rk_ref, mqg_ref, ones384_ref, ones256_ref, pq_ref, pc_ref, aug1_ref,
               qa_ref, kaug_ref, vb_ref, kn_ref, v_ref, logf_ref, gf_ref,
               rt_ref, at_ref, kh_ref, bh_ref, rv_ref, lc_ref, gr_ref, bg_ref, om_ref, shift_out_ref,
               carry_shift, carry_c):
    i = pl.program_id(1)
    tm = x_ref.shape[1]

    @pl.when(i == 0)
    def _():
        carry_shift[...] = shift0_ref[0]
        carry_c[...] = jnp.zeros_like(carry_c)

    x = x_ref[0]
    ms = jnp.mean(x * x, axis=-1, keepdims=True)
    xn = (x * lax.rsqrt(ms + NORM_EPS) * ng_ref[...]).astype(BF16)

    def proj(off, width):
        return _dot(xn, w_ref[:, off:off + width])

    ones384 = ones384_ref[...]
    row = lax.broadcasted_iota(jnp.int32, (tm, tm), 0)
    col = lax.broadcasted_iota(jnp.int32, (tm, tm), 1)
    tri = jnp.where(col <= row, 1.0, 0.0).astype(BF16)
    tri_chunk = jnp.where((col <= row) & ((row ^ col) < chunk), 1.0, 0.0).astype(BF16)

    hs = proj(OFF_SM, SMALL)
    qn = _head_rms(proj(OFF_FQ, W_FOX), fqg_ref[...], ones384)
    kn = _head_rms(proj(OFF_FK, W_FOX), fkg_ref[...], ones384)
    hv = proj(OFF_FV, W_FOX)
    hg = proj(OFF_FG, W_FOX)
    kn_ref[0] = kn
    v_ref[0] = hv
    vb_ref[0] = hv.astype(BF16)
    gf_ref[0] = _silu(hg)
    f = hs + bf_ref[...]
    lane = lax.broadcasted_iota(jnp.int32, (tm, SMALL), 1)
    logf = jnp.where(lane < N_FOX, jnp.minimum(f, 0.0) - jnp.log1p(jnp.exp(-jnp.abs(f))), 0.0)
    logf_ref[0] = logf[:, 0:N_FOX]
    c = _exact_lhs_dot(tri, logf, 3) + carry_c[...]
    carry_c[...] = c[tm - 1:tm, :]
    chi, cmid, clo = _split3(c)
    qs = (qn * (HEAD ** -0.5)).astype(BF16)
    qa = (_dot(qs, pq_ref[...]) + _dot(chi, pc_ref[0]) + _dot(cmid, pc_ref[1]) + _dot(clo, pc_ref[2])
          + aug1_ref[0:1, :])
    ka = (_dot(kn.astype(BF16), pq_ref[...]) - (_dot(chi, pc_ref[3]) + _dot(cmid, pc_ref[4]) + _dot(clo, pc_ref[5]))
          + aug1_ref[1:2, :])
    for h in range(N_FOX):
        qa_ref[0, h] = qa[:, AUG * h:AUG * (h + 1)].astype(BF16)
        kaug_ref[0, h] = ka[:, AUG * h:AUG * (h + 1)].astype(BF16)

    qm = _head_rms(proj(OFF_MQ, W_MEM), mqg_ref[...], ones256_ref[...])
    s = _dot((qm * (HEAD ** -0.5)).astype(BF16), mkt_ref[0])
    ps = []
    for h in range(N_MEM):
        sh = s[:, N_MEM_TOK * h:N_MEM_TOK * (h + 1)]
        e = jnp.exp(sh - jnp.max(sh, axis=1, keepdims=True))
        ps.append((e / jnp.sum(e, axis=1, keepdims=True)).astype(BF16))
    om = _dot(jnp.concatenate(ps, axis=1), mvb_ref[0])
    om_ref[0] = om * _silu(proj(OFF_MG, W_MEM))

    first_row = lax.broadcasted_iota(jnp.int32, (tm, 1), 0) == 0

    def tshift(cols, off, width):
        prev = jnp.where(first_row, carry_shift[:, off:off + width], pltpu.roll(cols, 1, 0))
        carry_shift[:, off:off + width] = cols[tm - 1:tm, :]
        return cols + (prev - cols) * mu_ref[:, off:off + width]

    r = tshift(proj(OFF_RR, W_RWKV), 0, W_RWKV)
    k = tshift(proj(OFF_RK, W_RWKV), W_RWKV, W_RWKV)
    v = tshift(proj(OFF_RV, W_RWKV), 2 * W_RWKV, W_RWKV)
    g = tshift(proj(OFF_RG, W_RWKV), 3 * W_RWKV, W_RWKV)
    sm = tshift(hs, 4 * W_RWKV, SMALL)
    shift_out_ref[0] = carry_shift[...]

    w_lin = w0_ref[...] + _dot3(_dot, jnp.tanh(sm), wup_ref[...])
    a = jax.nn.sigmoid(a0_ref[...] + _dot3(_dot, sm, aup_ref[...]))
    lw = -jnp.exp(-_softplus(-w_lin) - 0.5)
    kk = k * kk_ref[...]
    kk = kk * lax.rsqrt(jnp.maximum(_segsum(kk * kk, ones384), 1e-24))
    kt = k * (1.0 + (a - 1.0) * ka_ref[...])
    gr = _silu(g)
    gr_ref[0] = gr
    bg_ref[0] = _segsum(r * kt * rk_ref[...], ones384) * v * gr
    lc = _exact_lhs_dot(tri_chunk, lw, 3)
    e_neg = jnp.exp(-lc)
    rt = r * jnp.exp(lc)
    at = -kk * jnp.exp(lc - lw)
    kh = kt * e_neg
    bh = kk * a * e_neg
    for h in range(N_RWKV):
        sl = slice(HEAD * h, HEAD * (h + 1))
        rt_ref[0, h] = rt[:, sl]
        at_ref[0, h] = at[:, sl]
        kh_ref[0, h] = kh[:, sl]
        bh_ref[0, h] = bh[:, sl]
        rv_ref[0, h] = v[:, sl]
        lc_ref[0, h] = lc[:, sl]


def _proj(x, shift0, mkt, mvb, wts, tm, chunk):
    b, t, _ = x.shape
    grid = (b, t // tm)
    tok = lambda w: pl.BlockSpec((1, tm, w), lambda bi, i: (bi, i, 0))
    hm = lambda nh, w: pl.BlockSpec((1, nh, tm, w), lambda bi, i: (bi, 0, i, 0))
    per_b = lambda s1, s2: pl.BlockSpec((1, s1, s2), lambda bi, i: (bi, 0, 0))
    w_specs = [_full(a.shape) for a in wts]
    out_shape = [
        jax.ShapeDtypeStruct((b, N_FOX, t, AUG), BF16), jax.ShapeDtypeStruct((b, N_FOX, t, AUG), BF16),
        jax.ShapeDtypeStruct((b, t, W_FOX), BF16), jax.ShapeDtypeStruct((b, t, W_FOX), F32),
        jax.ShapeDtypeStruct((b, t, W_FOX), F32), jax.ShapeDtypeStruct((b, t, N_FOX), F32),
        jax.ShapeDtypeStruct((b, t, W_FOX), F32),
    ] + [jax.ShapeDtypeStruct((b, N_RWKV, t, HEAD), F32)] * 6 + [
        jax.ShapeDtypeStruct((b, t, W_RWKV), F32), jax.ShapeDtypeStruct((b, t, W_RWKV), F32),
        jax.ShapeDtypeStruct((b, t, W_MEM), F32), jax.ShapeDtypeStruct((b, 1, SHIFT_W), F32),
    ]
    out_specs = [hm(N_FOX, AUG), hm(N_FOX, AUG), tok(W_FOX), tok(W_FOX), tok(W_FOX), tok(N_FOX), tok(W_FOX)] \
        + [hm(N_RWKV, HEAD)] * 6 + [tok(W_RWKV), tok(W_RWKV), tok(W_MEM), per_b(1, SHIFT_W)]
    return pl.pallas_call(
        functools.partial(_proj_body, chunk),
        grid=grid,
        in_specs=[tok(D_MODEL), per_b(1, SHIFT_W), per_b(W_MEM, N_MEM * N_MEM_TOK),
                  per_b(N_MEM * N_MEM_TOK, W_MEM)] + w_specs,
        out_specs=out_specs,
        out_shape=out_shape,
        scratch_shapes=[pltpu.VMEM((1, SHIFT_W), F32), pltpu.VMEM((1, SMALL), F32)],
        compiler_params=_params(2),
        name="proj",
    )(x, shift0, mkt, mvb, *wts)


def _fox_body(qa_ref, ka_ref, vb_ref, o_ref, m_scr, l_scr, acc_scr):
    qi = pl.program_id(2)
    ki = pl.program_id(3)
    nk = pl.num_programs(3)
    tq = qa_ref.shape[2]
    tk = ka_ref.shape[2]

    @pl.when(ki == 0)
    def _():
        m_scr[...] = jnp.full_like(m_scr, -1e30)
        l_scr[...] = jnp.zeros_like(l_scr)
        acc_scr[...] = jnp.zeros_like(acc_scr)

    @pl.when(ki <= qi)
    def _():
        row = lax.broadcasted_iota(jnp.int32, (tq, tk), 0) + qi * tq
        col = lax.broadcasted_iota(jnp.int32, (tq, tk), 1) + ki * tk
        keep = col <= row
        vb = vb_ref[0]
        for hh in range(2):
            s = jnp.where(keep, _dot_nt(qa_ref[0, hh], ka_ref[0, hh]), -1e30)
            m_prev = m_scr[hh]
            m_new = jnp.maximum(m_prev, jnp.max(s, axis=1, keepdims=True))
            alpha = jnp.exp(m_prev - m_new)
            p = jnp.exp(s - m_new)
            l_scr[hh] = alpha * l_scr[hh] + jnp.sum(p, axis=1, keepdims=True)
            acc_scr[hh] = alpha * acc_scr[hh] + _dot(p.astype(BF16), vb)
            m_scr[hh] = m_new

    @pl.when(ki == nk - 1)
    def _():
        lane = lax.broadcasted_iota(jnp.int32, (tq, LANE), 1)
        o_ref[0] = jnp.where(lane < HEAD, acc_scr[0] / l_scr[0], acc_scr[1] / l_scr[1])


def _fox(qa, ka, vb, tq):
    b, _, t, _ = qa.shape
    nq = t // tq
    return pl.pallas_call(
        _fox_body,
        grid=(b, N_FOX // 2, nq, nq),
        in_specs=[pl.BlockSpec((1, 2, tq, AUG), lambda bi, p, qi, ki: (bi, p, qi, 0)),
                  pl.BlockSpec((1, 2, tq, AUG), lambda bi, p, qi, ki: (bi, p, jnp.minimum(ki, qi), 0)),
                  pl.BlockSpec((1, tq, LANE), lambda bi, p, qi, ki: (bi, jnp.minimum(ki, qi), p))],
        out_specs=pl.BlockSpec((1, tq, LANE), lambda bi, p, qi, ki: (bi, qi, p)),
        out_shape=jax.ShapeDtypeStruct((b, t, W_FOX), F32),
        scratch_shapes=[pltpu.VMEM((2, tq, 1), F32), pltpu.VMEM((2, tq, 1), F32), pltpu.VMEM((2, tq, LANE), F32)],
        compiler_params=_params(4),
        name="fox",
    )(qa, ka, vb)


def _fox_step_body(qa_ref, ka_ref, vn_ref, kp_ref, vp_ref, lfp_ref, pq_ref, pc_ref, aug1_ref, o_ref):
    s_new = qa_ref.shape[2]
    p_len = kp_ref.shape[1]
    row = lax.broadcasted_iota(jnp.int32, (p_len, p_len), 0)
    col = lax.broadcasted_iota(jnp.int32, (p_len, p_len), 1)
    upper = jnp.where(col > row, 1.0, 0.0).astype(BF16)
    suf = _exact_lhs_dot(upper, lfp_ref[0], 3)
    shi, smid, slo = _split3(suf)
    kpa = (_dot(kp_ref[0].astype(BF16), pq_ref[...])
           + (_dot(shi, pc_ref[3]) + _dot(smid, pc_ref[4]) + _dot(slo, pc_ref[5])) + aug1_ref[1:2, :])
    vp = vp_ref[0]
    vn = vn_ref[0]
    qrow = lax.broadcasted_iota(jnp.int32, (s_new, s_new), 0)
    kcol = lax.broadcasted_iota(jnp.int32, (s_new, s_new), 1)
    outs = []
    for h in range(N_FOX):
        q = qa_ref[0, h]
        sp = _dot_nt(q, kpa[:, AUG * h:AUG * (h + 1)].astype(BF16))
        sn = jnp.where(kcol <= qrow, _dot_nt(q, ka_ref[0, h]), -1e30)
        m = jnp.maximum(jnp.max(sp, axis=1, keepdims=True), jnp.max(sn, axis=1, keepdims=True))
        pp = jnp.exp(sp - m)
        pn = jnp.exp(sn - m)
        l = jnp.sum(pp, axis=1, keepdims=True) + jnp.sum(pn, axis=1, keepdims=True)
        sl = slice(HEAD * h, HEAD * (h + 1))
        o = (_dot((pp / l).astype(BF16), vp[:, sl].astype(BF16))
             + _dot((pn / l).astype(BF16), vn[:, sl].astype(BF16)))
        outs.append(o)
    o_ref[0] = jnp.concatenate(outs, axis=1)


def _fox_step(qa, ka, vb, k_past, v_past, logf_past_pad, pq, pc, aug1):
    b, _, s_new, _ = qa.shape
    p_len = k_past.shape[1]
    return pl.pallas_call(
        _fox_step_body,
        grid=(b,),
        in_specs=[pl.BlockSpec((1, N_FOX, s_new, AUG), lambda i: (i, 0, 0, 0)),
                  pl.BlockSpec((1, N_FOX, s_new, AUG), lambda i: (i, 0, 0, 0)),
                  pl.BlockSpec((1, s_new, W_FOX), lambda i: (i, 0, 0)),
                  pl.BlockSpec((1, p_len, W_FOX), lambda i: (i, 0, 0)),
                  pl.BlockSpec((1, p_len, W_FOX), lambda i: (i, 0, 0)),
                  pl.BlockSpec((1, p_len, SMALL), lambda i: (i, 0, 0)),
                  _full(pq.shape), _full(pc.shape), _full(aug1.shape)],
        out_specs=pl.BlockSpec((1, s_new, W_FOX), lambda i: (i, 0, 0)),
        out_shape=jax.ShapeDtypeStruct((b, s_new, W_FOX), F32),
        compiler_params=_params(1),
        name="fox_step",
    )(qa, ka, vb, k_past, v_past, logf_past_pad, pq, pc, aug1)


def _rwkv_body(chunk, rt_ref, at_ref, kh_ref, bh_ref, v_ref, lc_ref, s0_ref, y_ref, sout_ref, s_scr):
    j = pl.program_id(1)
    nj = pl.num_programs(1)
    n_chunks = rt_ref.shape[2] // chunk

    @pl.when(j == 0)
    def _():
        s_scr[...] = s0_ref[0]

    row = lax.broadcasted_iota(jnp.int32, (chunk, chunk), 0)
    col = lax.broadcasted_iota(jnp.int32, (chunk, chunk), 1)
    lower = col <= row
    strict = col < row
    eye = jnp.where(col == row, 1.0, 0.0)

    def one_chunk(ci, carry):
        sl = pl.ds(pl.multiple_of(ci * chunk, chunk), chunk)
        ys = []
        for h in range(N_RWKV):
            rt = rt_ref[0, h, sl, :]
            at = at_ref[0, h, sl, :]
            kh = kh_ref[0, h, sl, :]
            bh = bh_ref[0, h, sl, :]
            vv = v_ref[0, h, sl, :]
            pc = jnp.exp(lc_ref[0, h, pl.ds(ci * chunk + chunk - 1, 1), :])
            a_ab = jnp.where(strict, _dot3(_dot_nt, at, bh), 0.0)
            a_ak = jnp.where(strict, _dot3(_dot_nt, at, kh), 0.0)
            a_rk = jnp.where(lower, _dot1(_dot_nt, rt, kh), 0.0)
            a_rb = jnp.where(lower, _dot1(_dot_nt, rt, bh), 0.0)
            inv = eye + a_ab
            pw = a_ab
            n = 1
            while 2 * n < chunk:
                pw = _dot3(_dot, pw, pw)
                inv = inv + _dot3(_dot, inv, pw)
                n *= 2
            w_mat = _dot3(_dot, inv, at)
            u0 = _dot3(_dot, inv, _dot3(_dot, a_ak, vv))
            s_old = s_scr[h]
            u = _dot3(_dot_nt, w_mat, s_old) + u0
            y = _dot1(_dot_nt, rt, s_old) + _dot1(_dot, a_rk, vv) + _dot1(_dot, a_rb, u)
            s_scr[h] = s_old * pc + _dot3(_dot_tn, vv, kh * pc) + _dot3(_dot_tn, u, bh * pc)
            ys.append(y)
        y_ref[0, sl, :] = jnp.concatenate(ys, axis=1)
        return carry

    lax.fori_loop(0, n_chunks, one_chunk, 0)

    @pl.when(j == nj - 1)
    def _():
        sout_ref[0] = s_scr[...]


def _rwkv(rt, at, kh, bh, v, lc, s0, tc, chunk):
    b, _, t, _ = rt.shape
    hm = pl.BlockSpec((1, N_RWKV, tc, HEAD), lambda bi, j: (bi, 0, j, 0))
    st = pl.BlockSpec((1, N_RWKV, HEAD, HEAD), lambda bi, j: (bi, 0, 0, 0))
    return pl.pallas_call(
        functools.partial(_rwkv_body, chunk),
        grid=(b, t // tc),
        in_specs=[hm] * 6 + [st],
        out_specs=[pl.BlockSpec((1, tc, W_RWKV), lambda bi, j: (bi, j, 0)), st],
        out_shape=[jax.ShapeDtypeStruct((b, t, W_RWKV), F32), jax.ShapeDtypeStruct((b, N_RWKV, HEAD, HEAD), F32)],
        scratch_shapes=[pltpu.VMEM((N_RWKV, HEAD, HEAD), F32)],
        compiler_params=_params(2),
        name="rwkv",
    )(rt, at, kh, bh, v, lc, s0)


def _out_body(x_ref, of_ref, gf_ref, y_ref, gr_ref, bg_ref, om_ref, gnw_ref, gnb_ref, ones_ref,
              wf_ref, wr_ref, wm_ref, o_ref):
    ones384 = ones_ref[...]
    y = y_ref[0]
    mu = _segsum(y, ones384) * (1.0 / HEAD)
    d = y - mu
    var = _segsum(d * d, ones384) * (1.0 / HEAD)
    yn = d * lax.rsqrt(var + GN_EPS) * gnw_ref[...] + gnb_ref[...]
    o_r = yn * gr_ref[0] + bg_ref[0]
    acc = _dot((of_ref[0] * gf_ref[0]).astype(BF16), wf_ref[...])
    acc = acc + _dot(o_r.astype(BF16), wr_ref[...])
    acc = acc + _dot(om_ref[0].astype(BF16), wm_ref[...])
    o_ref[0] = x_ref[0] + acc


def _out(x, of, gf, y, gr, bg, om, gnw, gnb, ones384, wf, wr, wm, tm):
    b, t, _ = x.shape
    tok = lambda w: pl.BlockSpec((1, tm, w), lambda bi, i: (bi, i, 0))
    return pl.pallas_call(
        _out_body,
        grid=(b, t // tm),
        in_specs=[tok(D_MODEL), tok(W_FOX), tok(W_FOX), tok(W_RWKV), tok(W_RWKV), tok(W_RWKV), tok(W_MEM),
                  _full(gnw.shape), _full(gnb.shape), _full(ones384.shape),
                  _full(wf.shape), _full(wr.shape), _full(wm.shape)],
        out_specs=tok(D_MODEL),
        out_shape=jax.ShapeDtypeStruct((b, t, D_MODEL), F32),
        compiler_params=_params(2),
        name="out_proj",
    )(x, of, gf, y, gr, bg, om, gnw, gnb, ones384, wf, wr, wm)


def _padded_column_index():
    idx = np.full((N_PAD,), FOX_COLS + RWKV_COLS + MEM_COLS, np.int32)
    r0 = FOX_COLS
    m0 = FOX_COLS + RWKV_COLS
    idx[OFF_FQ:OFF_FQ + W_FOX] = np.arange(0, W_FOX)
    idx[OFF_FK:OFF_FK + W_FOX] = np.arange(W_FOX, 2 * W_FOX)
    idx[OFF_FV:OFF_FV + W_FOX] = np.arange(2 * W_FOX, 3 * W_FOX)
    idx[OFF_FG:OFF_FG + W_FOX] = np.arange(3 * W_FOX + N_FOX, 4 * W_FOX + N_FOX)
    idx[OFF_RR:OFF_RR + W_RWKV] = r0 + np.arange(0, W_RWKV)
    idx[OFF_RK:OFF_RK + W_RWKV] = r0 + np.arange(W_RWKV, 2 * W_RWKV)
    idx[OFF_RV:OFF_RV + W_RWKV] = r0 + np.arange(2 * W_RWKV, 3 * W_RWKV)
    idx[OFF_RG:OFF_RG + W_RWKV] = r0 + np.arange(3 * W_RWKV + 2 * LORA, 4 * W_RWKV + 2 * LORA)
    idx[OFF_MQ:OFF_MQ + W_MEM] = m0 + np.arange(0, W_MEM)
    idx[OFF_MG:OFF_MG + W_MEM] = m0 + np.arange(W_MEM, 2 * W_MEM)
    idx[OFF_SM:OFF_SM + N_FOX] = np.arange(3 * W_FOX, 3 * W_FOX + N_FOX)
    idx[OFF_SM + SM_WD:OFF_SM + SM_WD + LORA] = r0 + np.arange(3 * W_RWKV, 3 * W_RWKV + LORA)
    idx[OFF_SM + SM_AD:OFF_SM + SM_AD + LORA] = r0 + np.arange(3 * W_RWKV + LORA, 3 * W_RWKV + 2 * LORA)
    return idx


def _shift_row_index():
    idx = np.full((SHIFT_W,), RWKV_COLS, np.int32)
    idx[0:3 * W_RWKV] = np.arange(0, 3 * W_RWKV)
    idx[3 * W_RWKV:4 * W_RWKV] = np.arange(3 * W_RWKV + 2 * LORA, 4 * W_RWKV + 2 * LORA)
    idx[4 * W_RWKV + SM_WD:4 * W_RWKV + SM_WD + LORA] = np.arange(3 * W_RWKV, 3 * W_RWKV + LORA)
    idx[4 * W_RWKV + SM_AD:4 * W_RWKV + SM_AD + LORA] = np.arange(3 * W_RWKV + LORA, 3 * W_RWKV + 2 * LORA)
    return idx


def _shift_row_inverse():
    idx = np.zeros((RWKV_COLS,), np.int32)
    idx[0:3 * W_RWKV] = np.arange(0, 3 * W_RWKV)
    idx[3 * W_RWKV:3 * W_RWKV + LORA] = 4 * W_RWKV + SM_WD + np.arange(LORA)
    idx[3 * W_RWKV + LORA:3 * W_RWKV + 2 * LORA] = 4 * W_RWKV + SM_AD + np.arange(LORA)
    idx[3 * W_RWKV + 2 * LORA:] = 3 * W_RWKV + np.arange(W_RWKV)
    return idx


def _placement_constants():
    pq = np.zeros((W_FOX, N_FOX * AUG), np.float32)
    pc = np.zeros((6, SMALL, N_FOX * AUG), np.float32)
    aug1 = np.zeros((2, N_FOX * AUG), np.float32)
    for h in range(N_FOX):
        for d in range(HEAD):
            pq[HEAD * h + d, AUG * h + d] = 1.0
        for j in range(3):
            pc[j, h, AUG * h + HEAD + j] = 1.0
            pc[3 + j, h, AUG * h + HEAD + 3 + j] = 1.0
            aug1[0, AUG * h + HEAD + 3 + j] = 1.0
            aug1[1, AUG * h + HEAD + j] = 1.0
    return pq, pc, aug1


def _block_ones(width):
    h = np.arange(width) // HEAD
    return (h[:, None] == h[None, :]).astype(np.float32)


def _mem_block_diag(mk, mv):
    b = mk.shape[0]
    eye = jnp.eye(N_MEM, dtype=mk.dtype)
    kt = jnp.einsum('bmhd,hg->bhdgm', mk, eye).reshape(b, W_MEM, N_MEM * N_MEM_TOK)
    vb = jnp.einsum('bmhd,hg->bhmgd', mv, eye).reshape(b, N_MEM * N_MEM_TOK, W_MEM)
    return kt.astype(BF16), vb.astype(BF16)


def _layer(x, shift_prev, s0, mk, mv, fox_past, wts, consts, tm, tq, tc, chunk):
    (proj_wts, gnw, gnb, wf, wr, wm) = wts
    (ones384, pq, pc, aug1, shift_idx, shift_inv) = consts
    b, t, _ = x.shape
    shift0 = jnp.take(jnp.pad(shift_prev, ((0, 0), (0, 0), (0, 1))), shift_idx, axis=2)
    mkt, mvb = _mem_block_diag(mk, mv)
    (qa, ka, vb, kn, v, logf, gf, rt, at, kh, bh, rv, lc, gr, bg, om, shift_out) = _proj(
        x, shift0, mkt, mvb, proj_wts, tm, chunk)
    if fox_past is None:
        of = _fox(qa, ka, vb, tq)
    else:
        k_past, v_past, logf_past = fox_past
        p_len = k_past.shape[1]
        lfp = jnp.pad(logf_past.astype(F32), ((0, 0), (0, 0), (0, SMALL - N_FOX)))
        of = _fox_step(qa, ka, v, k_past.reshape(b, p_len, W_FOX), v_past.reshape(b, p_len, W_FOX), lfp,
                       pq, pc, aug1)
    y_r, s_new = _rwkv(rt, at, kh, bh, rv, lc, s0, tc, chunk)
    y = _out(x, of, gf, y_r, gr, bg, om, gnw, gnb, ones384, wf, wr, wm, tm)
    shift_new = jnp.take(shift_out, shift_inv, axis=2)
    return (y, kn.reshape(b, t, N_FOX, HEAD), v.reshape(b, t, N_FOX, HEAD), logf, s_new, shift_new)


def kernel(x_prompt, x_sample, mem_prompt, cache_fox_k, cache_fox_v, cache_fox_logf, cache_mem_k, cache_mem_v, state_rwkv, state_rwkv_shift, norm_g, w_in, fox_q_g, fox_k_g, fox_b_f, rwkv_mu, rwkv_w0, rwkv_w_up, rwkv_a0, rwkv_a_up, rwkv_k_k, rwkv_k_a, rwkv_r_k, rwkv_gn_w, rwkv_gn_b, mem_norm_g, w_mem_kv, mem_q_g, mem_k_g, w_out):
    depth = w_in.shape[0]
    bp = x_prompt.shape[0]
    col_idx = _padded_column_index()
    shift_idx = _shift_row_index()
    shift_inv = _shift_row_inverse()
    pq_np, pc_np, aug1_np = _placement_constants()
    ones384 = jnp.asarray(_block_ones(W_FOX), BF16)
    ones256 = jnp.asarray(_block_ones(W_MEM), BF16)
    pq = jnp.asarray(pq_np, BF16)
    pc = jnp.asarray(pc_np, BF16)
    aug1 = jnp.asarray(aug1_np, F32)
    consts = (ones384, pq, pc, aug1, shift_idx, shift_inv)

    yp, ys = x_prompt, x_sample
    outs = [[] for _ in range(12)]
    for l in range(depth):
        row = lambda a: a[l].reshape(1, -1).astype(F32)
        tile = lambda a, n: jnp.tile(a[l].reshape(1, -1).astype(F32), (1, n))
        w_pad = jnp.take(jnp.pad(w_in[l], ((0, 0), (0, 1))), col_idx, axis=1).astype(BF16)
        mu_pad = jnp.take(jnp.pad(row(rwkv_mu), ((0, 0), (0, 1))), shift_idx, axis=1)
        bf_pad = jnp.pad(row(fox_b_f), ((0, 0), (0, SMALL - N_FOX)))
        wup_pad = jnp.pad(rwkv_w_up[l].astype(F32), ((SM_WD, SMALL - SM_WD - LORA), (0, 0)))
        aup_pad = jnp.pad(rwkv_a_up[l].astype(F32), ((SM_AD, SMALL - SM_AD - LORA), (0, 0)))
        proj_wts = (row(norm_g), w_pad, tile(fox_q_g, N_FOX), tile(fox_k_g, N_FOX), bf_pad, mu_pad,
                    row(rwkv_w0), row(rwkv_a0), wup_pad, aup_pad, row(rwkv_k_k), row(rwkv_k_a), row(rwkv_r_k),
                    tile(mem_q_g, N_MEM), ones384, ones256, pq, pc, aug1)
        wo = w_out[l].astype(BF16)
        wts = (proj_wts, row(rwkv_gn_w), row(rwkv_gn_b), wo[:W_FOX], wo[W_FOX:W_FOX + W_RWKV], wo[W_FOX + W_RWKV:])

        mk2, mv2 = _mem_kv(mem_prompt, row(mem_norm_g), w_mem_kv[l].astype(BF16), tile(mem_k_g, N_MEM), ones256)
        mk = mk2.reshape(bp, N_MEM_TOK, N_MEM, HEAD)
        mv = mv2.reshape(bp, N_MEM_TOK, N_MEM, HEAD)
        shift_zero = jnp.zeros((bp, 1, RWKV_COLS), F32)
        s_zero = jnp.zeros((bp, N_RWKV, HEAD, HEAD), F32)
        yp, k, v, lf, s_new, sh_new = _layer(yp, shift_zero, s_zero, mk, mv, None, wts, consts,
                                             tm=PROMPT_TM, tq=PROMPT_TQ, tc=PROMPT_TC, chunk=RWKV_CHUNK)
        for lst, val in zip(outs[:7], (k, v, lf, mk, mv, s_new, sh_new)):
            lst.append(val)
        s_len = ys.shape[1]
        ys, k, v, lf, s_new, sh_new = _layer(
            ys, state_rwkv_shift[l], state_rwkv[l].astype(F32), cache_mem_k[l], cache_mem_v[l],
            (cache_fox_k[l], cache_fox_v[l], cache_fox_logf[l]), wts, consts,
            tm=s_len, tq=None, tc=s_len, chunk=s_len)
        for lst, val in zip(outs[7:], (k, v, lf, s_new, sh_new)):
            lst.append(val)
    return (yp, ys) + tuple(jnp.stack(o) for o in outs)
```

```python
import functools

import numpy as np
import jax
import jax.numpy as jnp
from jax import lax
from jax.experimental import pallas as pl
from jax.experimental.pallas import tpu as pltpu

F32 = jnp.float32
BF16 = jnp.bfloat16

D_MODEL = 1024
HEAD = 64
N_FOX = 6
N_RWKV = 6
N_MEM = 4
W_FOX = N_FOX * HEAD
W_RWKV = N_RWKV * HEAD
W_MEM = N_MEM * HEAD
N_MEM_TOK = 256
LORA = 32
NORM_EPS = 1e-6
GN_EPS = 64e-5
FOX_COLS = 4 * W_FOX + N_FOX
RWKV_COLS = 4 * W_RWKV + 2 * LORA
MEM_COLS = 2 * W_MEM

LANE = 128
SMALL = LANE
SM_WD = 32
SM_AD = 64
OFF_FQ, OFF_FK, OFF_FV, OFF_FG = 0, 384, 768, 1152
OFF_RR, OFF_RK, OFF_RV, OFF_RG = 1536, 1920, 2304, 2688
OFF_MQ, OFF_MG = 3072, 3328
OFF_SM = 3584
N_PAD = OFF_SM + SMALL
SHIFT_W = 4 * W_RWKV + SMALL
AUG = LANE
RWKV_CHUNK = 64
VMEM_LIMIT = 56 * 1024 * 1024
PROMPT_TM = 256
PROMPT_TQ = 512
PROMPT_TC = 256


def _dot(a, b):
    return jnp.dot(a, b, preferred_element_type=F32)


def _dot_nt(a, b):
    return lax.dot_general(a, b, (((1,), (1,)), ((), ())), preferred_element_type=F32)


def _dot_tn(a, b):
    return lax.dot_general(a, b, (((0,), (0,)), ((), ())), preferred_element_type=F32)


def _split2(x):
    hi = x.astype(BF16)
    lo = (x - hi.astype(F32)).astype(BF16)
    return hi, lo


def _split3(x):
    hi = x.astype(BF16)
    r1 = x - hi.astype(F32)
    mid = r1.astype(BF16)
    lo = (r1 - mid.astype(F32)).astype(BF16)
    return hi, mid, lo


def _dot3(fn, a, b):
    ah, al = _split2(a)
    bh, bl = _split2(b)
    return fn(ah, bh) + (fn(ah, bl) + fn(al, bh))


def _dot1(fn, a, b):
    return fn(a.astype(BF16), b.astype(BF16))


def _exact_lhs_dot(m_bf16, x, parts):
    pieces = _split3(x) if parts == 3 else _split2(x)
    acc = _dot(m_bf16, pieces[0])
    for p in pieces[1:]:
        acc = acc + _dot(m_bf16, p)
    return acc


def _segsum(x, ones_bd):
    hi, lo = _split2(x)
    return _dot(hi, ones_bd) + _dot(lo, ones_bd)


def _head_rms(t, gain, ones_bd):
    msq = _segsum(t * t, ones_bd) * (1.0 / HEAD)
    return t * lax.rsqrt(msq + NORM_EPS) * gain


def _silu(x):
    return x * jax.nn.sigmoid(x)


def _softplus(z):
    return jnp.maximum(z, 0.0) + jnp.log1p(jnp.exp(-jnp.abs(z)))


def _full(shape):
    n = len(shape)
    return pl.BlockSpec(shape, lambda *_: (0,) * n)


def _params(n_axes):
    return pltpu.CompilerParams(dimension_semantics=("arbitrary",) * n_axes, vmem_limit_bytes=VMEM_LIMIT)


def _mem_kv_body(mem_ref, g_ref, w_ref, kg_ref, ones_ref, k_ref, v_ref):
    x = mem_ref[0]
    ms = jnp.mean(x * x, axis=-1, keepdims=True)
    xn = (x * lax.rsqrt(ms + NORM_EPS) * g_ref[...]).astype(BF16)
    kv = _dot(xn, w_ref[...])
    k_ref[0] = _head_rms(kv[:, :W_MEM], kg_ref[...], ones_ref[...])
    v_ref[0] = kv[:, W_MEM:]


def _mem_kv(mem, g, w_bf16, kg4, ones256):
    b = mem.shape[0]
    blk = pl.BlockSpec((1, N_MEM_TOK, W_MEM), lambda i: (i, 0, 0))
    return pl.pallas_call(
        _mem_kv_body,
        grid=(b,),
        in_specs=[pl.BlockSpec((1, N_MEM_TOK, D_MODEL), lambda i: (i, 0, 0)),
                  _full((1, D_MODEL)), _full((D_MODEL, 2 * W_MEM)), _full((1, W_MEM)), _full((W_MEM, W_MEM))],
        out_specs=[blk, blk],
        out_shape=[jax.ShapeDtypeStruct((b, N_MEM_TOK, W_MEM), F32)] * 2,
        compiler_params=_params(1),
        name="mem_kv",
    )(mem, g, w_bf16, kg4, ones256)


def _proj_body(chunk,
               x_ref, shift0_ref, mkt_ref, mvb_ref, ng_ref, w_ref,
               fqg_ref, fkg_ref, bf_ref, mu_ref, w0_ref, a0_ref, wup_ref, aup_ref,
               kk_ref, ka_ref, rk_ref, mqg_ref, ones384_ref, ones256_ref, pq_ref, pc_ref, aug1_ref,
               qa_ref, kaug_ref, vb_ref, kn_ref, v_ref, logf_ref, gf_ref,
               rt_ref, at_ref, kh_ref, bh_ref, rv_ref, lc_ref, gr_ref, bg_ref, om_ref, shift_out_ref,
               carry_shift, carry_c):
    i = pl.program_id(1)
    tm = x_ref.shape[1]

    @pl.when(i == 0)
    def _():
        carry_shift[...] = shift0_ref[0]
        carry_c[...] = jnp.zeros_like(carry_c)

    x = x_ref[0]
    ms = jnp.mean(x * x, axis=-1, keepdims=True)
    xn = (x * lax.rsqrt(ms + NORM_EPS) * ng_ref[...]).astype(BF16)

    def proj(off, width):
        return _dot(xn, w_ref[:, off:off + width])

    ones384 = ones384_ref[...]
    row = lax.broadcasted_iota(jnp.int32, (tm, tm), 0)
    col = lax.broadcasted_iota(jnp.int32, (tm, tm), 1)
    tri = jnp.where(col <= row, 1.0, 0.0).astype(BF16)
    tri_chunk = jnp.where((col <= row) & ((row ^ col) < chunk), 1.0, 0.0).astype(BF16)

    hs = proj(OFF_SM, SMALL)
    qn = _head_rms(proj(OFF_FQ, W_FOX), fqg_ref[...], ones384)
    kn = _head_rms(proj(OFF_FK, W_FOX), fkg_ref[...], ones384)
    hv = proj(OFF_FV, W_FOX)
    hg = proj(OFF_FG, W_FOX)
    kn_ref[0] = kn
    v_ref[0] = hv
    vb_ref[0] = hv.astype(BF16)
    gf_ref[0] = _silu(hg)
    f = hs + bf_ref[...]
    lane = lax.broadcasted_iota(jnp.int32, (tm, SMALL), 1)
    logf = jnp.where(lane < N_FOX, jnp.minimum(f, 0.0) - jnp.log1p(jnp.exp(-jnp.abs(f))), 0.0)
    logf_ref[0] = logf[:, 0:N_FOX]
    c = _exact_lhs_dot(tri, logf, 3) + carry_c[...]
    carry_c[...] = c[tm - 1:tm, :]
    chi, cmid, clo = _split3(c)
    qs = (qn * (HEAD ** -0.5)).astype(BF16)
    qa = (_dot(qs, pq_ref[...]) + _dot(chi, pc_ref[0]) + _dot(cmid, pc_ref[1]) + _dot(clo, pc_ref[2])
          + aug1_ref[0:1, :])
    ka = (_dot(kn.astype(BF16), pq_ref[...]) - (_dot(chi, pc_ref[3]) + _dot(cmid, pc_ref[4]) + _dot(clo, pc_ref[5]))
          + aug1_ref[1:2, :])
    for h in range(N_FOX):
        qa_ref[0, h] = qa[:, AUG * h:AUG * (h + 1)].astype(BF16)
        kaug_ref[0, h] = ka[:, AUG * h:AUG * (h + 1)].astype(BF16)

    qm = _head_rms(proj(OFF_MQ, W_MEM), mqg_ref[...], ones256_ref[...])
    s = _dot((qm * (HEAD ** -0.5)).astype(BF16), mkt_ref[0])
    ps = []
    for h in range(N_MEM):
        sh = s[:, N_MEM_TOK * h:N_MEM_TOK * (h + 1)]
        e = jnp.exp(sh - jnp.max(sh, axis=1, keepdims=True))
        ps.append((e / jnp.sum(e, axis=1, keepdims=True)).astype(BF16))
    om = _dot(jnp.concatenate(ps, axis=1), mvb_ref[0])
    om_ref[0] = om * _silu(proj(OFF_MG, W_MEM))

    first_row = lax.broadcasted_iota(jnp.int32, (tm, 1), 0) == 0

    def tshift(cols, off, width):
        prev = jnp.where(first_row, carry_shift[:, off:off + width], pltpu.roll(cols, 1, 0))
        carry_shift[:, off:off + width] = cols[tm - 1:tm, :]
        return cols + (prev - cols) * mu_ref[:, off:off + width]

    r = tshift(proj(OFF_RR, W_RWKV), 0, W_RWKV)
    k = tshift(proj(OFF_RK, W_RWKV), W_RWKV, W_RWKV)
    v = tshift(proj(OFF_RV, W_RWKV), 2 * W_RWKV, W_RWKV)
    g = tshift(proj(OFF_RG, W_RWKV), 3 * W_RWKV, W_RWKV)
    sm = tshift(hs, 4 * W_RWKV, SMALL)
    shift_out_ref[0] = carry_shift[...]

    w_lin = w0_ref[...] + _dot3(_dot, jnp.tanh(sm), wup_ref[...])
    a = jax.nn.sigmoid(a0_ref[...] + _dot3(_dot, sm, aup_ref[...]))
    lw = -jnp.exp(-_softplus(-w_lin) - 0.5)
    kk = k * kk_ref[...]
    kk = kk * lax.rsqrt(jnp.maximum(_segsum(kk * kk, ones384), 1e-24))
    kt = k * (1.0 + (a - 1.0) * ka_ref[...])
    gr = _silu(g)
    gr_ref[0] = gr
    bg_ref[0] = _segsum(r * kt * rk_ref[...], ones384) * v * gr
    lc = _exact_lhs_dot(tri_chunk, lw, 3)
    e_neg = jnp.exp(-lc)
    rt = r * jnp.exp(lc)
    at = -kk * jnp.exp(lc - lw)
    kh = kt * e_neg
    bh = kk * a * e_neg
    for h in range(N_RWKV):
        sl = slice(HEAD * h, HEAD * (h + 1))
        rt_ref[0, h] = rt[:, sl]
        at_ref[0, h] = at[:, sl]
        kh_ref[0, h] = kh[:, sl]
        bh_ref[0, h] = bh[:, sl]
        rv_ref[0, h] = v[:, sl]
        lc_ref[0, h] = lc[:, sl]


def _proj(x, shift0, mkt, mvb, wts, tm, chunk):
    b, t, _ = x.shape
    grid = (b, t // tm)
    tok = lambda w: pl.BlockSpec((1, tm, w), lambda bi, i: (bi, i, 0))
    hm = lambda nh, w: pl.BlockSpec((1, nh, tm, w), lambda bi, i: (bi, 0, i, 0))
    per_b = lambda s1, s2: pl.BlockSpec((1, s1, s2), lambda bi, i: (bi, 0, 0))
    w_specs = [_full(a.shape) for a in wts]
    out_shape = [
        jax.ShapeDtypeStruct((b, N_FOX, t, AUG), BF16), jax.ShapeDtypeStruct((b, N_FOX, t, AUG), BF16),
        jax.ShapeDtypeStruct((b, t, W_FOX), BF16), jax.ShapeDtypeStruct((b, t, W_FOX), F32),
        jax.ShapeDtypeStruct((b, t, W_FOX), F32), jax.ShapeDtypeStruct((b, t, N_FOX), F32),
        jax.ShapeDtypeStruct((b, t, W_FOX), F32),
    ] + [jax.ShapeDtypeStruct((b, N_RWKV, t, HEAD), F32)] * 6 + [
        jax.ShapeDtypeStruct((b, t, W_RWKV), F32), jax.ShapeDtypeStruct((b, t, W_RWKV), F32),
        jax.ShapeDtypeStruct((b, t, W_MEM), F32), jax.ShapeDtypeStruct((b, 1, SHIFT_W), F32),
    ]
    out_specs = [hm(N_FOX, AUG), hm(N_FOX, AUG), tok(W_FOX), tok(W_FOX), tok(W_FOX), tok(N_FOX), tok(W_FOX)] \
        + [hm(N_RWKV, HEAD)] * 6 + [tok(W_RWKV), tok(W_RWKV), tok(W_MEM), per_b(1, SHIFT_W)]
    return pl.pallas_call(
        functools.partial(_proj_body, chunk),
        grid=grid,
        in_specs=[tok(D_MODEL), per_b(1, SHIFT_W), per_b(W_MEM, N_MEM * N_MEM_TOK),
                  per_b(N_MEM * N_MEM_TOK, W_MEM)] + w_specs,
        out_specs=out_specs,
        out_shape=out_shape,
        scratch_shapes=[pltpu.VMEM((1, SHIFT_W), F32), pltpu.VMEM((1, SMALL), F32)],
        compiler_params=_params(2),
        name="proj",
    )(x, shift0, mkt, mvb, *wts)


def _fox_body(qa_ref, ka_ref, vb_ref, o_ref, m_scr, l_scr, acc_scr):
    qi = pl.program_id(2)
    ki = pl.program_id(3)
    nk = pl.num_programs(3)
    tq = qa_ref.shape[2]
    tk = ka_ref.shape[2]

    @pl.when(ki == 0)
    def _():
        m_scr[...] = jnp.full_like(m_scr, -1e30)
        l_scr[...] = jnp.zeros_like(l_scr)
        acc_scr[...] = jnp.zeros_like(acc_scr)

    @pl.when(ki <= qi)
    def _():
        row = lax.broadcasted_iota(jnp.int32, (tq, tk), 0) + qi * tq
        col = lax.broadcasted_iota(jnp.int32, (tq, tk), 1) + ki * tk
        keep = col <= row
        vb = vb_ref[0]
        for hh in range(2):
            s = jnp.where(keep, _dot_nt(qa_ref[0, hh], ka_ref[0, hh]), -1e30)
            m_prev = m_scr[hh]
            m_new = jnp.maximum(m_prev, jnp.max(s, axis=1, keepdims=True))
            alpha = jnp.exp(m_prev - m_new)
            p = jnp.exp(s - m_new)
            l_scr[hh] = alpha * l_scr[hh] + jnp.sum(p, axis=1, keepdims=True)
            acc_scr[hh] = alpha * acc_scr[hh] + _dot(p.astype(BF16), vb)
            m_scr[hh] = m_new

    @pl.when(ki == nk - 1)
    def _():
        lane = lax.broadcasted_iota(jnp.int32, (tq, LANE), 1)
        o_ref[0] = jnp.where(lane < HEAD, acc_scr[0] / l_scr[0], acc_scr[1] / l_scr[1])


def _fox(qa, ka, vb, tq):
    b, _, t, _ = qa.shape
    nq = t // tq
    return pl.pallas_call(
        _fox_body,
        grid=(b, N_FOX // 2, nq, nq),
        in_specs=[pl.BlockSpec((1, 2, tq, AUG), lambda bi, p, qi, ki: (bi, p, qi, 0)),
                  pl.BlockSpec((1, 2, tq, AUG), lambda bi, p, qi, ki: (bi, p, jnp.minimum(ki, qi), 0)),
                  pl.BlockSpec((1, tq, LANE), lambda bi, p, qi, ki: (bi, jnp.minimum(ki, qi), p))],
        out_specs=pl.BlockSpec((1, tq, LANE), lambda bi, p, qi, ki: (bi, qi, p)),
        out_shape=jax.ShapeDtypeStruct((b, t, W_FOX), F32),
        scratch_shapes=[pltpu.VMEM((2, tq, 1), F32), pltpu.VMEM((2, tq, 1), F32), pltpu.VMEM((2, tq, LANE), F32)],
        compiler_params=_params(4),
        name="fox",
    )(qa, ka, vb)


def _fox_step_body(qa_ref, ka_ref, vn_ref, kp_ref, vp_ref, lfp_ref, pq_ref, pc_ref, aug1_ref, o_ref):
    s_new = qa_ref.shape[2]
    p_len = kp_ref.shape[1]
    row = lax.broadcasted_iota(jnp.int32, (p_len, p_len), 0)
    col = lax.broadcasted_iota(jnp.int32, (p_len, p_len), 1)
    upper = jnp.where(col > row, 1.0, 0.0).astype(BF16)
    suf = _exact_lhs_dot(upper, lfp_ref[0], 3)
    shi, smid, slo = _split3(suf)
    kpa = (_dot(kp_ref[0].astype(BF16), pq_ref[...])
           + (_dot(shi, pc_ref[3]) + _dot(smid, pc_ref[4]) + _dot(slo, pc_ref[5])) + aug1_ref[1:2, :])
    vp = vp_ref[0]
    vn = vn_ref[0]
    qrow = lax.broadcasted_iota(jnp.int32, (s_new, s_new), 0)
    kcol = lax.broadcasted_iota(jnp.int32, (s_new, s_new), 1)
    outs = []
    for h in range(N_FOX):
        q = qa_ref[0, h]
        sp = _dot_nt(q, kpa[:, AUG * h:AUG * (h + 1)].astype(BF16))
        sn = jnp.where(kcol <= qrow, _dot_nt(q, ka_ref[0, h]), -1e30)
        m = jnp.maximum(jnp.max(sp, axis=1, keepdims=True), jnp.max(sn, axis=1, keepdims=True))
        pp = jnp.exp(sp - m)
        pn = jnp.exp(sn - m)
        l = jnp.sum(pp, axis=1, keepdims=True) + jnp.sum(pn, axis=1, keepdims=True)
        sl = slice(HEAD * h, HEAD * (h + 1))
        o = (_dot((pp / l).astype(BF16), vp[:, sl].astype(BF16))
             + _dot((pn / l).astype(BF16), vn[:, sl].astype(BF16)))
        outs.append(o)
    o_ref[0] = jnp.concatenate(outs, axis=1)


def _fox_step(qa, ka, vb, k_past, v_past, logf_past_pad, pq, pc, aug1):
    b, _, s_new, _ = qa.shape
    p_len = k_past.shape[1]
    return pl.pallas_call(
        _fox_step_body,
        grid=(b,),
        in_specs=[pl.BlockSpec((1, N_FOX, s_new, AUG), lambda i: (i, 0, 0, 0)),
                  pl.BlockSpec((1, N_FOX, s_new, AUG), lambda i: (i, 0, 0, 0)),
                  pl.BlockSpec((1, s_new, W_FOX), lambda i: (i, 0, 0)),
                  pl.BlockSpec((1, p_len, W_FOX), lambda i: (i, 0, 0)),
                  pl.BlockSpec((1, p_len, W_FOX), lambda i: (i, 0, 0)),
                  pl.BlockSpec((1, p_len, SMALL), lambda i: (i, 0, 0)),
                  _full(pq.shape), _full(pc.shape), _full(aug1.shape)],
        out_specs=pl.BlockSpec((1, s_new, W_FOX), lambda i: (i, 0, 0)),
        out_shape=jax.ShapeDtypeStruct((b, s_new, W_FOX), F32),
        compiler_params=_params(1),
        name="fox_step",
    )(qa, ka, vb, k_past, v_past, logf_past_pad, pq, pc, aug1)


def _rwkv_body(chunk, rt_ref, at_ref, kh_ref, bh_ref, v_ref, lc_ref, s0_ref, y_ref, sout_ref, s_scr):
    j = pl.program_id(1)
    nj = pl.num_programs(1)
    n_chunks = rt_ref.shape[2] // chunk

    @pl.when(j == 0)
    def _():
        s_scr[...] = s0_ref[0]

    row = lax.broadcasted_iota(jnp.int32, (chunk, chunk), 0)
    col = lax.broadcasted_iota(jnp.int32, (chunk, chunk), 1)
    lower = col <= row
    strict = col < row
    eye = jnp.where(col == row, 1.0, 0.0)

    def one_chunk(ci, carry):
        sl = pl.ds(pl.multiple_of(ci * chunk, chunk), chunk)
        hs = range(N_RWKV)
        rt = [rt_ref[0, h, sl, :] for h in hs]
        at = [at_ref[0, h, sl, :] for h in hs]
        kh = [kh_ref[0, h, sl, :] for h in hs]
        bh = [bh_ref[0, h, sl, :] for h in hs]
        vv = [v_ref[0, h, sl, :] for h in hs]
        pc = [jnp.exp(lc_ref[0, h, pl.ds(ci * chunk + chunk - 1, 1), :]) for h in hs]
        a_ab = [jnp.where(strict, _dot3(_dot_nt, at[h], bh[h]), 0.0) for h in hs]
        a_ak = [jnp.where(strict, _dot3(_dot_nt, at[h], kh[h]), 0.0) for h in hs]
        a_rk = [jnp.where(lower, _dot1(_dot_nt, rt[h], kh[h]), 0.0) for h in hs]
        a_rb = [jnp.where(lower, _dot1(_dot_nt, rt[h], bh[h]), 0.0) for h in hs]
        akv = [_dot3(_dot, a_ak[h], vv[h]) for h in hs]
        yv = [_dot1(_dot, a_rk[h], vv[h]) for h in hs]
        inv = [eye + a_ab[h] for h in hs]
        pw = a_ab
        n = 1
        while 2 * n < chunk:
            pw = [_dot3(_dot, pw[h], pw[h]) for h in hs]
            inv = [inv[h] + _dot3(_dot, inv[h], pw[h]) for h in hs]
            n *= 2
        w_mat = [_dot3(_dot, inv[h], at[h]) for h in hs]
        u0 = [_dot3(_dot, inv[h], akv[h]) for h in hs]
        s_old = [s_scr[h] for h in hs]
        u = [_dot3(_dot_nt, w_mat[h], s_old[h]) + u0[h] for h in hs]
        ys = [_dot1(_dot_nt, rt[h], s_old[h]) + yv[h] + _dot1(_dot, a_rb[h], u[h]) for h in hs]
        for h in hs:
            s_scr[h] = (s_old[h] * pc[h] + _dot3(_dot_tn, vv[h], kh[h] * pc[h])
                        + _dot3(_dot_tn, u[h], bh[h] * pc[h]))
        y_ref[0, sl, :] = jnp.concatenate(ys, axis=1)
        return carry

    lax.fori_loop(0, n_chunks, one_chunk, 0)

    @pl.when(j == nj - 1)
    def _():
        sout_ref[0] = s_scr[...]


def _rwkv(rt, at, kh, bh, v, lc, s0, tc, chunk):
    b, _, t, _ = rt.shape
    hm = pl.BlockSpec((1, N_RWKV, tc, HEAD), lambda bi, j: (bi, 0, j, 0))
    st = pl.BlockSpec((1, N_RWKV, HEAD, HEAD), lambda bi, j: (bi, 0, 0, 0))
    return pl.pallas_call(
        functools.partial(_rwkv_body, chunk),
        grid=(b, t // tc),
        in_specs=[hm] * 6 + [st],
        out_specs=[pl.BlockSpec((1, tc, W_RWKV), lambda bi, j: (bi, j, 0)), st],
        out_shape=[jax.ShapeDtypeStruct((b, t, W_RWKV), F32), jax.ShapeDtypeStruct((b, N_RWKV, HEAD, HEAD), F32)],
        scratch_shapes=[pltpu.VMEM((N_RWKV, HEAD, HEAD), F32)],
        compiler_params=_params(2),
        name="rwkv",
    )(rt, at, kh, bh, v, lc, s0)


def _out_body(x_ref, of_ref, gf_ref, y_ref, gr_ref, bg_ref, om_ref, gnw_ref, gnb_ref, ones_ref,
              wf_ref, wr_ref, wm_ref, o_ref):
    ones384 = ones_ref[...]
    y = y_ref[0]
    mu = _segsum(y, ones384) * (1.0 / HEAD)
    d = y - mu
    var = _segsum(d * d, ones384) * (1.0 / HEAD)
    yn = d * lax.rsqrt(var + GN_EPS) * gnw_ref[...] + gnb_ref[...]
    o_r = yn * gr_ref[0] + bg_ref[0]
    acc = _dot((of_ref[0] * gf_ref[0]).astype(BF16), wf_ref[...])
    acc = acc + _dot(o_r.astype(BF16), wr_ref[...])
    acc = acc + _dot(om_ref[0].astype(BF16), wm_ref[...])
    o_ref[0] = x_ref[0] + acc


def _out(x, of, gf, y, gr, bg, om, gnw, gnb, ones384, wf, wr, wm, tm):
    b, t, _ = x.shape
    tok = lambda w: pl.BlockSpec((1, tm, w), lambda bi, i: (bi, i, 0))
    return pl.pallas_call(
        _out_body,
        grid=(b, t // tm),
        in_specs=[tok(D_MODEL), tok(W_FOX), tok(W_FOX), tok(W_RWKV), tok(W_RWKV), tok(W_RWKV), tok(W_MEM),
                  _full(gnw.shape), _full(gnb.shape), _full(ones384.shape),
                  _full(wf.shape), _full(wr.shape), _full(wm.shape)],
        out_specs=tok(D_MODEL),
        out_shape=jax.ShapeDtypeStruct((b, t, D_MODEL), F32),
        compiler_params=_params(2),
        name="out_proj",
    )(x, of, gf, y, gr, bg, om, gnw, gnb, ones384, wf, wr, wm)


def _padded_column_index():
    idx = np.full((N_PAD,), FOX_COLS + RWKV_COLS + MEM_COLS, np.int32)
    r0 = FOX_COLS
    m0 = FOX_COLS + RWKV_COLS
    idx[OFF_FQ:OFF_FQ + W_FOX] = np.arange(0, W_FOX)
    idx[OFF_FK:OFF_FK + W_FOX] = np.arange(W_FOX, 2 * W_FOX)
    idx[OFF_FV:OFF_FV + W_FOX] = np.arange(2 * W_FOX, 3 * W_FOX)
    idx[OFF_FG:OFF_FG + W_FOX] = np.arange(3 * W_FOX + N_FOX, 4 * W_FOX + N_FOX)
    idx[OFF_RR:OFF_RR + W_RWKV] = r0 + np.arange(0, W_RWKV)
    idx[OFF_RK:OFF_RK + W_RWKV] = r0 + np.arange(W_RWKV, 2 * W_RWKV)
    idx[OFF_RV:OFF_RV + W_RWKV] = r0 + np.arange(2 * W_RWKV, 3 * W_RWKV)
    idx[OFF_RG:OFF_RG + W_RWKV] = r0 + np.arange(3 * W_RWKV + 2 * LORA, 4 * W_RWKV + 2 * LORA)
    idx[OFF_MQ:OFF_MQ + W_MEM] = m0 + np.arange(0, W_MEM)
    idx[OFF_MG:OFF_MG + W_MEM] = m0 + np.arange(W_MEM, 2 * W_MEM)
    idx[OFF_SM:OFF_SM + N_FOX] = np.arange(3 * W_FOX, 3 * W_FOX + N_FOX)
    idx[OFF_SM + SM_WD:OFF_SM + SM_WD + LORA] = r0 + np.arange(3 * W_RWKV, 3 * W_RWKV + LORA)
    idx[OFF_SM + SM_AD:OFF_SM + SM_AD + LORA] = r0 + np.arange(3 * W_RWKV + LORA, 3 * W_RWKV + 2 * LORA)
    return idx


def _shift_row_index():
    idx = np.full((SHIFT_W,), RWKV_COLS, np.int32)
    idx[0:3 * W_RWKV] = np.arange(0, 3 * W_RWKV)
    idx[3 * W_RWKV:4 * W_RWKV] = np.arange(3 * W_RWKV + 2 * LORA, 4 * W_RWKV + 2 * LORA)
    idx[4 * W_RWKV + SM_WD:4 * W_RWKV + SM_WD + LORA] = np.arange(3 * W_RWKV, 3 * W_RWKV + LORA)
    idx[4 * W_RWKV + SM_AD:4 * W_RWKV + SM_AD + LORA] = np.arange(3 * W_RWKV + LORA, 3 * W_RWKV + 2 * LORA)
    return idx


def _shift_row_inverse():
    idx = np.zeros((RWKV_COLS,), np.int32)
    idx[0:3 * W_RWKV] = np.arange(0, 3 * W_RWKV)
    idx[3 * W_RWKV:3 * W_RWKV + LORA] = 4 * W_RWKV + SM_WD + np.arange(LORA)
    idx[3 * W_RWKV + LORA:3 * W_RWKV + 2 * LORA] = 4 * W_RWKV + SM_AD + np.arange(LORA)
    idx[3 * W_RWKV + 2 * LORA:] = 3 * W_RWKV + np.arange(W_RWKV)
    return idx


def _placement_constants():
    pq = np.zeros((W_FOX, N_FOX * AUG), np.float32)
    pc = np.zeros((6, SMALL, N_FOX * AUG), np.float32)
    aug1 = np.zeros((2, N_FOX * AUG), np.float32)
    for h in range(N_FOX):
        for d in range(HEAD):
            pq[HEAD * h + d, AUG * h + d] = 1.0
        for j in range(3):
            pc[j, h, AUG * h + HEAD + j] = 1.0
            pc[3 + j, h, AUG * h + HEAD + 3 + j] = 1.0
            aug1[0, AUG * h + HEAD + 3 + j] = 1.0
            aug1[1, AUG * h + HEAD + j] = 1.0
    return pq, pc, aug1


def _block_ones(width):
    h = np.arange(width) // HEAD
    return (h[:, None] == h[None, :]).astype(np.float32)


def _mem_block_diag(mk, mv):
    b = mk.shape[0]
    eye = jnp.eye(N_MEM, dtype=mk.dtype)
    kt = jnp.einsum('bmhd,hg->bhdgm', mk, eye).reshape(b, W_MEM, N_MEM * N_MEM_TOK)
    vb = jnp.einsum('bmhd,hg->bhmgd', mv, eye).reshape(b, N_MEM * N_MEM_TOK, W_MEM)
    return kt.astype(BF16), vb.astype(BF16)


def _layer(x, shift_prev, s0, mk, mv, fox_past, wts, consts, tm, tq, tc, chunk):
    (proj_wts, gnw, gnb, wf, wr, wm) = wts
    (ones384, pq, pc, aug1, shift_idx, shift_inv) = consts
    b, t, _ = x.shape
    shift0 = jnp.take(jnp.pad(shift_prev, ((0, 0), (0, 0), (0, 1))), shift_idx, axis=2)
    mkt, mvb = _mem_block_diag(mk, mv)
    (qa, ka, vb, kn, v, logf, gf, rt, at, kh, bh, rv, lc, gr, bg, om, shift_out) = _proj(
        x, shift0, mkt, mvb, proj_wts, tm, chunk)
    if fox_past is None:
        of = _fox(qa, ka, vb, tq)
    else:
        k_past, v_past, logf_past = fox_past
        p_len = k_past.shape[1]
        lfp = jnp.pad(logf_past.astype(F32), ((0, 0), (0, 0), (0, SMALL - N_FOX)))
        of = _fox_step(qa, ka, v, k_past.reshape(b, p_len, W_FOX), v_past.reshape(b, p_len, W_FOX), lfp,
                       pq, pc, aug1)
    y_r, s_new = _rwkv(rt, at, kh, bh, rv, lc, s0, tc, chunk)
    y = _out(x, of, gf, y_r, gr, bg, om, gnw, gnb, ones384, wf, wr, wm, tm)
    shift_new = jnp.take(shift_out, shift_inv, axis=2)
    return (y, kn.reshape(b, t, N_FOX, HEAD), v.reshape(b, t, N_FOX, HEAD), logf, s_new, shift_new)


def kernel(x_prompt, x_sample, mem_prompt, cache_fox_k, cache_fox_v, cache_fox_logf, cache_mem_k, cache_mem_v, state_rwkv, state_rwkv_shift, norm_g, w_in, fox_q_g, fox_k_g, fox_b_f, rwkv_mu, rwkv_w0, rwkv_w_up, rwkv_a0, rwkv_a_up, rwkv_k_k, rwkv_k_a, rwkv_r_k, rwkv_gn_w, rwkv_gn_b, mem_norm_g, w_mem_kv, mem_q_g, mem_k_g, w_out):
    depth = w_in.shape[0]
    bp = x_prompt.shape[0]
    col_idx = _padded_column_index()
    shift_idx = _shift_row_index()
    shift_inv = _shift_row_inverse()
    pq_np, pc_np, aug1_np = _placement_constants()
    ones384 = jnp.asarray(_block_ones(W_FOX), BF16)
    ones256 = jnp.asarray(_block_ones(W_MEM), BF16)
    pq = jnp.asarray(pq_np, BF16)
    pc = jnp.asarray(pc_np, BF16)
    aug1 = jnp.asarray(aug1_np, F32)
    consts = (ones384, pq, pc, aug1, shift_idx, shift_inv)

    yp, ys = x_prompt, x_sample
    outs = [[] for _ in range(12)]
    for l in range(depth):
        row = lambda a: a[l].reshape(1, -1).astype(F32)
        tile = lambda a, n: jnp.tile(a[l].reshape(1, -1).astype(F32), (1, n))
        w_pad = jnp.take(jnp.pad(w_in[l], ((0, 0), (0, 1))), col_idx, axis=1).astype(BF16)
        mu_pad = jnp.take(jnp.pad(row(rwkv_mu), ((0, 0), (0, 1))), shift_idx, axis=1)
        bf_pad = jnp.pad(row(fox_b_f), ((0, 0), (0, SMALL - N_FOX)))
        wup_pad = jnp.pad(rwkv_w_up[l].astype(F32), ((SM_WD, SMALL - SM_WD - LORA), (0, 0)))
        aup_pad = jnp.pad(rwkv_a_up[l].astype(F32), ((SM_AD, SMALL - SM_AD - LORA), (0, 0)))
        proj_wts = (row(norm_g), w_pad, tile(fox_q_g, N_FOX), tile(fox_k_g, N_FOX), bf_pad, mu_pad,
                    row(rwkv_w0), row(rwkv_a0), wup_pad, aup_pad, row(rwkv_k_k), row(rwkv_k_a), row(rwkv_r_k),
                    tile(mem_q_g, N_MEM), ones384, ones256, pq, pc, aug1)
        wo = w_out[l].astype(BF16)
        wts = (proj_wts, row(rwkv_gn_w), row(rwkv_gn_b), wo[:W_FOX], wo[W_FOX:W_FOX + W_RWKV], wo[W_FOX + W_RWKV:])

        mk2, mv2 = _mem_kv(mem_prompt, row(mem_norm_g), w_mem_kv[l].astype(BF16), tile(mem_k_g, N_MEM), ones256)
        mk = mk2.reshape(bp, N_MEM_TOK, N_MEM, HEAD)
        mv = mv2.reshape(bp, N_MEM_TOK, N_MEM, HEAD)
        shift_zero = jnp.zeros((bp, 1, RWKV_COLS), F32)
        s_zero = jnp.zeros((bp, N_RWKV, HEAD, HEAD), F32)
        yp, k, v, lf, s_new, sh_new = _layer(yp, shift_zero, s_zero, mk, mv, None, wts, consts,
                                             tm=PROMPT_TM, tq=PROMPT_TQ, tc=PROMPT_TC, chunk=RWKV_CHUNK)
        for lst, val in zip(outs[:7], (k, v, lf, mk, mv, s_new, sh_new)):
            lst.append(val)
        s_len = ys.shape[1]
        ys, k, v, lf, s_new, sh_new = _layer(
            ys, state_rwkv_shift[l], state_rwkv[l].astype(F32), cache_mem_k[l], cache_mem_v[l],
            (cache_fox_k[l], cache_fox_v[l], cache_fox_logf[l]), wts, consts,
            tm=s_len, tq=None, tc=s_len, chunk=s_len)
        for lst, val in zip(outs[7:], (k, v, lf, s_new, sh_new)):
            lst.append(val)
    return (yp, ys) + tuple(jnp.stack(o) for o in outs)
```

```python
import functools

import numpy as np
import jax
import jax.numpy as jnp
from jax import lax
from jax.experimental import pallas as pl
from jax.experimental.pallas import tpu as pltpu

F32 = jnp.float32
BF16 = jnp.bfloat16

D_MODEL = 1024
HEAD = 64
N_FOX = 6
N_RWKV = 6
N_MEM = 4
W_FOX = N_FOX * HEAD
W_RWKV = N_RWKV * HEAD
W_MEM = N_MEM * HEAD
N_MEM_TOK = 256
LORA = 32
NORM_EPS = 1e-6
GN_EPS = 64e-5
FOX_COLS = 4 * W_FOX + N_FOX
RWKV_COLS = 4 * W_RWKV + 2 * LORA
MEM_COLS = 2 * W_MEM

LANE = 128
SMALL = LANE
SM_WD = 32
SM_AD = 64
OFF_FQ, OFF_FK, OFF_FV, OFF_FG = 0, 384, 768, 1152
OFF_RR, OFF_RK, OFF_RV, OFF_RG = 1536, 1920, 2304, 2688
OFF_MQ, OFF_MG = 3072, 3328
OFF_SM = 3584
N_PAD = OFF_SM + SMALL
SHIFT_W = 4 * W_RWKV + SMALL
AUG = LANE
RWKV_CHUNK = 64
VMEM_LIMIT = 56 * 1024 * 1024
PROMPT_TM = 256
PROMPT_TQ = 512
PROMPT_TC = 256


def _dot(a, b):
    return jnp.dot(a, b, preferred_element_type=F32)


def _dot_nt(a, b):
    return lax.dot_general(a, b, (((1,), (1,)), ((), ())), preferred_element_type=F32)


def _dot_tn(a, b):
    return lax.dot_general(a, b, (((0,), (0,)), ((), ())), preferred_element_type=F32)


def _split2(x):
    hi = x.astype(BF16)
    lo = (x - hi.astype(F32)).astype(BF16)
    return hi, lo


def _split3(x):
    hi = x.astype(BF16)
    r1 = x - hi.astype(F32)
    mid = r1.astype(BF16)
    lo = (r1 - mid.astype(F32)).astype(BF16)
    return hi, mid, lo


def _dot3(fn, a, b):
    ah, al = _split2(a)
    bh, bl = _split2(b)
    return fn(ah, bh) + (fn(ah, bl) + fn(al, bh))


def _dot1(fn, a, b):
    return fn(a.astype(BF16), b.astype(BF16))


def _exact_lhs_dot(m_bf16, x, parts):
    pieces = _split3(x) if parts == 3 else _split2(x)
    acc = _dot(m_bf16, pieces[0])
    for p in pieces[1:]:
        acc = acc + _dot(m_bf16, p)
    return acc


def _segsum(x, ones_bd):
    hi, lo = _split2(x)
    return _dot(hi, ones_bd) + _dot(lo, ones_bd)


def _head_rms(t, gain, ones_bd):
    msq = _segsum(t * t, ones_bd) * (1.0 / HEAD)
    return t * lax.rsqrt(msq + NORM_EPS) * gain


def _silu(x):
    return x * jax.nn.sigmoid(x)


def _softplus(z):
    return jnp.maximum(z, 0.0) + jnp.log1p(jnp.exp(-jnp.abs(z)))


def _full(shape):
    n = len(shape)
    return pl.BlockSpec(shape, lambda *_: (0,) * n)


def _params(n_axes):
    return pltpu.CompilerParams(dimension_semantics=("arbitrary",) * n_axes, vmem_limit_bytes=VMEM_LIMIT)


def _mem_kv_body(mem_ref, g_ref, w_ref, kg_ref, ones_ref, k_ref, v_ref):
    x = mem_ref[0]
    ms = jnp.mean(x * x, axis=-1, keepdims=True)
    xn = (x * lax.rsqrt(ms + NORM_EPS) * g_ref[...]).astype(BF16)
    kv = _dot(xn, w_ref[...])
    k_ref[0] = _head_rms(kv[:, :W_MEM], kg_ref[...], ones_ref[...])
    v_ref[0] = kv[:, W_MEM:]


def _mem_kv(mem, g, w_bf16, kg4, ones256):
    b = mem.shape[0]
    blk = pl.BlockSpec((1, N_MEM_TOK, W_MEM), lambda i: (i, 0, 0))
    return pl.pallas_call(
        _mem_kv_body,
        grid=(b,),
        in_specs=[pl.BlockSpec((1, N_MEM_TOK, D_MODEL), lambda i: (i, 0, 0)),
                  _full((1, D_MODEL)), _full((D_MODEL, 2 * W_MEM)), _full((1, W_MEM)), _full((W_MEM, W_MEM))],
        out_specs=[blk, blk],
        out_shape=[jax.ShapeDtypeStruct((b, N_MEM_TOK, W_MEM), F32)] * 2,
        compiler_params=_params(1),
        name="mem_kv",
    )(mem, g, w_bf16, kg4, ones256)


def _proj_body(chunk, with_vt,
               x_ref, shift0_ref, mkt_ref, mvb_ref, ng_ref, w_ref,
               fqg_ref, fkg_ref, bf_ref, mu_ref, w0_ref, a0_ref, wup_ref, aup_ref,
               kk_ref, ka_ref, rk_ref, mqg_ref, ones384_ref, ones256_ref, pq_ref, pc_ref, aug1_ref,
               qa_ref, kaug_ref, kn_ref, v_ref, logf_ref, gf_ref,
               rt_ref, at_ref, kh_ref, bh_ref, rv_ref, lc_ref, gr_ref, bg_ref, om_ref, shift_out_ref,
               *vt_and_scratch):
    vt_ref = vt_and_scratch[0] if with_vt else None
    carry_shift, carry_c = vt_and_scratch[-2:]
    i = pl.program_id(1)
    tm = x_ref.shape[1]

    @pl.when(i == 0)
    def _():
        carry_shift[...] = shift0_ref[0]
        carry_c[...] = jnp.zeros_like(carry_c)

    x = x_ref[0]
    ms = jnp.mean(x * x, axis=-1, keepdims=True)
    xn = (x * lax.rsqrt(ms + NORM_EPS) * ng_ref[...]).astype(BF16)

    def proj(off, width):
        return _dot(xn, w_ref[:, off:off + width])

    ones384 = ones384_ref[...]
    row = lax.broadcasted_iota(jnp.int32, (tm, tm), 0)
    col = lax.broadcasted_iota(jnp.int32, (tm, tm), 1)
    tri = jnp.where(col <= row, 1.0, 0.0).astype(BF16)
    tri_chunk = jnp.where((col <= row) & ((row ^ col) < chunk), 1.0, 0.0).astype(BF16)

    hs = proj(OFF_SM, SMALL)
    qn = _head_rms(proj(OFF_FQ, W_FOX), fqg_ref[...], ones384)
    kn = _head_rms(proj(OFF_FK, W_FOX), fkg_ref[...], ones384)
    hv = proj(OFF_FV, W_FOX)
    hg = proj(OFF_FG, W_FOX)
    kn_ref[0] = kn
    v_ref[0] = hv
    if with_vt:
        vt_ref[0, 0] = hv.T.astype(BF16)
    gf_ref[0] = _silu(hg)
    f = hs + bf_ref[...]
    lane = lax.broadcasted_iota(jnp.int32, (tm, SMALL), 1)
    logf = jnp.where(lane < N_FOX, jnp.minimum(f, 0.0) - jnp.log1p(jnp.exp(-jnp.abs(f))), 0.0)
    logf_ref[0] = logf[:, 0:N_FOX]
    c = _exact_lhs_dot(tri, logf, 3) + carry_c[...]
    carry_c[...] = c[tm - 1:tm, :]
    chi, cmid, clo = _split3(c)
    qs = (qn * (HEAD ** -0.5)).astype(BF16)
    qa = (_dot(qs, pq_ref[...]) + _dot(chi, pc_ref[0]) + _dot(cmid, pc_ref[1]) + _dot(clo, pc_ref[2])
          + aug1_ref[0:1, :])
    ka = (_dot(kn.astype(BF16), pq_ref[...]) - (_dot(chi, pc_ref[3]) + _dot(cmid, pc_ref[4]) + _dot(clo, pc_ref[5]))
          + aug1_ref[1:2, :])
    for h in range(N_FOX):
        qa_ref[0, h] = qa[:, AUG * h:AUG * (h + 1)].astype(BF16)
        kaug_ref[0, h] = ka[:, AUG * h:AUG * (h + 1)].astype(BF16)

    qm = _head_rms(proj(OFF_MQ, W_MEM), mqg_ref[...], ones256_ref[...])
    s = _dot((qm * (HEAD ** -0.5)).astype(BF16), mkt_ref[0])
    ps = []
    for h in range(N_MEM):
        sh = s[:, N_MEM_TOK * h:N_MEM_TOK * (h + 1)]
        e = jnp.exp(sh - jnp.max(sh, axis=1, keepdims=True))
        ps.append((e / jnp.sum(e, axis=1, keepdims=True)).astype(BF16))
    om = _dot(jnp.concatenate(ps, axis=1), mvb_ref[0])
    om_ref[0] = om * _silu(proj(OFF_MG, W_MEM))

    first_row = lax.broadcasted_iota(jnp.int32, (tm, 1), 0) == 0

    def tshift(cols, off, width):
        prev = jnp.where(first_row, carry_shift[:, off:off + width], pltpu.roll(cols, 1, 0))
        carry_shift[:, off:off + width] = cols[tm - 1:tm, :]
        return cols + (prev - cols) * mu_ref[:, off:off + width]

    r = tshift(proj(OFF_RR, W_RWKV), 0, W_RWKV)
    k = tshift(proj(OFF_RK, W_RWKV), W_RWKV, W_RWKV)
    v = tshift(proj(OFF_RV, W_RWKV), 2 * W_RWKV, W_RWKV)
    g = tshift(proj(OFF_RG, W_RWKV), 3 * W_RWKV, W_RWKV)
    sm = tshift(hs, 4 * W_RWKV, SMALL)
    shift_out_ref[0] = carry_shift[...]

    w_lin = w0_ref[...] + _dot3(_dot, jnp.tanh(sm), wup_ref[...])
    a = jax.nn.sigmoid(a0_ref[...] + _dot3(_dot, sm, aup_ref[...]))
    lw = -jnp.exp(-_softplus(-w_lin) - 0.5)
    kk = k * kk_ref[...]
    kk = kk * lax.rsqrt(jnp.maximum(_segsum(kk * kk, ones384), 1e-24))
    kt = k * (1.0 + (a - 1.0) * ka_ref[...])
    gr = _silu(g)
    gr_ref[0] = gr
    bg_ref[0] = _segsum(r * kt * rk_ref[...], ones384) * v * gr
    lc = _exact_lhs_dot(tri_chunk, lw, 3)
    e_neg = jnp.exp(-lc)
    rt = r * jnp.exp(lc)
    at = -kk * jnp.exp(lc - lw)
    kh = kt * e_neg
    bh = kk * a * e_neg
    for h in range(N_RWKV):
        sl = slice(HEAD * h, HEAD * (h + 1))
        rt_ref[0, h] = rt[:, sl]
        at_ref[0, h] = at[:, sl]
        kh_ref[0, h] = kh[:, sl]
        bh_ref[0, h] = bh[:, sl]
        rv_ref[0, h] = v[:, sl]
        lc_ref[0, h] = lc[:, sl]


def _proj(x, shift0, mkt, mvb, wts, tm, chunk, with_vt):
    b, t, _ = x.shape
    grid = (b, t // tm)
    tok = lambda w: pl.BlockSpec((1, tm, w), lambda bi, i: (bi, i, 0))
    hm = lambda nh, w: pl.BlockSpec((1, nh, tm, w), lambda bi, i: (bi, 0, i, 0))
    per_b = lambda s1, s2: pl.BlockSpec((1, s1, s2), lambda bi, i: (bi, 0, 0))
    w_specs = [_full(a.shape) for a in wts]
    out_shape = [
        jax.ShapeDtypeStruct((b, N_FOX, t, AUG), BF16), jax.ShapeDtypeStruct((b, N_FOX, t, AUG), BF16),
        jax.ShapeDtypeStruct((b, t, W_FOX), F32),
        jax.ShapeDtypeStruct((b, t, W_FOX), F32), jax.ShapeDtypeStruct((b, t, N_FOX), F32),
        jax.ShapeDtypeStruct((b, t, W_FOX), F32),
    ] + [jax.ShapeDtypeStruct((b, N_RWKV, t, HEAD), F32)] * 6 + [
        jax.ShapeDtypeStruct((b, t, W_RWKV), F32), jax.ShapeDtypeStruct((b, t, W_RWKV), F32),
        jax.ShapeDtypeStruct((b, t, W_MEM), F32), jax.ShapeDtypeStruct((b, 1, SHIFT_W), F32),
    ]
    out_specs = [hm(N_FOX, AUG), hm(N_FOX, AUG), tok(W_FOX), tok(W_FOX), tok(N_FOX), tok(W_FOX)] \
        + [hm(N_RWKV, HEAD)] * 6 + [tok(W_RWKV), tok(W_RWKV), tok(W_MEM), per_b(1, SHIFT_W)]
    if with_vt:
        out_shape.append(jax.ShapeDtypeStruct((b, t // tm, W_FOX, tm), BF16))
        out_specs.append(pl.BlockSpec((1, 1, W_FOX, tm), lambda bi, i: (bi, i, 0, 0)))
    return pl.pallas_call(
        functools.partial(_proj_body, chunk, with_vt),
        grid=grid,
        in_specs=[tok(D_MODEL), per_b(1, SHIFT_W), per_b(W_MEM, N_MEM * N_MEM_TOK),
                  per_b(N_MEM * N_MEM_TOK, W_MEM)] + w_specs,
        out_specs=out_specs,
        out_shape=out_shape,
        scratch_shapes=[pltpu.VMEM((1, SHIFT_W), F32), pltpu.VMEM((1, SMALL), F32)],
        compiler_params=_params(2),
        name="proj",
    )(x, shift0, mkt, mvb, *wts)


def _fox_body(qa_ref, ka_ref, vt_ref, o_ref, m_scr, l_scr, acc_scr):
    qi = pl.program_id(2)
    tq = qa_ref.shape[2]
    tk = vt_ref.shape[3]
    m_scr[...] = jnp.full_like(m_scr, -1e30)
    l_scr[...] = jnp.zeros_like(l_scr)
    acc_scr[...] = jnp.zeros_like(acc_scr)

    def tile(ki, masked):
        rows = pl.ds(pl.multiple_of(ki * tk, tk), tk)
        s = [_dot_nt(ka_ref[0, hh, rows, :], qa_ref[0, hh]) for hh in range(2)]
        if masked:
            kpos = lax.broadcasted_iota(jnp.int32, (tk, tq), 0) + ki * tk
            qpos = lax.broadcasted_iota(jnp.int32, (tk, tq), 1) + qi * tq
            s = [jnp.where(kpos <= qpos, sh, -1e30) for sh in s]
        m_prev = [m_scr[hh] for hh in range(2)]
        m_new = [jnp.maximum(m_prev[hh], jnp.max(s[hh], axis=0, keepdims=True)) for hh in range(2)]
        p = [jnp.exp(s[hh] - m_new[hh]) for hh in range(2)]
        vt = vt_ref[0, ki]
        pv = [_dot(vt, p[hh].astype(BF16)) for hh in range(2)]
        for hh in range(2):
            alpha = jnp.exp(m_prev[hh] - m_new[hh])
            l_scr[hh] = alpha * l_scr[hh] + jnp.sum(p[hh], axis=0, keepdims=True)
            acc_scr[hh] = alpha * acc_scr[hh] + pv[hh]
            m_scr[hh] = m_new[hh]

    n_full = qi * (tq // tk)

    def full_tile(ki, carry):
        tile(ki, False)
        return carry

    lax.fori_loop(0, n_full, full_tile, 0)
    for d in range(tq // tk):
        tile(n_full + d, True)
    o_t = jnp.concatenate([acc_scr[0][0:HEAD] / l_scr[0], acc_scr[1][HEAD:2 * HEAD] / l_scr[1]], axis=0)
    o_ref[0] = o_t.T


def _fox(qa, ka, vt, tq):
    b, _, t, _ = qa.shape
    nk, tk = vt.shape[1], vt.shape[3]
    return pl.pallas_call(
        _fox_body,
        grid=(b, N_FOX // 2, t // tq),
        in_specs=[pl.BlockSpec((1, 2, tq, AUG), lambda bi, p, qi: (bi, p, qi, 0)),
                  pl.BlockSpec((1, 2, t, AUG), lambda bi, p, qi: (bi, p, 0, 0)),
                  pl.BlockSpec((1, nk, 2 * HEAD, tk), lambda bi, p, qi: (bi, 0, p, 0))],
        out_specs=pl.BlockSpec((1, tq, 2 * HEAD), lambda bi, p, qi: (bi, qi, p)),
        out_shape=jax.ShapeDtypeStruct((b, t, W_FOX), F32),
        scratch_shapes=[pltpu.VMEM((2, 1, tq), F32), pltpu.VMEM((2, 1, tq), F32), pltpu.VMEM((2, 2 * HEAD, tq), F32)],
        compiler_params=_params(3),
        name="fox",
    )(qa, ka, vt)


def _fox_step_body(qa_ref, ka_ref, vn_ref, kp_ref, vp_ref, lfp_ref, pq_ref, pc_ref, aug1_ref, o_ref):
    s_new = qa_ref.shape[2]
    p_len = kp_ref.shape[1]
    row = lax.broadcasted_iota(jnp.int32, (p_len, p_len), 0)
    col = lax.broadcasted_iota(jnp.int32, (p_len, p_len), 1)
    upper = jnp.where(col > row, 1.0, 0.0).astype(BF16)
    suf = _exact_lhs_dot(upper, lfp_ref[0], 3)
    shi, smid, slo = _split3(suf)
    kpa = (_dot(kp_ref[0].astype(BF16), pq_ref[...])
           + (_dot(shi, pc_ref[3]) + _dot(smid, pc_ref[4]) + _dot(slo, pc_ref[5])) + aug1_ref[1:2, :])
    vp = vp_ref[0]
    vn = vn_ref[0]
    qrow = lax.broadcasted_iota(jnp.int32, (s_new, s_new), 0)
    kcol = lax.broadcasted_iota(jnp.int32, (s_new, s_new), 1)
    outs = []
    for h in range(N_FOX):
        q = qa_ref[0, h]
        sp = _dot_nt(q, kpa[:, AUG * h:AUG * (h + 1)].astype(BF16))
        sn = jnp.where(kcol <= qrow, _dot_nt(q, ka_ref[0, h]), -1e30)
        m = jnp.maximum(jnp.max(sp, axis=1, keepdims=True), jnp.max(sn, axis=1, keepdims=True))
        pp = jnp.exp(sp - m)
        pn = jnp.exp(sn - m)
        l = jnp.sum(pp, axis=1, keepdims=True) + jnp.sum(pn, axis=1, keepdims=True)
        sl = slice(HEAD * h, HEAD * (h + 1))
        o = (_dot((pp / l).astype(BF16), vp[:, sl].astype(BF16))
             + _dot((pn / l).astype(BF16), vn[:, sl].astype(BF16)))
        outs.append(o)
    o_ref[0] = jnp.concatenate(outs, axis=1)


def _fox_step(qa, ka, vb, k_past, v_past, logf_past_pad, pq, pc, aug1):
    b, _, s_new, _ = qa.shape
    p_len = k_past.shape[1]
    return pl.pallas_call(
        _fox_step_body,
        grid=(b,),
        in_specs=[pl.BlockSpec((1, N_FOX, s_new, AUG), lambda i: (i, 0, 0, 0)),
                  pl.BlockSpec((1, N_FOX, s_new, AUG), lambda i: (i, 0, 0, 0)),
                  pl.BlockSpec((1, s_new, W_FOX), lambda i: (i, 0, 0)),
                  pl.BlockSpec((1, p_len, W_FOX), lambda i: (i, 0, 0)),
                  pl.BlockSpec((1, p_len, W_FOX), lambda i: (i, 0, 0)),
                  pl.BlockSpec((1, p_len, SMALL), lambda i: (i, 0, 0)),
                  _full(pq.shape), _full(pc.shape), _full(aug1.shape)],
        out_specs=pl.BlockSpec((1, s_new, W_FOX), lambda i: (i, 0, 0)),
        out_shape=jax.ShapeDtypeStruct((b, s_new, W_FOX), F32),
        compiler_params=_params(1),
        name="fox_step",
    )(qa, ka, vb, k_past, v_past, logf_past_pad, pq, pc, aug1)


def _rwkv_body(chunk, rt_ref, at_ref, kh_ref, bh_ref, v_ref, lc_ref, s0_ref, y_ref, sout_ref, s_scr):
    j = pl.program_id(1)
    nj = pl.num_programs(1)
    n_chunks = rt_ref.shape[2] // chunk

    @pl.when(j == 0)
    def _():
        s_scr[...] = s0_ref[0]

    row = lax.broadcasted_iota(jnp.int32, (chunk, chunk), 0)
    col = lax.broadcasted_iota(jnp.int32, (chunk, chunk), 1)
    lower = col <= row
    strict = col < row
    eye = jnp.where(col == row, 1.0, 0.0)

    def one_chunk(ci, carry):
        sl = pl.ds(pl.multiple_of(ci * chunk, chunk), chunk)
        hs = range(N_RWKV)
        rt = [rt_ref[0, h, sl, :] for h in hs]
        at = [at_ref[0, h, sl, :] for h in hs]
        kh = [kh_ref[0, h, sl, :] for h in hs]
        bh = [bh_ref[0, h, sl, :] for h in hs]
        vv = [v_ref[0, h, sl, :] for h in hs]
        pc = [jnp.exp(lc_ref[0, h, pl.ds(ci * chunk + chunk - 1, 1), :]) for h in hs]
        a_ab = [jnp.where(strict, _dot3(_dot_nt, at[h], bh[h]), 0.0) for h in hs]
        a_ak = [jnp.where(strict, _dot3(_dot_nt, at[h], kh[h]), 0.0) for h in hs]
        a_rk = [jnp.where(lower, _dot1(_dot_nt, rt[h], kh[h]), 0.0) for h in hs]
        a_rb = [jnp.where(lower, _dot1(_dot_nt, rt[h], bh[h]), 0.0) for h in hs]
        akv = [_dot3(_dot, a_ak[h], vv[h]) for h in hs]
        yv = [_dot1(_dot, a_rk[h], vv[h]) for h in hs]
        inv = [eye + a_ab[h] for h in hs]
        pw = a_ab
        n = 1
        while 2 * n < chunk:
            pw = [_dot3(_dot, pw[h], pw[h]) for h in hs]
            inv = [inv[h] + _dot3(_dot, inv[h], pw[h]) for h in hs]
            n *= 2
        w_mat = [_dot3(_dot, inv[h], at[h]) for h in hs]
        u0 = [_dot3(_dot, inv[h], akv[h]) for h in hs]
        s_old = [s_scr[h] for h in hs]
        u = [_dot3(_dot_nt, w_mat[h], s_old[h]) + u0[h] for h in hs]
        ys = [_dot1(_dot_nt, rt[h], s_old[h]) + yv[h] + _dot1(_dot, a_rb[h], u[h]) for h in hs]
        for h in hs:
            s_scr[h] = (s_old[h] * pc[h] + _dot3(_dot_tn, vv[h], kh[h] * pc[h])
                        + _dot3(_dot_tn, u[h], bh[h] * pc[h]))
        y_ref[0, sl, :] = jnp.concatenate(ys, axis=1)
        return carry

    lax.fori_loop(0, n_chunks, one_chunk, 0)

    @pl.when(j == nj - 1)
    def _():
        sout_ref[0] = s_scr[...]


def _rwkv(rt, at, kh, bh, v, lc, s0, tc, chunk):
    b, _, t, _ = rt.shape
    hm = pl.BlockSpec((1, N_RWKV, tc, HEAD), lambda bi, j: (bi, 0, j, 0))
    st = pl.BlockSpec((1, N_RWKV, HEAD, HEAD), lambda bi, j: (bi, 0, 0, 0))
    return pl.pallas_call(
        functools.partial(_rwkv_body, chunk),
        grid=(b, t // tc),
        in_specs=[hm] * 6 + [st],
        out_specs=[pl.BlockSpec((1, tc, W_RWKV), lambda bi, j: (bi, j, 0)), st],
        out_shape=[jax.ShapeDtypeStruct((b, t, W_RWKV), F32), jax.ShapeDtypeStruct((b, N_RWKV, HEAD, HEAD), F32)],
        scratch_shapes=[pltpu.VMEM((N_RWKV, HEAD, HEAD), F32)],
        compiler_params=_params(2),
        name="rwkv",
    )(rt, at, kh, bh, v, lc, s0)


def _out_body(x_ref, of_ref, gf_ref, y_ref, gr_ref, bg_ref, om_ref, gnw_ref, gnb_ref, ones_ref,
              wf_ref, wr_ref, wm_ref, o_ref):
    ones384 = ones_ref[...]
    y = y_ref[0]
    mu = _segsum(y, ones384) * (1.0 / HEAD)
    d = y - mu
    var = _segsum(d * d, ones384) * (1.0 / HEAD)
    yn = d * lax.rsqrt(var + GN_EPS) * gnw_ref[...] + gnb_ref[...]
    o_r = yn * gr_ref[0] + bg_ref[0]
    acc = _dot((of_ref[0] * gf_ref[0]).astype(BF16), wf_ref[...])
    acc = acc + _dot(o_r.astype(BF16), wr_ref[...])
    acc = acc + _dot(om_ref[0].astype(BF16), wm_ref[...])
    o_ref[0] = x_ref[0] + acc


def _out(x, of, gf, y, gr, bg, om, gnw, gnb, ones384, wf, wr, wm, tm):
    b, t, _ = x.shape
    tok = lambda w: pl.BlockSpec((1, tm, w), lambda bi, i: (bi, i, 0))
    return pl.pallas_call(
        _out_body,
        grid=(b, t // tm),
        in_specs=[tok(D_MODEL), tok(W_FOX), tok(W_FOX), tok(W_RWKV), tok(W_RWKV), tok(W_RWKV), tok(W_MEM),
                  _full(gnw.shape), _full(gnb.shape), _full(ones384.shape),
                  _full(wf.shape), _full(wr.shape), _full(wm.shape)],
        out_specs=tok(D_MODEL),
        out_shape=jax.ShapeDtypeStruct((b, t, D_MODEL), F32),
        compiler_params=_params(2),
        name="out_proj",
    )(x, of, gf, y, gr, bg, om, gnw, gnb, ones384, wf, wr, wm)


def _padded_column_index():
    idx = np.full((N_PAD,), FOX_COLS + RWKV_COLS + MEM_COLS, np.int32)
    r0 = FOX_COLS
    m0 = FOX_COLS + RWKV_COLS
    idx[OFF_FQ:OFF_FQ + W_FOX] = np.arange(0, W_FOX)
    idx[OFF_FK:OFF_FK + W_FOX] = np.arange(W_FOX, 2 * W_FOX)
    idx[OFF_FV:OFF_FV + W_FOX] = np.arange(2 * W_FOX, 3 * W_FOX)
    idx[OFF_FG:OFF_FG + W_FOX] = np.arange(3 * W_FOX + N_FOX, 4 * W_FOX + N_FOX)
    idx[OFF_RR:OFF_RR + W_RWKV] = r0 + np.arange(0, W_RWKV)
    idx[OFF_RK:OFF_RK + W_RWKV] = r0 + np.arange(W_RWKV, 2 * W_RWKV)
    idx[OFF_RV:OFF_RV + W_RWKV] = r0 + np.arange(2 * W_RWKV, 3 * W_RWKV)
    idx[OFF_RG:OFF_RG + W_RWKV] = r0 + np.arange(3 * W_RWKV + 2 * LORA, 4 * W_RWKV + 2 * LORA)
    idx[OFF_MQ:OFF_MQ + W_MEM] = m0 + np.arange(0, W_MEM)
    idx[OFF_MG:OFF_MG + W_MEM] = m0 + np.arange(W_MEM, 2 * W_MEM)
    idx[OFF_SM:OFF_SM + N_FOX] = np.arange(3 * W_FOX, 3 * W_FOX + N_FOX)
    idx[OFF_SM + SM_WD:OFF_SM + SM_WD + LORA] = r0 + np.arange(3 * W_RWKV, 3 * W_RWKV + LORA)
    idx[OFF_SM + SM_AD:OFF_SM + SM_AD + LORA] = r0 + np.arange(3 * W_RWKV + LORA, 3 * W_RWKV + 2 * LORA)
    return idx


def _shift_row_index():
    idx = np.full((SHIFT_W,), RWKV_COLS, np.int32)
    idx[0:3 * W_RWKV] = np.arange(0, 3 * W_RWKV)
    idx[3 * W_RWKV:4 * W_RWKV] = np.arange(3 * W_RWKV + 2 * LORA, 4 * W_RWKV + 2 * LORA)
    idx[4 * W_RWKV + SM_WD:4 * W_RWKV + SM_WD + LORA] = np.arange(3 * W_RWKV, 3 * W_RWKV + LORA)
    idx[4 * W_RWKV + SM_AD:4 * W_RWKV + SM_AD + LORA] = np.arange(3 * W_RWKV + LORA, 3 * W_RWKV + 2 * LORA)
    return idx


def _shift_row_inverse():
    idx = np.zeros((RWKV_COLS,), np.int32)
    idx[0:3 * W_RWKV] = np.arange(0, 3 * W_RWKV)
    idx[3 * W_RWKV:3 * W_RWKV + LORA] = 4 * W_RWKV + SM_WD + np.arange(LORA)
    idx[3 * W_RWKV + LORA:3 * W_RWKV + 2 * LORA] = 4 * W_RWKV + SM_AD + np.arange(LORA)
    idx[3 * W_RWKV + 2 * LORA:] = 3 * W_RWKV + np.arange(W_RWKV)
    return idx


def _placement_constants():
    pq = np.zeros((W_FOX, N_FOX * AUG), np.float32)
    pc = np.zeros((6, SMALL, N_FOX * AUG), np.float32)
    aug1 = np.zeros((2, N_FOX * AUG), np.float32)
    for h in range(N_FOX):
        for d in range(HEAD):
            pq[HEAD * h + d, AUG * h + d] = 1.0
        for j in range(3):
            pc[j, h, AUG * h + HEAD + j] = 1.0
            pc[3 + j, h, AUG * h + HEAD + 3 + j] = 1.0
            aug1[0, AUG * h + HEAD + 3 + j] = 1.0
            aug1[1, AUG * h + HEAD + j] = 1.0
    return pq, pc, aug1


def _block_ones(width):
    h = np.arange(width) // HEAD
    return (h[:, None] == h[None, :]).astype(np.float32)


def _mem_block_diag(mk, mv):
    b = mk.shape[0]
    eye = jnp.eye(N_MEM, dtype=mk.dtype)
    kt = jnp.einsum('bmhd,hg->bhdgm', mk, eye).reshape(b, W_MEM, N_MEM * N_MEM_TOK)
    vb = jnp.einsum('bmhd,hg->bhmgd', mv, eye).reshape(b, N_MEM * N_MEM_TOK, W_MEM)
    return kt.astype(BF16), vb.astype(BF16)


def _layer(x, shift_prev, s0, mk, mv, fox_past, wts, consts, tm, tq, tc, chunk):
    (proj_wts, gnw, gnb, wf, wr, wm) = wts
    (ones384, pq, pc, aug1, shift_idx, shift_inv) = consts
    b, t, _ = x.shape
    shift0 = jnp.take(jnp.pad(shift_prev, ((0, 0), (0, 0), (0, 1))), shift_idx, axis=2)
    mkt, mvb = _mem_block_diag(mk, mv)
    (qa, ka, kn, v, logf, gf, rt, at, kh, bh, rv, lc, gr, bg, om, shift_out, *vt) = _proj(
        x, shift0, mkt, mvb, proj_wts, tm, chunk, with_vt=fox_past is None)
    if fox_past is None:
        of = _fox(qa, ka, vt[0], tq)
    else:
        k_past, v_past, logf_past = fox_past
        p_len = k_past.shape[1]
        lfp = jnp.pad(logf_past.astype(F32), ((0, 0), (0, 0), (0, SMALL - N_FOX)))
        of = _fox_step(qa, ka, v, k_past.reshape(b, p_len, W_FOX), v_past.reshape(b, p_len, W_FOX), lfp,
                       pq, pc, aug1)
    y_r, s_new = _rwkv(rt, at, kh, bh, rv, lc, s0, tc, chunk)
    y = _out(x, of, gf, y_r, gr, bg, om, gnw, gnb, ones384, wf, wr, wm, tm)
    shift_new = jnp.take(shift_out, shift_inv, axis=2)
    return (y, kn.reshape(b, t, N_FOX, HEAD), v.reshape(b, t, N_FOX, HEAD), logf, s_new, shift_new)


def kernel(x_prompt, x_sample, mem_prompt, cache_fox_k, cache_fox_v, cache_fox_logf, cache_mem_k, cache_mem_v, state_rwkv, state_rwkv_shift, norm_g, w_in, fox_q_g, fox_k_g, fox_b_f, rwkv_mu, rwkv_w0, rwkv_w_up, rwkv_a0, rwkv_a_up, rwkv_k_k, rwkv_k_a, rwkv_r_k, rwkv_gn_w, rwkv_gn_b, mem_norm_g, w_mem_kv, mem_q_g, mem_k_g, w_out):
    depth = w_in.shape[0]
    bp = x_prompt.shape[0]
    col_idx = _padded_column_index()
    shift_idx = _shift_row_index()
    shift_inv = _shift_row_inverse()
    pq_np, pc_np, aug1_np = _placement_constants()
    ones384 = jnp.asarray(_block_ones(W_FOX), BF16)
    ones256 = jnp.asarray(_block_ones(W_MEM), BF16)
    pq = jnp.asarray(pq_np, BF16)
    pc = jnp.asarray(pc_np, BF16)
    aug1 = jnp.asarray(aug1_np, F32)
    consts = (ones384, pq, pc, aug1, shift_idx, shift_inv)

    yp, ys = x_prompt, x_sample
    outs = [[] for _ in range(12)]
    for l in range(depth):
        row = lambda a: a[l].reshape(1, -1).astype(F32)
        tile = lambda a, n: jnp.tile(a[l].reshape(1, -1).astype(F32), (1, n))
        w_pad = jnp.take(jnp.pad(w_in[l], ((0, 0), (0, 1))), col_idx, axis=1).astype(BF16)
        mu_pad = jnp.take(jnp.pad(row(rwkv_mu), ((0, 0), (0, 1))), shift_idx, axis=1)
        bf_pad = jnp.pad(row(fox_b_f), ((0, 0), (0, SMALL - N_FOX)))
        wup_pad = jnp.pad(rwkv_w_up[l].astype(F32), ((SM_WD, SMALL - SM_WD - LORA), (0, 0)))
        aup_pad = jnp.pad(rwkv_a_up[l].astype(F32), ((SM_AD, SMALL - SM_AD - LORA), (0, 0)))
        proj_wts = (row(norm_g), w_pad, tile(fox_q_g, N_FOX), tile(fox_k_g, N_FOX), bf_pad, mu_pad,
                    row(rwkv_w0), row(rwkv_a0), wup_pad, aup_pad, row(rwkv_k_k), row(rwkv_k_a), row(rwkv_r_k),
                    tile(mem_q_g, N_MEM), ones384, ones256, pq, pc, aug1)
        wo = w_out[l].astype(BF16)
        wts = (proj_wts, row(rwkv_gn_w), row(rwkv_gn_b), wo[:W_FOX], wo[W_FOX:W_FOX + W_RWKV], wo[W_FOX + W_RWKV:])

        mk2, mv2 = _mem_kv(mem_prompt, row(mem_norm_g), w_mem_kv[l].astype(BF16), tile(mem_k_g, N_MEM), ones256)
        mk = mk2.reshape(bp, N_MEM_TOK, N_MEM, HEAD)
        mv = mv2.reshape(bp, N_MEM_TOK, N_MEM, HEAD)
        shift_zero = jnp.zeros((bp, 1, RWKV_COLS), F32)
        s_zero = jnp.zeros((bp, N_RWKV, HEAD, HEAD), F32)
        yp, k, v, lf, s_new, sh_new = _layer(yp, shift_zero, s_zero, mk, mv, None, wts, consts,
                                             tm=PROMPT_TM, tq=PROMPT_TQ, tc=PROMPT_TC, chunk=RWKV_CHUNK)
        for lst, val in zip(outs[:7], (k, v, lf, mk, mv, s_new, sh_new)):
            lst.append(val)
        s_len = ys.shape[1]
        ys, k, v, lf, s_new, sh_new = _layer(
            ys, state_rwkv_shift[l], state_rwkv[l].astype(F32), cache_mem_k[l], cache_mem_v[l],
            (cache_fox_k[l], cache_fox_v[l], cache_fox_logf[l]), wts, consts,
            tm=s_len, tq=None, tc=s_len, chunk=s_len)
        for lst, val in zip(outs[7:], (k, v, lf, s_new, sh_new)):
            lst.append(val)
    return (yp, ys) + tuple(jnp.stack(o) for o in outs)
```

```python
import functools

import numpy as np
import jax
import jax.numpy as jnp
from jax import lax
from jax.experimental import pallas as pl
from jax.experimental.pallas import tpu as pltpu

F32 = jnp.float32
BF16 = jnp.bfloat16

D_MODEL = 1024
HEAD = 64
N_FOX = 6
N_RWKV = 6
N_MEM = 4
W_FOX = N_FOX * HEAD
W_RWKV = N_RWKV * HEAD
W_MEM = N_MEM * HEAD
N_MEM_TOK = 256
LORA = 32
NORM_EPS = 1e-6
GN_EPS = 64e-5
FOX_COLS = 4 * W_FOX + N_FOX
RWKV_COLS = 4 * W_RWKV + 2 * LORA
MEM_COLS = 2 * W_MEM

LANE = 128
SMALL = LANE
SM_WD = 32
SM_AD = 64
OFF_FQ, OFF_FK, OFF_FV, OFF_FG = 0, 384, 768, 1152
OFF_RR, OFF_RK, OFF_RV, OFF_RG = 1536, 1920, 2304, 2688
OFF_MQ, OFF_MG = 3072, 3328
OFF_SM = 3584
N_PAD = OFF_SM + SMALL
SHIFT_W = 4 * W_RWKV + SMALL
AUG = LANE
RWKV_CHUNK = 64
VMEM_LIMIT = 56 * 1024 * 1024
PROMPT_TM = 256
PROMPT_TQ = 512
PROMPT_TC = 256


def _dot(a, b):
    return jnp.dot(a, b, preferred_element_type=F32)


def _dot_nt(a, b):
    return lax.dot_general(a, b, (((1,), (1,)), ((), ())), preferred_element_type=F32)


def _dot_tn(a, b):
    return lax.dot_general(a, b, (((0,), (0,)), ((), ())), preferred_element_type=F32)


def _split2(x):
    hi = x.astype(BF16)
    lo = (x - hi.astype(F32)).astype(BF16)
    return hi, lo


def _split3(x):
    hi = x.astype(BF16)
    r1 = x - hi.astype(F32)
    mid = r1.astype(BF16)
    lo = (r1 - mid.astype(F32)).astype(BF16)
    return hi, mid, lo


def _dot3(fn, a, b):
    ah, al = _split2(a)
    bh, bl = _split2(b)
    return fn(ah, bh) + (fn(ah, bl) + fn(al, bh))


def _dot1(fn, a, b):
    return fn(a.astype(BF16), b.astype(BF16))


def _exact_lhs_dot(m_bf16, x, parts):
    pieces = _split3(x) if parts == 3 else _split2(x)
    acc = _dot(m_bf16, pieces[0])
    for p in pieces[1:]:
        acc = acc + _dot(m_bf16, p)
    return acc


def _segsum(x, ones_bd):
    hi, lo = _split2(x)
    return _dot(hi, ones_bd) + _dot(lo, ones_bd)


def _head_rms(t, gain, ones_bd):
    msq = _segsum(t * t, ones_bd) * (1.0 / HEAD)
    return t * lax.rsqrt(msq + NORM_EPS) * gain


def _silu(x):
    return x * jax.nn.sigmoid(x)


def _softplus(z):
    return jnp.maximum(z, 0.0) + jnp.log1p(jnp.exp(-jnp.abs(z)))


def _full(shape):
    n = len(shape)
    return pl.BlockSpec(shape, lambda *_: (0,) * n)


def _params(n_axes):
    return pltpu.CompilerParams(dimension_semantics=("arbitrary",) * n_axes, vmem_limit_bytes=VMEM_LIMIT)


def _mem_kv_body(mem_ref, g_ref, w_ref, kg_ref, ones_ref, k_ref, v_ref):
    x = mem_ref[0]
    ms = jnp.mean(x * x, axis=-1, keepdims=True)
    xn = (x * lax.rsqrt(ms + NORM_EPS) * g_ref[...]).astype(BF16)
    kv = _dot(xn, w_ref[...])
    k_ref[0] = _head_rms(kv[:, :W_MEM], kg_ref[...], ones_ref[...])
    v_ref[0] = kv[:, W_MEM:]


def _mem_kv(mem, g, w_bf16, kg4, ones256):
    b = mem.shape[0]
    blk = pl.BlockSpec((1, N_MEM_TOK, W_MEM), lambda i: (i, 0, 0))
    return pl.pallas_call(
        _mem_kv_body,
        grid=(b,),
        in_specs=[pl.BlockSpec((1, N_MEM_TOK, D_MODEL), lambda i: (i, 0, 0)),
                  _full((1, D_MODEL)), _full((D_MODEL, 2 * W_MEM)), _full((1, W_MEM)), _full((W_MEM, W_MEM))],
        out_specs=[blk, blk],
        out_shape=[jax.ShapeDtypeStruct((b, N_MEM_TOK, W_MEM), F32)] * 2,
        compiler_params=_params(1),
        name="mem_kv",
    )(mem, g, w_bf16, kg4, ones256)


def _proj_body(chunk, with_vt,
               x_ref, shift0_ref, mkt_ref, mvb_ref, ng_ref, w_ref,
               fqg_ref, fkg_ref, bf_ref, mu_ref, w0_ref, a0_ref, wup_ref, aup_ref,
               kk_ref, ka_ref, rk_ref, mqg_ref, ones384_ref, ones256_ref, pq_ref, pc_ref, aug1_ref,
               qa_ref, kaug_ref, kn_ref, v_ref, logf_ref, gf_ref,
               rt_ref, at_ref, kh_ref, bh_ref, rv_ref, lc_ref, gr_ref, bg_ref, om_ref, shift_out_ref,
               *vt_and_scratch):
    vt_ref = vt_and_scratch[0] if with_vt else None
    carry_shift, carry_c = vt_and_scratch[-2:]
    i = pl.program_id(1)
    tm = x_ref.shape[1]

    @pl.when(i == 0)
    def _():
        carry_shift[...] = shift0_ref[0]
        carry_c[...] = jnp.zeros_like(carry_c)

    x = x_ref[0]
    ms = jnp.mean(x * x, axis=-1, keepdims=True)
    xn = (x * lax.rsqrt(ms + NORM_EPS) * ng_ref[...]).astype(BF16)

    def proj(off, width):
        return _dot(xn, w_ref[:, off:off + width])

    ones384 = ones384_ref[...]
    row = lax.broadcasted_iota(jnp.int32, (tm, tm), 0)
    col = lax.broadcasted_iota(jnp.int32, (tm, tm), 1)
    tri = jnp.where(col <= row, 1.0, 0.0).astype(BF16)
    tri_chunk = jnp.where((col <= row) & ((row ^ col) < chunk), 1.0, 0.0).astype(BF16)

    hs = proj(OFF_SM, SMALL)
    qn = _head_rms(proj(OFF_FQ, W_FOX), fqg_ref[...], ones384)
    kn = _head_rms(proj(OFF_FK, W_FOX), fkg_ref[...], ones384)
    hv = proj(OFF_FV, W_FOX)
    hg = proj(OFF_FG, W_FOX)
    kn_ref[0] = kn
    v_ref[0] = hv
    if with_vt:
        vt_ref[0, 0] = hv.T.astype(BF16)
    gf_ref[0] = _silu(hg)
    f = hs + bf_ref[...]
    lane = lax.broadcasted_iota(jnp.int32, (tm, SMALL), 1)
    logf = jnp.where(lane < N_FOX, jnp.minimum(f, 0.0) - jnp.log1p(jnp.exp(-jnp.abs(f))), 0.0)
    logf_ref[0] = logf[:, 0:N_FOX]
    c = _exact_lhs_dot(tri, logf, 3) + carry_c[...]
    carry_c[...] = c[tm - 1:tm, :]
    chi, cmid, clo = _split3(c)
    qs = (qn * (HEAD ** -0.5)).astype(BF16)
    qa = (_dot(qs, pq_ref[...]) + _dot(chi, pc_ref[0]) + _dot(cmid, pc_ref[1]) + _dot(clo, pc_ref[2])
          + aug1_ref[0:1, :])
    ka = (_dot(kn.astype(BF16), pq_ref[...]) - (_dot(chi, pc_ref[3]) + _dot(cmid, pc_ref[4]) + _dot(clo, pc_ref[5]))
          + aug1_ref[1:2, :])
    for h in range(N_FOX):
        qa_ref[0, h] = qa[:, AUG * h:AUG * (h + 1)].astype(BF16)
        kaug_ref[0, h] = ka[:, AUG * h:AUG * (h + 1)].astype(BF16)

    qm = _head_rms(proj(OFF_MQ, W_MEM), mqg_ref[...], ones256_ref[...])
    s = _dot((qm * (HEAD ** -0.5)).astype(BF16), mkt_ref[0])
    ps = []
    for h in range(N_MEM):
        sh = s[:, N_MEM_TOK * h:N_MEM_TOK * (h + 1)]
        e = jnp.exp(sh - jnp.max(sh, axis=1, keepdims=True))
        ps.append((e / jnp.sum(e, axis=1, keepdims=True)).astype(BF16))
    om = _dot(jnp.concatenate(ps, axis=1), mvb_ref[0])
    om_ref[0] = om * _silu(proj(OFF_MG, W_MEM))

    first_row = lax.broadcasted_iota(jnp.int32, (tm, 1), 0) == 0

    def tshift(cols, off, width):
        prev = jnp.where(first_row, carry_shift[:, off:off + width], pltpu.roll(cols, 1, 0))
        carry_shift[:, off:off + width] = cols[tm - 1:tm, :]
        return cols + (prev - cols) * mu_ref[:, off:off + width]

    r = tshift(proj(OFF_RR, W_RWKV), 0, W_RWKV)
    k = tshift(proj(OFF_RK, W_RWKV), W_RWKV, W_RWKV)
    v = tshift(proj(OFF_RV, W_RWKV), 2 * W_RWKV, W_RWKV)
    g = tshift(proj(OFF_RG, W_RWKV), 3 * W_RWKV, W_RWKV)
    sm = tshift(hs, 4 * W_RWKV, SMALL)
    shift_out_ref[0] = carry_shift[...]

    w_lin = w0_ref[...] + _dot3(_dot, jnp.tanh(sm), wup_ref[...])
    a = jax.nn.sigmoid(a0_ref[...] + _dot3(_dot, sm, aup_ref[...]))
    lw = -jnp.exp(-_softplus(-w_lin) - 0.5)
    kk = k * kk_ref[...]
    kk = kk * lax.rsqrt(jnp.maximum(_segsum(kk * kk, ones384), 1e-24))
    kt = k * (1.0 + (a - 1.0) * ka_ref[...])
    gr = _silu(g)
    gr_ref[0] = gr
    bg_ref[0] = _segsum(r * kt * rk_ref[...], ones384) * v * gr
    lc = _exact_lhs_dot(tri_chunk, lw, 3)
    e_neg = jnp.exp(-lc)
    rt = r * jnp.exp(lc)
    at = -kk * jnp.exp(lc - lw)
    kh = kt * e_neg
    bh = kk * a * e_neg
    rt_ref[0] = rt.astype(BF16)
    at_ref[0] = at.astype(BF16)
    kh_ref[0] = kh.astype(BF16)
    bh_ref[0] = bh.astype(BF16)
    rv_ref[0] = v.astype(BF16)
    lc_ref[0] = lc


def _proj(x, shift0, mkt, mvb, wts, tm, chunk, with_vt):
    b, t, _ = x.shape
    grid = (b, t // tm)
    tok = lambda w: pl.BlockSpec((1, tm, w), lambda bi, i: (bi, i, 0))
    hm = lambda nh, w: pl.BlockSpec((1, nh, tm, w), lambda bi, i: (bi, 0, i, 0))
    per_b = lambda s1, s2: pl.BlockSpec((1, s1, s2), lambda bi, i: (bi, 0, 0))
    w_specs = [_full(a.shape) for a in wts]
    out_shape = [
        jax.ShapeDtypeStruct((b, N_FOX, t, AUG), BF16), jax.ShapeDtypeStruct((b, N_FOX, t, AUG), BF16),
        jax.ShapeDtypeStruct((b, t, W_FOX), F32),
        jax.ShapeDtypeStruct((b, t, W_FOX), F32), jax.ShapeDtypeStruct((b, t, N_FOX), F32),
        jax.ShapeDtypeStruct((b, t, W_FOX), F32),
    ] + [jax.ShapeDtypeStruct((b, t, W_RWKV), BF16)] * 5 + [jax.ShapeDtypeStruct((b, t, W_RWKV), F32)] + [
        jax.ShapeDtypeStruct((b, t, W_RWKV), F32), jax.ShapeDtypeStruct((b, t, W_RWKV), F32),
        jax.ShapeDtypeStruct((b, t, W_MEM), F32), jax.ShapeDtypeStruct((b, 1, SHIFT_W), F32),
    ]
    out_specs = [hm(N_FOX, AUG), hm(N_FOX, AUG), tok(W_FOX), tok(W_FOX), tok(N_FOX), tok(W_FOX)] \
        + [tok(W_RWKV)] * 6 + [tok(W_RWKV), tok(W_RWKV), tok(W_MEM), per_b(1, SHIFT_W)]
    if with_vt:
        out_shape.append(jax.ShapeDtypeStruct((b, t // tm, W_FOX, tm), BF16))
        out_specs.append(pl.BlockSpec((1, 1, W_FOX, tm), lambda bi, i: (bi, i, 0, 0)))
    return pl.pallas_call(
        functools.partial(_proj_body, chunk, with_vt),
        grid=grid,
        in_specs=[tok(D_MODEL), per_b(1, SHIFT_W), per_b(W_MEM, N_MEM * N_MEM_TOK),
                  per_b(N_MEM * N_MEM_TOK, W_MEM)] + w_specs,
        out_specs=out_specs,
        out_shape=out_shape,
        scratch_shapes=[pltpu.VMEM((1, SHIFT_W), F32), pltpu.VMEM((1, SMALL), F32)],
        compiler_params=_params(2),
        name="proj",
    )(x, shift0, mkt, mvb, *wts)


def _fox_body(qa_ref, ka_ref, vt_ref, o_ref, m_scr, l_scr, acc_scr):
    qi = pl.program_id(2)
    tq = qa_ref.shape[2]
    tk = vt_ref.shape[3]
    m_scr[...] = jnp.full_like(m_scr, -1e30)
    l_scr[...] = jnp.zeros_like(l_scr)
    acc_scr[...] = jnp.zeros_like(acc_scr)

    def tile(ki, masked):
        rows = pl.ds(pl.multiple_of(ki * tk, tk), tk)
        s = [_dot_nt(ka_ref[0, hh, rows, :], qa_ref[0, hh]) for hh in range(2)]
        if masked:
            kpos = lax.broadcasted_iota(jnp.int32, (tk, tq), 0) + ki * tk
            qpos = lax.broadcasted_iota(jnp.int32, (tk, tq), 1) + qi * tq
            s = [jnp.where(kpos <= qpos, sh, -1e30) for sh in s]
        m_prev = [m_scr[hh] for hh in range(2)]
        m_new = [jnp.maximum(m_prev[hh], jnp.max(s[hh], axis=0, keepdims=True)) for hh in range(2)]
        p = [jnp.exp(s[hh] - m_new[hh]) for hh in range(2)]
        vt = vt_ref[0, ki]
        pv = [_dot(vt, p[hh].astype(BF16)) for hh in range(2)]
        for hh in range(2):
            alpha = jnp.exp(m_prev[hh] - m_new[hh])
            l_scr[hh] = alpha * l_scr[hh] + jnp.sum(p[hh], axis=0, keepdims=True)
            acc_scr[hh] = alpha * acc_scr[hh] + pv[hh]
            m_scr[hh] = m_new[hh]

    n_full = qi * (tq // tk)

    def full_tile(ki, carry):
        tile(ki, False)
        return carry

    lax.fori_loop(0, n_full, full_tile, 0)
    for d in range(tq // tk):
        tile(n_full + d, True)
    o_t = jnp.concatenate([acc_scr[0][0:HEAD] / l_scr[0], acc_scr[1][HEAD:2 * HEAD] / l_scr[1]], axis=0)
    o_ref[0] = o_t.T


def _fox(qa, ka, vt, tq):
    b, _, t, _ = qa.shape
    nk, tk = vt.shape[1], vt.shape[3]
    return pl.pallas_call(
        _fox_body,
        grid=(b, N_FOX // 2, t // tq),
        in_specs=[pl.BlockSpec((1, 2, tq, AUG), lambda bi, p, qi: (bi, p, qi, 0)),
                  pl.BlockSpec((1, 2, t, AUG), lambda bi, p, qi: (bi, p, 0, 0)),
                  pl.BlockSpec((1, nk, 2 * HEAD, tk), lambda bi, p, qi: (bi, 0, p, 0))],
        out_specs=pl.BlockSpec((1, tq, 2 * HEAD), lambda bi, p, qi: (bi, qi, p)),
        out_shape=jax.ShapeDtypeStruct((b, t, W_FOX), F32),
        scratch_shapes=[pltpu.VMEM((2, 1, tq), F32), pltpu.VMEM((2, 1, tq), F32), pltpu.VMEM((2, 2 * HEAD, tq), F32)],
        compiler_params=_params(3),
        name="fox",
    )(qa, ka, vt)


def _fox_step_body(qa_ref, ka_ref, vn_ref, kp_ref, vp_ref, lfp_ref, pq_ref, pc_ref, aug1_ref, o_ref):
    s_new = qa_ref.shape[2]
    p_len = kp_ref.shape[1]
    row = lax.broadcasted_iota(jnp.int32, (p_len, p_len), 0)
    col = lax.broadcasted_iota(jnp.int32, (p_len, p_len), 1)
    upper = jnp.where(col > row, 1.0, 0.0).astype(BF16)
    suf = _exact_lhs_dot(upper, lfp_ref[0], 3)
    shi, smid, slo = _split3(suf)
    kpa = (_dot(kp_ref[0].astype(BF16), pq_ref[...])
           + (_dot(shi, pc_ref[3]) + _dot(smid, pc_ref[4]) + _dot(slo, pc_ref[5])) + aug1_ref[1:2, :])
    vp = vp_ref[0]
    vn = vn_ref[0]
    qrow = lax.broadcasted_iota(jnp.int32, (s_new, s_new), 0)
    kcol = lax.broadcasted_iota(jnp.int32, (s_new, s_new), 1)
    outs = []
    for h in range(N_FOX):
        q = qa_ref[0, h]
        sp = _dot_nt(q, kpa[:, AUG * h:AUG * (h + 1)].astype(BF16))
        sn = jnp.where(kcol <= qrow, _dot_nt(q, ka_ref[0, h]), -1e30)
        m = jnp.maximum(jnp.max(sp, axis=1, keepdims=True), jnp.max(sn, axis=1, keepdims=True))
        pp = jnp.exp(sp - m)
        pn = jnp.exp(sn - m)
        l = jnp.sum(pp, axis=1, keepdims=True) + jnp.sum(pn, axis=1, keepdims=True)
        sl = slice(HEAD * h, HEAD * (h + 1))
        o = (_dot((pp / l).astype(BF16), vp[:, sl].astype(BF16))
             + _dot((pn / l).astype(BF16), vn[:, sl].astype(BF16)))
        outs.append(o)
    o_ref[0] = jnp.concatenate(outs, axis=1)


def _fox_step(qa, ka, vb, k_past, v_past, logf_past_pad, pq, pc, aug1):
    b, _, s_new, _ = qa.shape
    p_len = k_past.shape[1]
    return pl.pallas_call(
        _fox_step_body,
        grid=(b,),
        in_specs=[pl.BlockSpec((1, N_FOX, s_new, AUG), lambda i: (i, 0, 0, 0)),
                  pl.BlockSpec((1, N_FOX, s_new, AUG), lambda i: (i, 0, 0, 0)),
                  pl.BlockSpec((1, s_new, W_FOX), lambda i: (i, 0, 0)),
                  pl.BlockSpec((1, p_len, W_FOX), lambda i: (i, 0, 0)),
                  pl.BlockSpec((1, p_len, W_FOX), lambda i: (i, 0, 0)),
                  pl.BlockSpec((1, p_len, SMALL), lambda i: (i, 0, 0)),
                  _full(pq.shape), _full(pc.shape), _full(aug1.shape)],
        out_specs=pl.BlockSpec((1, s_new, W_FOX), lambda i: (i, 0, 0)),
        out_shape=jax.ShapeDtypeStruct((b, s_new, W_FOX), F32),
        compiler_params=_params(1),
        name="fox_step",
    )(qa, ka, vb, k_past, v_past, logf_past_pad, pq, pc, aug1)


def _rwkv_body(chunk, rt_ref, at_ref, kh_ref, bh_ref, v_ref, lc_ref, s0_ref, y_ref, sout_ref, s_scr):
    j = pl.program_id(1)
    nj = pl.num_programs(1)
    c = chunk
    c2 = 2 * chunk
    n_chunks = rt_ref.shape[1] // c
    n_pairs = N_RWKV // 2
    pair_w = 2 * HEAD

    @pl.when(j == 0)
    def _():
        z = jnp.zeros((HEAD, HEAD), F32)
        for p in range(n_pairs):
            s_scr[p] = jnp.concatenate([jnp.concatenate([s0_ref[0, 2 * p], z], axis=1),
                                        jnp.concatenate([z, s0_ref[0, 2 * p + 1]], axis=1)], axis=0)

    lane_lo = lax.broadcasted_iota(jnp.int32, (c, pair_w), 1) < HEAD
    row = lax.broadcasted_iota(jnp.int32, (c2, c2), 0)
    col = lax.broadcasted_iota(jnp.int32, (c2, c2), 1)
    same = (row >= c) == (col >= c)
    strict = same & (col < row)
    lower = same & (col <= row)
    eye = jnp.where(col == row, 1.0, 0.0)
    srow = lax.broadcasted_iota(jnp.int32, (pair_w, pair_w), 0)
    scol = lax.broadcasted_iota(jnp.int32, (pair_w, pair_w), 1)
    same_state = (srow >= HEAD) == (scol >= HEAD)

    def stack(x):
        zero = jnp.zeros_like(x)
        return jnp.concatenate([jnp.where(lane_lo, x, zero), jnp.where(lane_lo, zero, x)], axis=0)

    def fold(a):
        return a[:c] + a[c:]

    units = [(ci, p) for ci in range(n_chunks) for p in range(n_pairs)]

    def tile(ref, u):
        ci, p = u
        return ref[0, ci * c:(ci + 1) * c, pair_w * p:pair_w * (p + 1)]

    rt = [tile(rt_ref, u) for u in units]
    at2 = [stack(tile(at_ref, u)) for u in units]
    kh = [tile(kh_ref, u) for u in units]
    bh = [tile(bh_ref, u) for u in units]
    v2 = [stack(tile(v_ref, u)) for u in units]
    g = [_dot_nt(jnp.concatenate([at2[i], stack(rt[i])], axis=0),
                 jnp.concatenate([kh[i], kh[i], bh[i], bh[i]], axis=0)) for i in range(len(units))]
    a_ak = [jnp.where(strict, x[:c2, :c2], 0.0) for x in g]
    a_ab = [jnp.where(strict, x[:c2, c2:], 0.0) for x in g]
    a_rk = [fold(jnp.where(lower, x[c2:, :c2], 0.0)) for x in g]
    a_rb = [fold(jnp.where(lower, x[c2:, c2:], 0.0)) for x in g]
    akv2 = [_dot1(_dot, a_ak[i], v2[i]) for i in range(len(units))]
    yv = [_dot1(_dot, a_rk[i], v2[i]) for i in range(len(units))]
    inv = [eye + x for x in a_ab]
    pw = a_ab
    n = 1
    while 2 * n < c:
        pw = [_dot1(_dot, x, x) for x in pw]
        inv = [inv[i] + _dot1(_dot, inv[i], pw[i]) for i in range(len(units))]
        n *= 2
    tw = [_dot(inv[i].astype(BF16), jnp.concatenate([at2[i], akv2[i].astype(BF16)], axis=1))
          for i in range(len(units))]
    w_mat = [fold(x[:, :pair_w]) for x in tw]
    u0 = [fold(x[:, pair_w:]) for x in tw]

    for ci in range(n_chunks):
        idx = [ci * n_pairs + p for p in range(n_pairs)]
        s_old = [s_scr[p] for p in range(n_pairs)]
        pc = [jnp.exp(lc_ref[0, (ci + 1) * c - 1:(ci + 1) * c, pair_w * p:pair_w * (p + 1)]) for p in range(n_pairs)]
        ws = [_dot_nt(jnp.concatenate([w_mat[i].astype(BF16), rt[i]], axis=0), s_old[p].astype(BF16))
              for p, i in enumerate(idx)]
        u = [ws[p][:c] + u0[i] for p, i in enumerate(idx)]
        y = [ws[p][c:] + yv[i] + _dot1(_dot, a_rb[i], stack(u[p])) for p, i in enumerate(idx)]
        upd = [_dot_tn(jnp.concatenate([tile(v_ref, units[i]), u[p].astype(BF16)], axis=0),
                       jnp.concatenate([(kh[i] * pc[p]).astype(BF16), (bh[i] * pc[p]).astype(BF16)], axis=0))
               for p, i in enumerate(idx)]
        for p in range(n_pairs):
            s_scr[p] = s_old[p] * pc[p] + jnp.where(same_state, upd[p], 0.0)
            y_ref[0, ci * c:(ci + 1) * c, pair_w * p:pair_w * (p + 1)] = y[p]

    @pl.when(j == nj - 1)
    def _():
        for p in range(n_pairs):
            s = s_scr[p]
            sout_ref[0, 2 * p] = s[:HEAD, :HEAD]
            sout_ref[0, 2 * p + 1] = s[HEAD:, HEAD:]


def _rwkv(rt, at, kh, bh, v, lc, s0, tc, chunk):
    b, t, _ = rt.shape
    tok = pl.BlockSpec((1, tc, W_RWKV), lambda bi, j: (bi, j, 0))
    st = pl.BlockSpec((1, N_RWKV, HEAD, HEAD), lambda bi, j: (bi, 0, 0, 0))
    return pl.pallas_call(
        functools.partial(_rwkv_body, chunk),
        grid=(b, t // tc),
        in_specs=[tok] * 6 + [st],
        out_specs=[tok, st],
        out_shape=[jax.ShapeDtypeStruct((b, t, W_RWKV), F32), jax.ShapeDtypeStruct((b, N_RWKV, HEAD, HEAD), F32)],
        scratch_shapes=[pltpu.VMEM((N_RWKV // 2, 2 * HEAD, 2 * HEAD), F32)],
        compiler_params=_params(2),
        name="rwkv",
    )(rt, at, kh, bh, v, lc, s0)


def _out_body(x_ref, of_ref, gf_ref, y_ref, gr_ref, bg_ref, om_ref, gnw_ref, gnb_ref, ones_ref,
              wf_ref, wr_ref, wm_ref, o_ref):
    ones384 = ones_ref[...]
    y = y_ref[0]
    mu = _segsum(y, ones384) * (1.0 / HEAD)
    d = y - mu
    var = _segsum(d * d, ones384) * (1.0 / HEAD)
    yn = d * lax.rsqrt(var + GN_EPS) * gnw_ref[...] + gnb_ref[...]
    o_r = yn * gr_ref[0] + bg_ref[0]
    acc = _dot((of_ref[0] * gf_ref[0]).astype(BF16), wf_ref[...])
    acc = acc + _dot(o_r.astype(BF16), wr_ref[...])
    acc = acc + _dot(om_ref[0].astype(BF16), wm_ref[...])
    o_ref[0] = x_ref[0] + acc


def _out(x, of, gf, y, gr, bg, om, gnw, gnb, ones384, wf, wr, wm, tm):
    b, t, _ = x.shape
    tok = lambda w: pl.BlockSpec((1, tm, w), lambda bi, i: (bi, i, 0))
    return pl.pallas_call(
        _out_body,
        grid=(b, t // tm),
        in_specs=[tok(D_MODEL), tok(W_FOX), tok(W_FOX), tok(W_RWKV), tok(W_RWKV), tok(W_RWKV), tok(W_MEM),
                  _full(gnw.shape), _full(gnb.shape), _full(ones384.shape),
                  _full(wf.shape), _full(wr.shape), _full(wm.shape)],
        out_specs=tok(D_MODEL),
        out_shape=jax.ShapeDtypeStruct((b, t, D_MODEL), F32),
        compiler_params=_params(2),
        name="out_proj",
    )(x, of, gf, y, gr, bg, om, gnw, gnb, ones384, wf, wr, wm)


def _padded_column_index():
    idx = np.full((N_PAD,), FOX_COLS + RWKV_COLS + MEM_COLS, np.int32)
    r0 = FOX_COLS
    m0 = FOX_COLS + RWKV_COLS
    idx[OFF_FQ:OFF_FQ + W_FOX] = np.arange(0, W_FOX)
    idx[OFF_FK:OFF_FK + W_FOX] = np.arange(W_FOX, 2 * W_FOX)
    idx[OFF_FV:OFF_FV + W_FOX] = np.arange(2 * W_FOX, 3 * W_FOX)
    idx[OFF_FG:OFF_FG + W_FOX] = np.arange(3 * W_FOX + N_FOX, 4 * W_FOX + N_FOX)
    idx[OFF_RR:OFF_RR + W_RWKV] = r0 + np.arange(0, W_RWKV)
    idx[OFF_RK:OFF_RK + W_RWKV] = r0 + np.arange(W_RWKV, 2 * W_RWKV)
    idx[OFF_RV:OFF_RV + W_RWKV] = r0 + np.arange(2 * W_RWKV, 3 * W_RWKV)
    idx[OFF_RG:OFF_RG + W_RWKV] = r0 + np.arange(3 * W_RWKV + 2 * LORA, 4 * W_RWKV + 2 * LORA)
    idx[OFF_MQ:OFF_MQ + W_MEM] = m0 + np.arange(0, W_MEM)
    idx[OFF_MG:OFF_MG + W_MEM] = m0 + np.arange(W_MEM, 2 * W_MEM)
    idx[OFF_SM:OFF_SM + N_FOX] = np.arange(3 * W_FOX, 3 * W_FOX + N_FOX)
    idx[OFF_SM + SM_WD:OFF_SM + SM_WD + LORA] = r0 + np.arange(3 * W_RWKV, 3 * W_RWKV + LORA)
    idx[OFF_SM + SM_AD:OFF_SM + SM_AD + LORA] = r0 + np.arange(3 * W_RWKV + LORA, 3 * W_RWKV + 2 * LORA)
    return idx


def _shift_row_index():
    idx = np.full((SHIFT_W,), RWKV_COLS, np.int32)
    idx[0:3 * W_RWKV] = np.arange(0, 3 * W_RWKV)
    idx[3 * W_RWKV:4 * W_RWKV] = np.arange(3 * W_RWKV + 2 * LORA, 4 * W_RWKV + 2 * LORA)
    idx[4 * W_RWKV + SM_WD:4 * W_RWKV + SM_WD + LORA] = np.arange(3 * W_RWKV, 3 * W_RWKV + LORA)
    idx[4 * W_RWKV + SM_AD:4 * W_RWKV + SM_AD + LORA] = np.arange(3 * W_RWKV + LORA, 3 * W_RWKV + 2 * LORA)
    return idx


def _shift_row_inverse():
    idx = np.zeros((RWKV_COLS,), np.int32)
    idx[0:3 * W_RWKV] = np.arange(0, 3 * W_RWKV)
    idx[3 * W_RWKV:3 * W_RWKV + LORA] = 4 * W_RWKV + SM_WD + np.arange(LORA)
    idx[3 * W_RWKV + LORA:3 * W_RWKV + 2 * LORA] = 4 * W_RWKV + SM_AD + np.arange(LORA)
    idx[3 * W_RWKV + 2 * LORA:] = 3 * W_RWKV + np.arange(W_RWKV)
    return idx


def _placement_constants():
    pq = np.zeros((W_FOX, N_FOX * AUG), np.float32)
    pc = np.zeros((6, SMALL, N_FOX * AUG), np.float32)
    aug1 = np.zeros((2, N_FOX * AUG), np.float32)
    for h in range(N_FOX):
        for d in range(HEAD):
            pq[HEAD * h + d, AUG * h + d] = 1.0
        for j in range(3):
            pc[j, h, AUG * h + HEAD + j] = 1.0
            pc[3 + j, h, AUG * h + HEAD + 3 + j] = 1.0
            aug1[0, AUG * h + HEAD + 3 + j] = 1.0
            aug1[1, AUG * h + HEAD + j] = 1.0
    return pq, pc, aug1


def _block_ones(width):
    h = np.arange(width) // HEAD
    return (h[:, None] == h[None, :]).astype(np.float32)


def _mem_block_diag(mk, mv):
    b = mk.shape[0]
    eye = jnp.eye(N_MEM, dtype=mk.dtype)
    kt = jnp.einsum('bmhd,hg->bhdgm', mk, eye).reshape(b, W_MEM, N_MEM * N_MEM_TOK)
    vb = jnp.einsum('bmhd,hg->bhmgd', mv, eye).reshape(b, N_MEM * N_MEM_TOK, W_MEM)
    return kt.astype(BF16), vb.astype(BF16)


def _layer(x, shift_prev, s0, mk, mv, fox_past, wts, consts, tm, tq, tc, chunk):
    (proj_wts, gnw, gnb, wf, wr, wm) = wts
    (ones384, pq, pc, aug1, shift_idx, shift_inv) = consts
    b, t, _ = x.shape
    shift0 = jnp.take(jnp.pad(shift_prev, ((0, 0), (0, 0), (0, 1))), shift_idx, axis=2)
    mkt, mvb = _mem_block_diag(mk, mv)
    (qa, ka, kn, v, logf, gf, rt, at, kh, bh, rv, lc, gr, bg, om, shift_out, *vt) = _proj(
        x, shift0, mkt, mvb, proj_wts, tm, chunk, with_vt=fox_past is None)
    if fox_past is None:
        of = _fox(qa, ka, vt[0], tq)
    else:
        k_past, v_past, logf_past = fox_past
        p_len = k_past.shape[1]
        lfp = jnp.pad(logf_past.astype(F32), ((0, 0), (0, 0), (0, SMALL - N_FOX)))
        of = _fox_step(qa, ka, v, k_past.reshape(b, p_len, W_FOX), v_past.reshape(b, p_len, W_FOX), lfp,
                       pq, pc, aug1)
    y_r, s_new = _rwkv(rt, at, kh, bh, rv, lc, s0, tc, chunk)
    y = _out(x, of, gf, y_r, gr, bg, om, gnw, gnb, ones384, wf, wr, wm, tm)
    shift_new = jnp.take(shift_out, shift_inv, axis=2)
    return (y, kn.reshape(b, t, N_FOX, HEAD), v.reshape(b, t, N_FOX, HEAD), logf, s_new, shift_new)


def kernel(x_prompt, x_sample, mem_prompt, cache_fox_k, cache_fox_v, cache_fox_logf, cache_mem_k, cache_mem_v, state_rwkv, state_rwkv_shift, norm_g, w_in, fox_q_g, fox_k_g, fox_b_f, rwkv_mu, rwkv_w0, rwkv_w_up, rwkv_a0, rwkv_a_up, rwkv_k_k, rwkv_k_a, rwkv_r_k, rwkv_gn_w, rwkv_gn_b, mem_norm_g, w_mem_kv, mem_q_g, mem_k_g, w_out):
    depth = w_in.shape[0]
    bp = x_prompt.shape[0]
    col_idx = _padded_column_index()
    shift_idx = _shift_row_index()
    shift_inv = _shift_row_inverse()
    pq_np, pc_np, aug1_np = _placement_constants()
    ones384 = jnp.asarray(_block_ones(W_FOX), BF16)
    ones256 = jnp.asarray(_block_ones(W_MEM), BF16)
    pq = jnp.asarray(pq_np, BF16)
    pc = jnp.asarray(pc_np, BF16)
    aug1 = jnp.asarray(aug1_np, F32)
    consts = (ones384, pq, pc, aug1, shift_idx, shift_inv)

    yp, ys = x_prompt, x_sample
    outs = [[] for _ in range(12)]
    for l in range(depth):
        row = lambda a: a[l].reshape(1, -1).astype(F32)
        tile = lambda a, n: jnp.tile(a[l].reshape(1, -1).astype(F32), (1, n))
        w_pad = jnp.take(jnp.pad(w_in[l], ((0, 0), (0, 1))), col_idx, axis=1).astype(BF16)
        mu_pad = jnp.take(jnp.pad(row(rwkv_mu), ((0, 0), (0, 1))), shift_idx, axis=1)
        bf_pad = jnp.pad(row(fox_b_f), ((0, 0), (0, SMALL - N_FOX)))
        wup_pad = jnp.pad(rwkv_w_up[l].astype(F32), ((SM_WD, SMALL - SM_WD - LORA), (0, 0)))
        aup_pad = jnp.pad(rwkv_a_up[l].astype(F32), ((SM_AD, SMALL - SM_AD - LORA), (0, 0)))
        proj_wts = (row(norm_g), w_pad, tile(fox_q_g, N_FOX), tile(fox_k_g, N_FOX), bf_pad, mu_pad,
                    row(rwkv_w0), row(rwkv_a0), wup_pad, aup_pad, row(rwkv_k_k), row(rwkv_k_a), row(rwkv_r_k),
                    tile(mem_q_g, N_MEM), ones384, ones256, pq, pc, aug1)
        wo = w_out[l].astype(BF16)
        wts = (proj_wts, row(rwkv_gn_w), row(rwkv_gn_b), wo[:W_FOX], wo[W_FOX:W_FOX + W_RWKV], wo[W_FOX + W_RWKV:])

        mk2, mv2 = _mem_kv(mem_prompt, row(mem_norm_g), w_mem_kv[l].astype(BF16), tile(mem_k_g, N_MEM), ones256)
        mk = mk2.reshape(bp, N_MEM_TOK, N_MEM, HEAD)
        mv = mv2.reshape(bp, N_MEM_TOK, N_MEM, HEAD)
        shift_zero = jnp.zeros((bp, 1, RWKV_COLS), F32)
        s_zero = jnp.zeros((bp, N_RWKV, HEAD, HEAD), F32)
        yp, k, v, lf, s_new, sh_new = _layer(yp, shift_zero, s_zero, mk, mv, None, wts, consts,
                                             tm=PROMPT_TM, tq=PROMPT_TQ, tc=PROMPT_TC, chunk=RWKV_CHUNK)
        for lst, val in zip(outs[:7], (k, v, lf, mk, mv, s_new, sh_new)):
            lst.append(val)
        s_len = ys.shape[1]
        ys, k, v, lf, s_new, sh_new = _layer(
            ys, state_rwkv_shift[l], state_rwkv[l].astype(F32), cache_mem_k[l], cache_mem_v[l],
            (cache_fox_k[l], cache_fox_v[l], cache_fox_logf[l]), wts, consts,
            tm=s_len, tq=None, tc=s_len, chunk=s_len)
        for lst, val in zip(outs[7:], (k, v, lf, s_new, sh_new)):
            lst.append(val)
    return (yp, ys) + tuple(jnp.stack(o) for o in outs)
```

```python
import functools

import numpy as np
import jax
import jax.numpy as jnp
from jax import lax
from jax.experimental import pallas as pl
from jax.experimental.pallas import tpu as pltpu

F32 = jnp.float32
BF16 = jnp.bfloat16

D_MODEL = 1024
HEAD = 64
N_FOX = 6
N_RWKV = 6
N_MEM = 4
W_FOX = N_FOX * HEAD
W_RWKV = N_RWKV * HEAD
W_MEM = N_MEM * HEAD
N_MEM_TOK = 256
LORA = 32
NORM_EPS = 1e-6
GN_EPS = 64e-5
LOG2E = float(np.log2(np.e))
FOX_COLS = 4 * W_FOX + N_FOX
RWKV_COLS = 4 * W_RWKV + 2 * LORA
MEM_COLS = 2 * W_MEM

LANE = 128
SMALL = LANE
SM_WD = 32
SM_AD = 64
OFF_FQ, OFF_FK, OFF_FV, OFF_FG = 0, 384, 768, 1152
OFF_RR, OFF_RK, OFF_RV, OFF_RG = 1536, 1920, 2304, 2688
OFF_MQ, OFF_MG = 3072, 3328
OFF_SM = 3584
N_PAD = OFF_SM + SMALL
SHIFT_W = 4 * W_RWKV + SMALL
AUG = LANE
RWKV_CHUNK = 64
VMEM_LIMIT = 56 * 1024 * 1024
PROMPT_TM = 256
PROMPT_TQ = 512
PROMPT_TC = 256


def _dot(a, b):
    return jnp.dot(a, b, preferred_element_type=F32)


def _dot_nt(a, b):
    return lax.dot_general(a, b, (((1,), (1,)), ((), ())), preferred_element_type=F32)


def _dot_tn(a, b):
    return lax.dot_general(a, b, (((0,), (0,)), ((), ())), preferred_element_type=F32)


def _split2(x):
    hi = x.astype(BF16)
    lo = (x - hi.astype(F32)).astype(BF16)
    return hi, lo


def _split3(x):
    hi = x.astype(BF16)
    r1 = x - hi.astype(F32)
    mid = r1.astype(BF16)
    lo = (r1 - mid.astype(F32)).astype(BF16)
    return hi, mid, lo


def _dot1(fn, a, b):
    return fn(a.astype(BF16), b.astype(BF16))


def _exact_lhs_dot(m_bf16, x, parts):
    pieces = _split3(x) if parts == 3 else _split2(x)
    acc = _dot(m_bf16, pieces[0])
    for p in pieces[1:]:
        acc = acc + _dot(m_bf16, p)
    return acc


def _segsum(x, ones_bd):
    return _dot(x.astype(BF16), ones_bd)


def _head_rms(t, gain, ones_bd):
    msq = _segsum(t * t, ones_bd) * (1.0 / HEAD)
    return t * lax.rsqrt(msq + NORM_EPS) * gain


def _silu(x):
    return x * jax.nn.sigmoid(x)


def _softplus(z):
    return jnp.maximum(z, 0.0) + jnp.log1p(jnp.exp(-jnp.abs(z)))


def _full(shape):
    n = len(shape)
    return pl.BlockSpec(shape, lambda *_: (0,) * n)


def _params(n_axes):
    return pltpu.CompilerParams(dimension_semantics=("arbitrary",) * n_axes, vmem_limit_bytes=VMEM_LIMIT)


def _mem_kv_body(mem_ref, g_ref, w_ref, kg_ref, ones_ref, k_ref, v_ref):
    x = mem_ref[0]
    ms = jnp.mean(x * x, axis=-1, keepdims=True)
    xn = (x * lax.rsqrt(ms + NORM_EPS) * g_ref[...]).astype(BF16)
    kv = _dot(xn, w_ref[...])
    k_ref[0] = _head_rms(kv[:, :W_MEM], kg_ref[...], ones_ref[...])
    v_ref[0] = kv[:, W_MEM:]


def _mem_kv(mem, g, w_bf16, kg4, ones256):
    b = mem.shape[0]
    blk = pl.BlockSpec((1, N_MEM_TOK, W_MEM), lambda i: (i, 0, 0))
    return pl.pallas_call(
        _mem_kv_body,
        grid=(b,),
        in_specs=[pl.BlockSpec((1, N_MEM_TOK, D_MODEL), lambda i: (i, 0, 0)),
                  _full((1, D_MODEL)), _full((D_MODEL, 2 * W_MEM)), _full((1, W_MEM)), _full((W_MEM, W_MEM))],
        out_specs=[blk, blk],
        out_shape=[jax.ShapeDtypeStruct((b, N_MEM_TOK, W_MEM), F32)] * 2,
        compiler_params=_params(1),
        name="mem_kv",
    )(mem, g, w_bf16, kg4, ones256)


def _proj_body(chunk, with_vt,
               x_ref, shift0_ref, mkt_ref, mvb_ref, ng_ref, w_ref,
               fqg_ref, fkg_ref, bf_ref, mu_ref, w0_ref, a0_ref, wup_ref, aup_ref,
               kk_ref, ka_ref, rk_ref, mqg_ref, ones384_ref, ones256_ref, pq_ref, pc_ref, aug1_ref,
               qa_ref, kaug_ref, kn_ref, v_ref, logf_ref, gf_ref,
               rt_ref, at_ref, kh_ref, bh_ref, rv_ref, lc_ref, gr_ref, bg_ref, om_ref, shift_out_ref,
               *vt_and_scratch):
    vt_ref = vt_and_scratch[0] if with_vt else None
    carry_shift, carry_c = vt_and_scratch[-2:]
    i = pl.program_id(1)
    tm = x_ref.shape[1]

    @pl.when(i == 0)
    def _():
        carry_shift[...] = shift0_ref[0]
        carry_c[...] = jnp.zeros_like(carry_c)

    x = x_ref[0]
    ms = jnp.mean(x * x, axis=-1, keepdims=True)
    xn = (x * lax.rsqrt(ms + NORM_EPS) * ng_ref[...]).astype(BF16)

    def proj(off, width):
        return _dot(xn, w_ref[:, off:off + width])

    ones384 = ones384_ref[...]
    row = lax.broadcasted_iota(jnp.int32, (tm, tm), 0)
    col = lax.broadcasted_iota(jnp.int32, (tm, tm), 1)
    tri = jnp.where(col <= row, 1.0, 0.0).astype(BF16)
    tri_chunk = jnp.where((col <= row) & ((row ^ col) < chunk), 1.0, 0.0).astype(BF16)

    hs = proj(OFF_SM, SMALL)
    qn = _head_rms(proj(OFF_FQ, W_FOX), fqg_ref[...], ones384)
    kn = _head_rms(proj(OFF_FK, W_FOX), fkg_ref[...], ones384)
    hv = proj(OFF_FV, W_FOX)
    hg = proj(OFF_FG, W_FOX)
    kn_ref[0] = kn
    v_ref[0] = hv
    if with_vt:
        vt_ref[0, 0] = hv.T.astype(BF16)
    gf_ref[0] = _silu(hg)
    f = hs + bf_ref[...]
    lane = lax.broadcasted_iota(jnp.int32, (tm, SMALL), 1)
    logf = jnp.where(lane < N_FOX, jnp.minimum(f, 0.0) - jnp.log1p(jnp.exp(-jnp.abs(f))), 0.0)
    logf_ref[0] = logf[:, 0:N_FOX]
    c = _exact_lhs_dot(tri, logf, 3) + carry_c[...]
    carry_c[...] = c[tm - 1:tm, :]
    chi, cmid, clo = _split3(c * LOG2E)
    qs = (qn * (HEAD ** -0.5 * LOG2E)).astype(BF16)
    qa = (_dot(qs, pq_ref[...]) + _dot(chi, pc_ref[0]) + _dot(cmid, pc_ref[1]) + _dot(clo, pc_ref[2])
          + aug1_ref[0:1, :])
    ka = (_dot(kn.astype(BF16), pq_ref[...]) - (_dot(chi, pc_ref[3]) + _dot(cmid, pc_ref[4]) + _dot(clo, pc_ref[5]))
          + aug1_ref[1:2, :])
    for h in range(N_FOX):
        qa_ref[0, h] = qa[:, AUG * h:AUG * (h + 1)].astype(BF16)
        kaug_ref[0, h] = ka[:, AUG * h:AUG * (h + 1)].astype(BF16)

    qm = _head_rms(proj(OFF_MQ, W_MEM), mqg_ref[...], ones256_ref[...])
    s = _dot((qm * (HEAD ** -0.5)).astype(BF16), mkt_ref[0])
    ps = []
    for h in range(N_MEM):
        sh = s[:, N_MEM_TOK * h:N_MEM_TOK * (h + 1)]
        e = jnp.exp(sh - jnp.max(sh, axis=1, keepdims=True))
        ps.append((e / jnp.sum(e, axis=1, keepdims=True)).astype(BF16))
    om = _dot(jnp.concatenate(ps, axis=1), mvb_ref[0])
    om_ref[0] = om * _silu(proj(OFF_MG, W_MEM))

    first_row = lax.broadcasted_iota(jnp.int32, (tm, 1), 0) == 0

    def tshift(cols, off, width):
        prev = jnp.where(first_row, carry_shift[:, off:off + width], pltpu.roll(cols, 1, 0))
        carry_shift[:, off:off + width] = cols[tm - 1:tm, :]
        return cols + (prev - cols) * mu_ref[:, off:off + width]

    r = tshift(proj(OFF_RR, W_RWKV), 0, W_RWKV)
    k = tshift(proj(OFF_RK, W_RWKV), W_RWKV, W_RWKV)
    v = tshift(proj(OFF_RV, W_RWKV), 2 * W_RWKV, W_RWKV)
    g = tshift(proj(OFF_RG, W_RWKV), 3 * W_RWKV, W_RWKV)
    sm = tshift(hs, 4 * W_RWKV, SMALL)
    shift_out_ref[0] = carry_shift[...]

    w_lin = w0_ref[...] + _dot1(_dot, jnp.tanh(sm), wup_ref[...])
    a = jax.nn.sigmoid(a0_ref[...] + _dot1(_dot, sm, aup_ref[...]))
    lw = -jnp.exp(-_softplus(-w_lin) - 0.5)
    kk = k * kk_ref[...]
    kk = kk * lax.rsqrt(jnp.maximum(_segsum(kk * kk, ones384), 1e-24))
    kt = k * (1.0 + (a - 1.0) * ka_ref[...])
    gr = _silu(g)
    gr_ref[0] = gr
    bg_ref[0] = _segsum(r * kt * rk_ref[...], ones384) * v * gr
    lc = _exact_lhs_dot(tri_chunk, lw, 2)
    e_neg = jnp.exp(-lc)
    rt = r * jnp.exp(lc)
    at = -kk * jnp.exp(lc - lw)
    kh = kt * e_neg
    bh = kk * a * e_neg
    rt_ref[0] = rt.astype(BF16)
    at_ref[0] = at.astype(BF16)
    kh_ref[0] = kh.astype(BF16)
    bh_ref[0] = bh.astype(BF16)
    rv_ref[0] = v.astype(BF16)
    lc_ref[0] = lc


def _proj(x, shift0, mkt, mvb, wts, tm, chunk, with_vt):
    b, t, _ = x.shape
    grid = (b, t // tm)
    tok = lambda w: pl.BlockSpec((1, tm, w), lambda bi, i: (bi, i, 0))
    hm = lambda nh, w: pl.BlockSpec((1, nh, tm, w), lambda bi, i: (bi, 0, i, 0))
    per_b = lambda s1, s2: pl.BlockSpec((1, s1, s2), lambda bi, i: (bi, 0, 0))
    w_specs = [_full(a.shape) for a in wts]
    out_shape = [
        jax.ShapeDtypeStruct((b, N_FOX, t, AUG), BF16), jax.ShapeDtypeStruct((b, N_FOX, t, AUG), BF16),
        jax.ShapeDtypeStruct((b, t, W_FOX), F32),
        jax.ShapeDtypeStruct((b, t, W_FOX), F32), jax.ShapeDtypeStruct((b, t, N_FOX), F32),
        jax.ShapeDtypeStruct((b, t, W_FOX), F32),
    ] + [jax.ShapeDtypeStruct((b, t, W_RWKV), BF16)] * 5 + [jax.ShapeDtypeStruct((b, t, W_RWKV), F32)] + [
        jax.ShapeDtypeStruct((b, t, W_RWKV), F32), jax.ShapeDtypeStruct((b, t, W_RWKV), F32),
        jax.ShapeDtypeStruct((b, t, W_MEM), F32), jax.ShapeDtypeStruct((b, 1, SHIFT_W), F32),
    ]
    out_specs = [hm(N_FOX, AUG), hm(N_FOX, AUG), tok(W_FOX), tok(W_FOX), tok(N_FOX), tok(W_FOX)] \
        + [tok(W_RWKV)] * 6 + [tok(W_RWKV), tok(W_RWKV), tok(W_MEM), per_b(1, SHIFT_W)]
    if with_vt:
        out_shape.append(jax.ShapeDtypeStruct((b, t // tm, W_FOX, tm), BF16))
        out_specs.append(pl.BlockSpec((1, 1, W_FOX, tm), lambda bi, i: (bi, i, 0, 0)))
    return pl.pallas_call(
        functools.partial(_proj_body, chunk, with_vt),
        grid=grid,
        in_specs=[tok(D_MODEL), per_b(1, SHIFT_W), per_b(W_MEM, N_MEM * N_MEM_TOK),
                  per_b(N_MEM * N_MEM_TOK, W_MEM)] + w_specs,
        out_specs=out_specs,
        out_shape=out_shape,
        scratch_shapes=[pltpu.VMEM((1, SHIFT_W), F32), pltpu.VMEM((1, SMALL), F32)],
        compiler_params=_params(2),
        name="proj",
    )(x, shift0, mkt, mvb, *wts)


def _fox_body(tk, qa_ref, ka_ref, vt_ref, o_ref, m_scr, l_scr, acc_scr):
    qi = pl.program_id(2)
    tq = qa_ref.shape[2]
    slab = vt_ref.shape[3]
    m_scr[...] = jnp.full_like(m_scr, -1e30)
    l_scr[...] = jnp.zeros_like(l_scr)
    acc_scr[...] = jnp.zeros_like(acc_scr)

    def tile(ki, masked):
        rows = pl.ds(pl.multiple_of(ki * tk, tk), tk)
        s = [_dot_nt(ka_ref[0, hh, rows, :], qa_ref[0, hh]) for hh in range(2)]
        if masked:
            kpos = lax.broadcasted_iota(jnp.int32, (tk, tq), 0) + ki * tk
            qpos = lax.broadcasted_iota(jnp.int32, (tk, tq), 1) + qi * tq
            s = [jnp.where(kpos <= qpos, sh, -1e30) for sh in s]
        m_prev = [m_scr[hh] for hh in range(2)]
        m_new = [jnp.maximum(m_prev[hh], jnp.max(s[hh], axis=0, keepdims=True)) for hh in range(2)]
        p = [jnp.exp2(s[hh] - m_new[hh]) for hh in range(2)]
        vt = jnp.concatenate([vt_ref[0, ki * (tk // slab) + d] for d in range(tk // slab)], axis=1)
        pv = [_dot(vt, p[hh].astype(BF16)) for hh in range(2)]
        for hh in range(2):
            alpha = jnp.exp2(m_prev[hh] - m_new[hh])
            l_scr[hh] = alpha * l_scr[hh] + jnp.sum(p[hh], axis=0, keepdims=True)
            acc_scr[hh] = alpha * acc_scr[hh] + pv[hh]
            m_scr[hh] = m_new[hh]

    n_full = qi * (tq // tk)

    def full_tile(ki, carry):
        tile(ki, False)
        return carry

    lax.fori_loop(0, n_full, full_tile, 0)
    for d in range(tq // tk):
        tile(n_full + d, True)
    o_t = jnp.concatenate([acc_scr[0][0:HEAD] / l_scr[0], acc_scr[1][HEAD:2 * HEAD] / l_scr[1]], axis=0)
    o_ref[0] = o_t.T


def _fox(qa, ka, vt, tq, tk):
    b, _, t, _ = qa.shape
    n_slab, slab = vt.shape[1], vt.shape[3]
    return pl.pallas_call(
        functools.partial(_fox_body, tk),
        grid=(b, N_FOX // 2, t // tq),
        in_specs=[pl.BlockSpec((1, 2, tq, AUG), lambda bi, p, qi: (bi, p, qi, 0)),
                  pl.BlockSpec((1, 2, t, AUG), lambda bi, p, qi: (bi, p, 0, 0)),
                  pl.BlockSpec((1, n_slab, 2 * HEAD, slab), lambda bi, p, qi: (bi, 0, p, 0))],
        out_specs=pl.BlockSpec((1, tq, 2 * HEAD), lambda bi, p, qi: (bi, qi, p)),
        out_shape=jax.ShapeDtypeStruct((b, t, W_FOX), F32),
        scratch_shapes=[pltpu.VMEM((2, 1, tq), F32), pltpu.VMEM((2, 1, tq), F32), pltpu.VMEM((2, 2 * HEAD, tq), F32)],
        compiler_params=_params(3),
        name="fox",
    )(qa, ka, vt)


def _fox_step_body(qa_ref, ka_ref, vn_ref, kp_ref, vp_ref, lfp_ref, pq_ref, pc_ref, aug1_ref, o_ref):
    s_new = qa_ref.shape[2]
    p_len = kp_ref.shape[1]
    row = lax.broadcasted_iota(jnp.int32, (p_len, p_len), 0)
    col = lax.broadcasted_iota(jnp.int32, (p_len, p_len), 1)
    upper = jnp.where(col > row, 1.0, 0.0).astype(BF16)
    suf = _exact_lhs_dot(upper, lfp_ref[0], 3)
    shi, smid, slo = _split3(suf * LOG2E)
    kpa = (_dot(kp_ref[0].astype(BF16), pq_ref[...])
           + (_dot(shi, pc_ref[3]) + _dot(smid, pc_ref[4]) + _dot(slo, pc_ref[5])) + aug1_ref[1:2, :])
    vp = vp_ref[0]
    vn = vn_ref[0]
    qrow = lax.broadcasted_iota(jnp.int32, (s_new, s_new), 0)
    kcol = lax.broadcasted_iota(jnp.int32, (s_new, s_new), 1)
    outs = []
    for h in range(N_FOX):
        q = qa_ref[0, h]
        sp = _dot_nt(q, kpa[:, AUG * h:AUG * (h + 1)].astype(BF16))
        sn = jnp.where(kcol <= qrow, _dot_nt(q, ka_ref[0, h]), -1e30)
        m = jnp.maximum(jnp.max(sp, axis=1, keepdims=True), jnp.max(sn, axis=1, keepdims=True))
        pp = jnp.exp2(sp - m)
        pn = jnp.exp2(sn - m)
        l = jnp.sum(pp, axis=1, keepdims=True) + jnp.sum(pn, axis=1, keepdims=True)
        sl = slice(HEAD * h, HEAD * (h + 1))
        o = (_dot((pp / l).astype(BF16), vp[:, sl].astype(BF16))
             + _dot((pn / l).astype(BF16), vn[:, sl].astype(BF16)))
        outs.append(o)
    o_ref[0] = jnp.concatenate(outs, axis=1)


def _fox_step(qa, ka, vb, k_past, v_past, logf_past_pad, pq, pc, aug1):
    b, _, s_new, _ = qa.shape
    p_len = k_past.shape[1]
    return pl.pallas_call(
        _fox_step_body,
        grid=(b,),
        in_specs=[pl.BlockSpec((1, N_FOX, s_new, AUG), lambda i: (i, 0, 0, 0)),
                  pl.BlockSpec((1, N_FOX, s_new, AUG), lambda i: (i, 0, 0, 0)),
                  pl.BlockSpec((1, s_new, W_FOX), lambda i: (i, 0, 0)),
                  pl.BlockSpec((1, p_len, W_FOX), lambda i: (i, 0, 0)),
                  pl.BlockSpec((1, p_len, W_FOX), lambda i: (i, 0, 0)),
                  pl.BlockSpec((1, p_len, SMALL), lambda i: (i, 0, 0)),
                  _full(pq.shape), _full(pc.shape), _full(aug1.shape)],
        out_specs=pl.BlockSpec((1, s_new, W_FOX), lambda i: (i, 0, 0)),
        out_shape=jax.ShapeDtypeStruct((b, s_new, W_FOX), F32),
        compiler_params=_params(1),
        name="fox_step",
    )(qa, ka, vb, k_past, v_past, logf_past_pad, pq, pc, aug1)


def _rwkv_body(chunk, rt_ref, at_ref, kh_ref, bh_ref, v_ref, lc_ref, s0_ref, y_ref, sout_ref, s_scr):
    j = pl.program_id(1)
    nj = pl.num_programs(1)
    c = chunk
    c2 = 2 * chunk
    n_chunks = rt_ref.shape[1] // c
    n_pairs = N_RWKV // 2
    pair_w = 2 * HEAD

    @pl.when(j == 0)
    def _():
        z = jnp.zeros((HEAD, HEAD), F32)
        for p in range(n_pairs):
            s_scr[p] = jnp.concatenate([jnp.concatenate([s0_ref[0, 2 * p], z], axis=1),
                                        jnp.concatenate([z, s0_ref[0, 2 * p + 1]], axis=1)], axis=0)

    lane_lo = lax.broadcasted_iota(jnp.int32, (c, pair_w), 1) < HEAD
    row = lax.broadcasted_iota(jnp.int32, (c2, c2), 0)
    col = lax.broadcasted_iota(jnp.int32, (c2, c2), 1)
    same = (row >= c) == (col >= c)
    strict = same & (col < row)
    lower = same & (col <= row)
    eye = jnp.where(col == row, 1.0, 0.0)
    srow = lax.broadcasted_iota(jnp.int32, (pair_w, pair_w), 0)
    scol = lax.broadcasted_iota(jnp.int32, (pair_w, pair_w), 1)
    same_state = (srow >= HEAD) == (scol >= HEAD)

    def stack(x):
        zero = jnp.zeros_like(x)
        return jnp.concatenate([jnp.where(lane_lo, x, zero), jnp.where(lane_lo, zero, x)], axis=0)

    def fold(a):
        return a[:c] + a[c:]

    units = [(ci, p) for ci in range(n_chunks) for p in range(n_pairs)]

    def tile(ref, u):
        ci, p = u
        return ref[0, ci * c:(ci + 1) * c, pair_w * p:pair_w * (p + 1)]

    rt = [tile(rt_ref, u) for u in units]
    at2 = [stack(tile(at_ref, u)) for u in units]
    kh = [tile(kh_ref, u) for u in units]
    bh = [tile(bh_ref, u) for u in units]
    v2 = [stack(tile(v_ref, u)) for u in units]
    g = [_dot_nt(jnp.concatenate([at2[i], stack(rt[i])], axis=0),
                 jnp.concatenate([kh[i], kh[i], bh[i], bh[i]], axis=0)) for i in range(len(units))]
    a_ak = [jnp.where(strict, x[:c2, :c2], 0.0) for x in g]
    a_ab = [jnp.where(strict, x[:c2, c2:], 0.0) for x in g]
    a_rk = [fold(jnp.where(lower, x[c2:, :c2], 0.0)) for x in g]
    a_rb = [fold(jnp.where(lower, x[c2:, c2:], 0.0)) for x in g]
    akv2 = [_dot1(_dot, a_ak[i], v2[i]) for i in range(len(units))]
    yv = [_dot1(_dot, a_rk[i], v2[i]) for i in range(len(units))]
    inv = [eye + x for x in a_ab]
    pw = a_ab
    n = 1
    while 2 * n < c:
        pw = [_dot1(_dot, x, x) for x in pw]
        inv = [inv[i] + _dot1(_dot, inv[i], pw[i]) for i in range(len(units))]
        n *= 2
    tw = [_dot(inv[i].astype(BF16), jnp.concatenate([at2[i], akv2[i].astype(BF16)], axis=1))
          for i in range(len(units))]
    w_mat = [fold(x[:, :pair_w]) for x in tw]
    u0 = [fold(x[:, pair_w:]) for x in tw]

    for ci in range(n_chunks):
        idx = [ci * n_pairs + p for p in range(n_pairs)]
        s_old = [s_scr[p] for p in range(n_pairs)]
        pc = [jnp.exp(lc_ref[0, (ci + 1) * c - 1:(ci + 1) * c, pair_w * p:pair_w * (p + 1)]) for p in range(n_pairs)]
        ws = [_dot_nt(jnp.concatenate([w_mat[i].astype(BF16), rt[i]], axis=0), s_old[p].astype(BF16))
              for p, i in enumerate(idx)]
        u = [ws[p][:c] + u0[i] for p, i in enumerate(idx)]
        y = [ws[p][c:] + yv[i] + _dot1(_dot, a_rb[i], stack(u[p])) for p, i in enumerate(idx)]
        upd = [_dot_tn(jnp.concatenate([tile(v_ref, units[i]), u[p].astype(BF16)], axis=0),
                       jnp.concatenate([(kh[i] * pc[p]).astype(BF16), (bh[i] * pc[p]).astype(BF16)], axis=0))
               for p, i in enumerate(idx)]
        for p in range(n_pairs):
            s_scr[p] = s_old[p] * pc[p] + jnp.where(same_state, upd[p], 0.0)
            y_ref[0, ci * c:(ci + 1) * c, pair_w * p:pair_w * (p + 1)] = y[p]

    @pl.when(j == nj - 1)
    def _():
        for p in range(n_pairs):
            s = s_scr[p]
            sout_ref[0, 2 * p] = s[:HEAD, :HEAD]
            sout_ref[0, 2 * p + 1] = s[HEAD:, HEAD:]


def _rwkv(rt, at, kh, bh, v, lc, s0, tc, chunk):
    b, t, _ = rt.shape
    tok = pl.BlockSpec((1, tc, W_RWKV), lambda bi, j: (bi, j, 0))
    st = pl.BlockSpec((1, N_RWKV, HEAD, HEAD), lambda bi, j: (bi, 0, 0, 0))
    return pl.pallas_call(
        functools.partial(_rwkv_body, chunk),
        grid=(b, t // tc),
        in_specs=[tok] * 6 + [st],
        out_specs=[tok, st],
        out_shape=[jax.ShapeDtypeStruct((b, t, W_RWKV), F32), jax.ShapeDtypeStruct((b, N_RWKV, HEAD, HEAD), F32)],
        scratch_shapes=[pltpu.VMEM((N_RWKV // 2, 2 * HEAD, 2 * HEAD), F32)],
        compiler_params=_params(2),
        name="rwkv",
    )(rt, at, kh, bh, v, lc, s0)


def _out_body(x_ref, of_ref, gf_ref, y_ref, gr_ref, bg_ref, om_ref, gnw_ref, gnb_ref, ones_ref,
              wf_ref, wr_ref, wm_ref, o_ref):
    ones384 = ones_ref[...]
    y = y_ref[0]
    mu = _segsum(y, ones384) * (1.0 / HEAD)
    d = y - mu
    var = _segsum(d * d, ones384) * (1.0 / HEAD)
    yn = d * lax.rsqrt(var + GN_EPS) * gnw_ref[...] + gnb_ref[...]
    o_r = yn * gr_ref[0] + bg_ref[0]
    acc = _dot((of_ref[0] * gf_ref[0]).astype(BF16), wf_ref[...])
    acc = acc + _dot(o_r.astype(BF16), wr_ref[...])
    acc = acc + _dot(om_ref[0].astype(BF16), wm_ref[...])
    o_ref[0] = x_ref[0] + acc


def _out(x, of, gf, y, gr, bg, om, gnw, gnb, ones384, wf, wr, wm, tm):
    b, t, _ = x.shape
    tok = lambda w: pl.BlockSpec((1, tm, w), lambda bi, i: (bi, i, 0))
    return pl.pallas_call(
        _out_body,
        grid=(b, t // tm),
        in_specs=[tok(D_MODEL), tok(W_FOX), tok(W_FOX), tok(W_RWKV), tok(W_RWKV), tok(W_RWKV), tok(W_MEM),
                  _full(gnw.shape), _full(gnb.shape), _full(ones384.shape),
                  _full(wf.shape), _full(wr.shape), _full(wm.shape)],
        out_specs=tok(D_MODEL),
        out_shape=jax.ShapeDtypeStruct((b, t, D_MODEL), F32),
        compiler_params=_params(2),
        name="out_proj",
    )(x, of, gf, y, gr, bg, om, gnw, gnb, ones384, wf, wr, wm)


def _padded_column_index():
    idx = np.full((N_PAD,), FOX_COLS + RWKV_COLS + MEM_COLS, np.int32)
    r0 = FOX_COLS
    m0 = FOX_COLS + RWKV_COLS
    idx[OFF_FQ:OFF_FQ + W_FOX] = np.arange(0, W_FOX)
    idx[OFF_FK:OFF_FK + W_FOX] = np.arange(W_FOX, 2 * W_FOX)
    idx[OFF_FV:OFF_FV + W_FOX] = np.arange(2 * W_FOX, 3 * W_FOX)
    idx[OFF_FG:OFF_FG + W_FOX] = np.arange(3 * W_FOX + N_FOX, 4 * W_FOX + N_FOX)
    idx[OFF_RR:OFF_RR + W_RWKV] = r0 + np.arange(0, W_RWKV)
    idx[OFF_RK:OFF_RK + W_RWKV] = r0 + np.arange(W_RWKV, 2 * W_RWKV)
    idx[OFF_RV:OFF_RV + W_RWKV] = r0 + np.arange(2 * W_RWKV, 3 * W_RWKV)
    idx[OFF_RG:OFF_RG + W_RWKV] = r0 + np.arange(3 * W_RWKV + 2 * LORA, 4 * W_RWKV + 2 * LORA)
    idx[OFF_MQ:OFF_MQ + W_MEM] = m0 + np.arange(0, W_MEM)
    idx[OFF_MG:OFF_MG + W_MEM] = m0 + np.arange(W_MEM, 2 * W_MEM)
    idx[OFF_SM:OFF_SM + N_FOX] = np.arange(3 * W_FOX, 3 * W_FOX + N_FOX)
    idx[OFF_SM + SM_WD:OFF_SM + SM_WD + LORA] = r0 + np.arange(3 * W_RWKV, 3 * W_RWKV + LORA)
    idx[OFF_SM + SM_AD:OFF_SM + SM_AD + LORA] = r0 + np.arange(3 * W_RWKV + LORA, 3 * W_RWKV + 2 * LORA)
    return idx


def _shift_row_index():
    idx = np.full((SHIFT_W,), RWKV_COLS, np.int32)
    idx[0:3 * W_RWKV] = np.arange(0, 3 * W_RWKV)
    idx[3 * W_RWKV:4 * W_RWKV] = np.arange(3 * W_RWKV + 2 * LORA, 4 * W_RWKV + 2 * LORA)
    idx[4 * W_RWKV + SM_WD:4 * W_RWKV + SM_WD + LORA] = np.arange(3 * W_RWKV, 3 * W_RWKV + LORA)
    idx[4 * W_RWKV + SM_AD:4 * W_RWKV + SM_AD + LORA] = np.arange(3 * W_RWKV + LORA, 3 * W_RWKV + 2 * LORA)
    return idx


def _shift_row_inverse():
    idx = np.zeros((RWKV_COLS,), np.int32)
    idx[0:3 * W_RWKV] = np.arange(0, 3 * W_RWKV)
    idx[3 * W_RWKV:3 * W_RWKV + LORA] = 4 * W_RWKV + SM_WD + np.arange(LORA)
    idx[3 * W_RWKV + LORA:3 * W_RWKV + 2 * LORA] = 4 * W_RWKV + SM_AD + np.arange(LORA)
    idx[3 * W_RWKV + 2 * LORA:] = 3 * W_RWKV + np.arange(W_RWKV)
    return idx


def _placement_constants():
    pq = np.zeros((W_FOX, N_FOX * AUG), np.float32)
    pc = np.zeros((6, SMALL, N_FOX * AUG), np.float32)
    aug1 = np.zeros((2, N_FOX * AUG), np.float32)
    for h in range(N_FOX):
        for d in range(HEAD):
            pq[HEAD * h + d, AUG * h + d] = 1.0
        for j in range(3):
            pc[j, h, AUG * h + HEAD + j] = 1.0
            pc[3 + j, h, AUG * h + HEAD + 3 + j] = 1.0
            aug1[0, AUG * h + HEAD + 3 + j] = 1.0
            aug1[1, AUG * h + HEAD + j] = 1.0
    return pq, pc, aug1


def _block_ones(width):
    h = np.arange(width) // HEAD
    return (h[:, None] == h[None, :]).astype(np.float32)


def _mem_block_diag(mk, mv):
    b = mk.shape[0]
    eye = jnp.eye(N_MEM, dtype=mk.dtype)
    kt = jnp.einsum('bmhd,hg->bhdgm', mk, eye).reshape(b, W_MEM, N_MEM * N_MEM_TOK)
    vb = jnp.einsum('bmhd,hg->bhmgd', mv, eye).reshape(b, N_MEM * N_MEM_TOK, W_MEM)
    return kt.astype(BF16), vb.astype(BF16)


def _layer(x, shift_prev, s0, mk, mv, fox_past, wts, consts, tm, tq, tc, chunk):
    (proj_wts, gnw, gnb, wf, wr, wm) = wts
    (ones384, pq, pc, aug1, shift_idx, shift_inv) = consts
    b, t, _ = x.shape
    shift0 = jnp.take(jnp.pad(shift_prev, ((0, 0), (0, 0), (0, 1))), shift_idx, axis=2)
    mkt, mvb = _mem_block_diag(mk, mv)
    (qa, ka, kn, v, logf, gf, rt, at, kh, bh, rv, lc, gr, bg, om, shift_out, *vt) = _proj(
        x, shift0, mkt, mvb, proj_wts, tm, chunk, with_vt=fox_past is None)
    if fox_past is None:
        of = _fox(qa, ka, vt[0], tq, tq)
    else:
        k_past, v_past, logf_past = fox_past
        p_len = k_past.shape[1]
        lfp = jnp.pad(logf_past.astype(F32), ((0, 0), (0, 0), (0, SMALL - N_FOX)))
        of = _fox_step(qa, ka, v, k_past.reshape(b, p_len, W_FOX), v_past.reshape(b, p_len, W_FOX), lfp,
                       pq, pc, aug1)
    y_r, s_new = _rwkv(rt, at, kh, bh, rv, lc, s0, tc, chunk)
    y = _out(x, of, gf, y_r, gr, bg, om, gnw, gnb, ones384, wf, wr, wm, tm)
    shift_new = jnp.take(shift_out, shift_inv, axis=2)
    return (y, kn.reshape(b, t, N_FOX, HEAD), v.reshape(b, t, N_FOX, HEAD), logf, s_new, shift_new)


def kernel(x_prompt, x_sample, mem_prompt, cache_fox_k, cache_fox_v, cache_fox_logf, cache_mem_k, cache_mem_v, state_rwkv, state_rwkv_shift, norm_g, w_in, fox_q_g, fox_k_g, fox_b_f, rwkv_mu, rwkv_w0, rwkv_w_up, rwkv_a0, rwkv_a_up, rwkv_k_k, rwkv_k_a, rwkv_r_k, rwkv_gn_w, rwkv_gn_b, mem_norm_g, w_mem_kv, mem_q_g, mem_k_g, w_out):
    depth = w_in.shape[0]
    bp = x_prompt.shape[0]
    col_idx = _padded_column_index()
    shift_idx = _shift_row_index()
    shift_inv = _shift_row_inverse()
    pq_np, pc_np, aug1_np = _placement_constants()
    ones384 = jnp.asarray(_block_ones(W_FOX), BF16)
    ones256 = jnp.asarray(_block_ones(W_MEM), BF16)
    pq = jnp.asarray(pq_np, BF16)
    pc = jnp.asarray(pc_np, BF16)
    aug1 = jnp.asarray(aug1_np, F32)
    consts = (ones384, pq, pc, aug1, shift_idx, shift_inv)

    yp, ys = x_prompt, x_sample
    outs = [[] for _ in range(12)]
    for l in range(depth):
        row = lambda a: a[l].reshape(1, -1).astype(F32)
        tile = lambda a, n: jnp.tile(a[l].reshape(1, -1).astype(F32), (1, n))
        w_pad = jnp.take(jnp.pad(w_in[l], ((0, 0), (0, 1))), col_idx, axis=1).astype(BF16)
        mu_pad = jnp.take(jnp.pad(row(rwkv_mu), ((0, 0), (0, 1))), shift_idx, axis=1)
        bf_pad = jnp.pad(row(fox_b_f), ((0, 0), (0, SMALL - N_FOX)))
        wup_pad = jnp.pad(rwkv_w_up[l].astype(F32), ((SM_WD, SMALL - SM_WD - LORA), (0, 0)))
        aup_pad = jnp.pad(rwkv_a_up[l].astype(F32), ((SM_AD, SMALL - SM_AD - LORA), (0, 0)))
        proj_wts = (row(norm_g), w_pad, tile(fox_q_g, N_FOX), tile(fox_k_g, N_FOX), bf_pad, mu_pad,
                    row(rwkv_w0), row(rwkv_a0), wup_pad, aup_pad, row(rwkv_k_k), row(rwkv_k_a), row(rwkv_r_k),
                    tile(mem_q_g, N_MEM), ones384, ones256, pq, pc, aug1)
        wo = w_out[l].astype(BF16)
        wts = (proj_wts, row(rwkv_gn_w), row(rwkv_gn_b), wo[:W_FOX], wo[W_FOX:W_FOX + W_RWKV], wo[W_FOX + W_RWKV:])

        mk2, mv2 = _mem_kv(mem_prompt, row(mem_norm_g), w_mem_kv[l].astype(BF16), tile(mem_k_g, N_MEM), ones256)
        mk = mk2.reshape(bp, N_MEM_TOK, N_MEM, HEAD)
        mv = mv2.reshape(bp, N_MEM_TOK, N_MEM, HEAD)
        shift_zero = jnp.zeros((bp, 1, RWKV_COLS), F32)
        s_zero = jnp.zeros((bp, N_RWKV, HEAD, HEAD), F32)
        yp, k, v, lf, s_new, sh_new = _layer(yp, shift_zero, s_zero, mk, mv, None, wts, consts,
                                             tm=PROMPT_TM, tq=PROMPT_TQ, tc=PROMPT_TC, chunk=RWKV_CHUNK)
        for lst, val in zip(outs[:7], (k, v, lf, mk, mv, s_new, sh_new)):
            lst.append(val)
        s_len = ys.shape[1]
        ys, k, v, lf, s_new, sh_new = _layer(
            ys, state_rwkv_shift[l], state_rwkv[l].astype(F32), cache_mem_k[l], cache_mem_v[l],
            (cache_fox_k[l], cache_fox_v[l], cache_fox_logf[l]), wts, consts,
            tm=s_len, tq=None, tc=s_len, chunk=s_len)
        for lst, val in zip(outs[7:], (k, v, lf, s_new, sh_new)):
            lst.append(val)
    return (yp, ys) + tuple(jnp.stack(o) for o in outs)
```

```python
import functools

import numpy as np
import jax
import jax.numpy as jnp
from jax import lax
from jax.experimental import pallas as pl
from jax.experimental.pallas import tpu as pltpu

F32 = jnp.float32
BF16 = jnp.bfloat16

D_MODEL = 1024
HEAD = 64
N_FOX = 6
N_RWKV = 6
N_MEM = 4
W_FOX = N_FOX * HEAD
W_RWKV = N_RWKV * HEAD
W_MEM = N_MEM * HEAD
N_MEM_TOK = 256
LORA = 32
NORM_EPS = 1e-6
GN_EPS = 64e-5
LOG2E = float(np.log2(np.e))
FOX_COLS = 4 * W_FOX + N_FOX
RWKV_COLS = 4 * W_RWKV + 2 * LORA
MEM_COLS = 2 * W_MEM

LANE = 128
SMALL = LANE
SM_WD = 32
SM_AD = 64
OFF_FQ, OFF_FK, OFF_FV, OFF_FG = 0, 384, 768, 1152
OFF_RR, OFF_RK, OFF_RV, OFF_RG = 1536, 1920, 2304, 2688
OFF_MQ, OFF_MG = 3072, 3328
OFF_SM = 3584
N_PAD = OFF_SM + SMALL
SHIFT_W = 4 * W_RWKV + SMALL
AUG = LANE
RWKV_CHUNK = 64
VMEM_LIMIT = 56 * 1024 * 1024
PROMPT_TM = 256
PROMPT_TQ = 512
PROMPT_TC = 256
STEP_SUFFIX_BLOCK = 256


def _dot(a, b):
    return jnp.dot(a, b, preferred_element_type=F32)


def _dot_nt(a, b):
    return lax.dot_general(a, b, (((1,), (1,)), ((), ())), preferred_element_type=F32)


def _dot_tn(a, b):
    return lax.dot_general(a, b, (((0,), (0,)), ((), ())), preferred_element_type=F32)


def _split2(x):
    hi = x.astype(BF16)
    lo = (x - hi.astype(F32)).astype(BF16)
    return hi, lo


def _split3(x):
    hi = x.astype(BF16)
    r1 = x - hi.astype(F32)
    mid = r1.astype(BF16)
    lo = (r1 - mid.astype(F32)).astype(BF16)
    return hi, mid, lo


def _dot1(fn, a, b):
    return fn(a.astype(BF16), b.astype(BF16))


def _exact_lhs_dot(m_bf16, x, parts):
    pieces = _split3(x) if parts == 3 else _split2(x)
    acc = _dot(m_bf16, pieces[0])
    for p in pieces[1:]:
        acc = acc + _dot(m_bf16, p)
    return acc


def _segsum(x, ones_bd):
    return _dot(x.astype(BF16), ones_bd)


def _head_rms(t, gain, ones_bd):
    msq = _segsum(t * t, ones_bd) * (1.0 / HEAD)
    return t * lax.rsqrt(msq + NORM_EPS) * gain


def _silu(x):
    return x * jax.nn.sigmoid(x)


def _softplus(z):
    return jnp.maximum(z, 0.0) + jnp.log1p(jnp.exp(-jnp.abs(z)))


def _full(shape):
    n = len(shape)
    return pl.BlockSpec(shape, lambda *_: (0,) * n)


def _params(n_axes):
    return pltpu.CompilerParams(dimension_semantics=("arbitrary",) * n_axes, vmem_limit_bytes=VMEM_LIMIT)


def _mem_kv_body(mem_ref, g_ref, w_ref, kg_ref, ones_ref, k_ref, v_ref):
    x = mem_ref[0]
    ms = jnp.mean(x * x, axis=-1, keepdims=True)
    xn = (x * lax.rsqrt(ms + NORM_EPS) * g_ref[...]).astype(BF16)
    kv = _dot(xn, w_ref[...])
    k_ref[0] = _head_rms(kv[:, :W_MEM], kg_ref[...], ones_ref[...])
    v_ref[0] = kv[:, W_MEM:]


def _mem_kv(mem, g, w_bf16, kg4, ones256):
    b = mem.shape[0]
    blk = pl.BlockSpec((1, N_MEM_TOK, W_MEM), lambda i: (i, 0, 0))
    return pl.pallas_call(
        _mem_kv_body,
        grid=(b,),
        in_specs=[pl.BlockSpec((1, N_MEM_TOK, D_MODEL), lambda i: (i, 0, 0)),
                  _full((1, D_MODEL)), _full((D_MODEL, 2 * W_MEM)), _full((1, W_MEM)), _full((W_MEM, W_MEM))],
        out_specs=[blk, blk],
        out_shape=[jax.ShapeDtypeStruct((b, N_MEM_TOK, W_MEM), F32)] * 2,
        compiler_params=_params(1),
        name="mem_kv",
    )(mem, g, w_bf16, kg4, ones256)


def _proj_body(chunk, with_vt,
               x_ref, shift0_ref, mkt_ref, mvb_ref, ng_ref, w_ref,
               fqg_ref, fkg_ref, bf_ref, mu_ref, w0_ref, a0_ref, wup_ref, aup_ref,
               kk_ref, ka_ref, rk_ref, mqg_ref, ones384_ref, ones256_ref, pq_ref, pc_ref, aug1_ref,
               qa_ref, kaug_ref, kn_ref, v_ref, logf_ref, gf_ref,
               rt_ref, at_ref, kh_ref, bh_ref, rv_ref, lc_ref, gr_ref, bg_ref, om_ref, shift_out_ref,
               *vt_and_scratch):
    extra_ref = vt_and_scratch[0]
    carry_shift, carry_c = vt_and_scratch[-2:]
    i = pl.program_id(1)
    tm = x_ref.shape[1]

    @pl.when(i == 0)
    def _():
        carry_shift[...] = shift0_ref[0]
        carry_c[...] = jnp.zeros_like(carry_c)

    x = x_ref[0]
    ms = jnp.mean(x * x, axis=-1, keepdims=True)
    xn = (x * lax.rsqrt(ms + NORM_EPS) * ng_ref[...]).astype(BF16)

    def proj(off, width):
        return _dot(xn, w_ref[:, off:off + width])

    ones384 = ones384_ref[...]
    row = lax.broadcasted_iota(jnp.int32, (tm, tm), 0)
    col = lax.broadcasted_iota(jnp.int32, (tm, tm), 1)
    tri = jnp.where(col <= row, 1.0, 0.0).astype(BF16)
    tri_chunk = jnp.where((col <= row) & ((row ^ col) < chunk), 1.0, 0.0).astype(BF16)

    hs = proj(OFF_SM, SMALL)
    qn = _head_rms(proj(OFF_FQ, W_FOX), fqg_ref[...], ones384)
    kn = _head_rms(proj(OFF_FK, W_FOX), fkg_ref[...], ones384)
    hv = proj(OFF_FV, W_FOX)
    hg = proj(OFF_FG, W_FOX)
    kn_ref[0] = kn
    v_ref[0] = hv
    if with_vt:
        extra_ref[0, 0] = hv.T.astype(BF16)
    gf_ref[0] = _silu(hg).astype(BF16)
    f = hs + bf_ref[...]
    lane = lax.broadcasted_iota(jnp.int32, (tm, SMALL), 1)
    logf = jnp.where(lane < N_FOX, jnp.minimum(f, 0.0) - jnp.log1p(jnp.exp(-jnp.abs(f))), 0.0)
    logf_ref[0] = logf[:, 0:N_FOX]
    c = _exact_lhs_dot(tri, logf, 3) + carry_c[...]
    carry_c[...] = c[tm - 1:tm, :]
    chi, cmid, clo = _split3(c * LOG2E)
    qs = (qn * (HEAD ** -0.5 * LOG2E)).astype(BF16)
    if not with_vt:
        extra_ref[0] = qs
    qa = (_dot(qs, pq_ref[...]) + _dot(chi, pc_ref[0]) + _dot(cmid, pc_ref[1]) + _dot(clo, pc_ref[2])
          + aug1_ref[0:1, :])
    ka = (_dot(kn.astype(BF16), pq_ref[...]) - (_dot(chi, pc_ref[3]) + _dot(cmid, pc_ref[4]) + _dot(clo, pc_ref[5]))
          + aug1_ref[1:2, :])
    for h in range(N_FOX):
        qa_ref[0, h] = qa[:, AUG * h:AUG * (h + 1)].astype(BF16)
        kaug_ref[0, h] = ka[:, AUG * h:AUG * (h + 1)].astype(BF16)

    qm = _head_rms(proj(OFF_MQ, W_MEM), mqg_ref[...], ones256_ref[...])
    s = _dot((qm * (HEAD ** -0.5)).astype(BF16), mkt_ref[0])
    ps = []
    for h in range(N_MEM):
        sh = s[:, N_MEM_TOK * h:N_MEM_TOK * (h + 1)]
        e = jnp.exp(sh - jnp.max(sh, axis=1, keepdims=True))
        ps.append((e / jnp.sum(e, axis=1, keepdims=True)).astype(BF16))
    om = _dot(jnp.concatenate(ps, axis=1), mvb_ref[0])
    om_ref[0] = (om * _silu(proj(OFF_MG, W_MEM))).astype(BF16)

    first_row = lax.broadcasted_iota(jnp.int32, (tm, 1), 0) == 0

    def tshift(cols, off, width):
        prev = jnp.where(first_row, carry_shift[:, off:off + width], pltpu.roll(cols, 1, 0))
        carry_shift[:, off:off + width] = cols[tm - 1:tm, :]
        return cols + (prev - cols) * mu_ref[:, off:off + width]

    r = tshift(proj(OFF_RR, W_RWKV), 0, W_RWKV)
    k = tshift(proj(OFF_RK, W_RWKV), W_RWKV, W_RWKV)
    v = tshift(proj(OFF_RV, W_RWKV), 2 * W_RWKV, W_RWKV)
    g = tshift(proj(OFF_RG, W_RWKV), 3 * W_RWKV, W_RWKV)
    sm = tshift(hs, 4 * W_RWKV, SMALL)
    shift_out_ref[0] = carry_shift[...]

    w_lin = w0_ref[...] + _dot1(_dot, jnp.tanh(sm), wup_ref[...])
    a = jax.nn.sigmoid(a0_ref[...] + _dot1(_dot, sm, aup_ref[...]))
    lw = -jnp.exp(-_softplus(-w_lin) - 0.5)
    kk = k * kk_ref[...]
    kk = kk * lax.rsqrt(jnp.maximum(_segsum(kk * kk, ones384), 1e-24))
    kt = k * (1.0 + (a - 1.0) * ka_ref[...])
    gr = _silu(g)
    gr_ref[0] = gr.astype(BF16)
    bg_ref[0] = (_segsum(r * kt * rk_ref[...], ones384) * v * gr).astype(BF16)
    lc = _exact_lhs_dot(tri_chunk, lw, 2)
    e_neg = jnp.exp(-lc)
    rt = r * jnp.exp(lc)
    at = -kk * jnp.exp(lc - lw)
    kh = kt * e_neg
    bh = kk * a * e_neg
    rt_ref[0] = rt.astype(BF16)
    at_ref[0] = at.astype(BF16)
    kh_ref[0] = kh.astype(BF16)
    bh_ref[0] = bh.astype(BF16)
    rv_ref[0] = v.astype(BF16)
    lc_ref[0] = lc


def _proj(x, shift0, mkt, mvb, wts, tm, chunk, with_vt):
    b, t, _ = x.shape
    grid = (b, t // tm)
    tok = lambda w: pl.BlockSpec((1, tm, w), lambda bi, i: (bi, i, 0))
    hm = lambda nh, w: pl.BlockSpec((1, nh, tm, w), lambda bi, i: (bi, 0, i, 0))
    per_b = lambda s1, s2: pl.BlockSpec((1, s1, s2), lambda bi, i: (bi, 0, 0))
    w_specs = [_full(a.shape) for a in wts]
    out_shape = [
        jax.ShapeDtypeStruct((b, N_FOX, t, AUG), BF16), jax.ShapeDtypeStruct((b, N_FOX, t, AUG), BF16),
        jax.ShapeDtypeStruct((b, t, W_FOX), F32),
        jax.ShapeDtypeStruct((b, t, W_FOX), F32), jax.ShapeDtypeStruct((b, t, N_FOX), F32),
        jax.ShapeDtypeStruct((b, t, W_FOX), BF16),
    ] + [jax.ShapeDtypeStruct((b, t, W_RWKV), BF16)] * 5 + [jax.ShapeDtypeStruct((b, t, W_RWKV), F32)] + [
        jax.ShapeDtypeStruct((b, t, W_RWKV), BF16), jax.ShapeDtypeStruct((b, t, W_RWKV), BF16),
        jax.ShapeDtypeStruct((b, t, W_MEM), BF16), jax.ShapeDtypeStruct((b, 1, SHIFT_W), F32),
    ]
    out_specs = [hm(N_FOX, AUG), hm(N_FOX, AUG), tok(W_FOX), tok(W_FOX), tok(N_FOX), tok(W_FOX)] \
        + [tok(W_RWKV)] * 6 + [tok(W_RWKV), tok(W_RWKV), tok(W_MEM), per_b(1, SHIFT_W)]
    if with_vt:
        out_shape.append(jax.ShapeDtypeStruct((b, t // tm, W_FOX, tm), BF16))
        out_specs.append(pl.BlockSpec((1, 1, W_FOX, tm), lambda bi, i: (bi, i, 0, 0)))
    else:
        out_shape.append(jax.ShapeDtypeStruct((b, t, W_FOX), BF16))
        out_specs.append(tok(W_FOX))
    return pl.pallas_call(
        functools.partial(_proj_body, chunk, with_vt),
        grid=grid,
        in_specs=[tok(D_MODEL), per_b(1, SHIFT_W), per_b(W_MEM, N_MEM * N_MEM_TOK),
                  per_b(N_MEM * N_MEM_TOK, W_MEM)] + w_specs,
        out_specs=out_specs,
        out_shape=out_shape,
        scratch_shapes=[pltpu.VMEM((1, SHIFT_W), F32), pltpu.VMEM((1, SMALL), F32)],
        compiler_params=_params(2),
        name="proj",
    )(x, shift0, mkt, mvb, *wts)


def _fox_body(tk, qa_ref, ka_ref, vt_ref, o_ref, m_scr, l_scr, acc_scr):
    qi = pl.program_id(2)
    tq = qa_ref.shape[2]
    slab = vt_ref.shape[3]
    m_scr[...] = jnp.full_like(m_scr, -1e30)
    l_scr[...] = jnp.zeros_like(l_scr)
    acc_scr[...] = jnp.zeros_like(acc_scr)

    def tile(ki, masked):
        rows = pl.ds(pl.multiple_of(ki * tk, tk), tk)
        s = [_dot_nt(ka_ref[0, hh, rows, :], qa_ref[0, hh]) for hh in range(2)]
        if masked:
            kpos = lax.broadcasted_iota(jnp.int32, (tk, tq), 0) + ki * tk
            qpos = lax.broadcasted_iota(jnp.int32, (tk, tq), 1) + qi * tq
            s = [jnp.where(kpos <= qpos, sh, -1e30) for sh in s]
        m_prev = [m_scr[hh] for hh in range(2)]
        m_new = [jnp.maximum(m_prev[hh], jnp.max(s[hh], axis=0, keepdims=True)) for hh in range(2)]
        p = [jnp.exp2(s[hh] - m_new[hh]) for hh in range(2)]
        vt = jnp.concatenate([vt_ref[0, ki * (tk // slab) + d] for d in range(tk // slab)], axis=1)
        pv = [_dot(vt, p[hh].astype(BF16)) for hh in range(2)]
        for hh in range(2):
            alpha = jnp.exp2(m_prev[hh] - m_new[hh])
            l_scr[hh] = alpha * l_scr[hh] + jnp.sum(p[hh], axis=0, keepdims=True)
            acc_scr[hh] = alpha * acc_scr[hh] + pv[hh]
            m_scr[hh] = m_new[hh]

    n_full = qi * (tq // tk)

    def full_tile(ki, carry):
        tile(ki, False)
        return carry

    lax.fori_loop(0, n_full, full_tile, 0)
    for d in range(tq // tk):
        tile(n_full + d, True)
    o_t = jnp.concatenate([acc_scr[0][0:HEAD] / l_scr[0], acc_scr[1][HEAD:2 * HEAD] / l_scr[1]], axis=0)
    o_ref[0] = o_t.T.astype(BF16)


def _fox(qa, ka, vt, tq, tk):
    b, _, t, _ = qa.shape
    n_slab, slab = vt.shape[1], vt.shape[3]
    return pl.pallas_call(
        functools.partial(_fox_body, tk),
        grid=(b, N_FOX // 2, t // tq),
        in_specs=[pl.BlockSpec((1, 2, tq, AUG), lambda bi, p, qi: (bi, p, qi, 0)),
                  pl.BlockSpec((1, 2, t, AUG), lambda bi, p, qi: (bi, p, 0, 0)),
                  pl.BlockSpec((1, n_slab, 2 * HEAD, slab), lambda bi, p, qi: (bi, 0, p, 0))],
        out_specs=pl.BlockSpec((1, tq, 2 * HEAD), lambda bi, p, qi: (bi, qi, p)),
        out_shape=jax.ShapeDtypeStruct((b, t, W_FOX), BF16),
        scratch_shapes=[pltpu.VMEM((2, 1, tq), F32), pltpu.VMEM((2, 1, tq), F32), pltpu.VMEM((2, 2 * HEAD, tq), F32)],
        compiler_params=_params(3),
        name="fox",
    )(qa, ka, vt)


def _fox_step_body(qs_ref, qa_ref, ka_ref, vn_ref, kp_ref, vp_ref, lfp_ref, o_ref):
    s_new = qa_ref.shape[2]
    p_len = kp_ref.shape[1]
    blk = min(p_len, STEP_SUFFIX_BLOCK)
    row = lax.broadcasted_iota(jnp.int32, (blk, blk), 0)
    col = lax.broadcasted_iota(jnp.int32, (blk, blk), 1)
    upper = jnp.where(col > row, 1.0, 0.0).astype(BF16)
    later = jnp.zeros((1, SMALL), F32)
    sufs = []
    for bi in reversed(range(p_len // blk)):
        lf = lfp_ref[0, bi * blk:(bi + 1) * blk, :]
        within = _exact_lhs_dot(upper, lf, 3)
        sufs.insert(0, within + later)
        later = later + within[0:1, :] + lf[0:1, :]
    suf_t = (jnp.concatenate(sufs, axis=0) * LOG2E).T
    qs = qs_ref[0]
    head_of_lane = lax.broadcasted_iota(jnp.int32, (s_new, W_FOX), 1) // HEAD
    heads = range(N_FOX)
    q6 = jnp.concatenate([jnp.where(head_of_lane == h, qs, jnp.zeros_like(qs)) for h in heads], axis=0)
    cq, sn, bias = [], [], []
    qrow = lax.broadcasted_iota(jnp.int32, (s_new, s_new), 0)
    kcol = lax.broadcasted_iota(jnp.int32, (s_new, s_new), 1)
    for h in heads:
        q = qa_ref[0, h]
        qf = q.astype(F32)
        cq.append(qf[:, HEAD:HEAD + 1] + qf[:, HEAD + 1:HEAD + 2] + qf[:, HEAD + 2:HEAD + 3])
        sn.append(jnp.where(kcol <= qrow, _dot_nt(q, ka_ref[0, h]), -1e30))
        bias.append(jnp.broadcast_to(suf_t[h:h + 1, :], (s_new, p_len)))
    sp = _dot_nt(q6, kp_ref[0]) + jnp.concatenate(cq, axis=0) + jnp.concatenate(bias, axis=0)
    sn = jnp.concatenate(sn, axis=0)
    m = jnp.maximum(jnp.max(sp, axis=1, keepdims=True), jnp.max(sn, axis=1, keepdims=True))
    pp = jnp.exp2(sp - m)
    pn = jnp.exp2(sn - m)
    l = jnp.sum(pp, axis=1, keepdims=True) + jnp.sum(pn, axis=1, keepdims=True)
    o6 = _dot((pp / l).astype(BF16), vp_ref[0]) + _dot((pn / l).astype(BF16), vn_ref[0].astype(BF16))
    o = jnp.zeros((s_new, W_FOX), F32)
    for h in heads:
        o = jnp.where(head_of_lane == h, o6[h * s_new:(h + 1) * s_new], o)
    o_ref[0] = o.astype(BF16)


def _fox_step(qs, qa, ka, v_new, k_past, v_past, logf_past_pad):
    b, _, s_new, _ = qa.shape
    p_len = k_past.shape[1]
    tok = lambda rows, w: pl.BlockSpec((1, rows, w), lambda i: (i, 0, 0))
    aug = pl.BlockSpec((1, N_FOX, s_new, AUG), lambda i: (i, 0, 0, 0))
    return pl.pallas_call(
        _fox_step_body,
        grid=(b,),
        in_specs=[tok(s_new, W_FOX), aug, aug, tok(s_new, W_FOX), tok(p_len, W_FOX), tok(p_len, W_FOX),
                  tok(p_len, SMALL)],
        out_specs=tok(s_new, W_FOX),
        out_shape=jax.ShapeDtypeStruct((b, s_new, W_FOX), BF16),
        compiler_params=_params(1),
        name="fox_step",
    )(qs, qa, ka, v_new, k_past, v_past, logf_past_pad)


def _rwkv_body(chunk, rt_ref, at_ref, kh_ref, bh_ref, v_ref, lc_ref, s0_ref, y_ref, sout_ref, s_scr):
    j = pl.program_id(1)
    nj = pl.num_programs(1)
    c = chunk
    c2 = 2 * chunk
    n_chunks = rt_ref.shape[1] // c
    n_pairs = N_RWKV // 2
    pair_w = 2 * HEAD

    @pl.when(j == 0)
    def _():
        z = jnp.zeros((HEAD, HEAD), F32)
        for p in range(n_pairs):
            s_scr[p] = jnp.concatenate([jnp.concatenate([s0_ref[0, 2 * p], z], axis=1),
                                        jnp.concatenate([z, s0_ref[0, 2 * p + 1]], axis=1)], axis=0)

    lane_lo = lax.broadcasted_iota(jnp.int32, (c, pair_w), 1) < HEAD
    row = lax.broadcasted_iota(jnp.int32, (c2, c2), 0)
    col = lax.broadcasted_iota(jnp.int32, (c2, c2), 1)
    same = (row >= c) == (col >= c)
    strict = same & (col < row)
    lower = same & (col <= row)
    eye = jnp.where(col == row, 1.0, 0.0)
    srow = lax.broadcasted_iota(jnp.int32, (pair_w, pair_w), 0)
    scol = lax.broadcasted_iota(jnp.int32, (pair_w, pair_w), 1)
    same_state = (srow >= HEAD) == (scol >= HEAD)

    def stack(x):
        zero = jnp.zeros_like(x)
        return jnp.concatenate([jnp.where(lane_lo, x, zero), jnp.where(lane_lo, zero, x)], axis=0)

    def fold(a):
        return a[:c] + a[c:]

    units = [(ci, p) for ci in range(n_chunks) for p in range(n_pairs)]

    def tile(ref, u):
        ci, p = u
        return ref[0, ci * c:(ci + 1) * c, pair_w * p:pair_w * (p + 1)]

    rt = [tile(rt_ref, u) for u in units]
    at2 = [stack(tile(at_ref, u)) for u in units]
    kh = [tile(kh_ref, u) for u in units]
    bh = [tile(bh_ref, u) for u in units]
    v2 = [stack(tile(v_ref, u)) for u in units]
    g = [_dot_nt(jnp.concatenate([at2[i], stack(rt[i])], axis=0),
                 jnp.concatenate([kh[i], kh[i], bh[i], bh[i]], axis=0)) for i in range(len(units))]
    a_ak = [jnp.where(strict, x[:c2, :c2], 0.0) for x in g]
    a_ab = [jnp.where(strict, x[:c2, c2:], 0.0) for x in g]
    a_rk = [fold(jnp.where(lower, x[c2:, :c2], 0.0)) for x in g]
    a_rb = [fold(jnp.where(lower, x[c2:, c2:], 0.0)) for x in g]
    akv2 = [_dot1(_dot, a_ak[i], v2[i]) for i in range(len(units))]
    yv = [_dot1(_dot, a_rk[i], v2[i]) for i in range(len(units))]
    inv = [eye + x for x in a_ab]
    pw = a_ab
    n = 1
    while 2 * n < c:
        pw = [_dot1(_dot, x, x) for x in pw]
        inv = [inv[i] + _dot1(_dot, inv[i], pw[i]) for i in range(len(units))]
        n *= 2
    tw = [_dot(inv[i].astype(BF16), jnp.concatenate([at2[i], akv2[i].astype(BF16)], axis=1))
          for i in range(len(units))]
    w_mat = [fold(x[:, :pair_w]) for x in tw]
    u0 = [fold(x[:, pair_w:]) for x in tw]

    for ci in range(n_chunks):
        idx = [ci * n_pairs + p for p in range(n_pairs)]
        s_old = [s_scr[p] for p in range(n_pairs)]
        pc = [jnp.exp(lc_ref[0, (ci + 1) * c - 1:(ci + 1) * c, pair_w * p:pair_w * (p + 1)]) for p in range(n_pairs)]
        ws = [_dot_nt(jnp.concatenate([w_mat[i].astype(BF16), rt[i]], axis=0), s_old[p].astype(BF16))
              for p, i in enumerate(idx)]
        u = [ws[p][:c] + u0[i] for p, i in enumerate(idx)]
        y = [ws[p][c:] + yv[i] + _dot1(_dot, a_rb[i], stack(u[p])) for p, i in enumerate(idx)]
        upd = [_dot_tn(jnp.concatenate([tile(v_ref, units[i]), u[p].astype(BF16)], axis=0),
                       jnp.concatenate([(kh[i] * pc[p]).astype(BF16), (bh[i] * pc[p]).astype(BF16)], axis=0))
               for p, i in enumerate(idx)]
        for p in range(n_pairs):
            s_scr[p] = s_old[p] * pc[p] + jnp.where(same_state, upd[p], 0.0)
            y_ref[0, ci * c:(ci + 1) * c, pair_w * p:pair_w * (p + 1)] = y[p]

    @pl.when(j == nj - 1)
    def _():
        for p in range(n_pairs):
            s = s_scr[p]
            sout_ref[0, 2 * p] = s[:HEAD, :HEAD]
            sout_ref[0, 2 * p + 1] = s[HEAD:, HEAD:]


def _rwkv(rt, at, kh, bh, v, lc, s0, tc, chunk):
    b, t, _ = rt.shape
    tok = pl.BlockSpec((1, tc, W_RWKV), lambda bi, j: (bi, j, 0))
    st = pl.BlockSpec((1, N_RWKV, HEAD, HEAD), lambda bi, j: (bi, 0, 0, 0))
    return pl.pallas_call(
        functools.partial(_rwkv_body, chunk),
        grid=(b, t // tc),
        in_specs=[tok] * 6 + [st],
        out_specs=[tok, st],
        out_shape=[jax.ShapeDtypeStruct((b, t, W_RWKV), F32), jax.ShapeDtypeStruct((b, N_RWKV, HEAD, HEAD), F32)],
        scratch_shapes=[pltpu.VMEM((N_RWKV // 2, 2 * HEAD, 2 * HEAD), F32)],
        compiler_params=_params(2),
        name="rwkv",
    )(rt, at, kh, bh, v, lc, s0)


def _out_body(x_ref, of_ref, gf_ref, y_ref, gr_ref, bg_ref, om_ref, gnw_ref, gnb_ref, ones_ref,
              wf_ref, wr_ref, wm_ref, o_ref):
    ones384 = ones_ref[...]
    y = y_ref[0]
    mu = _segsum(y, ones384) * (1.0 / HEAD)
    d = y - mu
    var = _segsum(d * d, ones384) * (1.0 / HEAD)
    yn = d * lax.rsqrt(var + GN_EPS) * gnw_ref[...] + gnb_ref[...]
    o_r = yn * gr_ref[0].astype(F32) + bg_ref[0].astype(F32)
    acc = _dot((of_ref[0].astype(F32) * gf_ref[0].astype(F32)).astype(BF16), wf_ref[...])
    acc = acc + _dot(o_r.astype(BF16), wr_ref[...])
    acc = acc + _dot(om_ref[0], wm_ref[...])
    o_ref[0] = x_ref[0] + acc


def _out(x, of, gf, y, gr, bg, om, gnw, gnb, ones384, wf, wr, wm, tm):
    b, t, _ = x.shape
    tok = lambda w: pl.BlockSpec((1, tm, w), lambda bi, i: (bi, i, 0))
    return pl.pallas_call(
        _out_body,
        grid=(b, t // tm),
        in_specs=[tok(D_MODEL), tok(W_FOX), tok(W_FOX), tok(W_RWKV), tok(W_RWKV), tok(W_RWKV), tok(W_MEM),
                  _full(gnw.shape), _full(gnb.shape), _full(ones384.shape),
                  _full(wf.shape), _full(wr.shape), _full(wm.shape)],
        out_specs=tok(D_MODEL),
        out_shape=jax.ShapeDtypeStruct((b, t, D_MODEL), F32),
        compiler_params=_params(2),
        name="out_proj",
    )(x, of, gf, y, gr, bg, om, gnw, gnb, ones384, wf, wr, wm)


def _padded_column_index():
    idx = np.full((N_PAD,), FOX_COLS + RWKV_COLS + MEM_COLS, np.int32)
    r0 = FOX_COLS
    m0 = FOX_COLS + RWKV_COLS
    idx[OFF_FQ:OFF_FQ + W_FOX] = np.arange(0, W_FOX)
    idx[OFF_FK:OFF_FK + W_FOX] = np.arange(W_FOX, 2 * W_FOX)
    idx[OFF_FV:OFF_FV + W_FOX] = np.arange(2 * W_FOX, 3 * W_FOX)
    idx[OFF_FG:OFF_FG + W_FOX] = np.arange(3 * W_FOX + N_FOX, 4 * W_FOX + N_FOX)
    idx[OFF_RR:OFF_RR + W_RWKV] = r0 + np.arange(0, W_RWKV)
    idx[OFF_RK:OFF_RK + W_RWKV] = r0 + np.arange(W_RWKV, 2 * W_RWKV)
    idx[OFF_RV:OFF_RV + W_RWKV] = r0 + np.arange(2 * W_RWKV, 3 * W_RWKV)
    idx[OFF_RG:OFF_RG + W_RWKV] = r0 + np.arange(3 * W_RWKV + 2 * LORA, 4 * W_RWKV + 2 * LORA)
    idx[OFF_MQ:OFF_MQ + W_MEM] = m0 + np.arange(0, W_MEM)
    idx[OFF_MG:OFF_MG + W_MEM] = m0 + np.arange(W_MEM, 2 * W_MEM)
    idx[OFF_SM:OFF_SM + N_FOX] = np.arange(3 * W_FOX, 3 * W_FOX + N_FOX)
    idx[OFF_SM + SM_WD:OFF_SM + SM_WD + LORA] = r0 + np.arange(3 * W_RWKV, 3 * W_RWKV + LORA)
    idx[OFF_SM + SM_AD:OFF_SM + SM_AD + LORA] = r0 + np.arange(3 * W_RWKV + LORA, 3 * W_RWKV + 2 * LORA)
    return idx


def _shift_row_index():
    idx = np.full((SHIFT_W,), RWKV_COLS, np.int32)
    idx[0:3 * W_RWKV] = np.arange(0, 3 * W_RWKV)
    idx[3 * W_RWKV:4 * W_RWKV] = np.arange(3 * W_RWKV + 2 * LORA, 4 * W_RWKV + 2 * LORA)
    idx[4 * W_RWKV + SM_WD:4 * W_RWKV + SM_WD + LORA] = np.arange(3 * W_RWKV, 3 * W_RWKV + LORA)
    idx[4 * W_RWKV + SM_AD:4 * W_RWKV + SM_AD + LORA] = np.arange(3 * W_RWKV + LORA, 3 * W_RWKV + 2 * LORA)
    return idx


def _shift_row_inverse():
    idx = np.zeros((RWKV_COLS,), np.int32)
    idx[0:3 * W_RWKV] = np.arange(0, 3 * W_RWKV)
    idx[3 * W_RWKV:3 * W_RWKV + LORA] = 4 * W_RWKV + SM_WD + np.arange(LORA)
    idx[3 * W_RWKV + LORA:3 * W_RWKV + 2 * LORA] = 4 * W_RWKV + SM_AD + np.arange(LORA)
    idx[3 * W_RWKV + 2 * LORA:] = 3 * W_RWKV + np.arange(W_RWKV)
    return idx


def _placement_constants():
    pq = np.zeros((W_FOX, N_FOX * AUG), np.float32)
    pc = np.zeros((6, SMALL, N_FOX * AUG), np.float32)
    aug1 = np.zeros((2, N_FOX * AUG), np.float32)
    for h in range(N_FOX):
        for d in range(HEAD):
            pq[HEAD * h + d, AUG * h + d] = 1.0
        for j in range(3):
            pc[j, h, AUG * h + HEAD + j] = 1.0
            pc[3 + j, h, AUG * h + HEAD + 3 + j] = 1.0
            aug1[0, AUG * h + HEAD + 3 + j] = 1.0
            aug1[1, AUG * h + HEAD + j] = 1.0
    return pq, pc, aug1


def _block_ones(width):
    h = np.arange(width) // HEAD
    return (h[:, None] == h[None, :]).astype(np.float32)


def _mem_block_diag(mk, mv):
    b = mk.shape[0]
    eye = jnp.eye(N_MEM, dtype=mk.dtype)
    kt = jnp.einsum('bmhd,hg->bhdgm', mk, eye).reshape(b, W_MEM, N_MEM * N_MEM_TOK)
    vb = jnp.einsum('bmhd,hg->bhmgd', mv, eye).reshape(b, N_MEM * N_MEM_TOK, W_MEM)
    return kt.astype(BF16), vb.astype(BF16)


def _layer(x, shift_prev, s0, mk, mv, fox_past, wts, consts, tm, tq, tc, chunk):
    (proj_wts, gnw, gnb, wf, wr, wm) = wts
    (ones384, pq, pc, aug1, shift_idx, shift_inv) = consts
    b, t, _ = x.shape
    shift0 = jnp.take(jnp.pad(shift_prev, ((0, 0), (0, 0), (0, 1))), shift_idx, axis=2)
    mkt, mvb = _mem_block_diag(mk, mv)
    (qa, ka, kn, v, logf, gf, rt, at, kh, bh, rv, lc, gr, bg, om, shift_out, *vt) = _proj(
        x, shift0, mkt, mvb, proj_wts, tm, chunk, with_vt=fox_past is None)
    if fox_past is None:
        of = _fox(qa, ka, vt[0], tq, tq)
    else:
        k_past, v_past, logf_past = fox_past
        lfp =jnp.pad(logf_past.astype(F32), ((0, 0), (0, 0), (0, SMALL - N_FOX)))
        p_len = k_past.shape[1]
        of = _fox_step(vt[0], qa, ka, v, k_past.reshape(b, p_len, W_FOX).astype(BF16),
                       v_past.reshape(b, p_len, W_FOX).astype(BF16), lfp)
    y_r, s_new = _rwkv(rt, at, kh, bh, rv, lc, s0, tc, chunk)
    y = _out(x, of, gf, y_r, gr, bg, om, gnw, gnb, ones384, wf, wr, wm, tm)
    shift_new = jnp.take(shift_out, shift_inv, axis=2)
    return (y, kn.reshape(b, t, N_FOX, HEAD), v.reshape(b, t, N_FOX, HEAD), logf, s_new, shift_new)


def kernel(x_prompt, x_sample, mem_prompt, cache_fox_k, cache_fox_v, cache_fox_logf, cache_mem_k, cache_mem_v, state_rwkv, state_rwkv_shift, norm_g, w_in, fox_q_g, fox_k_g, fox_b_f, rwkv_mu, rwkv_w0, rwkv_w_up, rwkv_a0, rwkv_a_up, rwkv_k_k, rwkv_k_a, rwkv_r_k, rwkv_gn_w, rwkv_gn_b, mem_norm_g, w_mem_kv, mem_q_g, mem_k_g, w_out):
    depth = w_in.shape[0]
    bp = x_prompt.shape[0]
    col_idx = _padded_column_index()
    shift_idx = _shift_row_index()
    shift_inv = _shift_row_inverse()
    pq_np, pc_np, aug1_np = _placement_constants()
    ones384 = jnp.asarray(_block_ones(W_FOX), BF16)
    ones256 = jnp.asarray(_block_ones(W_MEM), BF16)
    pq = jnp.asarray(pq_np, BF16)
    pc = jnp.asarray(pc_np, BF16)
    aug1 = jnp.asarray(aug1_np, F32)
    consts = (ones384, pq, pc, aug1, shift_idx, shift_inv)

    yp, ys = x_prompt, x_sample
    outs = [[] for _ in range(12)]
    for l in range(depth):
        row = lambda a: a[l].reshape(1, -1).astype(F32)
        tile = lambda a, n: jnp.tile(a[l].reshape(1, -1).astype(F32), (1, n))
        w_pad = jnp.take(jnp.pad(w_in[l], ((0, 0), (0, 1))), col_idx, axis=1).astype(BF16)
        mu_pad = jnp.take(jnp.pad(row(rwkv_mu), ((0, 0), (0, 1))), shift_idx, axis=1)
        bf_pad = jnp.pad(row(fox_b_f), ((0, 0), (0, SMALL - N_FOX)))
        wup_pad = jnp.pad(rwkv_w_up[l].astype(F32), ((SM_WD, SMALL - SM_WD - LORA), (0, 0)))
        aup_pad = jnp.pad(rwkv_a_up[l].astype(F32), ((SM_AD, SMALL - SM_AD - LORA), (0, 0)))
        proj_wts = (row(norm_g), w_pad, tile(fox_q_g, N_FOX), tile(fox_k_g, N_FOX), bf_pad, mu_pad,
                    row(rwkv_w0), row(rwkv_a0), wup_pad, aup_pad, row(rwkv_k_k), row(rwkv_k_a), row(rwkv_r_k),
                    tile(mem_q_g, N_MEM), ones384, ones256, pq, pc, aug1)
        wo = w_out[l].astype(BF16)
        wts = (proj_wts, row(rwkv_gn_w), row(rwkv_gn_b), wo[:W_FOX], wo[W_FOX:W_FOX + W_RWKV], wo[W_FOX + W_RWKV:])

        mk2, mv2 = _mem_kv(mem_prompt, row(mem_norm_g), w_mem_kv[l].astype(BF16), tile(mem_k_g, N_MEM), ones256)
        mk = mk2.reshape(bp, N_MEM_TOK, N_MEM, HEAD)
        mv = mv2.reshape(bp, N_MEM_TOK, N_MEM, HEAD)
        shift_zero = jnp.zeros((bp, 1, RWKV_COLS), F32)
        s_zero = jnp.zeros((bp, N_RWKV, HEAD, HEAD), F32)
        yp, k, v, lf, s_new, sh_new = _layer(yp, shift_zero, s_zero, mk, mv, None, wts, consts,
                                             tm=PROMPT_TM, tq=PROMPT_TQ, tc=PROMPT_TC, chunk=RWKV_CHUNK)
        for lst, val in zip(outs[:7], (k, v, lf, mk, mv, s_new, sh_new)):
            lst.append(val)
        s_len = ys.shape[1]
        ys, k, v, lf, s_new, sh_new = _layer(
            ys, state_rwkv_shift[l], state_rwkv[l].astype(F32), cache_mem_k[l], cache_mem_v[l],
            (cache_fox_k[l], cache_fox_v[l], cache_fox_logf[l]), wts, consts,
            tm=s_len, tq=None, tc=s_len, chunk=s_len)
        for lst, val in zip(outs[7:], (k, v, lf, s_new, sh_new)):
            lst.append(val)
    return (yp, ys) + tuple(jnp.stack(o) for o in outs)
```

```python
import functools

import numpy as np
import jax
import jax.numpy as jnp
from jax import lax
from jax.experimental import pallas as pl
from jax.experimental.pallas import tpu as pltpu

F32 = jnp.float32
BF16 = jnp.bfloat16

D_MODEL = 1024
HEAD = 64
N_FOX = 6
N_RWKV = 6
N_MEM = 4
W_FOX = N_FOX * HEAD
W_RWKV = N_RWKV * HEAD
W_MEM = N_MEM * HEAD
N_MEM_TOK = 256
LORA = 32
NORM_EPS = 1e-6
GN_EPS = 64e-5
LOG2E = float(np.log2(np.e))
FOX_COLS = 4 * W_FOX + N_FOX
RWKV_COLS = 4 * W_RWKV + 2 * LORA
MEM_COLS = 2 * W_MEM

LANE = 128
SMALL = LANE
SM_WD = 32
SM_AD = 64
OFF_FQ, OFF_FK, OFF_FV, OFF_FG = 0, 384, 768, 1152
OFF_RR, OFF_RK, OFF_RV, OFF_RG = 1536, 1920, 2304, 2688
OFF_MQ, OFF_MG = 3072, 3328
OFF_SM = 3584
N_PAD = OFF_SM + SMALL
SHIFT_W = 4 * W_RWKV + SMALL
AUG = LANE
BF16_ROWS = 16
VROWS = HEAD + BF16_ROWS
RWKV_CHUNK = 64
VMEM_LIMIT = 56 * 1024 * 1024
PROMPT_TM = 256
PROMPT_TQ = 512
PROMPT_TC = 512
STEP_SUFFIX_BLOCK = 256


def _dot(a, b):
    return jnp.dot(a, b, preferred_element_type=F32)


def _dot_nt(a, b):
    return lax.dot_general(a, b, (((1,), (1,)), ((), ())), preferred_element_type=F32)


def _dot_tn(a, b):
    return lax.dot_general(a, b, (((0,), (0,)), ((), ())), preferred_element_type=F32)


def _split2(x):
    hi = x.astype(BF16)
    lo = (x - hi.astype(F32)).astype(BF16)
    return hi, lo


def _split3(x):
    hi = x.astype(BF16)
    r1 = x - hi.astype(F32)
    mid = r1.astype(BF16)
    lo = (r1 - mid.astype(F32)).astype(BF16)
    return hi, mid, lo


def _dot1(fn, a, b):
    return fn(a.astype(BF16), b.astype(BF16))


def _exact_lhs_dot(m_bf16, x, parts):
    pieces = _split3(x) if parts == 3 else _split2(x)
    acc = _dot(m_bf16, pieces[0])
    for p in pieces[1:]:
        acc = acc + _dot(m_bf16, p)
    return acc


def _segsum(x, ones_bd):
    return _dot(x.astype(BF16), ones_bd)


def _head_rms(t, gain, ones_bd):
    msq = _segsum(t * t, ones_bd) * (1.0 / HEAD)
    return t * lax.rsqrt(msq + NORM_EPS) * gain


def _silu(x):
    return x * jax.nn.sigmoid(x)


def _softplus(z):
    return jnp.maximum(z, 0.0) + jnp.log1p(jnp.exp(-jnp.abs(z)))


def _full(shape):
    n = len(shape)
    return pl.BlockSpec(shape, lambda *_: (0,) * n)


def _params(n_axes):
    return pltpu.CompilerParams(dimension_semantics=("arbitrary",) * n_axes, vmem_limit_bytes=VMEM_LIMIT)


def _mem_kv_body(mem_ref, g_ref, w_ref, kg_ref, ones_ref, k_ref, v_ref):
    x = mem_ref[0]
    ms = jnp.mean(x * x, axis=-1, keepdims=True)
    xn = (x * lax.rsqrt(ms + NORM_EPS) * g_ref[...]).astype(BF16)
    kv = _dot(xn, w_ref[...])
    k_ref[0] = _head_rms(kv[:, :W_MEM], kg_ref[...], ones_ref[...])
    v_ref[0] = kv[:, W_MEM:]


def _mem_kv(mem, g, w_bf16, kg4, ones256):
    b = mem.shape[0]
    blk = pl.BlockSpec((1, N_MEM_TOK, W_MEM), lambda i: (i, 0, 0))
    return pl.pallas_call(
        _mem_kv_body,
        grid=(b,),
        in_specs=[pl.BlockSpec((1, N_MEM_TOK, D_MODEL), lambda i: (i, 0, 0)),
                  _full((1, D_MODEL)), _full((D_MODEL, 2 * W_MEM)), _full((1, W_MEM)), _full((W_MEM, W_MEM))],
        out_specs=[blk, blk],
        out_shape=[jax.ShapeDtypeStruct((b, N_MEM_TOK, W_MEM), F32)] * 2,
        compiler_params=_params(1),
        name="mem_kv",
    )(mem, g, w_bf16, kg4, ones256)


def _proj_body(chunk, with_vt,
               x_ref, shift0_ref, mkt_ref, mvb_ref, ng_ref, w_ref,
               fqg_ref, fkg_ref, bf_ref, mu_ref, w0_ref, a0_ref, wup_ref, aup_ref,
               kk_ref, ka_ref, rk_ref, mqg_ref, ones384_ref, ones256_ref, pq_ref, pc_ref, aug1_ref,
               qa_ref, kaug_ref, kn_ref, v_ref, logf_ref, gf_ref,
               rt_ref, at_ref, kh_ref, bh_ref, rv_ref, lc_ref, gr_ref, bg_ref, om_ref, shift_out_ref,
               *vt_and_scratch):
    extra_ref = vt_and_scratch[0]
    carry_shift, carry_c = vt_and_scratch[-2:]
    i = pl.program_id(1)
    tm = x_ref.shape[1]

    @pl.when(i == 0)
    def _():
        carry_shift[...] = shift0_ref[0]
        carry_c[...] = jnp.zeros_like(carry_c)

    x = x_ref[0]
    ms = jnp.mean(x * x, axis=-1, keepdims=True)
    xn = (x * lax.rsqrt(ms + NORM_EPS) * ng_ref[...]).astype(BF16)

    def proj(off, width):
        return _dot(xn, w_ref[:, off:off + width])

    ones384 = ones384_ref[...]
    row = lax.broadcasted_iota(jnp.int32, (tm, tm), 0)
    col = lax.broadcasted_iota(jnp.int32, (tm, tm), 1)
    tri = jnp.where(col <= row, 1.0, 0.0).astype(BF16)
    tri_chunk = jnp.where((col <= row) & ((row ^ col) < chunk), 1.0, 0.0).astype(BF16)

    hs = proj(OFF_SM, SMALL)
    qn = _head_rms(proj(OFF_FQ, W_FOX), fqg_ref[...], ones384)
    kn = _head_rms(proj(OFF_FK, W_FOX), fkg_ref[...], ones384)
    hv = proj(OFF_FV, W_FOX)
    hg = proj(OFF_FG, W_FOX)
    kn_ref[0] = kn
    v_ref[0] = hv
    if with_vt:
        hvt = hv.T
        ones_row = jnp.where(lax.broadcasted_iota(jnp.int32, (VROWS - HEAD, tm), 0) == 0, 1.0, 0.0)
        extra_ref[0, 0] = jnp.concatenate(
            [blk for h in range(N_FOX) for blk in (hvt[HEAD * h:HEAD * (h + 1)], ones_row)], axis=0).astype(BF16)
    gf_ref[0] = _silu(hg).astype(BF16)
    f = hs + bf_ref[...]
    lane = lax.broadcasted_iota(jnp.int32, (tm, SMALL), 1)
    logf = jnp.where(lane < N_FOX, jnp.minimum(f, 0.0) - jnp.log1p(jnp.exp(-jnp.abs(f))), 0.0)
    logf_ref[0] = logf[:, 0:N_FOX]
    c = _exact_lhs_dot(tri, logf, 3) + carry_c[...]
    carry_c[...] = c[tm - 1:tm, :]
    chi, cmid, clo = _split3(c * LOG2E)
    qs = (qn * (HEAD ** -0.5 * LOG2E)).astype(BF16)
    if not with_vt:
        extra_ref[0] = qs
    qa = (_dot(qs, pq_ref[...]) + _dot(chi, pc_ref[0]) + _dot(cmid, pc_ref[1]) + _dot(clo, pc_ref[2])
          + aug1_ref[0:1, :])
    ka = (_dot(kn.astype(BF16), pq_ref[...]) - (_dot(chi, pc_ref[3]) + _dot(cmid, pc_ref[4]) + _dot(clo, pc_ref[5]))
          + aug1_ref[1:2, :])
    for h in range(N_FOX):
        qa_ref[0, h] = qa[:, AUG * h:AUG * (h + 1)].astype(BF16)
        kaug_ref[0, h] = ka[:, AUG * h:AUG * (h + 1)].astype(BF16)

    qm = _head_rms(proj(OFF_MQ, W_MEM), mqg_ref[...], ones256_ref[...])
    s = _dot((qm * (HEAD ** -0.5)).astype(BF16), mkt_ref[0])
    ps = []
    for h in range(N_MEM):
        sh = s[:, N_MEM_TOK * h:N_MEM_TOK * (h + 1)]
        e = jnp.exp(sh - jnp.max(sh, axis=1, keepdims=True))
        ps.append((e / jnp.sum(e, axis=1, keepdims=True)).astype(BF16))
    om = _dot(jnp.concatenate(ps, axis=1), mvb_ref[0])
    om_ref[0] = (om * _silu(proj(OFF_MG, W_MEM))).astype(BF16)

    first_row = lax.broadcasted_iota(jnp.int32, (tm, 1), 0) == 0

    def tshift(cols, off, width):
        prev = jnp.where(first_row, carry_shift[:, off:off + width], pltpu.roll(cols, 1, 0))
        carry_shift[:, off:off + width] = cols[tm - 1:tm, :]
        return cols + (prev - cols) * mu_ref[:, off:off + width]

    r = tshift(proj(OFF_RR, W_RWKV), 0, W_RWKV)
    k = tshift(proj(OFF_RK, W_RWKV), W_RWKV, W_RWKV)
    v = tshift(proj(OFF_RV, W_RWKV), 2 * W_RWKV, W_RWKV)
    g = tshift(proj(OFF_RG, W_RWKV), 3 * W_RWKV, W_RWKV)
    sm = tshift(hs, 4 * W_RWKV, SMALL)
    shift_out_ref[0] = carry_shift[...]

    w_lin = w0_ref[...] + _dot1(_dot, jnp.tanh(sm), wup_ref[...])
    a = jax.nn.sigmoid(a0_ref[...] + _dot1(_dot, sm, aup_ref[...]))
    lw = -jnp.exp(-_softplus(-w_lin) - 0.5)
    kk = k * kk_ref[...]
    kk = kk * lax.rsqrt(jnp.maximum(_segsum(kk * kk, ones384), 1e-24))
    kt = k * (1.0 + (a - 1.0) * ka_ref[...])
    gr = _silu(g)
    gr_ref[0] = gr.astype(BF16)
    bg_ref[0] = (_segsum(r * kt * rk_ref[...], ones384) * v * gr).astype(BF16)
    lc = _exact_lhs_dot(tri_chunk, lw, 2)
    e_neg = jnp.exp(-lc)
    rt = r * jnp.exp(lc)
    at = -kk * jnp.exp(lc - lw)
    kh = kt * e_neg
    bh = kk * a * e_neg
    rt_ref[0] = rt.astype(BF16)
    at_ref[0] = at.astype(BF16)
    kh_ref[0] = kh.astype(BF16)
    bh_ref[0] = bh.astype(BF16)
    rv_ref[0] = v.astype(BF16)
    lc_ref[0] = lc


def _proj(x, shift0, mkt, mvb, wts, tm, chunk, with_vt):
    b, t, _ = x.shape
    grid = (b, t // tm)
    tok = lambda w: pl.BlockSpec((1, tm, w), lambda bi, i: (bi, i, 0))
    hm = lambda nh, w: pl.BlockSpec((1, nh, tm, w), lambda bi, i: (bi, 0, i, 0))
    per_b = lambda s1, s2: pl.BlockSpec((1, s1, s2), lambda bi, i: (bi, 0, 0))
    w_specs = [_full(a.shape) for a in wts]
    out_shape = [
        jax.ShapeDtypeStruct((b, N_FOX, t, AUG), BF16), jax.ShapeDtypeStruct((b, N_FOX, t, AUG), BF16),
        jax.ShapeDtypeStruct((b, t, W_FOX), F32),
        jax.ShapeDtypeStruct((b, t, W_FOX), F32), jax.ShapeDtypeStruct((b, t, N_FOX), F32),
        jax.ShapeDtypeStruct((b, t, W_FOX), BF16),
    ] + [jax.ShapeDtypeStruct((b, t, W_RWKV), BF16)] * 5 + [jax.ShapeDtypeStruct((b, t, W_RWKV), F32)] + [
        jax.ShapeDtypeStruct((b, t, W_RWKV), BF16), jax.ShapeDtypeStruct((b, t, W_RWKV), BF16),
        jax.ShapeDtypeStruct((b, t, W_MEM), BF16), jax.ShapeDtypeStruct((b, 1, SHIFT_W), F32),
    ]
    out_specs = [hm(N_FOX, AUG), hm(N_FOX, AUG), tok(W_FOX), tok(W_FOX), tok(N_FOX), tok(W_FOX)] \
        + [tok(W_RWKV)] * 6 + [tok(W_RWKV), tok(W_RWKV), tok(W_MEM), per_b(1, SHIFT_W)]
    if with_vt:
        out_shape.append(jax.ShapeDtypeStruct((b, t // tm, N_FOX * VROWS, tm), BF16))
        out_specs.append(pl.BlockSpec((1, 1, N_FOX * VROWS, tm), lambda bi, i: (bi, i, 0, 0)))
    else:
        out_shape.append(jax.ShapeDtypeStruct((b, t, W_FOX), BF16))
        out_specs.append(tok(W_FOX))
    return pl.pallas_call(
        functools.partial(_proj_body, chunk, with_vt),
        grid=grid,
        in_specs=[tok(D_MODEL), per_b(1, SHIFT_W), per_b(W_MEM, N_MEM * N_MEM_TOK),
                  per_b(N_MEM * N_MEM_TOK, W_MEM)] + w_specs,
        out_specs=out_specs,
        out_shape=out_shape,
        scratch_shapes=[pltpu.VMEM((1, SHIFT_W), F32), pltpu.VMEM((1, SMALL), F32)],
        compiler_params=_params(2),
        name="proj",
    )(x, shift0, mkt, mvb, *wts)


def _fox_body(qa_ref, ka_ref, vt_ref, o_ref, m_scr, acc_scr, sa_scr, sb_scr):
    qi = pl.program_id(2)
    tq = qa_ref.shape[2]
    slab = vt_ref.shape[3]
    slabs_per_tile = tq // slab
    m_scr[...] = jnp.full_like(m_scr, -1e30)
    acc_scr[...] = jnp.zeros_like(acc_scr)

    def scores(ki, s_scr):
        rows = pl.ds(pl.multiple_of(ki * tq, tq), tq)
        for hh in range(2):
            s_scr[hh] = _dot_nt(ka_ref[0, hh, rows, :], qa_ref[0, hh])

    def consume(ki, s_scr, masked):
        s = [s_scr[hh] for hh in range(2)]
        if masked:
            kpos = lax.broadcasted_iota(jnp.int32, (tq, tq), 0)
            qpos = lax.broadcasted_iota(jnp.int32, (tq, tq), 1)
            s = [jnp.where(kpos <= qpos, sh, -1e30) for sh in s]
        m_prev = [m_scr[hh] for hh in range(2)]
        m_new = [jnp.maximum(m_prev[hh], jnp.max(s[hh], axis=0, keepdims=True)) for hh in range(2)]
        p = [jnp.exp2(s[hh] - m_new[hh]).astype(BF16) for hh in range(2)]
        pv = []
        for hh in range(2):
            vt = jnp.concatenate([vt_ref[0, ki * slabs_per_tile + d, VROWS * hh:VROWS * (hh + 1), :]
                                  for d in range(slabs_per_tile)], axis=1)
            pv.append(_dot(vt, p[hh]))
        for hh in range(2):
            acc_scr[hh] = jnp.exp2(m_prev[hh] - m_new[hh]) * acc_scr[hh] + pv[hh]
            m_scr[hh] = m_new[hh]

    scores(0, sa_scr)

    def two_tiles(j, carry):
        scores(2 * j + 1, sb_scr)
        consume(2 * j, sa_scr, False)
        scores(2 * j + 2, sa_scr)
        consume(2 * j + 1, sb_scr, False)
        return carry

    lax.fori_loop(0, qi // 2, two_tiles, 0)

    @pl.when(qi % 2 == 0)
    def _():
        consume(qi, sa_scr, True)

    @pl.when(qi % 2 == 1)
    def _():
        scores(qi, sb_scr)
        consume(qi - 1, sa_scr, False)
        consume(qi, sb_scr, True)

    o_t = jnp.concatenate([acc_scr[hh][0:HEAD] / acc_scr[hh][HEAD:HEAD + 1] for hh in range(2)], axis=0)
    o_ref[0] = o_t.T.astype(BF16)


def _fox(qa, ka, vt, tq):
    b, _, t, _ = qa.shape
    n_slab, slab = vt.shape[1], vt.shape[3]
    return pl.pallas_call(
        _fox_body,
        grid=(b, N_FOX // 2, t // tq),
        in_specs=[pl.BlockSpec((1, 2, tq, AUG), lambda bi, p, qi: (bi, p, qi, 0)),
                  pl.BlockSpec((1, 2, t, AUG), lambda bi, p, qi: (bi, p, 0, 0)),
                  pl.BlockSpec((1, n_slab, 2 * VROWS, slab), lambda bi, p, qi: (bi, 0, p, 0))],
        out_specs=pl.BlockSpec((1, tq, 2 * HEAD), lambda bi, p, qi: (bi, qi, p)),
        out_shape=jax.ShapeDtypeStruct((b, t, W_FOX), BF16),
        scratch_shapes=[pltpu.VMEM((2, 1, tq), F32), pltpu.VMEM((2, VROWS, tq), F32),
                        pltpu.VMEM((2, tq, tq), F32), pltpu.VMEM((2, tq, tq), F32)],
        compiler_params=_params(3),
        name="fox",
    )(qa, ka, vt)


def _fox_step_body(qs_ref, qa_ref, ka_ref, vn_ref, kp_ref, vp_ref, lfp_ref, o_ref):
    s_new = qa_ref.shape[2]
    p_len = kp_ref.shape[1]
    blk = min(p_len, STEP_SUFFIX_BLOCK)
    row = lax.broadcasted_iota(jnp.int32, (blk, blk), 0)
    col = lax.broadcasted_iota(jnp.int32, (blk, blk), 1)
    upper = jnp.where(col > row, 1.0, 0.0).astype(BF16)
    later = jnp.zeros((1, SMALL), F32)
    sufs = []
    for bi in reversed(range(p_len // blk)):
        lf = lfp_ref[0, bi * blk:(bi + 1) * blk, :]
        within = _exact_lhs_dot(upper, lf, 3)
        sufs.insert(0, within + later)
        later = later + within[0:1, :] + lf[0:1, :]
    suf_t = (jnp.concatenate(sufs, axis=0) * LOG2E).T
    qs = qs_ref[0]
    head_of_lane = lax.broadcasted_iota(jnp.int32, (s_new, W_FOX), 1) // HEAD
    heads = range(N_FOX)
    q6 = jnp.concatenate([jnp.where(head_of_lane == h, qs, jnp.zeros_like(qs)) for h in heads], axis=0)
    cq, sn, bias = [], [], []
    qrow = lax.broadcasted_iota(jnp.int32, (s_new, s_new), 0)
    kcol = lax.broadcasted_iota(jnp.int32, (s_new, s_new), 1)
    for h in heads:
        q = qa_ref[0, h]
        qf = q.astype(F32)
        cq.append(qf[:, HEAD:HEAD + 1] + qf[:, HEAD + 1:HEAD + 2] + qf[:, HEAD + 2:HEAD + 3])
        sn.append(jnp.where(kcol <= qrow, _dot_nt(q, ka_ref[0, h]), -1e30))
        bias.append(jnp.broadcast_to(suf_t[h:h + 1, :], (s_new, p_len)))
    sp = _dot_nt(q6, kp_ref[0]) + jnp.concatenate(cq, axis=0) + jnp.concatenate(bias, axis=0)
    sn = jnp.concatenate(sn, axis=0)
    m = jnp.maximum(jnp.max(sp, axis=1, keepdims=True), jnp.max(sn, axis=1, keepdims=True))
    pp = jnp.exp2(sp - m)
    pn = jnp.exp2(sn - m)
    l = jnp.sum(pp, axis=1, keepdims=True) + jnp.sum(pn, axis=1, keepdims=True)
    o6 = _dot((pp / l).astype(BF16), vp_ref[0]) + _dot((pn / l).astype(BF16), vn_ref[0].astype(BF16))
    o = jnp.zeros((s_new, W_FOX), F32)
    for h in heads:
        o = jnp.where(head_of_lane == h, o6[h * s_new:(h + 1) * s_new], o)
    o_ref[0] = o.astype(BF16)


def _fox_step(qs, qa, ka, v_new, k_past, v_past, logf_past_pad):
    b, _, s_new, _ = qa.shape
    p_len = k_past.shape[1]
    tok = lambda rows, w: pl.BlockSpec((1, rows, w), lambda i: (i, 0, 0))
    aug = pl.BlockSpec((1, N_FOX, s_new, AUG), lambda i: (i, 0, 0, 0))
    return pl.pallas_call(
        _fox_step_body,
        grid=(b,),
        in_specs=[tok(s_new, W_FOX), aug, aug, tok(s_new, W_FOX), tok(p_len, W_FOX), tok(p_len, W_FOX),
                  tok(p_len, SMALL)],
        out_specs=tok(s_new, W_FOX),
        out_shape=jax.ShapeDtypeStruct((b, s_new, W_FOX), BF16),
        compiler_params=_params(1),
        name="fox_step",
    )(qs, qa, ka, v_new, k_past, v_past, logf_past_pad)


def _rwkv_body(chunk, rt_ref, at_ref, kh_ref, bh_ref, v_ref, lc_ref, s0_ref, y_ref, sout_ref, s_scr):
    j = pl.program_id(1)
    nj = pl.num_programs(1)
    c = chunk
    c2 = 2 * chunk
    n_chunks = rt_ref.shape[1] // c
    n_pairs = N_RWKV // 2
    pair_w = 2 * HEAD

    @pl.when(j == 0)
    def _():
        z = jnp.zeros((HEAD, HEAD), F32)
        for p in range(n_pairs):
            s_scr[p] = jnp.concatenate([jnp.concatenate([s0_ref[0, 2 * p], z], axis=1),
                                        jnp.concatenate([z, s0_ref[0, 2 * p + 1]], axis=1)], axis=0)

    lane_lo = lax.broadcasted_iota(jnp.int32, (c, pair_w), 1) < HEAD
    row = lax.broadcasted_iota(jnp.int32, (c2, c2), 0)
    col = lax.broadcasted_iota(jnp.int32, (c2, c2), 1)
    same = (row >= c) == (col >= c)
    strict = same & (col < row)
    lower = same & (col <= row)
    eye = jnp.where(col == row, 1.0, 0.0)
    srow = lax.broadcasted_iota(jnp.int32, (pair_w, pair_w), 0)
    scol = lax.broadcasted_iota(jnp.int32, (pair_w, pair_w), 1)
    same_state = (srow >= HEAD) == (scol >= HEAD)

    def stack(x):
        zero = jnp.zeros_like(x)
        return jnp.concatenate([jnp.where(lane_lo, x, zero), jnp.where(lane_lo, zero, x)], axis=0)

    def fold(a):
        return a[:c] + a[c:]

    units = [(ci, p) for ci in range(n_chunks) for p in range(n_pairs)]

    def tile(ref, u):
        ci, p = u
        return ref[0, ci * c:(ci + 1) * c, pair_w * p:pair_w * (p + 1)]

    rt = [tile(rt_ref, u) for u in units]
    at2 = [stack(tile(at_ref, u)) for u in units]
    kh = [tile(kh_ref, u) for u in units]
    bh = [tile(bh_ref, u) for u in units]
    v2 = [stack(tile(v_ref, u)) for u in units]
    g = [_dot_nt(jnp.concatenate([at2[i], stack(rt[i])], axis=0),
                 jnp.concatenate([kh[i], kh[i], bh[i], bh[i]], axis=0)) for i in range(len(units))]
    a_ak = [jnp.where(strict, x[:c2, :c2], 0.0) for x in g]
    a_ab = [jnp.where(strict, x[:c2, c2:], 0.0) for x in g]
    a_rk = [fold(jnp.where(lower, x[c2:, :c2], 0.0)) for x in g]
    a_rb = [fold(jnp.where(lower, x[c2:, c2:], 0.0)) for x in g]
    akv2 = [_dot1(_dot, a_ak[i], v2[i]) for i in range(len(units))]
    yv = [_dot1(_dot, a_rk[i], v2[i]) for i in range(len(units))]
    inv = [eye + x for x in a_ab]
    pw = a_ab
    n = 1
    while 2 * n < c:
        pw = [_dot1(_dot, x, x) for x in pw]
        inv = [inv[i] + _dot1(_dot, inv[i], pw[i]) for i in range(len(units))]
        n *= 2
    tw = [_dot(inv[i].astype(BF16), jnp.concatenate([at2[i], akv2[i].astype(BF16)], axis=1))
          for i in range(len(units))]
    w_mat = [fold(x[:, :pair_w]) for x in tw]
    u0 = [fold(x[:, pair_w:]) for x in tw]

    for ci in range(n_chunks):
        idx = [ci * n_pairs + p for p in range(n_pairs)]
        s_old = [s_scr[p] for p in range(n_pairs)]
        pc = [jnp.exp(lc_ref[0, (ci + 1) * c - 1:(ci + 1) * c, pair_w * p:pair_w * (p + 1)]) for p in range(n_pairs)]
        ws = [_dot_nt(jnp.concatenate([w_mat[i].astype(BF16), rt[i]], axis=0), s_old[p].astype(BF16))
              for p, i in enumerate(idx)]
        u = [ws[p][:c] + u0[i] for p, i in enumerate(idx)]
        y = [ws[p][c:] + yv[i] + _dot1(_dot, a_rb[i], stack(u[p])) for p, i in enumerate(idx)]
        upd = [_dot_tn(jnp.concatenate([tile(v_ref, units[i]), u[p].astype(BF16)], axis=0),
                       jnp.concatenate([(kh[i] * pc[p]).astype(BF16), (bh[i] * pc[p]).astype(BF16)], axis=0))
               for p, i in enumerate(idx)]
        for p in range(n_pairs):
            s_scr[p] = s_old[p] * pc[p] + jnp.where(same_state, upd[p], 0.0)
            y_ref[0, ci * c:(ci + 1) * c, pair_w * p:pair_w * (p + 1)] = y[p]

    @pl.when(j == nj - 1)
    def _():
        for p in range(n_pairs):
            s = s_scr[p]
            sout_ref[0, 2 * p] = s[:HEAD, :HEAD]
            sout_ref[0, 2 * p + 1] = s[HEAD:, HEAD:]


def _rwkv(rt, at, kh, bh, v, lc, s0, tc, chunk):
    b, t, _ = rt.shape
    tok = pl.BlockSpec((1, tc, W_RWKV), lambda bi, j: (bi, j, 0))
    st = pl.BlockSpec((1, N_RWKV, HEAD, HEAD), lambda bi, j: (bi, 0, 0, 0))
    return pl.pallas_call(
        functools.partial(_rwkv_body, chunk),
        grid=(b, t // tc),
        in_specs=[tok] * 6 + [st],
        out_specs=[tok, st],
        out_shape=[jax.ShapeDtypeStruct((b, t, W_RWKV), F32), jax.ShapeDtypeStruct((b, N_RWKV, HEAD, HEAD), F32)],
        scratch_shapes=[pltpu.VMEM((N_RWKV // 2, 2 * HEAD, 2 * HEAD), F32)],
        compiler_params=_params(2),
        name="rwkv",
    )(rt, at, kh, bh, v, lc, s0)


def _out_body(x_ref, of_ref, gf_ref, y_ref, gr_ref, bg_ref, om_ref, gnw_ref, gnb_ref, ones_ref,
              wf_ref, wr_ref, wm_ref, o_ref):
    ones384 = ones_ref[...]
    y = y_ref[0]
    mu = _segsum(y, ones384) * (1.0 / HEAD)
    d = y - mu
    var = _segsum(d * d, ones384) * (1.0 / HEAD)
    yn = d * lax.rsqrt(var + GN_EPS) * gnw_ref[...] + gnb_ref[...]
    o_r = yn * gr_ref[0].astype(F32) + bg_ref[0].astype(F32)
    acc = _dot((of_ref[0].astype(F32) * gf_ref[0].astype(F32)).astype(BF16), wf_ref[...])
    acc = acc + _dot(o_r.astype(BF16), wr_ref[...])
    acc = acc + _dot(om_ref[0], wm_ref[...])
    o_ref[0] = x_ref[0] + acc


def _out(x, of, gf, y, gr, bg, om, gnw, gnb, ones384, wf, wr, wm, tm):
    b, t, _ = x.shape
    tok = lambda w: pl.BlockSpec((1, tm, w), lambda bi, i: (bi, i, 0))
    return pl.pallas_call(
        _out_body,
        grid=(b, t // tm),
        in_specs=[tok(D_MODEL), tok(W_FOX), tok(W_FOX), tok(W_RWKV), tok(W_RWKV), tok(W_RWKV), tok(W_MEM),
                  _full(gnw.shape), _full(gnb.shape), _full(ones384.shape),
                  _full(wf.shape), _full(wr.shape), _full(wm.shape)],
        out_specs=tok(D_MODEL),
        out_shape=jax.ShapeDtypeStruct((b, t, D_MODEL), F32),
        compiler_params=_params(2),
        name="out_proj",
    )(x, of, gf, y, gr, bg, om, gnw, gnb, ones384, wf, wr, wm)


def _padded_column_index():
    idx = np.full((N_PAD,), FOX_COLS + RWKV_COLS + MEM_COLS, np.int32)
    r0 = FOX_COLS
    m0 = FOX_COLS + RWKV_COLS
    idx[OFF_FQ:OFF_FQ + W_FOX] = np.arange(0, W_FOX)
    idx[OFF_FK:OFF_FK + W_FOX] = np.arange(W_FOX, 2 * W_FOX)
    idx[OFF_FV:OFF_FV + W_FOX] = np.arange(2 * W_FOX, 3 * W_FOX)
    idx[OFF_FG:OFF_FG + W_FOX] = np.arange(3 * W_FOX + N_FOX, 4 * W_FOX + N_FOX)
    idx[OFF_RR:OFF_RR + W_RWKV] = r0 + np.arange(0, W_RWKV)
    idx[OFF_RK:OFF_RK + W_RWKV] = r0 + np.arange(W_RWKV, 2 * W_RWKV)
    idx[OFF_RV:OFF_RV + W_RWKV] = r0 + np.arange(2 * W_RWKV, 3 * W_RWKV)
    idx[OFF_RG:OFF_RG + W_RWKV] = r0 + np.arange(3 * W_RWKV + 2 * LORA, 4 * W_RWKV + 2 * LORA)
    idx[OFF_MQ:OFF_MQ + W_MEM] = m0 + np.arange(0, W_MEM)
    idx[OFF_MG:OFF_MG + W_MEM] = m0 + np.arange(W_MEM, 2 * W_MEM)
    idx[OFF_SM:OFF_SM + N_FOX] = np.arange(3 * W_FOX, 3 * W_FOX + N_FOX)
    idx[OFF_SM + SM_WD:OFF_SM + SM_WD + LORA] = r0 + np.arange(3 * W_RWKV, 3 * W_RWKV + LORA)
    idx[OFF_SM + SM_AD:OFF_SM + SM_AD + LORA] = r0 + np.arange(3 * W_RWKV + LORA, 3 * W_RWKV + 2 * LORA)
    return idx


def _shift_row_index():
    idx = np.full((SHIFT_W,), RWKV_COLS, np.int32)
    idx[0:3 * W_RWKV] = np.arange(0, 3 * W_RWKV)
    idx[3 * W_RWKV:4 * W_RWKV] = np.arange(3 * W_RWKV + 2 * LORA, 4 * W_RWKV + 2 * LORA)
    idx[4 * W_RWKV + SM_WD:4 * W_RWKV + SM_WD + LORA] = np.arange(3 * W_RWKV, 3 * W_RWKV + LORA)
    idx[4 * W_RWKV + SM_AD:4 * W_RWKV + SM_AD + LORA] = np.arange(3 * W_RWKV + LORA, 3 * W_RWKV + 2 * LORA)
    return idx


def _shift_row_inverse():
    idx = np.zeros((RWKV_COLS,), np.int32)
    idx[0:3 * W_RWKV] = np.arange(0, 3 * W_RWKV)
    idx[3 * W_RWKV:3 * W_RWKV + LORA] = 4 * W_RWKV + SM_WD + np.arange(LORA)
    idx[3 * W_RWKV + LORA:3 * W_RWKV + 2 * LORA] = 4 * W_RWKV + SM_AD + np.arange(LORA)
    idx[3 * W_RWKV + 2 * LORA:] = 3 * W_RWKV + np.arange(W_RWKV)
    return idx


def _placement_constants():
    pq = np.zeros((W_FOX, N_FOX * AUG), np.float32)
    pc = np.zeros((6, SMALL, N_FOX * AUG), np.float32)
    aug1 = np.zeros((2, N_FOX * AUG), np.float32)
    for h in range(N_FOX):
        for d in range(HEAD):
            pq[HEAD * h + d, AUG * h + d] = 1.0
        for j in range(3):
            pc[j, h, AUG * h + HEAD + j] = 1.0
            pc[3 + j, h, AUG * h + HEAD + 3 + j] = 1.0
            aug1[0, AUG * h + HEAD + 3 + j] = 1.0
            aug1[1, AUG * h + HEAD + j] = 1.0
    return pq, pc, aug1


def _block_ones(width):
    h = np.arange(width) // HEAD
    return (h[:, None] == h[None, :]).astype(np.float32)


def _mem_block_diag(mk, mv):
    b = mk.shape[0]
    eye = jnp.eye(N_MEM, dtype=mk.dtype)
    kt = jnp.einsum('bmhd,hg->bhdgm', mk, eye).reshape(b, W_MEM, N_MEM * N_MEM_TOK)
    vb = jnp.einsum('bmhd,hg->bhmgd', mv, eye).reshape(b, N_MEM * N_MEM_TOK, W_MEM)
    return kt.astype(BF16), vb.astype(BF16)


def _layer(x, shift_prev, s0, mk, mv, fox_past, wts, consts, tm, tq, tc, chunk):
    (proj_wts, gnw, gnb, wf, wr, wm) = wts
    (ones384, pq, pc, aug1, shift_idx, shift_inv) = consts
    b, t, _ = x.shape
    shift0 = jnp.take(jnp.pad(shift_prev, ((0, 0), (0, 0), (0, 1))), shift_idx, axis=2)
    mkt, mvb = _mem_block_diag(mk, mv)
    (qa, ka, kn, v, logf, gf, rt, at, kh, bh, rv, lc, gr, bg, om, shift_out, *vt) = _proj(
        x, shift0, mkt, mvb, proj_wts, tm, chunk, with_vt=fox_past is None)
    if fox_past is None:
        of = _fox(qa, ka, vt[0], tq)
    else:
        k_past, v_past, logf_past = fox_past
        lfp =jnp.pad(logf_past.astype(F32), ((0, 0), (0, 0), (0, SMALL - N_FOX)))
        p_len = k_past.shape[1]
        of = _fox_step(vt[0], qa, ka, v, k_past.reshape(b, p_len, W_FOX).astype(BF16),
                       v_past.reshape(b, p_len, W_FOX).astype(BF16), lfp)
    y_r, s_new = _rwkv(rt, at, kh, bh, rv, lc, s0, tc, chunk)
    y = _out(x, of, gf, y_r, gr, bg, om, gnw, gnb, ones384, wf, wr, wm, tm)
    shift_new = jnp.take(shift_out, shift_inv, axis=2)
    return (y, kn.reshape(b, t, N_FOX, HEAD), v.reshape(b, t, N_FOX, HEAD), logf, s_new, shift_new)


def kernel(x_prompt, x_sample, mem_prompt, cache_fox_k, cache_fox_v, cache_fox_logf, cache_mem_k, cache_mem_v, state_rwkv, state_rwkv_shift, norm_g, w_in, fox_q_g, fox_k_g, fox_b_f, rwkv_mu, rwkv_w0, rwkv_w_up, rwkv_a0, rwkv_a_up, rwkv_k_k, rwkv_k_a, rwkv_r_k, rwkv_gn_w, rwkv_gn_b, mem_norm_g, w_mem_kv, mem_q_g, mem_k_g, w_out):
    depth = w_in.shape[0]
    bp = x_prompt.shape[0]
    col_idx = _padded_column_index()
    shift_idx = _shift_row_index()
    shift_inv = _shift_row_inverse()
    pq_np, pc_np, aug1_np = _placement_constants()
    ones384 = jnp.asarray(_block_ones(W_FOX), BF16)
    ones256 = jnp.asarray(_block_ones(W_MEM), BF16)
    pq = jnp.asarray(pq_np, BF16)
    pc = jnp.asarray(pc_np, BF16)
    aug1 = jnp.asarray(aug1_np, F32)
    consts = (ones384, pq, pc, aug1, shift_idx, shift_inv)

    yp, ys = x_prompt, x_sample
    outs = [[] for _ in range(12)]
    for l in range(depth):
        row = lambda a: a[l].reshape(1, -1).astype(F32)
        tile = lambda a, n: jnp.tile(a[l].reshape(1, -1).astype(F32), (1, n))
        w_pad = jnp.take(jnp.pad(w_in[l], ((0, 0), (0, 1))), col_idx, axis=1).astype(BF16)
        mu_pad = jnp.take(jnp.pad(row(rwkv_mu), ((0, 0), (0, 1))), shift_idx, axis=1)
        bf_pad = jnp.pad(row(fox_b_f), ((0, 0), (0, SMALL - N_FOX)))
        wup_pad = jnp.pad(rwkv_w_up[l].astype(F32), ((SM_WD, SMALL - SM_WD - LORA), (0, 0)))
        aup_pad = jnp.pad(rwkv_a_up[l].astype(F32), ((SM_AD, SMALL - SM_AD - LORA), (0, 0)))
        proj_wts = (row(norm_g), w_pad, tile(fox_q_g, N_FOX), tile(fox_k_g, N_FOX), bf_pad, mu_pad,
                    row(rwkv_w0), row(rwkv_a0), wup_pad, aup_pad, row(rwkv_k_k), row(rwkv_k_a), row(rwkv_r_k),
                    tile(mem_q_g, N_MEM), ones384, ones256, pq, pc, aug1)
        wo = w_out[l].astype(BF16)
        wts = (proj_wts, row(rwkv_gn_w), row(rwkv_gn_b), wo[:W_FOX], wo[W_FOX:W_FOX + W_RWKV], wo[W_FOX + W_RWKV:])

        mk2, mv2 = _mem_kv(mem_prompt, row(mem_norm_g), w_mem_kv[l].astype(BF16), tile(mem_k_g, N_MEM), ones256)
        mk = mk2.reshape(bp, N_MEM_TOK, N_MEM, HEAD)
        mv = mv2.reshape(bp, N_MEM_TOK, N_MEM, HEAD)
        shift_zero = jnp.zeros((bp, 1, RWKV_COLS), F32)
        s_zero = jnp.zeros((bp, N_RWKV, HEAD, HEAD), F32)
        yp, k, v, lf, s_new, sh_new = _layer(yp, shift_zero, s_zero, mk, mv, None, wts, consts,
                                             tm=PROMPT_TM, tq=PROMPT_TQ, tc=PROMPT_TC, chunk=RWKV_CHUNK)
        for lst, val in zip(outs[:7], (k, v, lf, mk, mv, s_new, sh_new)):
            lst.append(val)
        s_len = ys.shape[1]
        ys, k, v, lf, s_new, sh_new = _layer(
            ys, state_rwkv_shift[l], state_rwkv[l].astype(F32), cache_mem_k[l], cache_mem_v[l],
            (cache_fox_k[l], cache_fox_v[l], cache_fox_logf[l]), wts, consts,
            tm=s_len, tq=None, tc=s_len, chunk=s_len)
        for lst, val in zip(outs[7:], (k, v, lf, s_new, sh_new)):
            lst.append(val)
    return (yp, ys) + tuple(jnp.stack(o) for o in outs)
```

```python
import functools

import numpy as np
import jax
import jax.numpy as jnp
from jax import lax
from jax.experimental import pallas as pl
from jax.experimental.pallas import tpu as pltpu

F32 = jnp.float32
BF16 = jnp.bfloat16

D_MODEL = 1024
HEAD = 64
N_FOX = 6
N_RWKV = 6
N_MEM = 4
W_FOX = N_FOX * HEAD
W_RWKV = N_RWKV * HEAD
W_MEM = N_MEM * HEAD
N_MEM_TOK = 256
LORA = 32
NORM_EPS = 1e-6
GN_EPS = 64e-5
LOG2E = float(np.log2(np.e))
FOX_COLS = 4 * W_FOX + N_FOX
RWKV_COLS = 4 * W_RWKV + 2 * LORA
MEM_COLS = 2 * W_MEM

LANE = 128
PAIR = 2 * HEAD
SMALL = LANE
SM_WD = 32
SM_AD = 64
OFF_FQ, OFF_FK, OFF_FV, OFF_FG = 0, 384, 768, 1152
OFF_RR, OFF_RK, OFF_RV, OFF_RG = 1536, 1920, 2304, 2688
OFF_MQ, OFF_MG = 3072, 3328
OFF_SM = 3584
N_PAD = OFF_SM + SMALL
SHIFT_W = 4 * W_RWKV + SMALL
N_BIAS = 3
BIAS_STRIDE = 8
BF16_ROWS = 16
VROWS = HEAD + BF16_ROWS
RWKV_CHUNK = 64
VMEM_LIMIT = 56 * 1024 * 1024
PROMPT_TM = 512
PROMPT_TQ = 512
PROMPT_TC = 512
CUMSUM_BLOCK = 256
STEP_SUFFIX_BLOCK = 256


def _dot(a, b):
    return jnp.dot(a, b, preferred_element_type=F32)


def _dot_nt(a, b):
    return lax.dot_general(a, b, (((1,), (1,)), ((), ())), preferred_element_type=F32)


def _dot_tn(a, b):
    return lax.dot_general(a, b, (((0,), (0,)), ((), ())), preferred_element_type=F32)


def _split2(x):
    hi = x.astype(BF16)
    lo = (x - hi.astype(F32)).astype(BF16)
    return hi, lo


def _split3(x):
    hi = x.astype(BF16)
    r1 = x - hi.astype(F32)
    mid = r1.astype(BF16)
    lo = (r1 - mid.astype(F32)).astype(BF16)
    return hi, mid, lo


def _dot1(fn, a, b):
    return fn(a.astype(BF16), b.astype(BF16))


def _exact_lhs_dot(m_bf16, x, parts):
    pieces = _split3(x) if parts == 3 else _split2(x)
    acc = _dot(m_bf16, pieces[0])
    for p in pieces[1:]:
        acc = acc + _dot(m_bf16, p)
    return acc


def _segsum(x, ones_bd):
    return _dot(x.astype(BF16), ones_bd)


def _head_rms(t, gain, ones_bd):
    msq = _segsum(t * t, ones_bd) * (1.0 / HEAD)
    return t * lax.rsqrt(msq + NORM_EPS) * gain


def _silu(x):
    return x * jax.nn.sigmoid(x)


def _softplus(z):
    return jnp.maximum(z, 0.0) + jnp.log1p(jnp.exp(-jnp.abs(z)))


def _stack_heads(x, n_heads):
    head_of_lane = lax.broadcasted_iota(jnp.int32, x.shape, 1) // HEAD
    zero = jnp.zeros_like(x)
    return jnp.concatenate([jnp.where(head_of_lane == h, x, zero) for h in range(n_heads)], axis=0)


def _unstack_heads(x6, n_heads):
    rows = x6.shape[0] // n_heads
    head_of_lane = lax.broadcasted_iota(jnp.int32, (rows, x6.shape[1]), 1) // HEAD
    out = jnp.zeros((rows, x6.shape[1]), x6.dtype)
    for h in range(n_heads):
        out = jnp.where(head_of_lane == h, x6[h * rows:(h + 1) * rows], out)
    return out


def _full(shape):
    n = len(shape)
    return pl.BlockSpec(shape, lambda *_: (0,) * n)


def _params(n_axes):
    return pltpu.CompilerParams(dimension_semantics=("arbitrary",) * n_axes, vmem_limit_bytes=VMEM_LIMIT)


def _mem_kv_body(mem_ref, g_ref, w_ref, kg_ref, ones_ref, k_ref, v_ref, kb_ref, vb_ref):
    x = mem_ref[0]
    ms = jnp.mean(x * x, axis=-1, keepdims=True)
    xn = (x * lax.rsqrt(ms + NORM_EPS) * g_ref[...]).astype(BF16)
    kv = _dot(xn, w_ref[...])
    k = _head_rms(kv[:, :W_MEM], kg_ref[...], ones_ref[...])
    v = kv[:, W_MEM:]
    k_ref[0] = k
    v_ref[0] = v
    kb_ref[0] = k.astype(BF16)
    vb_ref[0] = v.astype(BF16)


def _mem_kv(mem, g, w_bf16, kg4, ones256):
    b = mem.shape[0]
    blk = pl.BlockSpec((1, N_MEM_TOK, W_MEM), lambda i: (i, 0, 0))
    return pl.pallas_call(
        _mem_kv_body,
        grid=(b,),
        in_specs=[pl.BlockSpec((1, N_MEM_TOK, D_MODEL), lambda i: (i, 0, 0)),
                  _full((1, D_MODEL)), _full((D_MODEL, 2 * W_MEM)), _full((1, W_MEM)), _full((W_MEM, W_MEM))],
        out_specs=[blk] * 4,
        out_shape=[jax.ShapeDtypeStruct((b, N_MEM_TOK, W_MEM), F32)] * 2
        + [jax.ShapeDtypeStruct((b, N_MEM_TOK, W_MEM), BF16)] * 2,
        compiler_params=_params(1),
        name="mem_kv",
    )(mem, g, w_bf16, kg4, ones256)


def _proj_body(chunk, with_vt,
               x_ref, shift0_ref, mk_ref, mv_ref, ng_ref, w_ref,
               fqg_ref, fkg_ref, bf_ref, mu_ref, w0_ref, a0_ref, wup_ref, aup_ref,
               kk_ref, ka_ref, rk_ref, mqg_ref, ones384_ref, ones256_ref, pbias_ref, bias1_ref,
               qs_ref, kb_ref, augq_ref, augk_ref, kn_ref, v_ref, logf_ref, gf_ref,
               rt_ref, at_ref, kh_ref, bh_ref, rv_ref, lc_ref, gr_ref, bg_ref, om_ref, shift_out_ref,
               *vt_and_scratch):
    carry_shift, carry_c = vt_and_scratch[-2:]
    i = pl.program_id(1)
    tm = x_ref.shape[1]
    sub = min(tm, CUMSUM_BLOCK)

    @pl.when(i == 0)
    def _():
        carry_shift[...] = shift0_ref[0]
        carry_c[...] = jnp.zeros_like(carry_c)

    x = x_ref[0]
    ms = jnp.mean(x * x, axis=-1, keepdims=True)
    xn = (x * lax.rsqrt(ms + NORM_EPS) * ng_ref[...]).astype(BF16)

    def proj(off, width):
        return _dot(xn, w_ref[:, off:off + width])

    ones384 = ones384_ref[...]
    row = lax.broadcasted_iota(jnp.int32, (sub, sub), 0)
    col = lax.broadcasted_iota(jnp.int32, (sub, sub), 1)
    tri = jnp.where(col <= row, 1.0, 0.0).astype(BF16)
    tri_chunk = jnp.where((col <= row) & ((row ^ col) < chunk), 1.0, 0.0).astype(BF16)
    blocks = [slice(s * sub, (s + 1) * sub) for s in range(tm // sub)]

    hs = proj(OFF_SM, SMALL)
    qn = _head_rms(proj(OFF_FQ, W_FOX), fqg_ref[...], ones384)
    kn = _head_rms(proj(OFF_FK, W_FOX), fkg_ref[...], ones384)
    hv = proj(OFF_FV, W_FOX)
    hg = proj(OFF_FG, W_FOX)
    kn_ref[0] = kn
    kb_ref[0] = kn.astype(BF16)
    qs_ref[0] = (qn * (HEAD ** -0.5 * LOG2E)).astype(BF16)
    v_ref[0] = hv
    if with_vt:
        hvt = hv.T
        ones_row = jnp.where(lax.broadcasted_iota(jnp.int32, (VROWS - HEAD, tm), 0) == 0, 1.0, 0.0)
        vt_and_scratch[0][0, 0] = jnp.concatenate(
            [blk for h in range(N_FOX) for blk in (hvt[HEAD * h:HEAD * (h + 1)], ones_row)], axis=0).astype(BF16)
    gf_ref[0] = _silu(hg).astype(BF16)
    f = hs + bf_ref[...]
    lane = lax.broadcasted_iota(jnp.int32, (tm, SMALL), 1)
    logf = jnp.where(lane < N_FOX, jnp.minimum(f, 0.0) - jnp.log1p(jnp.exp(-jnp.abs(f))), 0.0)
    logf_ref[0] = logf[:, 0:N_FOX]
    carry = carry_c[...]
    c_blocks = []
    for blk in blocks:
        cb = _exact_lhs_dot(tri, logf[blk], 3) + carry
        carry = cb[sub - 1:sub, :]
        c_blocks.append(cb)
    carry_c[...] = carry
    c = jnp.concatenate(c_blocks, axis=0)
    pieces = _split3(c * LOG2E)
    packed = pieces[0].astype(F32)
    for j in range(1, N_BIAS):
        packed = packed + pltpu.roll(pieces[j].astype(F32), BIAS_STRIDE * j, 1)
    aug = _dot(packed.astype(BF16), pbias_ref[...]) + bias1_ref[...]
    augq_ref[0] = aug[:, :W_FOX].astype(BF16)
    augk_ref[0] = aug[:, W_FOX:].astype(BF16)

    qm = _head_rms(proj(OFF_MQ, W_MEM), mqg_ref[...], ones256_ref[...])
    s4 = _dot_nt(_stack_heads((qm * (HEAD ** -0.5 * LOG2E)).astype(BF16), N_MEM), mk_ref[0])
    e = jnp.exp2(s4 - jnp.max(s4, axis=1, keepdims=True))
    p4 = (e / jnp.sum(e, axis=1, keepdims=True)).astype(BF16)
    om = _unstack_heads(_dot(p4, mv_ref[0]), N_MEM)
    om_ref[0] = (om * _silu(proj(OFF_MG, W_MEM))).astype(BF16)

    first_row = lax.broadcasted_iota(jnp.int32, (tm, 1), 0) == 0

    def tshift(cols, off, width):
        prev = jnp.where(first_row, carry_shift[:, off:off + width], pltpu.roll(cols, 1, 0))
        carry_shift[:, off:off + width] = cols[tm - 1:tm, :]
        return cols + (prev - cols) * mu_ref[:, off:off + width]

    r = tshift(proj(OFF_RR, W_RWKV), 0, W_RWKV)
    k = tshift(proj(OFF_RK, W_RWKV), W_RWKV, W_RWKV)
    v = tshift(proj(OFF_RV, W_RWKV), 2 * W_RWKV, W_RWKV)
    g = tshift(proj(OFF_RG, W_RWKV), 3 * W_RWKV, W_RWKV)
    sm = tshift(hs, 4 * W_RWKV, SMALL)
    shift_out_ref[0] = carry_shift[...]

    w_lin = w0_ref[...] + _dot1(_dot, jnp.tanh(sm), wup_ref[...])
    a = jax.nn.sigmoid(a0_ref[...] + _dot1(_dot, sm, aup_ref[...]))
    lw = -jnp.exp(-_softplus(-w_lin) - 0.5)
    kk = k * kk_ref[...]
    kk = kk * lax.rsqrt(jnp.maximum(_segsum(kk * kk, ones384), 1e-24))
    kt = k * (1.0 + (a - 1.0) * ka_ref[...])
    gr = _silu(g)
    gr_ref[0] = gr.astype(BF16)
    bg_ref[0] = (_segsum(r * kt * rk_ref[...], ones384) * v * gr).astype(BF16)
    lc = jnp.concatenate([_exact_lhs_dot(tri_chunk, lw[blk], 2) for blk in blocks], axis=0)
    e_neg = jnp.exp(-lc)
    rt_ref[0] = (r * jnp.exp(lc)).astype(BF16)
    at_ref[0] = (-kk * jnp.exp(lc - lw)).astype(BF16)
    kh_ref[0] = (kt * e_neg).astype(BF16)
    bh_ref[0] = (kk * a * e_neg).astype(BF16)
    rv_ref[0] = v.astype(BF16)
    lc_ref[0] = lc


def _proj(x, shift0, mk, mv, wts, tm, chunk, with_vt):
    b, t, _ = x.shape
    grid = (b, t // tm)
    tok = lambda w: pl.BlockSpec((1, tm, w), lambda bi, i: (bi, i, 0))
    per_b = lambda s1, s2: pl.BlockSpec((1, s1, s2), lambda bi, i: (bi, 0, 0))
    w_specs = [_full(a.shape) for a in wts]
    bf = lambda w: jax.ShapeDtypeStruct((b, t, w), BF16)
    f32 = lambda w: jax.ShapeDtypeStruct((b, t, w), F32)
    out_shape = [bf(W_FOX)] * 4 + [f32(W_FOX), f32(W_FOX), f32(N_FOX), bf(W_FOX)] \
        + [bf(W_RWKV)] * 5 + [f32(W_RWKV)] + [bf(W_RWKV), bf(W_RWKV), bf(W_MEM),
                                               jax.ShapeDtypeStruct((b, 1, SHIFT_W), F32)]
    out_specs = [tok(W_FOX)] * 4 + [tok(W_FOX), tok(W_FOX), tok(N_FOX), tok(W_FOX)] \
        + [tok(W_RWKV)] * 6 + [tok(W_RWKV), tok(W_RWKV), tok(W_MEM), per_b(1, SHIFT_W)]
    if with_vt:
        out_shape.append(jax.ShapeDtypeStruct((b, t // tm, N_FOX * VROWS, tm), BF16))
        out_specs.append(pl.BlockSpec((1, 1, N_FOX * VROWS, tm), lambda bi, i: (bi, i, 0, 0)))
    return pl.pallas_call(
        functools.partial(_proj_body, chunk, with_vt),
        grid=grid,
        in_specs=[tok(D_MODEL), per_b(1, SHIFT_W), per_b(N_MEM_TOK, W_MEM), per_b(N_MEM_TOK, W_MEM)] + w_specs,
        out_specs=out_specs,
        out_shape=out_shape,
        scratch_shapes=[pltpu.VMEM((1, SHIFT_W), F32), pltpu.VMEM((1, SMALL), F32)],
        compiler_params=_params(2),
        name="proj",
    )(x, shift0, mk, mv, *wts)


def _fox_body(qs_ref, aq_ref, kb_ref, ak_ref, vt_ref, o_ref, m_scr, acc_scr, sa_scr, sb_scr, qm_scr):
    qi = pl.program_id(2)
    tq = qs_ref.shape[1]
    slab = vt_ref.shape[3]
    slabs_per_tile = tq // slab
    m_scr[...] = jnp.full_like(m_scr, -1e30)
    acc_scr[...] = jnp.zeros_like(acc_scr)
    q = qs_ref[0]
    aq = aq_ref[0]
    lane_lo = lax.broadcasted_iota(jnp.int32, (tq, PAIR), 1) < HEAD
    zero = jnp.zeros_like(q)
    qm_scr[0] = jnp.concatenate([jnp.where(lane_lo, q, zero), jnp.where(lane_lo, aq, zero)], axis=1)
    qm_scr[1] = jnp.concatenate([jnp.where(lane_lo, zero, q), jnp.where(lane_lo, zero, aq)], axis=1)

    def scores(ki, s_scr):
        rows = pl.ds(pl.multiple_of(ki * tq, tq), tq)
        k2 = jnp.concatenate([kb_ref[0, rows, :], ak_ref[0, rows, :]], axis=1)
        for hh in range(2):
            s_scr[hh] = _dot_nt(k2, qm_scr[hh])

    def consume(ki, s_scr, masked):
        s = [s_scr[hh] for hh in range(2)]
        if masked:
            kpos = lax.broadcasted_iota(jnp.int32, (tq, tq), 0)
            qpos = lax.broadcasted_iota(jnp.int32, (tq, tq), 1)
            s = [jnp.where(kpos <= qpos, sh, -1e30) for sh in s]
        m_prev = [m_scr[hh] for hh in range(2)]
        m_new = [jnp.maximum(m_prev[hh], jnp.max(s[hh], axis=0, keepdims=True)) for hh in range(2)]
        p = [jnp.exp2(s[hh] - m_new[hh]).astype(BF16) for hh in range(2)]
        pv = []
        for hh in range(2):
            vt = jnp.concatenate([vt_ref[0, ki * slabs_per_tile + d, VROWS * hh:VROWS * (hh + 1), :]
                                  for d in range(slabs_per_tile)], axis=1)
            pv.append(_dot(vt, p[hh]))
        for hh in range(2):
            acc_scr[hh] = jnp.exp2(m_prev[hh] - m_new[hh]) * acc_scr[hh] + pv[hh]
            m_scr[hh] = m_new[hh]

    scores(0, sa_scr)

    def two_tiles(j, carry):
        scores(2 * j + 1, sb_scr)
        consume(2 * j, sa_scr, False)
        scores(2 * j + 2, sa_scr)
        consume(2 * j + 1, sb_scr, False)
        return carry

    lax.fori_loop(0, qi // 2, two_tiles, 0)

    @pl.when(qi % 2 == 0)
    def _():
        consume(qi, sa_scr, True)

    @pl.when(qi % 2 == 1)
    def _():
        scores(qi, sb_scr)
        consume(qi - 1, sa_scr, False)
        consume(qi, sb_scr, True)

    o_t = jnp.concatenate([acc_scr[hh][0:HEAD] / acc_scr[hh][HEAD:HEAD + 1] for hh in range(2)], axis=0)
    o_ref[0] = o_t.T.astype(BF16)


def _fox(qs, augq, kb, augk, vt, tq):
    b, t, _ = qs.shape
    n_slab, slab = vt.shape[1], vt.shape[3]
    q_blk = pl.BlockSpec((1, tq, PAIR), lambda bi, p, qi: (bi, qi, p))
    k_blk = pl.BlockSpec((1, t, PAIR), lambda bi, p, qi: (bi, 0, p))
    return pl.pallas_call(
        _fox_body,
        grid=(b, N_FOX // 2, t // tq),
        in_specs=[q_blk, q_blk, k_blk, k_blk,
                  pl.BlockSpec((1, n_slab, 2 * VROWS, slab), lambda bi, p, qi: (bi, 0, p, 0))],
        out_specs=q_blk,
        out_shape=jax.ShapeDtypeStruct((b, t, W_FOX), BF16),
        scratch_shapes=[pltpu.VMEM((2, 1, tq), F32), pltpu.VMEM((2, VROWS, tq), F32),
                        pltpu.VMEM((2, tq, tq), F32), pltpu.VMEM((2, tq, tq), F32),
                        pltpu.VMEM((2, tq, 2 * PAIR), BF16)],
        compiler_params=_params(3),
        name="fox",
    )(qs, augq, kb, augk, vt)


def _fox_step_body(qs_ref, aq_ref, kn_ref, ak_ref, vn_ref, kp_ref, vp_ref, lfp_ref, o_ref):
    s_new = qs_ref.shape[1]
    p_len = kp_ref.shape[1]
    blk = min(p_len, STEP_SUFFIX_BLOCK)
    row = lax.broadcasted_iota(jnp.int32, (blk, blk), 0)
    col = lax.broadcasted_iota(jnp.int32, (blk, blk), 1)
    upper = jnp.where(col > row, 1.0, 0.0).astype(BF16)
    later = jnp.zeros((1, SMALL), F32)
    sufs = []
    for bi in reversed(range(p_len // blk)):
        lf = lfp_ref[0, bi * blk:(bi + 1) * blk, :]
        within = _exact_lhs_dot(upper, lf, 3)
        sufs.insert(0, within + later)
        later = later + within[0:1, :] + lf[0:1, :]
    suf_t = (jnp.concatenate(sufs, axis=0) * LOG2E).T
    q6 = _stack_heads(qs_ref[0], N_FOX)
    a6 = _stack_heads(aq_ref[0], N_FOX)
    lane_in_head = lax.broadcasted_iota(jnp.int32, a6.shape, 1) % HEAD
    cq6 = jnp.sum(jnp.where(lane_in_head < N_BIAS, a6.astype(F32), 0.0), axis=1, keepdims=True)
    bias6 = jnp.concatenate([jnp.broadcast_to(suf_t[h:h + 1, :], (s_new, p_len)) for h in range(N_FOX)], axis=0)
    sp = _dot_nt(q6, kp_ref[0]) + cq6 + bias6
    qrow = lax.broadcasted_iota(jnp.int32, (N_FOX * s_new, s_new), 0) % s_new
    kcol = lax.broadcasted_iota(jnp.int32, (N_FOX * s_new, s_new), 1)
    sn = jnp.where(kcol <= qrow, _dot_nt(q6, kn_ref[0]) + _dot_nt(a6, ak_ref[0]), -1e30)
    m = jnp.maximum(jnp.max(sp, axis=1, keepdims=True), jnp.max(sn, axis=1, keepdims=True))
    pp = jnp.exp2(sp - m)
    pn = jnp.exp2(sn - m)
    l = jnp.sum(pp, axis=1, keepdims=True) + jnp.sum(pn, axis=1, keepdims=True)
    o6 = _dot((pp / l).astype(BF16), vp_ref[0]) + _dot((pn / l).astype(BF16), vn_ref[0].astype(BF16))
    o_ref[0] = _unstack_heads(o6, N_FOX).astype(BF16)


def _fox_step(qs, augq, kb, augk, v_new, k_past, v_past, logf_past_pad):
    b, s_new, _ = qs.shape
    p_len = k_past.shape[1]
    tok = lambda rows, w: pl.BlockSpec((1, rows, w), lambda i: (i, 0, 0))
    return pl.pallas_call(
        _fox_step_body,
        grid=(b,),
        in_specs=[tok(s_new, W_FOX)] * 5 + [tok(p_len, W_FOX), tok(p_len, W_FOX), tok(p_len, SMALL)],
        out_specs=tok(s_new, W_FOX),
        out_shape=jax.ShapeDtypeStruct((b, s_new, W_FOX), BF16),
        compiler_params=_params(1),
        name="fox_step",
    )(qs, augq, kb, augk, v_new, k_past, v_past, logf_past_pad)


def _rwkv_body(chunk, rt_ref, at_ref, kh_ref, bh_ref, v_ref, lc_ref, s0_ref, y_ref, sout_ref, s_scr):
    j = pl.program_id(1)
    nj = pl.num_programs(1)
    c = chunk
    c2 = 2 * chunk
    n_chunks = rt_ref.shape[1] // c
    n_pairs = N_RWKV // 2

    @pl.when(j == 0)
    def _():
        z = jnp.zeros((HEAD, HEAD), F32)
        for p in range(n_pairs):
            s_scr[p] = jnp.concatenate([jnp.concatenate([s0_ref[0, 2 * p], z], axis=1),
                                        jnp.concatenate([z, s0_ref[0, 2 * p + 1]], axis=1)], axis=0)

    lane_lo = lax.broadcasted_iota(jnp.int32, (c, PAIR), 1) < HEAD
    row = lax.broadcasted_iota(jnp.int32, (c2, c2), 0)
    col = lax.broadcasted_iota(jnp.int32, (c2, c2), 1)
    same = (row >= c) == (col >= c)
    strict = same & (col < row)
    lower = same & (col <= row)
    eye = jnp.where(col == row, 1.0, 0.0)
    srow = lax.broadcasted_iota(jnp.int32, (PAIR, PAIR), 0)
    scol = lax.broadcasted_iota(jnp.int32, (PAIR, PAIR), 1)
    same_state = (srow >= HEAD) == (scol >= HEAD)

    def stack(x):
        zero = jnp.zeros_like(x)
        return jnp.concatenate([jnp.where(lane_lo, x, zero), jnp.where(lane_lo, zero, x)], axis=0)

    def fold(a):
        return a[:c] + a[c:]

    units = [(ci, p) for ci in range(n_chunks) for p in range(n_pairs)]

    def tile(ref, u):
        ci, p = u
        return ref[0, ci * c:(ci + 1) * c, PAIR * p:PAIR * (p + 1)]

    rt = [tile(rt_ref, u) for u in units]
    at2 = [stack(tile(at_ref, u)) for u in units]
    kh = [tile(kh_ref, u) for u in units]
    bh = [tile(bh_ref, u) for u in units]
    v2 = [stack(tile(v_ref, u)) for u in units]
    g = [_dot_nt(jnp.concatenate([at2[i], stack(rt[i])], axis=0),
                 jnp.concatenate([kh[i], kh[i], bh[i], bh[i]], axis=0)) for i in range(len(units))]
    a_ak = [jnp.where(strict, x[:c2, :c2], 0.0) for x in g]
    a_ab = [jnp.where(strict, x[:c2, c2:], 0.0) for x in g]
    a_rk = [fold(jnp.where(lower, x[c2:, :c2], 0.0)) for x in g]
    a_rb = [fold(jnp.where(lower, x[c2:, c2:], 0.0)) for x in g]
    akv2 = [_dot1(_dot, a_ak[i], v2[i]) for i in range(len(units))]
    yv = [_dot1(_dot, a_rk[i], v2[i]) for i in range(len(units))]
    inv = [eye + x for x in a_ab]
    pw = a_ab
    n = 1
    while 2 * n < c:
        pw = [_dot1(_dot, x, x) for x in pw]
        inv = [inv[i] + _dot1(_dot, inv[i], pw[i]) for i in range(len(units))]
        n *= 2
    tw = [_dot(inv[i].astype(BF16), jnp.concatenate([at2[i], akv2[i].astype(BF16)], axis=1))
          for i in range(len(units))]
    w_mat = [fold(x[:, :PAIR]) for x in tw]
    u0 = [fold(x[:, PAIR:]) for x in tw]

    for ci in range(n_chunks):
        idx = [ci * n_pairs + p for p in range(n_pairs)]
        s_old = [s_scr[p] for p in range(n_pairs)]
        pc = [jnp.exp(lc_ref[0, (ci + 1) * c - 1:(ci + 1) * c, PAIR * p:PAIR * (p + 1)]) for p in range(n_pairs)]
        ws = [_dot_nt(jnp.concatenate([w_mat[i].astype(BF16), rt[i]], axis=0), s_old[p].astype(BF16))
              for p, i in enumerate(idx)]
        u = [ws[p][:c] + u0[i] for p, i in enumerate(idx)]
        y = [ws[p][c:] + yv[i] + _dot1(_dot, a_rb[i], stack(u[p])) for p, i in enumerate(idx)]
        upd = [_dot_tn(jnp.concatenate([tile(v_ref, units[i]), u[p].astype(BF16)], axis=0),
                       jnp.concatenate([(kh[i] * pc[p]).astype(BF16), (bh[i] * pc[p]).astype(BF16)], axis=0))
               for p, i in enumerate(idx)]
        for p in range(n_pairs):
            s_scr[p] = s_old[p] * pc[p] + jnp.where(same_state, upd[p], 0.0)
            y_ref[0, ci * c:(ci + 1) * c, PAIR * p:PAIR * (p + 1)] = y[p]

    @pl.when(j == nj - 1)
    def _():
        for p in range(n_pairs):
            s = s_scr[p]
            sout_ref[0, 2 * p] = s[:HEAD, :HEAD]
            sout_ref[0, 2 * p + 1] = s[HEAD:, HEAD:]


def _rwkv(rt, at, kh, bh, v, lc, s0, tc, chunk):
    b, t, _ = rt.shape
    tok = pl.BlockSpec((1, tc, W_RWKV), lambda bi, j: (bi, j, 0))
    st = pl.BlockSpec((1, N_RWKV, HEAD, HEAD), lambda bi, j: (bi, 0, 0, 0))
    return pl.pallas_call(
        functools.partial(_rwkv_body, chunk),
        grid=(b, t // tc),
        in_specs=[tok] * 6 + [st],
        out_specs=[tok, st],
        out_shape=[jax.ShapeDtypeStruct((b, t, W_RWKV), F32), jax.ShapeDtypeStruct((b, N_RWKV, HEAD, HEAD), F32)],
        scratch_shapes=[pltpu.VMEM((N_RWKV // 2, PAIR, PAIR), F32)],
        compiler_params=_params(2),
        name="rwkv",
    )(rt, at, kh, bh, v, lc, s0)


def _out_body(x_ref, of_ref, gf_ref, y_ref, gr_ref, bg_ref, om_ref, gnw_ref, gnb_ref, ones_ref,
              wf_ref, wr_ref, wm_ref, o_ref):
    ones384 = ones_ref[...]
    y = y_ref[0]
    mu = _segsum(y, ones384) * (1.0 / HEAD)
    d = y - mu
    var = _segsum(d * d, ones384) * (1.0 / HEAD)
    yn = d * lax.rsqrt(var + GN_EPS) * gnw_ref[...] + gnb_ref[...]
    o_r = yn * gr_ref[0].astype(F32) + bg_ref[0].astype(F32)
    acc = _dot((of_ref[0].astype(F32) * gf_ref[0].astype(F32)).astype(BF16), wf_ref[...])
    acc = acc + _dot(o_r.astype(BF16), wr_ref[...])
    acc = acc + _dot(om_ref[0], wm_ref[...])
    o_ref[0] = x_ref[0] + acc


def _out(x, of, gf, y, gr, bg, om, gnw, gnb, ones384, wf, wr, wm, tm):
    b, t, _ = x.shape
    tok = lambda w: pl.BlockSpec((1, tm, w), lambda bi, i: (bi, i, 0))
    return pl.pallas_call(
        _out_body,
        grid=(b, t // tm),
        in_specs=[tok(D_MODEL), tok(W_FOX), tok(W_FOX), tok(W_RWKV), tok(W_RWKV), tok(W_RWKV), tok(W_MEM),
                  _full(gnw.shape), _full(gnb.shape), _full(ones384.shape),
                  _full(wf.shape), _full(wr.shape), _full(wm.shape)],
        out_specs=tok(D_MODEL),
        out_shape=jax.ShapeDtypeStruct((b, t, D_MODEL), F32),
        compiler_params=_params(2),
        name="out_proj",
    )(x, of, gf, y, gr, bg, om, gnw, gnb, ones384, wf, wr, wm)


def _pad_in_columns(w):
    r0 = FOX_COLS
    m0 = FOX_COLS + RWKV_COLS
    zeros = lambda n: jnp.zeros(w.shape[:-1] + (n,), w.dtype)
    return jnp.concatenate([
        w[..., 0:3 * W_FOX], w[..., 3 * W_FOX + N_FOX:FOX_COLS],
        w[..., r0:r0 + 3 * W_RWKV], w[..., r0 + 3 * W_RWKV + 2 * LORA:r0 + RWKV_COLS],
        w[..., m0:m0 + MEM_COLS],
        w[..., 3 * W_FOX:3 * W_FOX + N_FOX], zeros(SM_WD - N_FOX),
        w[..., r0 + 3 * W_RWKV:r0 + 3 * W_RWKV + 2 * LORA], zeros(SMALL - SM_AD - LORA),
    ], axis=-1)


def _pad_shift_row(a):
    zeros = lambda n: jnp.zeros(a.shape[:-1] + (n,), a.dtype)
    return jnp.concatenate([
        a[..., 0:3 * W_RWKV], a[..., 3 * W_RWKV + 2 * LORA:RWKV_COLS],
        zeros(SM_WD), a[..., 3 * W_RWKV:3 * W_RWKV + 2 * LORA], zeros(SMALL - SM_AD - LORA)], axis=-1)


def _unpad_shift_row(a):
    small = 4 * W_RWKV
    return jnp.concatenate([a[..., 0:3 * W_RWKV], a[..., small + SM_WD:small + SM_AD + LORA],
                            a[..., 3 * W_RWKV:4 * W_RWKV]], axis=-1)


def _bias_placement():
    place = np.zeros((SMALL, 2 * W_FOX), np.float32)
    const = np.zeros((1, 2 * W_FOX), np.float32)
    for h in range(N_FOX):
        for j in range(N_BIAS):
            place[BIAS_STRIDE * j + h, HEAD * h + j] = 1.0
            place[BIAS_STRIDE * j + h, W_FOX + HEAD * h + N_BIAS + j] = -1.0
            const[0, HEAD * h + N_BIAS + j] = 1.0
            const[0, W_FOX + HEAD * h + j] = 1.0
    return place, const


def _block_ones(width):
    h = np.arange(width) // HEAD
    return (h[:, None] == h[None, :]).astype(np.float32)


def _layer(x, shift_prev, s0, mk, mv, fox_past, wts, ones384, tm, tq, tc, chunk):
    (proj_wts, gnw, gnb, wf, wr, wm) = wts
    b, t, _ = x.shape
    (qs, kb, augq, augk, kn, v, logf, gf, rt, at, kh, bh, rv, lc, gr, bg, om, shift_out, *vt) = _proj(
        x, _pad_shift_row(shift_prev), mk, mv, proj_wts, tm, chunk, with_vt=fox_past is None)
    if fox_past is None:
        of = _fox(qs, augq, kb, augk, vt[0], tq)
    else:
        k_past, v_past, logf_past = fox_past
        lfp = jnp.pad(logf_past.astype(F32), ((0, 0), (0, 0), (0, SMALL - N_FOX)))
        p_len = k_past.shape[1]
        of = _fox_step(qs, augq, kb, augk, v, k_past.reshape(b, p_len, W_FOX).astype(BF16),
                       v_past.reshape(b, p_len, W_FOX).astype(BF16), lfp)
    y_r, s_new = _rwkv(rt, at, kh, bh, rv, lc, s0, tc, chunk)
    y = _out(x, of, gf, y_r, gr, bg, om, gnw, gnb, ones384, wf, wr, wm, tm)
    return (y, kn.reshape(b, t, N_FOX, HEAD), v.reshape(b, t, N_FOX, HEAD), logf, s_new, _unpad_shift_row(shift_out))


def kernel(x_prompt, x_sample, mem_prompt, cache_fox_k, cache_fox_v, cache_fox_logf, cache_mem_k, cache_mem_v, state_rwkv, state_rwkv_shift, norm_g, w_in, fox_q_g, fox_k_g, fox_b_f, rwkv_mu, rwkv_w0, rwkv_w_up, rwkv_a0, rwkv_a_up, rwkv_k_k, rwkv_k_a, rwkv_r_k, rwkv_gn_w, rwkv_gn_b, mem_norm_g, w_mem_kv, mem_q_g, mem_k_g, w_out):
    depth = w_in.shape[0]
    bp = x_prompt.shape[0]
    place_np, const_np = _bias_placement()
    ones384 = jnp.asarray(_block_ones(W_FOX), BF16)
    ones256 = jnp.asarray(_block_ones(W_MEM), BF16)
    pbias = jnp.asarray(place_np, BF16)
    bias1 = jnp.asarray(const_np, F32)

    yp, ys = x_prompt, x_sample
    outs = [[] for _ in range(12)]
    for l in range(depth):
        row = lambda a: a[l].reshape(1, -1).astype(F32)
        tile = lambda a, n: jnp.tile(a[l].reshape(1, -1).astype(F32), (1, n))
        w_pad = _pad_in_columns(w_in[l]).astype(BF16)
        bf_pad = jnp.pad(row(fox_b_f), ((0, 0), (0, SMALL - N_FOX)))
        wup_pad = jnp.pad(rwkv_w_up[l].astype(F32), ((SM_WD, SMALL - SM_WD - LORA), (0, 0)))
        aup_pad = jnp.pad(rwkv_a_up[l].astype(F32), ((SM_AD, SMALL - SM_AD - LORA), (0, 0)))
        proj_wts = (row(norm_g), w_pad, tile(fox_q_g, N_FOX), tile(fox_k_g, N_FOX), bf_pad,
                    _pad_shift_row(row(rwkv_mu)),
                    row(rwkv_w0), row(rwkv_a0), wup_pad, aup_pad, row(rwkv_k_k), row(rwkv_k_a), row(rwkv_r_k),
                    tile(mem_q_g, N_MEM), ones384, ones256, pbias, bias1)
        wo = w_out[l].astype(BF16)
        wts = (proj_wts, row(rwkv_gn_w), row(rwkv_gn_b), wo[:W_FOX], wo[W_FOX:W_FOX + W_RWKV], wo[W_FOX + W_RWKV:])

        mk2, mv2, mkb, mvb = _mem_kv(mem_prompt, row(mem_norm_g), w_mem_kv[l].astype(BF16), tile(mem_k_g, N_MEM),
                                     ones256)
        shift_zero = jnp.zeros((bp, 1, RWKV_COLS), F32)
        s_zero = jnp.zeros((bp, N_RWKV, HEAD, HEAD), F32)
        yp, k, v, lf, s_new, sh_new = _layer(yp, shift_zero, s_zero, mkb, mvb, None, wts, ones384,
                                             tm=PROMPT_TM, tq=PROMPT_TQ, tc=PROMPT_TC, chunk=RWKV_CHUNK)
        mk = mk2.reshape(bp, N_MEM_TOK, N_MEM, HEAD)
        mv = mv2.reshape(bp, N_MEM_TOK, N_MEM, HEAD)
        for lst, val in zip(outs[:7], (k, v, lf, mk, mv, s_new, sh_new)):
            lst.append(val)
        bs, s_len = ys.shape[0], ys.shape[1]
        ys, k, v, lf, s_new, sh_new = _layer(
            ys, state_rwkv_shift[l], state_rwkv[l].astype(F32),
            cache_mem_k[l].reshape(bs, N_MEM_TOK, W_MEM).astype(BF16),
            cache_mem_v[l].reshape(bs, N_MEM_TOK, W_MEM).astype(BF16),
            (cache_fox_k[l], cache_fox_v[l], cache_fox_logf[l]), wts, ones384,
            tm=s_len, tq=None, tc=s_len, chunk=s_len)
        for lst, val in zip(outs[7:], (k, v, lf, s_new, sh_new)):
            lst.append(val)
    return (yp, ys) + tuple(jnp.stack(o) for o in outs)
```

```python
import functools

import numpy as np
import jax
import jax.numpy as jnp
from jax import lax
from jax.experimental import pallas as pl
from jax.experimental.pallas import tpu as pltpu

F32 = jnp.float32
BF16 = jnp.bfloat16

D_MODEL = 1024
HEAD = 64
N_FOX = 6
N_RWKV = 6
N_MEM = 4
W_FOX = N_FOX * HEAD
W_RWKV = N_RWKV * HEAD
W_MEM = N_MEM * HEAD
N_MEM_TOK = 256
LORA = 32
NORM_EPS = 1e-6
GN_EPS = 64e-5
LOG2E = float(np.log2(np.e))
FOX_COLS = 4 * W_FOX + N_FOX
RWKV_COLS = 4 * W_RWKV + 2 * LORA
MEM_COLS = 2 * W_MEM

LANE = 128
PAIR = 2 * HEAD
SMALL = LANE
SM_WD = 32
SM_AD = 64
OFF_FQ, OFF_FK, OFF_FV, OFF_FG = 0, 384, 768, 1152
OFF_RR, OFF_RK, OFF_RV, OFF_RG = 1536, 1920, 2304, 2688
OFF_MQ, OFF_MG = 3072, 3328
OFF_SM = 3584
N_PAD = OFF_SM + SMALL
SHIFT_W = 4 * W_RWKV + SMALL
N_BIAS = 3
BIAS_STRIDE = 8
BF16_ROWS = 16
VROWS = HEAD + BF16_ROWS
RWKV_CHUNK = 64
VMEM_LIMIT = 56 * 1024 * 1024
PROMPT_TM = 512
PROMPT_TQ = 512
PROMPT_TC = 512
CUMSUM_BLOCK = 256
STEP_SUFFIX_BLOCK = 256
STEP_SEQS_PER_TILE = 8


def _dot(a, b):
    return jnp.dot(a, b, preferred_element_type=F32)


def _dot_nt(a, b):
    return lax.dot_general(a, b, (((1,), (1,)), ((), ())), preferred_element_type=F32)


def _dot_tn(a, b):
    return lax.dot_general(a, b, (((0,), (0,)), ((), ())), preferred_element_type=F32)


def _split2(x):
    hi = x.astype(BF16)
    lo = (x - hi.astype(F32)).astype(BF16)
    return hi, lo


def _split3(x):
    hi = x.astype(BF16)
    r1 = x - hi.astype(F32)
    mid = r1.astype(BF16)
    lo = (r1 - mid.astype(F32)).astype(BF16)
    return hi, mid, lo


def _dot1(fn, a, b):
    return fn(a.astype(BF16), b.astype(BF16))


def _exact_lhs_dot(m_bf16, x, parts):
    pieces = _split3(x) if parts == 3 else _split2(x)
    acc = _dot(m_bf16, pieces[0])
    for p in pieces[1:]:
        acc = acc + _dot(m_bf16, p)
    return acc


def _segsum(x, ones_pair):
    xb = x.astype(BF16)
    return jnp.concatenate([_dot(xb[:, PAIR * p:PAIR * (p + 1)], ones_pair) for p in range(x.shape[1] // PAIR)],
                           axis=1)


def _head_rms(t, gain, ones_pair):
    msq = _segsum(t * t, ones_pair) * (1.0 / HEAD)
    return t * lax.rsqrt(msq + NORM_EPS) * gain


def _silu(x):
    return x * jax.nn.sigmoid(x)


def _softplus(z):
    return jnp.maximum(z, 0.0) + jnp.log1p(jnp.exp(-jnp.abs(z)))


def _stack_heads(x, n_heads):
    head_of_lane = lax.broadcasted_iota(jnp.int32, x.shape, 1) // HEAD
    zero = jnp.zeros_like(x)
    return jnp.concatenate([jnp.where(head_of_lane == h, x, zero) for h in range(n_heads)], axis=0)


def _unstack_heads(x6, n_heads):
    rows = x6.shape[0] // n_heads
    head_of_lane = lax.broadcasted_iota(jnp.int32, (rows, x6.shape[1]), 1) // HEAD
    out = jnp.zeros((rows, x6.shape[1]), x6.dtype)
    for h in range(n_heads):
        out = jnp.where(head_of_lane == h, x6[h * rows:(h + 1) * rows], out)
    return out


def _full(shape):
    n = len(shape)
    return pl.BlockSpec(shape, lambda *_: (0,) * n)


def _params(n_axes):
    return pltpu.CompilerParams(dimension_semantics=("arbitrary",) * n_axes, vmem_limit_bytes=VMEM_LIMIT)


def _mem_kv_body(mem_ref, g_ref, w_ref, kg_ref, ones_ref, k_ref, v_ref, kb_ref, vb_ref):
    x = mem_ref[0]
    ms = jnp.mean(x * x, axis=-1, keepdims=True)
    xn = (x * lax.rsqrt(ms + NORM_EPS) * g_ref[...]).astype(BF16)
    kv = _dot(xn, w_ref[...])
    k = _head_rms(kv[:, :W_MEM], kg_ref[...], ones_ref[...])
    v = kv[:, W_MEM:]
    k_ref[0] = k
    v_ref[0] = v
    kb_ref[0] = k.astype(BF16)
    vb_ref[0] = v.astype(BF16)


def _mem_kv(mem, g, w_bf16, kg4, ones_pair):
    b = mem.shape[0]
    blk = pl.BlockSpec((1, N_MEM_TOK, W_MEM), lambda i: (i, 0, 0))
    return pl.pallas_call(
        _mem_kv_body,
        grid=(b,),
        in_specs=[pl.BlockSpec((1, N_MEM_TOK, D_MODEL), lambda i: (i, 0, 0)),
                  _full((1, D_MODEL)), _full((D_MODEL, 2 * W_MEM)), _full((1, W_MEM)), _full((PAIR, PAIR))],
        out_specs=[blk] * 4,
        out_shape=[jax.ShapeDtypeStruct((b, N_MEM_TOK, W_MEM), F32)] * 2
        + [jax.ShapeDtypeStruct((b, N_MEM_TOK, W_MEM), BF16)] * 2,
        compiler_params=_params(1),
        name="mem_kv",
    )(mem, g, w_bf16, kg4, ones_pair)


def _proj_body(chunk, with_vt, seq,
               x_ref, shift0_ref, mk_ref, mv_ref, ng_ref, w_ref,
               fqg_ref, fkg_ref, bf_ref, mu_ref, w0_ref, a0_ref, wup_ref, aup_ref,
               kk_ref, ka_ref, rk_ref, mqg_ref, ones_ref, pbias_ref, bias1_ref,
               qs_ref, kb_ref, augq_ref, augk_ref, kn_ref, v_ref, logf_ref, gf_ref,
               rt_ref, at_ref, kh_ref, bh_ref, rv_ref, lc_ref, gr_ref, bg_ref, om_ref, shift_out_ref,
               *vt_and_scratch):
    carry_shift, carry_c = vt_and_scratch[-2:]
    i = pl.program_id(1)
    tm = x_ref.shape[1]
    sub = min(tm, CUMSUM_BLOCK)

    if seq is None:
        @pl.when(i == 0)
        def _():
            carry_shift[...] = shift0_ref[0]
            carry_c[...] = jnp.zeros_like(carry_c)

    x = x_ref[0]
    ms = jnp.mean(x * x, axis=-1, keepdims=True)
    xn = (x * lax.rsqrt(ms + NORM_EPS) * ng_ref[...]).astype(BF16)

    def proj(off, width):
        return _dot(xn, w_ref[:, off:off + width])

    ones_pair = ones_ref[...]
    row = lax.broadcasted_iota(jnp.int32, (sub, sub), 0)
    col = lax.broadcasted_iota(jnp.int32, (sub, sub), 1)
    tri = jnp.where((col <= row) & ((row ^ col) < (sub if seq is None else seq)), 1.0, 0.0).astype(BF16)
    tri_chunk = jnp.where((col <= row) & ((row ^ col) < chunk), 1.0, 0.0).astype(BF16)
    blocks = [slice(s * sub, (s + 1) * sub) for s in range(tm // sub)]

    hs = proj(OFF_SM, SMALL)
    qn = _head_rms(proj(OFF_FQ, W_FOX), fqg_ref[...], ones_pair)
    kn = _head_rms(proj(OFF_FK, W_FOX), fkg_ref[...], ones_pair)
    hv = proj(OFF_FV, W_FOX)
    hg = proj(OFF_FG, W_FOX)
    kn_ref[0] = kn
    kb_ref[0] = kn.astype(BF16)
    qs_ref[0] = (qn * (HEAD ** -0.5 * LOG2E)).astype(BF16)
    v_ref[0] = hv
    if with_vt:
        hvt = hv.T
        ones_row = jnp.where(lax.broadcasted_iota(jnp.int32, (VROWS - HEAD, tm), 0) == 0, 1.0, 0.0)
        vt_and_scratch[0][0, 0] = jnp.concatenate(
            [blk for h in range(N_FOX) for blk in (hvt[HEAD * h:HEAD * (h + 1)], ones_row)], axis=0).astype(BF16)
    gf_ref[0] = _silu(hg).astype(BF16)
    f = hs + bf_ref[...]
    lane = lax.broadcasted_iota(jnp.int32, (tm, SMALL), 1)
    logf = jnp.where(lane < N_FOX, jnp.minimum(f, 0.0) - jnp.log1p(jnp.exp(-jnp.abs(f))), 0.0)
    logf_ref[0] = logf[:, 0:N_FOX]
    if seq is None:
        carry = carry_c[...]
        c_blocks = []
        for blk in blocks:
            cb = _exact_lhs_dot(tri, logf[blk], 3) + carry
            carry = cb[sub - 1:sub, :]
            c_blocks.append(cb)
        carry_c[...] = carry
    else:
        c_blocks = [_exact_lhs_dot(tri, logf[blk], 3) for blk in blocks]
    c = jnp.concatenate(c_blocks, axis=0)
    pieces = _split3(c * LOG2E)
    packed = pieces[0].astype(F32)
    for j in range(1, N_BIAS):
        packed = packed + pltpu.roll(pieces[j].astype(F32), BIAS_STRIDE * j, 1)
    aug = _dot(packed.astype(BF16), pbias_ref[...]) + bias1_ref[...]
    augq_ref[0] = aug[:, :W_FOX].astype(BF16)
    augk_ref[0] = aug[:, W_FOX:].astype(BF16)

    qm = _head_rms(proj(OFF_MQ, W_MEM), mqg_ref[...], ones_pair)
    qms = (qm * (HEAD ** -0.5 * LOG2E)).astype(BF16)

    def mem_attend(q_rows, g):
        s4 = _dot_nt(_stack_heads(q_rows, N_MEM), mk_ref[g].astype(BF16))
        e = jnp.exp2(s4 - jnp.max(s4, axis=1, keepdims=True))
        p4 = (e / jnp.sum(e, axis=1, keepdims=True)).astype(BF16)
        return _unstack_heads(_dot(p4, mv_ref[g].astype(BF16)), N_MEM)

    if seq is None:
        om = mem_attend(qms, 0)
    else:
        om = jnp.concatenate([mem_attend(qms[g * seq:(g + 1) * seq], g) for g in range(tm // seq)], axis=0)
    om_ref[0] = (om * _silu(proj(OFF_MG, W_MEM))).astype(BF16)

    row_in_seq = lax.broadcasted_iota(jnp.int32, (tm, 1), 0) % (tm if seq is None else seq)

    def tshift(cols, off, width):
        if seq is None:
            before_first = carry_shift[:, off:off + width]
            carry_shift[:, off:off + width] = cols[tm - 1:tm, :]
        else:
            before_first = shift0_ref[0, :, off:off + width]
            shift_out_ref[0, :, off:off + width] = cols
        prev = jnp.where(row_in_seq == 0, before_first, pltpu.roll(cols, 1, 0))
        return cols + (prev - cols) * mu_ref[:, off:off + width]

    r = tshift(proj(OFF_RR, W_RWKV), 0, W_RWKV)
    k = tshift(proj(OFF_RK, W_RWKV), W_RWKV, W_RWKV)
    v = tshift(proj(OFF_RV, W_RWKV), 2 * W_RWKV, W_RWKV)
    g = tshift(proj(OFF_RG, W_RWKV), 3 * W_RWKV, W_RWKV)
    sm = tshift(hs, 4 * W_RWKV, SMALL)
    if seq is None:
        shift_out_ref[0] = carry_shift[...]

    w_lin = w0_ref[...] + _dot1(_dot, jnp.tanh(sm), wup_ref[...])
    a = jax.nn.sigmoid(a0_ref[...] + _dot1(_dot, sm, aup_ref[...]))
    lw = -jnp.exp(-_softplus(-w_lin) - 0.5)
    kk = k * kk_ref[...]
    kk = kk * lax.rsqrt(jnp.maximum(_segsum(kk * kk, ones_pair), 1e-24))
    kt = k * (1.0 + (a - 1.0) * ka_ref[...])
    gr = _silu(g)
    gr_ref[0] = gr.astype(BF16)
    bg_ref[0] = (_segsum(r * kt * rk_ref[...], ones_pair) * v * gr).astype(BF16)
    lc = jnp.concatenate([_exact_lhs_dot(tri_chunk, lw[blk], 2) for blk in blocks], axis=0)
    e_neg = jnp.exp(-lc)
    rt_ref[0] = (r * jnp.exp(lc)).astype(BF16)
    at_ref[0] = (-kk * jnp.exp(lc - lw)).astype(BF16)
    kh_ref[0] = (kt * e_neg).astype(BF16)
    bh_ref[0] = (kk * a * e_neg).astype(BF16)
    rv_ref[0] = v.astype(BF16)
    lc_ref[0] = lc


def _proj(x, shift0, mk, mv, wts, tm, chunk, with_vt, seq=None):
    b, t, _ = x.shape
    grid = (b, t // tm)
    tok = lambda w: pl.BlockSpec((1, tm, w), lambda bi, i: (bi, i, 0))
    per_b = lambda s1, s2: pl.BlockSpec((1, s1, s2), lambda bi, i: (bi, 0, 0))
    w_specs = [_full(a.shape) for a in wts]
    bf = lambda w: jax.ShapeDtypeStruct((b, t, w), BF16)
    f32 = lambda w: jax.ShapeDtypeStruct((b, t, w), F32)
    out_shape = [bf(W_FOX)] * 4 + [f32(W_FOX), f32(W_FOX), f32(N_FOX), bf(W_FOX)] \
        + [bf(W_RWKV)] * 5 + [f32(W_RWKV)] + [bf(W_RWKV), bf(W_RWKV), bf(W_MEM),
                                               jax.ShapeDtypeStruct(shift0.shape, F32)]
    shift_spec = per_b(1, SHIFT_W) if seq is None else tok(SHIFT_W)
    mem_spec = pl.BlockSpec((mk.shape[0] // b, N_MEM_TOK, W_MEM), lambda bi, i: (bi, 0, 0))
    out_specs = [tok(W_FOX)] * 4 + [tok(W_FOX), tok(W_FOX), tok(N_FOX), tok(W_FOX)] \
        + [tok(W_RWKV)] * 6 + [tok(W_RWKV), tok(W_RWKV), tok(W_MEM), shift_spec]
    if with_vt:
        out_shape.append(jax.ShapeDtypeStruct((b, t // tm, N_FOX * VROWS, tm), BF16))
        out_specs.append(pl.BlockSpec((1, 1, N_FOX * VROWS, tm), lambda bi, i: (bi, i, 0, 0)))
    return pl.pallas_call(
        functools.partial(_proj_body, chunk, with_vt, seq),
        grid=grid,
        in_specs=[tok(D_MODEL), shift_spec, mem_spec, mem_spec] + w_specs,
        out_specs=out_specs,
        out_shape=out_shape,
        scratch_shapes=[pltpu.VMEM((1, SHIFT_W), F32), pltpu.VMEM((1, SMALL), F32)],
        compiler_params=_params(2),
        name="proj",
    )(x, shift0, mk, mv, *wts)


def _fox_body(qs_ref, aq_ref, kb_ref, ak_ref, vt_ref, o_ref, m_scr, acc_scr, sa_scr, sb_scr, qm_scr):
    qi = pl.program_id(2)
    tq = qs_ref.shape[1]
    slab = vt_ref.shape[3]
    slabs_per_tile = tq // slab
    m_scr[...] = jnp.full_like(m_scr, -1e30)
    acc_scr[...] = jnp.zeros_like(acc_scr)
    q = qs_ref[0]
    aq = aq_ref[0]
    lane_lo = lax.broadcasted_iota(jnp.int32, (tq, PAIR), 1) < HEAD
    zero = jnp.zeros_like(q)
    qm_scr[0] = jnp.concatenate([jnp.where(lane_lo, q, zero), jnp.where(lane_lo, aq, zero)], axis=1)
    qm_scr[1] = jnp.concatenate([jnp.where(lane_lo, zero, q), jnp.where(lane_lo, zero, aq)], axis=1)

    def scores(ki, s_scr):
        rows = pl.ds(pl.multiple_of(ki * tq, tq), tq)
        k2 = jnp.concatenate([kb_ref[0, rows, :], ak_ref[0, rows, :]], axis=1)
        for hh in range(2):
            s_scr[hh] = _dot_nt(k2, qm_scr[hh])

    def consume(ki, s_scr, masked):
        s = [s_scr[hh] for hh in range(2)]
        if masked:
            kpos = lax.broadcasted_iota(jnp.int32, (tq, tq), 0)
            qpos = lax.broadcasted_iota(jnp.int32, (tq, tq), 1)
            s = [jnp.where(kpos <= qpos, sh, -1e30) for sh in s]
        m_prev = [m_scr[hh] for hh in range(2)]
        m_new = [jnp.maximum(m_prev[hh], jnp.max(s[hh], axis=0, keepdims=True)) for hh in range(2)]
        p = [jnp.exp2(s[hh] - m_new[hh]).astype(BF16) for hh in range(2)]
        pv = []
        for hh in range(2):
            vt = jnp.concatenate([vt_ref[0, ki * slabs_per_tile + d, VROWS * hh:VROWS * (hh + 1), :]
                                  for d in range(slabs_per_tile)], axis=1)
            pv.append(_dot(vt, p[hh]))
        for hh in range(2):
            acc_scr[hh] = jnp.exp2(m_prev[hh] - m_new[hh]) * acc_scr[hh] + pv[hh]
            m_scr[hh] = m_new[hh]

    scores(0, sa_scr)

    def two_tiles(j, carry):
        scores(2 * j + 1, sb_scr)
        consume(2 * j, sa_scr, False)
        scores(2 * j + 2, sa_scr)
        consume(2 * j + 1, sb_scr, False)
        return carry

    lax.fori_loop(0, qi // 2, two_tiles, 0)

    @pl.when(qi % 2 == 0)
    def _():
        consume(qi, sa_scr, True)

    @pl.when(qi % 2 == 1)
    def _():
        scores(qi, sb_scr)
        consume(qi - 1, sa_scr, False)
        consume(qi, sb_scr, True)

    o_t = jnp.concatenate([acc_scr[hh][0:HEAD] / acc_scr[hh][HEAD:HEAD + 1] for hh in range(2)], axis=0)
    o_ref[0] = o_t.T.astype(BF16)


def _fox(qs, augq, kb, augk, vt, tq):
    b, t, _ = qs.shape
    n_slab, slab = vt.shape[1], vt.shape[3]
    q_blk = pl.BlockSpec((1, tq, PAIR), lambda bi, p, qi: (bi, qi, p))
    k_blk = pl.BlockSpec((1, t, PAIR), lambda bi, p, qi: (bi, 0, p))
    return pl.pallas_call(
        _fox_body,
        grid=(b, N_FOX // 2, t // tq),
        in_specs=[q_blk, q_blk, k_blk, k_blk,
                  pl.BlockSpec((1, n_slab, 2 * VROWS, slab), lambda bi, p, qi: (bi, 0, p, 0))],
        out_specs=q_blk,
        out_shape=jax.ShapeDtypeStruct((b, t, W_FOX), BF16),
        scratch_shapes=[pltpu.VMEM((2, 1, tq), F32), pltpu.VMEM((2, VROWS, tq), F32),
                        pltpu.VMEM((2, tq, tq), F32), pltpu.VMEM((2, tq, tq), F32),
                        pltpu.VMEM((2, tq, 2 * PAIR), BF16)],
        compiler_params=_params(3),
        name="fox",
    )(qs, augq, kb, augk, vt)


def _fox_step_body(qs_ref, aq_ref, kn_ref, ak_ref, vn_ref, kp_ref, vp_ref, lfp_ref, o_ref):
    s_new = qs_ref.shape[1]
    p_len = kp_ref.shape[1]
    blk = min(p_len, STEP_SUFFIX_BLOCK)
    row = lax.broadcasted_iota(jnp.int32, (blk, blk), 0)
    col = lax.broadcasted_iota(jnp.int32, (blk, blk), 1)
    upper = jnp.where(col > row, 1.0, 0.0).astype(BF16)
    later = jnp.zeros((1, SMALL), F32)
    sufs = []
    for bi in reversed(range(p_len // blk)):
        lf = lfp_ref[0, bi * blk:(bi + 1) * blk, :]
        within = _exact_lhs_dot(upper, lf, 3)
        sufs.insert(0, within + later)
        later = later + within[0:1, :] + lf[0:1, :]
    suf_t = (jnp.concatenate(sufs, axis=0) * LOG2E).T
    q6 = _stack_heads(qs_ref[0], N_FOX)
    a6 = _stack_heads(aq_ref[0], N_FOX)
    lane_in_head = lax.broadcasted_iota(jnp.int32, a6.shape, 1) % HEAD
    cq6 = jnp.sum(jnp.where(lane_in_head < N_BIAS, a6.astype(F32), 0.0), axis=1, keepdims=True)
    bias6 = jnp.concatenate([jnp.broadcast_to(suf_t[h:h + 1, :], (s_new, p_len)) for h in range(N_FOX)], axis=0)
    sp = _dot_nt(q6, kp_ref[0].astype(BF16)) + cq6 + bias6
    qrow = lax.broadcasted_iota(jnp.int32, (N_FOX * s_new, s_new), 0) % s_new
    kcol = lax.broadcasted_iota(jnp.int32, (N_FOX * s_new, s_new), 1)
    sn = jnp.where(kcol <= qrow, _dot_nt(q6, kn_ref[0]) + _dot_nt(a6, ak_ref[0]), -1e30)
    m = jnp.maximum(jnp.max(sp, axis=1, keepdims=True), jnp.max(sn, axis=1, keepdims=True))
    pp = jnp.exp2(sp - m)
    pn = jnp.exp2(sn - m)
    l = jnp.sum(pp, axis=1, keepdims=True) + jnp.sum(pn, axis=1, keepdims=True)
    o6 = (_dot((pp / l).astype(BF16), vp_ref[0].astype(BF16))
          + _dot((pn / l).astype(BF16), vn_ref[0].astype(BF16)))
    o_ref[0] = _unstack_heads(o6, N_FOX).astype(BF16)


def _fox_step(qs, augq, kb, augk, v_new, k_past, v_past, logf_past_pad):
    b, s_new, _ = qs.shape
    p_len = k_past.shape[1]
    tok = lambda rows, w: pl.BlockSpec((1, rows, w), lambda i: (i, 0, 0))
    return pl.pallas_call(
        _fox_step_body,
        grid=(b,),
        in_specs=[tok(s_new, W_FOX)] * 5 + [tok(p_len, W_FOX), tok(p_len, W_FOX), tok(p_len, SMALL)],
        out_specs=tok(s_new, W_FOX),
        out_shape=jax.ShapeDtypeStruct((b, s_new, W_FOX), BF16),
        compiler_params=_params(1),
        name="fox_step",
    )(qs, augq, kb, augk, v_new, k_past, v_past, logf_past_pad)


def _rwkv_body(chunk, independent, rt_ref, at_ref, kh_ref, bh_ref, v_ref, lc_ref, s0_ref, y_ref, sout_ref, s_scr):
    j = pl.program_id(1)
    nj = pl.num_programs(1)
    c = chunk
    c2 = 2 * chunk
    n_chunks = rt_ref.shape[1] // c
    n_pairs = N_RWKV // 2

    def pair_state(seq_i, p):
        z = jnp.zeros((HEAD, HEAD), F32)
        return jnp.concatenate([jnp.concatenate([s0_ref[seq_i, 2 * p], z], axis=1),
                                jnp.concatenate([z, s0_ref[seq_i, 2 * p + 1]], axis=1)], axis=0)

    if not independent:
        @pl.when(j == 0)
        def _():
            for p in range(n_pairs):
                s_scr[p] = pair_state(0, p)

    lane_lo = lax.broadcasted_iota(jnp.int32, (c, PAIR), 1) < HEAD
    row = lax.broadcasted_iota(jnp.int32, (c2, c2), 0)
    col = lax.broadcasted_iota(jnp.int32, (c2, c2), 1)
    same = (row >= c) == (col >= c)
    strict = same & (col < row)
    lower = same & (col <= row)
    eye = jnp.where(col == row, 1.0, 0.0)
    srow = lax.broadcasted_iota(jnp.int32, (PAIR, PAIR), 0)
    scol = lax.broadcasted_iota(jnp.int32, (PAIR, PAIR), 1)
    same_state = (srow >= HEAD) == (scol >= HEAD)

    def stack(x):
        zero = jnp.zeros_like(x)
        return jnp.concatenate([jnp.where(lane_lo, x, zero), jnp.where(lane_lo, zero, x)], axis=0)

    def fold(a):
        return a[:c] + a[c:]

    units = [(ci, p) for ci in range(n_chunks) for p in range(n_pairs)]

    def tile(ref, u):
        ci, p = u
        return ref[0, ci * c:(ci + 1) * c, PAIR * p:PAIR * (p + 1)]

    rt = [tile(rt_ref, u) for u in units]
    at2 = [stack(tile(at_ref, u)) for u in units]
    kh = [tile(kh_ref, u) for u in units]
    bh = [tile(bh_ref, u) for u in units]
    v2 = [stack(tile(v_ref, u)) for u in units]
    g = [_dot_nt(jnp.concatenate([at2[i], stack(rt[i])], axis=0),
                 jnp.concatenate([kh[i], kh[i], bh[i], bh[i]], axis=0)) for i in range(len(units))]
    a_ak = [jnp.where(strict, x[:c2, :c2], 0.0) for x in g]
    a_ab = [jnp.where(strict, x[:c2, c2:], 0.0) for x in g]
    a_rk = [fold(jnp.where(lower, x[c2:, :c2], 0.0)) for x in g]
    a_rb = [fold(jnp.where(lower, x[c2:, c2:], 0.0)) for x in g]
    akv2 = [_dot1(_dot, a_ak[i], v2[i]) for i in range(len(units))]
    yv = [_dot1(_dot, a_rk[i], v2[i]) for i in range(len(units))]
    inv = [eye + x for x in a_ab]
    pw = a_ab
    n = 1
    while 2 * n < c:
        pw = [_dot1(_dot, x, x) for x in pw]
        inv = [inv[i] + _dot1(_dot, inv[i], pw[i]) for i in range(len(units))]
        n *= 2
    tw = [_dot(inv[i].astype(BF16), jnp.concatenate([at2[i], akv2[i].astype(BF16)], axis=1))
          for i in range(len(units))]
    w_mat = [fold(x[:, :PAIR]) for x in tw]
    u0 = [fold(x[:, PAIR:]) for x in tw]

    for ci in range(n_chunks):
        idx = [ci * n_pairs + p for p in range(n_pairs)]
        s_old = [pair_state(ci, p) if independent else s_scr[p] for p in range(n_pairs)]
        pc = [jnp.exp(lc_ref[0, (ci + 1) * c - 1:(ci + 1) * c, PAIR * p:PAIR * (p + 1)]) for p in range(n_pairs)]
        ws = [_dot_nt(jnp.concatenate([w_mat[i].astype(BF16), rt[i]], axis=0), s_old[p].astype(BF16))
              for p, i in enumerate(idx)]
        u = [ws[p][:c] + u0[i] for p, i in enumerate(idx)]
        y = [ws[p][c:] + yv[i] + _dot1(_dot, a_rb[i], stack(u[p])) for p, i in enumerate(idx)]
        upd = [_dot_tn(jnp.concatenate([tile(v_ref, units[i]), u[p].astype(BF16)], axis=0),
                       jnp.concatenate([(kh[i] * pc[p]).astype(BF16), (bh[i] * pc[p]).astype(BF16)], axis=0))
               for p, i in enumerate(idx)]
        for p in range(n_pairs):
            s_new = s_old[p] * pc[p] + jnp.where(same_state, upd[p], 0.0)
            y_ref[0, ci * c:(ci + 1) * c, PAIR * p:PAIR * (p + 1)] = y[p]
            if independent:
                sout_ref[ci, 2 * p] = s_new[:HEAD, :HEAD]
                sout_ref[ci, 2 * p + 1] = s_new[HEAD:, HEAD:]
            else:
                s_scr[p] = s_new

    if not independent:
        @pl.when(j == nj - 1)
        def _():
            for p in range(n_pairs):
                s = s_scr[p]
                sout_ref[0, 2 * p] = s[:HEAD, :HEAD]
                sout_ref[0, 2 * p + 1] = s[HEAD:, HEAD:]


def _rwkv(rt, at, kh, bh, v, lc, s0, tc, chunk, independent=False):
    b, t, _ = rt.shape
    tok = pl.BlockSpec((1, tc, W_RWKV), lambda bi, j: (bi, j, 0))
    if independent:
        st = pl.BlockSpec((tc // chunk, N_RWKV, HEAD, HEAD), lambda bi, j: (j, 0, 0, 0))
    else:
        st = pl.BlockSpec((1, N_RWKV, HEAD, HEAD), lambda bi, j: (bi, 0, 0, 0))
    return pl.pallas_call(
        functools.partial(_rwkv_body, chunk, independent),
        grid=(b, t // tc),
        in_specs=[tok] * 6 + [st],
        out_specs=[tok, st],
        out_shape=[jax.ShapeDtypeStruct((b, t, W_RWKV), F32), jax.ShapeDtypeStruct(s0.shape, F32)],
        scratch_shapes=[pltpu.VMEM((N_RWKV // 2, PAIR, PAIR), F32)],
        compiler_params=_params(2),
        name="rwkv",
    )(rt, at, kh, bh, v, lc, s0)


def _out_body(x_ref, of_ref, gf_ref, y_ref, gr_ref, bg_ref, om_ref, gnw_ref, gnb_ref, ones_ref,
              wf_ref, wr_ref, wm_ref, o_ref):
    ones_pair = ones_ref[...]
    y = y_ref[0]
    mu = _segsum(y, ones_pair) * (1.0 / HEAD)
    d = y - mu
    var = _segsum(d * d, ones_pair) * (1.0 / HEAD)
    yn = d * lax.rsqrt(var + GN_EPS) * gnw_ref[...] + gnb_ref[...]
    o_r = yn * gr_ref[0].astype(F32) + bg_ref[0].astype(F32)
    acc = _dot((of_ref[0].astype(F32) * gf_ref[0].astype(F32)).astype(BF16), wf_ref[...])
    acc = acc + _dot(o_r.astype(BF16), wr_ref[...])
    acc = acc + _dot(om_ref[0], wm_ref[...])
    o_ref[0] = x_ref[0] + acc


def _out(x, of, gf, y, gr, bg, om, gnw, gnb, ones_pair, wf, wr, wm, tm):
    b, t, _ = x.shape
    tok = lambda w: pl.BlockSpec((1, tm, w), lambda bi, i: (bi, i, 0))
    return pl.pallas_call(
        _out_body,
        grid=(b, t // tm),
        in_specs=[tok(D_MODEL), tok(W_FOX), tok(W_FOX), tok(W_RWKV), tok(W_RWKV), tok(W_RWKV), tok(W_MEM),
                  _full(gnw.shape), _full(gnb.shape), _full(ones_pair.shape),
                  _full(wf.shape), _full(wr.shape), _full(wm.shape)],
        out_specs=tok(D_MODEL),
        out_shape=jax.ShapeDtypeStruct((b, t, D_MODEL), F32),
        compiler_params=_params(2),
        name="out_proj",
    )(x, of, gf, y, gr, bg, om, gnw, gnb, ones_pair, wf, wr, wm)


def _pad_in_columns(w):
    r0 = FOX_COLS
    m0 = FOX_COLS + RWKV_COLS
    zeros = lambda n: jnp.zeros(w.shape[:-1] + (n,), w.dtype)
    return jnp.concatenate([
        w[..., 0:3 * W_FOX], w[..., 3 * W_FOX + N_FOX:FOX_COLS],
        w[..., r0:r0 + 3 * W_RWKV], w[..., r0 + 3 * W_RWKV + 2 * LORA:r0 + RWKV_COLS],
        w[..., m0:m0 + MEM_COLS],
        w[..., 3 * W_FOX:3 * W_FOX + N_FOX], zeros(SM_WD - N_FOX),
        w[..., r0 + 3 * W_RWKV:r0 + 3 * W_RWKV + 2 * LORA], zeros(SMALL - SM_AD - LORA),
    ], axis=-1)


def _pad_shift_row(a):
    zeros = lambda n: jnp.zeros(a.shape[:-1] + (n,), a.dtype)
    return jnp.concatenate([
        a[..., 0:3 * W_RWKV], a[..., 3 * W_RWKV + 2 * LORA:RWKV_COLS],
        zeros(SM_WD), a[..., 3 * W_RWKV:3 * W_RWKV + 2 * LORA], zeros(SMALL - SM_AD - LORA)], axis=-1)


def _unpad_shift_row(a):
    small = 4 * W_RWKV
    return jnp.concatenate([a[..., 0:3 * W_RWKV], a[..., small + SM_WD:small + SM_AD + LORA],
                            a[..., 3 * W_RWKV:4 * W_RWKV]], axis=-1)


def _bias_placement():
    place = np.zeros((SMALL, 2 * W_FOX), np.float32)
    const = np.zeros((1, 2 * W_FOX), np.float32)
    for h in range(N_FOX):
        for j in range(N_BIAS):
            place[BIAS_STRIDE * j + h, HEAD * h + j] = 1.0
            place[BIAS_STRIDE * j + h, W_FOX + HEAD * h + N_BIAS + j] = -1.0
            const[0, HEAD * h + N_BIAS + j] = 1.0
            const[0, W_FOX + HEAD * h + j] = 1.0
    return place, const


def _block_ones(width):
    h = np.arange(width) // HEAD
    return (h[:, None] == h[None, :]).astype(np.float32)


def _layer(x, shift_prev, s0, mk, mv, fox_past, wts, ones_pair, tm, tq, tc, chunk, flat_out=False):
    (proj_wts, gnw, gnb, wf, wr, wm) = wts
    b, t, _ = x.shape
    if flat_out:
        first = jnp.pad(_pad_shift_row(shift_prev), ((0, 0), (0, t - 1), (0, 0))).reshape(1, b * t, SHIFT_W)
        outs = _proj(x.reshape(1, b * t, D_MODEL), first, mk, mv, proj_wts, b * t, chunk, with_vt=False, seq=t)
        outs = [o.reshape(b, t, o.shape[-1]) for o in outs]
        outs[-1] = outs[-1][:, t - 1:t, :]
    else:
        outs = _proj(x, _pad_shift_row(shift_prev), mk, mv, proj_wts, tm, chunk, with_vt=fox_past is None)
    (qs, kb, augq, augk, kn, v, logf, gf, rt, at, kh, bh, rv, lc, gr, bg, om, shift_out, *vt) = outs
    if fox_past is None:
        of = _fox(qs, augq, kb, augk, vt[0], tq)
    else:
        k_past, v_past, logf_past = fox_past
        lfp = jnp.pad(logf_past.astype(F32), ((0, 0), (0, 0), (0, SMALL - N_FOX)))
        p_len = k_past.shape[1]
        of = _fox_step(qs, augq, kb, augk, v, k_past.reshape(b, p_len, W_FOX).astype(F32),
                       v_past.reshape(b, p_len, W_FOX).astype(F32), lfp)
    rows = lambda a: a.reshape(1, b * t, a.shape[-1]) if flat_out else a
    if flat_out:
        y_r, s_new = _rwkv(rows(rt), rows(at), rows(kh), rows(bh), rows(rv), rows(lc), s0,
                           min(b, STEP_SEQS_PER_TILE) * t, chunk, independent=True)
        y_r = y_r.reshape(b, t, W_RWKV)
    else:
        y_r, s_new = _rwkv(rt, at, kh, bh, rv, lc, s0, tc, chunk)
    y = _out(rows(x), rows(of), rows(gf), rows(y_r), rows(gr), rows(bg), rows(om), gnw, gnb, ones_pair, wf, wr, wm,
             b * t if flat_out else tm).reshape(b, t, D_MODEL)
    return (y, kn.reshape(b, t, N_FOX, HEAD), v.reshape(b, t, N_FOX, HEAD), logf, s_new, _unpad_shift_row(shift_out))


def kernel(x_prompt, x_sample, mem_prompt, cache_fox_k, cache_fox_v, cache_fox_logf, cache_mem_k, cache_mem_v, state_rwkv, state_rwkv_shift, norm_g, w_in, fox_q_g, fox_k_g, fox_b_f, rwkv_mu, rwkv_w0, rwkv_w_up, rwkv_a0, rwkv_a_up, rwkv_k_k, rwkv_k_a, rwkv_r_k, rwkv_gn_w, rwkv_gn_b, mem_norm_g, w_mem_kv, mem_q_g, mem_k_g, w_out):
    depth = w_in.shape[0]
    bp = x_prompt.shape[0]
    place_np, const_np = _bias_placement()
    ones_pair = jnp.asarray(_block_ones(PAIR), BF16)
    pbias = jnp.asarray(place_np, BF16)
    bias1 = jnp.asarray(const_np, F32)

    yp, ys = x_prompt, x_sample
    outs = [[] for _ in range(12)]
    for l in range(depth):
        row = lambda a: a[l].reshape(1, -1).astype(F32)
        tile = lambda a, n: jnp.tile(a[l].reshape(1, -1).astype(F32), (1, n))
        w_pad = _pad_in_columns(w_in[l]).astype(BF16)
        bf_pad = jnp.pad(row(fox_b_f), ((0, 0), (0, SMALL - N_FOX)))
        wup_pad = jnp.pad(rwkv_w_up[l].astype(F32), ((SM_WD, SMALL - SM_WD - LORA), (0, 0)))
        aup_pad = jnp.pad(rwkv_a_up[l].astype(F32), ((SM_AD, SMALL - SM_AD - LORA), (0, 0)))
        proj_wts = (row(norm_g), w_pad, tile(fox_q_g, N_FOX), tile(fox_k_g, N_FOX), bf_pad,
                    _pad_shift_row(row(rwkv_mu)),
                    row(rwkv_w0), row(rwkv_a0), wup_pad, aup_pad, row(rwkv_k_k), row(rwkv_k_a), row(rwkv_r_k),
                    tile(mem_q_g, N_MEM), ones_pair, pbias, bias1)
        wo = w_out[l].astype(BF16)
        wts = (proj_wts, row(rwkv_gn_w), row(rwkv_gn_b), wo[:W_FOX], wo[W_FOX:W_FOX + W_RWKV], wo[W_FOX + W_RWKV:])

        mk2, mv2, mkb, mvb = _mem_kv(mem_prompt, row(mem_norm_g), w_mem_kv[l].astype(BF16), tile(mem_k_g, N_MEM),
                                     ones_pair)
        shift_zero = jnp.zeros((bp, 1, RWKV_COLS), F32)
        s_zero = jnp.zeros((bp, N_RWKV, HEAD, HEAD), F32)
        yp, k, v, lf, s_new, sh_new = _layer(yp, shift_zero, s_zero, mkb, mvb, None, wts, ones_pair,
                                             tm=PROMPT_TM, tq=PROMPT_TQ, tc=PROMPT_TC, chunk=RWKV_CHUNK)
        mk = mk2.reshape(bp, N_MEM_TOK, N_MEM, HEAD)
        mv = mv2.reshape(bp, N_MEM_TOK, N_MEM, HEAD)
        for lst, val in zip(outs[:7], (k, v, lf, mk, mv, s_new, sh_new)):
            lst.append(val)
        bs, s_len = ys.shape[0], ys.shape[1]
        ys, k, v, lf, s_new, sh_new = _layer(
            ys, state_rwkv_shift[l], state_rwkv[l].astype(F32),
            cache_mem_k[l].reshape(bs, N_MEM_TOK, W_MEM).astype(F32),
            cache_mem_v[l].reshape(bs, N_MEM_TOK, W_MEM).astype(F32),
            (cache_fox_k[l], cache_fox_v[l], cache_fox_logf[l]), wts, ones_pair,
            tm=s_len, tq=None, tc=s_len, chunk=s_len, flat_out=True)
        for lst, val in zip(outs[7:], (k, v, lf, s_new, sh_new)):
            lst.append(val)
    return (yp, ys) + tuple(jnp.stack(o) for o in outs)
```

```python
import functools

import numpy as np
import jax
import jax.numpy as jnp
from jax import lax
from jax.experimental import pallas as pl
from jax.experimental.pallas import tpu as pltpu

F32 = jnp.float32
BF16 = jnp.bfloat16

D_MODEL = 1024
HEAD = 64
N_FOX = 6
N_RWKV = 6
N_MEM = 4
W_FOX = N_FOX * HEAD
W_RWKV = N_RWKV * HEAD
W_MEM = N_MEM * HEAD
N_MEM_TOK = 256
LORA = 32
NORM_EPS = 1e-6
GN_EPS = 64e-5
LOG2E = float(np.log2(np.e))
FOX_COLS = 4 * W_FOX + N_FOX
RWKV_COLS = 4 * W_RWKV + 2 * LORA
MEM_COLS = 2 * W_MEM

LANE = 128
PAIR = 2 * HEAD
SMALL = LANE
SM_WD = 32
SM_AD = 64
OFF_FQ, OFF_FK, OFF_FV, OFF_FG = 0, 384, 768, 1152
OFF_RR, OFF_RK, OFF_RV, OFF_RG = 1536, 1920, 2304, 2688
OFF_MQ, OFF_MG = 3072, 3328
OFF_SM = 3584
N_PAD = OFF_SM + SMALL
SHIFT_W = 4 * W_RWKV + SMALL
N_BIAS = 3
BIAS_STRIDE = 8
BF16_ROWS = 16
VROWS = HEAD + BF16_ROWS
RWKV_CHUNK = 64
VMEM_LIMIT = 56 * 1024 * 1024
PROMPT_TM = 512
PROMPT_TQ = 512
PROMPT_TC = 512
CUMSUM_BLOCK = 256
STEP_SUFFIX_BLOCK = 256
STEP_SEQS_PER_TILE = 8


def _dot(a, b):
    return jnp.dot(a, b, preferred_element_type=F32)


def _dot_nt(a, b):
    return lax.dot_general(a, b, (((1,), (1,)), ((), ())), preferred_element_type=F32)


def _dot_tn(a, b):
    return lax.dot_general(a, b, (((0,), (0,)), ((), ())), preferred_element_type=F32)


def _split2(x):
    hi = x.astype(BF16)
    lo = (x - hi.astype(F32)).astype(BF16)
    return hi, lo


def _split3(x):
    hi = x.astype(BF16)
    r1 = x - hi.astype(F32)
    mid = r1.astype(BF16)
    lo = (r1 - mid.astype(F32)).astype(BF16)
    return hi, mid, lo


def _dot1(fn, a, b):
    return fn(a.astype(BF16), b.astype(BF16))


def _exact_lhs_dot(m_bf16, x, parts):
    pieces = _split3(x) if parts == 3 else _split2(x)
    acc = _dot(m_bf16, pieces[0])
    for p in pieces[1:]:
        acc = acc + _dot(m_bf16, p)
    return acc


def _segsum(x, ones_pair):
    xb = x.astype(BF16)
    return jnp.concatenate([_dot(xb[:, PAIR * p:PAIR * (p + 1)], ones_pair) for p in range(x.shape[1] // PAIR)],
                           axis=1)


def _head_rms(t, gain, ones_pair):
    msq = _segsum(t * t, ones_pair) * (1.0 / HEAD)
    return t * lax.rsqrt(msq + NORM_EPS) * gain


def _silu(x):
    return x * jax.nn.sigmoid(x)


def _softplus(z):
    return jnp.maximum(z, 0.0) + jnp.log1p(jnp.exp(-jnp.abs(z)))


def _stack_heads(x, n_heads):
    head_of_lane = lax.broadcasted_iota(jnp.int32, x.shape, 1) // HEAD
    zero = jnp.zeros_like(x)
    return jnp.concatenate([jnp.where(head_of_lane == h, x, zero) for h in range(n_heads)], axis=0)


def _unstack_heads(x6, n_heads):
    rows = x6.shape[0] // n_heads
    head_of_lane = lax.broadcasted_iota(jnp.int32, (rows, x6.shape[1]), 1) // HEAD
    out = jnp.zeros((rows, x6.shape[1]), x6.dtype)
    for h in range(n_heads):
        out = jnp.where(head_of_lane == h, x6[h * rows:(h + 1) * rows], out)
    return out


def _full(shape):
    n = len(shape)
    return pl.BlockSpec(shape, lambda *_: (0,) * n)


def _params(n_axes):
    return pltpu.CompilerParams(dimension_semantics=("arbitrary",) * n_axes, vmem_limit_bytes=VMEM_LIMIT)


def _mem_kv_body(mem_ref, g_ref, w_ref, kg_ref, ones_ref, k_ref, v_ref, kb_ref, vb_ref):
    x = mem_ref[0]
    ms = jnp.mean(x * x, axis=-1, keepdims=True)
    xn = (x * lax.rsqrt(ms + NORM_EPS) * g_ref[...]).astype(BF16)
    kv = _dot(xn, w_ref[...])
    k = _head_rms(kv[:, :W_MEM], kg_ref[...], ones_ref[...])
    v = kv[:, W_MEM:]
    k_ref[0] = k
    v_ref[0] = v
    kb_ref[0] = k.astype(BF16)
    vb_ref[0] = v.astype(BF16)


def _mem_kv(mem, g, w_bf16, kg4, ones_pair):
    b = mem.shape[0]
    blk = pl.BlockSpec((1, N_MEM_TOK, W_MEM), lambda i: (i, 0, 0))
    return pl.pallas_call(
        _mem_kv_body,
        grid=(b,),
        in_specs=[pl.BlockSpec((1, N_MEM_TOK, D_MODEL), lambda i: (i, 0, 0)),
                  _full((1, D_MODEL)), _full((D_MODEL, 2 * W_MEM)), _full((1, W_MEM)), _full((PAIR, PAIR))],
        out_specs=[blk] * 4,
        out_shape=[jax.ShapeDtypeStruct((b, N_MEM_TOK, W_MEM), F32)] * 2
        + [jax.ShapeDtypeStruct((b, N_MEM_TOK, W_MEM), BF16)] * 2,
        compiler_params=_params(1),
        name="mem_kv",
    )(mem, g, w_bf16, kg4, ones_pair)


def _proj_body(chunk, with_vt, seq,
               x_ref, shift0_ref, mk_ref, mv_ref, ng_ref, w_ref,
               fqg_ref, fkg_ref, bf_ref, mu_ref, w0_ref, a0_ref, wup_ref, aup_ref,
               kk_ref, ka_ref, rk_ref, mqg_ref, ones_ref, pbias_ref, bias1_ref,
               qs_ref, kb_ref, augq_ref, augk_ref, kn_ref, v_ref, logf_ref, gf_ref,
               rt_ref, at_ref, kh_ref, bh_ref, rv_ref, lc_ref, gr_ref, bg_ref, om_ref, shift_out_ref,
               *vt_and_scratch):
    carry_shift, carry_c = vt_and_scratch[-2:]
    i = pl.program_id(1)
    tm = x_ref.shape[1]
    sub = min(tm, CUMSUM_BLOCK)

    if seq is None:
        @pl.when(i == 0)
        def _():
            carry_shift[...] = shift0_ref[0]
            carry_c[...] = jnp.zeros_like(carry_c)

    x = x_ref[0]
    ms = jnp.mean(x * x, axis=-1, keepdims=True)
    xn = (x * lax.rsqrt(ms + NORM_EPS) * ng_ref[...]).astype(BF16)

    def proj(off, width):
        return _dot(xn, w_ref[:, off:off + width])

    ones_pair = ones_ref[...]
    row = lax.broadcasted_iota(jnp.int32, (sub, sub), 0)
    col = lax.broadcasted_iota(jnp.int32, (sub, sub), 1)
    tri = jnp.where((col <= row) & ((row ^ col) < (sub if seq is None else seq)), 1.0, 0.0).astype(BF16)
    tri_chunk = jnp.where((col <= row) & ((row ^ col) < chunk), 1.0, 0.0).astype(BF16)
    blocks = [slice(s * sub, (s + 1) * sub) for s in range(tm // sub)]

    hs = proj(OFF_SM, SMALL)
    qn = _head_rms(proj(OFF_FQ, W_FOX), fqg_ref[...], ones_pair)
    kn = _head_rms(proj(OFF_FK, W_FOX), fkg_ref[...], ones_pair)
    hv = proj(OFF_FV, W_FOX)
    hg = proj(OFF_FG, W_FOX)
    kn_ref[0] = kn
    kb_ref[0] = kn.astype(BF16)
    qs_ref[0] = (qn * (HEAD ** -0.5 * LOG2E)).astype(BF16)
    v_ref[0] = hv
    if with_vt:
        hvt = hv.T
        ones_row = jnp.where(lax.broadcasted_iota(jnp.int32, (VROWS - HEAD, tm), 0) == 0, 1.0, 0.0)
        vt_and_scratch[0][0, 0] = jnp.concatenate(
            [blk for h in range(N_FOX) for blk in (hvt[HEAD * h:HEAD * (h + 1)], ones_row)], axis=0).astype(BF16)
    gf_ref[0] = _silu(hg).astype(BF16)
    f = hs + bf_ref[...]
    lane = lax.broadcasted_iota(jnp.int32, (tm, SMALL), 1)
    logf = jnp.where(lane < N_FOX, jnp.minimum(f, 0.0) - jnp.log1p(jnp.exp(-jnp.abs(f))), 0.0)
    logf_ref[0] = logf[:, 0:N_FOX]
    if seq is None:
        carry = carry_c[...]
        c_blocks = []
        for blk in blocks:
            cb = _exact_lhs_dot(tri, logf[blk], 3) + carry
            carry = cb[sub - 1:sub, :]
            c_blocks.append(cb)
        carry_c[...] = carry
    else:
        c_blocks = [_exact_lhs_dot(tri, logf[blk], 3) for blk in blocks]
    c = jnp.concatenate(c_blocks, axis=0)
    pieces = _split3(c * LOG2E)
    packed = pieces[0].astype(F32)
    for j in range(1, N_BIAS):
        packed = packed + pltpu.roll(pieces[j].astype(F32), BIAS_STRIDE * j, 1)
    aug = _dot(packed.astype(BF16), pbias_ref[...]) + bias1_ref[...]
    augq_ref[0] = aug[:, :W_FOX].astype(BF16)
    augk_ref[0] = aug[:, W_FOX:].astype(BF16)

    qm = _head_rms(proj(OFF_MQ, W_MEM), mqg_ref[...], ones_pair)
    qms = (qm * (HEAD ** -0.5 * LOG2E)).astype(BF16)

    def mem_attend(q_rows, g):
        s4 = _dot_nt(_stack_heads(q_rows, N_MEM), mk_ref[g].astype(BF16))
        e = jnp.exp2(s4 - jnp.max(s4, axis=1, keepdims=True))
        p4 = (e / jnp.sum(e, axis=1, keepdims=True)).astype(BF16)
        return _unstack_heads(_dot(p4, mv_ref[g].astype(BF16)), N_MEM)

    if seq is None:
        om = mem_attend(qms, 0)
    else:
        om = jnp.concatenate([mem_attend(qms[g * seq:(g + 1) * seq], g) for g in range(tm // seq)], axis=0)
    om_ref[0] = (om * _silu(proj(OFF_MG, W_MEM))).astype(BF16)

    row_in_seq = lax.broadcasted_iota(jnp.int32, (tm, 1), 0) % (tm if seq is None else seq)

    def tshift(cols, off, width):
        if seq is None:
            before_first = carry_shift[:, off:off + width]
            carry_shift[:, off:off + width] = cols[tm - 1:tm, :]
        else:
            before_first = shift0_ref[0, :, off:off + width]
            shift_out_ref[0, :, off:off + width] = cols
        prev = jnp.where(row_in_seq == 0, before_first, pltpu.roll(cols, 1, 0))
        return cols + (prev - cols) * mu_ref[:, off:off + width]

    r = tshift(proj(OFF_RR, W_RWKV), 0, W_RWKV)
    k = tshift(proj(OFF_RK, W_RWKV), W_RWKV, W_RWKV)
    v = tshift(proj(OFF_RV, W_RWKV), 2 * W_RWKV, W_RWKV)
    g = tshift(proj(OFF_RG, W_RWKV), 3 * W_RWKV, W_RWKV)
    sm = tshift(hs, 4 * W_RWKV, SMALL)
    if seq is None:
        shift_out_ref[0] = carry_shift[...]

    w_lin = w0_ref[...] + _dot1(_dot, jnp.tanh(sm), wup_ref[...])
    a = jax.nn.sigmoid(a0_ref[...] + _dot1(_dot, sm, aup_ref[...]))
    lw = -jnp.exp(-_softplus(-w_lin) - 0.5)
    kk = k * kk_ref[...]
    kk = kk * lax.rsqrt(jnp.maximum(_segsum(kk * kk, ones_pair), 1e-24))
    kt = k * (1.0 + (a - 1.0) * ka_ref[...])
    gr = _silu(g)
    gr_ref[0] = gr.astype(BF16)
    bg_ref[0] = (_segsum(r * kt * rk_ref[...], ones_pair) * v * gr).astype(BF16)
    lc = jnp.concatenate([_exact_lhs_dot(tri_chunk, lw[blk], 2) for blk in blocks], axis=0)
    e_neg = jnp.exp(-lc)
    rt_ref[0] = (r * jnp.exp(lc)).astype(BF16)
    at_ref[0] = (-kk * jnp.exp(lc - lw)).astype(BF16)
    kh_ref[0] = (kt * e_neg).astype(BF16)
    bh_ref[0] = (kk * a * e_neg).astype(BF16)
    rv_ref[0] = v.astype(BF16)
    n_ch = tm // chunk
    pick = (lax.broadcasted_iota(jnp.int32, (n_ch, tm), 1)
            == chunk * lax.broadcasted_iota(jnp.int32, (n_ch, tm), 0) + (chunk - 1))
    lc_ref[0] = _exact_lhs_dot(jnp.where(pick, 1.0, 0.0).astype(BF16), lc, 3)


def _proj(x, shift0, mk, mv, wts, tm, chunk, with_vt, seq=None):
    b, t, _ = x.shape
    grid = (b, t // tm)
    tok = lambda w: pl.BlockSpec((1, tm, w), lambda bi, i: (bi, i, 0))
    per_b = lambda s1, s2: pl.BlockSpec((1, s1, s2), lambda bi, i: (bi, 0, 0))
    w_specs = [_full(a.shape) for a in wts]
    bf = lambda w: jax.ShapeDtypeStruct((b, t, w), BF16)
    f32 = lambda w: jax.ShapeDtypeStruct((b, t, w), F32)
    out_shape = [bf(W_FOX)] * 4 + [f32(W_FOX), f32(W_FOX), f32(N_FOX), bf(W_FOX)] \
        + [bf(W_RWKV)] * 5 + [jax.ShapeDtypeStruct((b, t // chunk, W_RWKV), F32)] + [bf(W_RWKV), bf(W_RWKV), bf(W_MEM),
                                               jax.ShapeDtypeStruct(shift0.shape, F32)]
    shift_spec = per_b(1, SHIFT_W) if seq is None else tok(SHIFT_W)
    mem_spec = pl.BlockSpec((mk.shape[0] // b, N_MEM_TOK, W_MEM), lambda bi, i: (bi, 0, 0))
    out_specs = [tok(W_FOX)] * 4 + [tok(W_FOX), tok(W_FOX), tok(N_FOX), tok(W_FOX)] \
        + [tok(W_RWKV)] * 5 + [pl.BlockSpec((1, tm // chunk, W_RWKV), lambda bi, i: (bi, i, 0))] \
        + [tok(W_RWKV), tok(W_RWKV), tok(W_MEM), shift_spec]
    if with_vt:
        out_shape.append(jax.ShapeDtypeStruct((b, t // tm, N_FOX * VROWS, tm), BF16))
        out_specs.append(pl.BlockSpec((1, 1, N_FOX * VROWS, tm), lambda bi, i: (bi, i, 0, 0)))
    return pl.pallas_call(
        functools.partial(_proj_body, chunk, with_vt, seq),
        grid=grid,
        in_specs=[tok(D_MODEL), shift_spec, mem_spec, mem_spec] + w_specs,
        out_specs=out_specs,
        out_shape=out_shape,
        scratch_shapes=[pltpu.VMEM((1, SHIFT_W), F32), pltpu.VMEM((1, SMALL), F32)],
        compiler_params=_params(2),
        name="proj",
    )(x, shift0, mk, mv, *wts)


def _fox_body(qs_ref, aq_ref, kb_ref, ak_ref, vt_ref, o_ref, m_scr, acc_scr, sa_scr, sb_scr, qm_scr):
    qi = pl.program_id(2)
    tq = qs_ref.shape[1]
    slab = vt_ref.shape[3]
    slabs_per_tile = tq // slab
    m_scr[...] = jnp.full_like(m_scr, -1e30)
    acc_scr[...] = jnp.zeros_like(acc_scr)
    q = qs_ref[0]
    aq = aq_ref[0]
    lane_lo = lax.broadcasted_iota(jnp.int32, (tq, PAIR), 1) < HEAD
    zero = jnp.zeros_like(q)
    qm_scr[0] = jnp.concatenate([jnp.where(lane_lo, q, zero), jnp.where(lane_lo, aq, zero)], axis=1)
    qm_scr[1] = jnp.concatenate([jnp.where(lane_lo, zero, q), jnp.where(lane_lo, zero, aq)], axis=1)

    def scores(ki, s_scr):
        rows = pl.ds(pl.multiple_of(ki * tq, tq), tq)
        k2 = jnp.concatenate([kb_ref[0, rows, :], ak_ref[0, rows, :]], axis=1)
        for hh in range(2):
            s_scr[hh] = _dot_nt(k2, qm_scr[hh])

    def consume(ki, s_scr, masked):
        s = [s_scr[hh] for hh in range(2)]
        if masked:
            kpos = lax.broadcasted_iota(jnp.int32, (tq, tq), 0)
            qpos = lax.broadcasted_iota(jnp.int32, (tq, tq), 1)
            s = [jnp.where(kpos <= qpos, sh, -1e30) for sh in s]
        m_prev = [m_scr[hh] for hh in range(2)]
        m_new = [jnp.maximum(m_prev[hh], jnp.max(s[hh], axis=0, keepdims=True)) for hh in range(2)]
        p = [jnp.exp2(s[hh] - m_new[hh]).astype(BF16) for hh in range(2)]
        pv = []
        for hh in range(2):
            vt = jnp.concatenate([vt_ref[0, ki * slabs_per_tile + d, VROWS * hh:VROWS * (hh + 1), :]
                                  for d in range(slabs_per_tile)], axis=1)
            pv.append(_dot(vt, p[hh]))
        for hh in range(2):
            acc_scr[hh] = jnp.exp2(m_prev[hh] - m_new[hh]) * acc_scr[hh] + pv[hh]
            m_scr[hh] = m_new[hh]

    scores(0, sa_scr)

    def two_tiles(j, carry):
        scores(2 * j + 1, sb_scr)
        consume(2 * j, sa_scr, False)
        scores(2 * j + 2, sa_scr)
        consume(2 * j + 1, sb_scr, False)
        return carry

    lax.fori_loop(0, qi // 2, two_tiles, 0)

    @pl.when(qi % 2 == 0)
    def _():
        consume(qi, sa_scr, True)

    @pl.when(qi % 2 == 1)
    def _():
        scores(qi, sb_scr)
        consume(qi - 1, sa_scr, False)
        consume(qi, sb_scr, True)

    o_t = jnp.concatenate([acc_scr[hh][0:HEAD] / acc_scr[hh][HEAD:HEAD + 1] for hh in range(2)], axis=0)
    o_ref[0] = o_t.T.astype(BF16)


def _fox(qs, augq, kb, augk, vt, tq):
    b, t, _ = qs.shape
    n_slab, slab = vt.shape[1], vt.shape[3]
    q_blk = pl.BlockSpec((1, tq, PAIR), lambda bi, p, qi: (bi, qi, p))
    k_blk = pl.BlockSpec((1, t, PAIR), lambda bi, p, qi: (bi, 0, p))
    return pl.pallas_call(
        _fox_body,
        grid=(b, N_FOX // 2, t // tq),
        in_specs=[q_blk, q_blk, k_blk, k_blk,
                  pl.BlockSpec((1, n_slab, 2 * VROWS, slab), lambda bi, p, qi: (bi, 0, p, 0))],
        out_specs=q_blk,
        out_shape=jax.ShapeDtypeStruct((b, t, W_FOX), BF16),
        scratch_shapes=[pltpu.VMEM((2, 1, tq), F32), pltpu.VMEM((2, VROWS, tq), F32),
                        pltpu.VMEM((2, tq, tq), F32), pltpu.VMEM((2, tq, tq), F32),
                        pltpu.VMEM((2, tq, 2 * PAIR), BF16)],
        compiler_params=_params(3),
        name="fox",
    )(qs, augq, kb, augk, vt)


def _fox_step_body(qs_ref, aq_ref, kn_ref, ak_ref, vn_ref, kp_ref, vp_ref, lfp_ref, o_ref):
    s_new = qs_ref.shape[1]
    p_len = kp_ref.shape[1]
    blk = min(p_len, STEP_SUFFIX_BLOCK)
    row = lax.broadcasted_iota(jnp.int32, (blk, blk), 0)
    col = lax.broadcasted_iota(jnp.int32, (blk, blk), 1)
    upper = jnp.where(col > row, 1.0, 0.0).astype(BF16)
    later = jnp.zeros((1, SMALL), F32)
    sufs = []
    for bi in reversed(range(p_len // blk)):
        lf = lfp_ref[0, bi * blk:(bi + 1) * blk, :]
        within = _exact_lhs_dot(upper, lf, 3)
        sufs.insert(0, within + later)
        later = later + within[0:1, :] + lf[0:1, :]
    suf_t = (jnp.concatenate(sufs, axis=0) * LOG2E).T
    q6 = _stack_heads(qs_ref[0], N_FOX)
    a6 = _stack_heads(aq_ref[0], N_FOX)
    lane_in_head = lax.broadcasted_iota(jnp.int32, a6.shape, 1) % HEAD
    cq6 = jnp.sum(jnp.where(lane_in_head < N_BIAS, a6.astype(F32), 0.0), axis=1, keepdims=True)
    bias6 = jnp.concatenate([jnp.broadcast_to(suf_t[h:h + 1, :], (s_new, p_len)) for h in range(N_FOX)], axis=0)
    sp = _dot_nt(q6, kp_ref[0].astype(BF16)) + cq6 + bias6
    qrow = lax.broadcasted_iota(jnp.int32, (N_FOX * s_new, s_new), 0) % s_new
    kcol = lax.broadcasted_iota(jnp.int32, (N_FOX * s_new, s_new), 1)
    sn = jnp.where(kcol <= qrow, _dot_nt(q6, kn_ref[0]) + _dot_nt(a6, ak_ref[0]), -1e30)
    m = jnp.maximum(jnp.max(sp, axis=1, keepdims=True), jnp.max(sn, axis=1, keepdims=True))
    pp = jnp.exp2(sp - m)
    pn = jnp.exp2(sn - m)
    l = jnp.sum(pp, axis=1, keepdims=True) + jnp.sum(pn, axis=1, keepdims=True)
    o6 = (_dot((pp / l).astype(BF16), vp_ref[0].astype(BF16))
          + _dot((pn / l).astype(BF16), vn_ref[0].astype(BF16)))
    o_ref[0] = _unstack_heads(o6, N_FOX).astype(BF16)


def _fox_step(qs, augq, kb, augk, v_new, k_past, v_past, logf_past_pad):
    b, s_new, _ = qs.shape
    p_len = k_past.shape[1]
    tok = lambda rows, w: pl.BlockSpec((1, rows, w), lambda i: (i, 0, 0))
    return pl.pallas_call(
        _fox_step_body,
        grid=(b,),
        in_specs=[tok(s_new, W_FOX)] * 5 + [tok(p_len, W_FOX), tok(p_len, W_FOX), tok(p_len, SMALL)],
        out_specs=tok(s_new, W_FOX),
        out_shape=jax.ShapeDtypeStruct((b, s_new, W_FOX), BF16),
        compiler_params=_params(1),
        name="fox_step",
    )(qs, augq, kb, augk, v_new, k_past, v_past, logf_past_pad)


def _rwkv_body(chunk, independent, rt_ref, at_ref, kh_ref, bh_ref, v_ref, lc_ref, s0_ref, y_ref, sout_ref, s_scr):
    j = pl.program_id(1)
    nj = pl.num_programs(1)
    c = chunk
    c2 = 2 * chunk
    n_chunks = rt_ref.shape[1] // c
    n_pairs = N_RWKV // 2

    def pair_state(seq_i, p):
        z = jnp.zeros((HEAD, HEAD), F32)
        return jnp.concatenate([jnp.concatenate([s0_ref[seq_i, 2 * p], z], axis=1),
                                jnp.concatenate([z, s0_ref[seq_i, 2 * p + 1]], axis=1)], axis=0)

    if not independent:
        @pl.when(j == 0)
        def _():
            for p in range(n_pairs):
                s_scr[p] = pair_state(0, p)

    lane_lo = lax.broadcasted_iota(jnp.int32, (c, PAIR), 1) < HEAD
    row = lax.broadcasted_iota(jnp.int32, (c2, c2), 0)
    col = lax.broadcasted_iota(jnp.int32, (c2, c2), 1)
    same = (row >= c) == (col >= c)
    strict = same & (col < row)
    lower = same & (col <= row)
    eye = jnp.where(col == row, 1.0, 0.0)
    srow = lax.broadcasted_iota(jnp.int32, (PAIR, PAIR), 0)
    scol = lax.broadcasted_iota(jnp.int32, (PAIR, PAIR), 1)
    same_state = (srow >= HEAD) == (scol >= HEAD)

    def stack(x):
        zero = jnp.zeros_like(x)
        return jnp.concatenate([jnp.where(lane_lo, x, zero), jnp.where(lane_lo, zero, x)], axis=0)

    def fold(a):
        return a[:c] + a[c:]

    units = [(ci, p) for ci in range(n_chunks) for p in range(n_pairs)]

    def tile(ref, u):
        ci, p = u
        return ref[0, ci * c:(ci + 1) * c, PAIR * p:PAIR * (p + 1)]

    rt = [tile(rt_ref, u) for u in units]
    at2 = [stack(tile(at_ref, u)) for u in units]
    kh = [tile(kh_ref, u) for u in units]
    bh = [tile(bh_ref, u) for u in units]
    v2 = [stack(tile(v_ref, u)) for u in units]
    g = [_dot_nt(jnp.concatenate([at2[i], stack(rt[i])], axis=0),
                 jnp.concatenate([kh[i], kh[i], bh[i], bh[i]], axis=0)) for i in range(len(units))]
    a_ak = [jnp.where(strict, x[:c2, :c2], 0.0) for x in g]
    a_ab = [jnp.where(strict, x[:c2, c2:], 0.0) for x in g]
    a_rk = [fold(jnp.where(lower, x[c2:, :c2], 0.0)) for x in g]
    a_rb = [fold(jnp.where(lower, x[c2:, c2:], 0.0)) for x in g]
    akv2 = [_dot1(_dot, a_ak[i], v2[i]) for i in range(len(units))]
    yv = [_dot1(_dot, a_rk[i], v2[i]) for i in range(len(units))]
    inv = [eye + x for x in a_ab]
    pw = a_ab
    n = 1
    while 2 * n < c:
        pw = [_dot1(_dot, x, x) for x in pw]
        inv = [inv[i] + _dot1(_dot, inv[i], pw[i]) for i in range(len(units))]
        n *= 2
    tw = [_dot(inv[i].astype(BF16), jnp.concatenate([at2[i], akv2[i].astype(BF16)], axis=1))
          for i in range(len(units))]
    w_mat = [fold(x[:, :PAIR]) for x in tw]
    u0 = [fold(x[:, PAIR:]) for x in tw]

    for ci in range(n_chunks):
        idx = [ci * n_pairs + p for p in range(n_pairs)]
        s_old = [pair_state(ci, p) if independent else s_scr[p] for p in range(n_pairs)]
        pc = [jnp.exp(lc_ref[0, ci:ci + 1, PAIR * p:PAIR * (p + 1)]) for p in range(n_pairs)]
        ws = [_dot_nt(jnp.concatenate([w_mat[i].astype(BF16), rt[i]], axis=0), s_old[p].astype(BF16))
              for p, i in enumerate(idx)]
        u = [ws[p][:c] + u0[i] for p, i in enumerate(idx)]
        y = [ws[p][c:] + yv[i] + _dot1(_dot, a_rb[i], stack(u[p])) for p, i in enumerate(idx)]
        upd = [_dot_tn(jnp.concatenate([tile(v_ref, units[i]), u[p].astype(BF16)], axis=0),
                       jnp.concatenate([(kh[i] * pc[p]).astype(BF16), (bh[i] * pc[p]).astype(BF16)], axis=0))
               for p, i in enumerate(idx)]
        for p in range(n_pairs):
            s_new = s_old[p] * pc[p] + jnp.where(same_state, upd[p], 0.0)
            y_ref[0, ci * c:(ci + 1) * c, PAIR * p:PAIR * (p + 1)] = y[p].astype(BF16)
            if independent:
                sout_ref[ci, 2 * p] = s_new[:HEAD, :HEAD]
                sout_ref[ci, 2 * p + 1] = s_new[HEAD:, HEAD:]
            else:
                s_scr[p] = s_new

    if not independent:
        @pl.when(j == nj - 1)
        def _():
            for p in range(n_pairs):
                s = s_scr[p]
                sout_ref[0, 2 * p] = s[:HEAD, :HEAD]
                sout_ref[0, 2 * p + 1] = s[HEAD:, HEAD:]


def _rwkv(rt, at, kh, bh, v, lc, s0, tc, chunk, independent=False):
    b, t, _ = rt.shape
    tok = pl.BlockSpec((1, tc, W_RWKV), lambda bi, j: (bi, j, 0))
    if independent:
        st = pl.BlockSpec((tc // chunk, N_RWKV, HEAD, HEAD), lambda bi, j: (j, 0, 0, 0))
    else:
        st = pl.BlockSpec((1, N_RWKV, HEAD, HEAD), lambda bi, j: (bi, 0, 0, 0))
    return pl.pallas_call(
        functools.partial(_rwkv_body, chunk, independent),
        grid=(b, t // tc),
        in_specs=[tok] * 5 + [pl.BlockSpec((1, tc // chunk, W_RWKV), lambda bi, j: (bi, j, 0)), st],
        out_specs=[tok, st],
        out_shape=[jax.ShapeDtypeStruct((b, t, W_RWKV), BF16), jax.ShapeDtypeStruct(s0.shape, F32)],
        scratch_shapes=[pltpu.VMEM((N_RWKV // 2, PAIR, PAIR), F32)],
        compiler_params=_params(2),
        name="rwkv",
    )(rt, at, kh, bh, v, lc, s0)


def _out_body(x_ref, of_ref, gf_ref, y_ref, gr_ref, bg_ref, om_ref, gnw_ref, gnb_ref, ones_ref,
              wf_ref, wr_ref, wm_ref, o_ref):
    ones_pair = ones_ref[...]
    y = y_ref[0].astype(F32)
    mu = _segsum(y, ones_pair) * (1.0 / HEAD)
    d = y - mu
    var = _segsum(d * d, ones_pair) * (1.0 / HEAD)
    yn = d * lax.rsqrt(var + GN_EPS) * gnw_ref[...] + gnb_ref[...]
    o_r = yn * gr_ref[0].astype(F32) + bg_ref[0].astype(F32)
    acc = _dot((of_ref[0].astype(F32) * gf_ref[0].astype(F32)).astype(BF16), wf_ref[...])
    acc = acc + _dot(o_r.astype(BF16), wr_ref[...])
    acc = acc + _dot(om_ref[0], wm_ref[...])
    o_ref[0] = x_ref[0] + acc


def _out(x, of, gf, y, gr, bg, om, gnw, gnb, ones_pair, wf, wr, wm, tm):
    b, t, _ = x.shape
    tok = lambda w: pl.BlockSpec((1, tm, w), lambda bi, i: (bi, i, 0))
    return pl.pallas_call(
        _out_body,
        grid=(b, t // tm),
        in_specs=[tok(D_MODEL), tok(W_FOX), tok(W_FOX), tok(W_RWKV), tok(W_RWKV), tok(W_RWKV), tok(W_MEM),
                  _full(gnw.shape), _full(gnb.shape), _full(ones_pair.shape),
                  _full(wf.shape), _full(wr.shape), _full(wm.shape)],
        out_specs=tok(D_MODEL),
        out_shape=jax.ShapeDtypeStruct((b, t, D_MODEL), F32),
        compiler_params=_params(2),
        name="out_proj",
    )(x, of, gf, y, gr, bg, om, gnw, gnb, ones_pair, wf, wr, wm)


def _pad_in_columns(w):
    r0 = FOX_COLS
    m0 = FOX_COLS + RWKV_COLS
    zeros = lambda n: jnp.zeros(w.shape[:-1] + (n,), w.dtype)
    return jnp.concatenate([
        w[..., 0:3 * W_FOX], w[..., 3 * W_FOX + N_FOX:FOX_COLS],
        w[..., r0:r0 + 3 * W_RWKV], w[..., r0 + 3 * W_RWKV + 2 * LORA:r0 + RWKV_COLS],
        w[..., m0:m0 + MEM_COLS],
        w[..., 3 * W_FOX:3 * W_FOX + N_FOX], zeros(SM_WD - N_FOX),
        w[..., r0 + 3 * W_RWKV:r0 + 3 * W_RWKV + 2 * LORA], zeros(SMALL - SM_AD - LORA),
    ], axis=-1)


def _pad_shift_row(a):
    zeros = lambda n: jnp.zeros(a.shape[:-1] + (n,), a.dtype)
    return jnp.concatenate([
        a[..., 0:3 * W_RWKV], a[..., 3 * W_RWKV + 2 * LORA:RWKV_COLS],
        zeros(SM_WD), a[..., 3 * W_RWKV:3 * W_RWKV + 2 * LORA], zeros(SMALL - SM_AD - LORA)], axis=-1)


def _unpad_shift_row(a):
    small = 4 * W_RWKV
    return jnp.concatenate([a[..., 0:3 * W_RWKV], a[..., small + SM_WD:small + SM_AD + LORA],
                            a[..., 3 * W_RWKV:4 * W_RWKV]], axis=-1)


def _bias_placement():
    place = np.zeros((SMALL, 2 * W_FOX), np.float32)
    const = np.zeros((1, 2 * W_FOX), np.float32)
    for h in range(N_FOX):
        for j in range(N_BIAS):
            place[BIAS_STRIDE * j + h, HEAD * h + j] = 1.0
            place[BIAS_STRIDE * j + h, W_FOX + HEAD * h + N_BIAS + j] = -1.0
            const[0, HEAD * h + N_BIAS + j] = 1.0
            const[0, W_FOX + HEAD * h + j] = 1.0
    return place, const


def _block_ones(width):
    h = np.arange(width) // HEAD
    return (h[:, None] == h[None, :]).astype(np.float32)


def _layer(x, shift_prev, s0, mk, mv, fox_past, wts, ones_pair, tm, tq, tc, chunk, flat_out=False):
    (proj_wts, gnw, gnb, wf, wr, wm) = wts
    b, t, _ = x.shape
    if flat_out:
        first = jnp.pad(_pad_shift_row(shift_prev), ((0, 0), (0, t - 1), (0, 0))).reshape(1, b * t, SHIFT_W)
        outs = _proj(x.reshape(1, b * t, D_MODEL), first, mk, mv, proj_wts, b * t, chunk, with_vt=False, seq=t)
        outs = [o.reshape(b, -1, o.shape[-1]) for o in outs]
        outs[-1] = outs[-1][:, t - 1:t, :]
    else:
        outs = _proj(x, _pad_shift_row(shift_prev), mk, mv, proj_wts, tm, chunk, with_vt=fox_past is None)
    (qs, kb, augq, augk, kn, v, logf, gf, rt, at, kh, bh, rv, lc, gr, bg, om, shift_out, *vt) = outs
    if fox_past is None:
        of = _fox(qs, augq, kb, augk, vt[0], tq)
    else:
        k_past, v_past, logf_past = fox_past
        lfp = jnp.pad(logf_past.astype(F32), ((0, 0), (0, 0), (0, SMALL - N_FOX)))
        p_len = k_past.shape[1]
        of = _fox_step(qs, augq, kb, augk, v, k_past.reshape(b, p_len, W_FOX).astype(F32),
                       v_past.reshape(b, p_len, W_FOX).astype(F32), lfp)
    rows = lambda a: a.reshape(1, -1, a.shape[-1]) if flat_out else a
    if flat_out:
        y_r, s_new = _rwkv(rows(rt), rows(at), rows(kh), rows(bh), rows(rv), rows(lc), s0,
                           min(b, STEP_SEQS_PER_TILE) * t, chunk, independent=True)
        y_r = y_r.reshape(b, t, W_RWKV)
    else:
        y_r, s_new = _rwkv(rt, at, kh, bh, rv, lc, s0, tc, chunk)
    y = _out(rows(x), rows(of), rows(gf), rows(y_r), rows(gr), rows(bg), rows(om), gnw, gnb, ones_pair, wf, wr, wm,
             b * t if flat_out else tm).reshape(b, t, D_MODEL)
    return (y, kn.reshape(b, t, N_FOX, HEAD), v.reshape(b, t, N_FOX, HEAD), logf, s_new, _unpad_shift_row(shift_out))


def kernel(x_prompt, x_sample, mem_prompt, cache_fox_k, cache_fox_v, cache_fox_logf, cache_mem_k, cache_mem_v, state_rwkv, state_rwkv_shift, norm_g, w_in, fox_q_g, fox_k_g, fox_b_f, rwkv_mu, rwkv_w0, rwkv_w_up, rwkv_a0, rwkv_a_up, rwkv_k_k, rwkv_k_a, rwkv_r_k, rwkv_gn_w, rwkv_gn_b, mem_norm_g, w_mem_kv, mem_q_g, mem_k_g, w_out):
    depth = w_in.shape[0]
    bp = x_prompt.shape[0]
    place_np, const_np = _bias_placement()
    ones_pair = jnp.asarray(_block_ones(PAIR), BF16)
    pbias = jnp.asarray(place_np, BF16)
    bias1 = jnp.asarray(const_np, F32)

    yp, ys = x_prompt, x_sample
    outs = [[] for _ in range(12)]
    for l in range(depth):
        row = lambda a: a[l].reshape(1, -1).astype(F32)
        tile = lambda a, n: jnp.tile(a[l].reshape(1, -1).astype(F32), (1, n))
        w_pad = _pad_in_columns(w_in[l]).astype(BF16)
        bf_pad = jnp.pad(row(fox_b_f), ((0, 0), (0, SMALL - N_FOX)))
        wup_pad = jnp.pad(rwkv_w_up[l].astype(F32), ((SM_WD, SMALL - SM_WD - LORA), (0, 0)))
        aup_pad = jnp.pad(rwkv_a_up[l].astype(F32), ((SM_AD, SMALL - SM_AD - LORA), (0, 0)))
        proj_wts = (row(norm_g), w_pad, tile(fox_q_g, N_FOX), tile(fox_k_g, N_FOX), bf_pad,
                    _pad_shift_row(row(rwkv_mu)),
                    row(rwkv_w0), row(rwkv_a0), wup_pad, aup_pad, row(rwkv_k_k), row(rwkv_k_a), row(rwkv_r_k),
                    tile(mem_q_g, N_MEM), ones_pair, pbias, bias1)
        wo = w_out[l].astype(BF16)
        wts = (proj_wts, row(rwkv_gn_w), row(rwkv_gn_b), wo[:W_FOX], wo[W_FOX:W_FOX + W_RWKV], wo[W_FOX + W_RWKV:])

        mk2, mv2, mkb, mvb = _mem_kv(mem_prompt, row(mem_norm_g), w_mem_kv[l].astype(BF16), tile(mem_k_g, N_MEM),
                                     ones_pair)
        shift_zero = jnp.zeros((bp, 1, RWKV_COLS), F32)
        s_zero = jnp.zeros((bp, N_RWKV, HEAD, HEAD), F32)
        yp, k, v, lf, s_new, sh_new = _layer(yp, shift_zero, s_zero, mkb, mvb, None, wts, ones_pair,
                                             tm=PROMPT_TM, tq=PROMPT_TQ, tc=PROMPT_TC, chunk=RWKV_CHUNK)
        mk = mk2.reshape(bp, N_MEM_TOK, N_MEM, HEAD)
        mv = mv2.reshape(bp, N_MEM_TOK, N_MEM, HEAD)
        for lst, val in zip(outs[:7], (k, v, lf, mk, mv, s_new, sh_new)):
            lst.append(val)
        bs, s_len = ys.shape[0], ys.shape[1]
        ys, k, v, lf, s_new, sh_new = _layer(
            ys, state_rwkv_shift[l], state_rwkv[l].astype(F32),
            cache_mem_k[l].reshape(bs, N_MEM_TOK, W_MEM).astype(F32),
            cache_mem_v[l].reshape(bs, N_MEM_TOK, W_MEM).astype(F32),
            (cache_fox_k[l], cache_fox_v[l], cache_fox_logf[l]), wts, ones_pair,
            tm=s_len, tq=None, tc=s_len, chunk=s_len, flat_out=True)
        for lst, val in zip(outs[7:], (k, v, lf, s_new, sh_new)):
            lst.append(val)
    return (yp, ys) + tuple(jnp.stack(o) for o in outs)
```

```python
import functools

import numpy as np
import jax
import jax.numpy as jnp
from jax import lax
from jax.experimental import pallas as pl
from jax.experimental.pallas import tpu as pltpu

F32 = jnp.float32
BF16 = jnp.bfloat16

D_MODEL = 1024
HEAD = 64
N_FOX = 6
N_RWKV = 6
N_MEM = 4
W_FOX = N_FOX * HEAD
W_RWKV = N_RWKV * HEAD
W_MEM = N_MEM * HEAD
N_MEM_TOK = 256
LORA = 32
NORM_EPS = 1e-6
GN_EPS = 64e-5
LOG2E = float(np.log2(np.e))
FOX_COLS = 4 * W_FOX + N_FOX
RWKV_COLS = 4 * W_RWKV + 2 * LORA
MEM_COLS = 2 * W_MEM

LANE = 128
PAIR = 2 * HEAD
SMALL = LANE
SM_WD = 32
SM_AD = 64
OFF_FQ, OFF_FK, OFF_FV, OFF_FG = 0, 384, 768, 1152
OFF_RR, OFF_RK, OFF_RV, OFF_RG = 1536, 1920, 2304, 2688
OFF_MQ, OFF_MG = 3072, 3328
OFF_SM = 3584
N_PAD = OFF_SM + SMALL
SHIFT_W = 4 * W_RWKV + SMALL
N_BIAS = 3
BIAS_STRIDE = 8
BF16_ROWS = 16
VROWS = HEAD + BF16_ROWS
RWKV_CHUNK = 64
VMEM_LIMIT = 56 * 1024 * 1024
PROMPT_TM = 512
PROMPT_TQ = 512
PROMPT_TC = 512
FOX_UNROLL = 4
CUMSUM_BLOCK = 256
STEP_SUFFIX_BLOCK = 256
STEP_SEQS_PER_TILE = 8


def _dot(a, b):
    return jnp.dot(a, b, preferred_element_type=F32)


def _dot_nt(a, b):
    return lax.dot_general(a, b, (((1,), (1,)), ((), ())), preferred_element_type=F32)


def _dot_tn(a, b):
    return lax.dot_general(a, b, (((0,), (0,)), ((), ())), preferred_element_type=F32)


def _split2(x):
    hi = x.astype(BF16)
    lo = (x - hi.astype(F32)).astype(BF16)
    return hi, lo


def _split3(x):
    hi = x.astype(BF16)
    r1 = x - hi.astype(F32)
    mid = r1.astype(BF16)
    lo = (r1 - mid.astype(F32)).astype(BF16)
    return hi, mid, lo


def _dot1(fn, a, b):
    return fn(a.astype(BF16), b.astype(BF16))


def _exact_lhs_dot(m_bf16, x, parts):
    pieces = _split3(x) if parts == 3 else _split2(x)
    acc = _dot(m_bf16, pieces[0])
    for p in pieces[1:]:
        acc = acc + _dot(m_bf16, p)
    return acc


def _segsum(x, ones_pair):
    xb = x.astype(BF16)
    return jnp.concatenate([_dot(xb[:, PAIR * p:PAIR * (p + 1)], ones_pair) for p in range(x.shape[1] // PAIR)],
                           axis=1)


def _head_rms(t, gain, ones_pair):
    msq = _segsum(t * t, ones_pair) * (1.0 / HEAD)
    return t * lax.rsqrt(msq + NORM_EPS) * gain


def _silu(x):
    return x * jax.nn.sigmoid(x)


def _softplus(z):
    return jnp.maximum(z, 0.0) + jnp.log1p(jnp.exp(-jnp.abs(z)))


def _stack_heads(x, n_heads):
    head_of_lane = lax.broadcasted_iota(jnp.int32, x.shape, 1) // HEAD
    zero = jnp.zeros_like(x)
    return jnp.concatenate([jnp.where(head_of_lane == h, x, zero) for h in range(n_heads)], axis=0)


def _unstack_heads(x6, n_heads):
    rows = x6.shape[0] // n_heads
    head_of_lane = lax.broadcasted_iota(jnp.int32, (rows, x6.shape[1]), 1) // HEAD
    out = jnp.zeros((rows, x6.shape[1]), x6.dtype)
    for h in range(n_heads):
        out = jnp.where(head_of_lane == h, x6[h * rows:(h + 1) * rows], out)
    return out


def _full(shape):
    n = len(shape)
    return pl.BlockSpec(shape, lambda *_: (0,) * n)


def _params(n_axes):
    return pltpu.CompilerParams(dimension_semantics=("arbitrary",) * n_axes, vmem_limit_bytes=VMEM_LIMIT)


def _mem_kv_body(mem_ref, g_ref, w_ref, kg_ref, ones_ref, k_ref, v_ref, kb_ref, vb_ref):
    x = mem_ref[0]
    ms = jnp.mean(x * x, axis=-1, keepdims=True)
    xn = (x * lax.rsqrt(ms + NORM_EPS) * g_ref[...]).astype(BF16)
    kv = _dot(xn, w_ref[...])
    k = _head_rms(kv[:, :W_MEM], kg_ref[...], ones_ref[...])
    v = kv[:, W_MEM:]
    k_ref[0] = k
    v_ref[0] = v
    kb_ref[0] = k.astype(BF16)
    vb_ref[0] = v.astype(BF16)


def _mem_kv(mem, g, w_bf16, kg4, ones_pair):
    b = mem.shape[0]
    blk = pl.BlockSpec((1, N_MEM_TOK, W_MEM), lambda i: (i, 0, 0))
    return pl.pallas_call(
        _mem_kv_body,
        grid=(b,),
        in_specs=[pl.BlockSpec((1, N_MEM_TOK, D_MODEL), lambda i: (i, 0, 0)),
                  _full((1, D_MODEL)), _full((D_MODEL, 2 * W_MEM)), _full((1, W_MEM)), _full((PAIR, PAIR))],
        out_specs=[blk] * 4,
        out_shape=[jax.ShapeDtypeStruct((b, N_MEM_TOK, W_MEM), F32)] * 2
        + [jax.ShapeDtypeStruct((b, N_MEM_TOK, W_MEM), BF16)] * 2,
        compiler_params=_params(1),
        name="mem_kv",
    )(mem, g, w_bf16, kg4, ones_pair)


def _proj_body(chunk, with_vt, seq,
               x_ref, shift0_ref, mk_ref, mv_ref, ng_ref, w_ref,
               fqg_ref, fkg_ref, bf_ref, mu_ref, w0_ref, a0_ref, wup_ref, aup_ref,
               kk_ref, ka_ref, rk_ref, mqg_ref, ones_ref, pbias_ref, bias1_ref,
               qs_ref, kb_ref, augq_ref, augk_ref, kn_ref, v_ref, logf_ref, gf_ref,
               rt_ref, at_ref, kh_ref, bh_ref, rv_ref, lc_ref, gr_ref, bg_ref, om_ref, shift_out_ref,
               *vt_and_scratch):
    carry_shift, carry_c = vt_and_scratch[-2:]
    i = pl.program_id(1)
    tm = x_ref.shape[1]
    sub = min(tm, CUMSUM_BLOCK)

    if seq is None:
        @pl.when(i == 0)
        def _():
            carry_shift[...] = shift0_ref[0]
            carry_c[...] = jnp.zeros_like(carry_c)

    x = x_ref[0]
    ms = jnp.mean(x * x, axis=-1, keepdims=True)
    xn = (x * lax.rsqrt(ms + NORM_EPS) * ng_ref[...]).astype(BF16)

    def proj(off, width):
        return _dot(xn, w_ref[:, off:off + width])

    ones_pair = ones_ref[...]
    row = lax.broadcasted_iota(jnp.int32, (sub, sub), 0)
    col = lax.broadcasted_iota(jnp.int32, (sub, sub), 1)
    tri = jnp.where((col <= row) & ((row ^ col) < (sub if seq is None else seq)), 1.0, 0.0).astype(BF16)
    tri_chunk = jnp.where((col <= row) & ((row ^ col) < chunk), 1.0, 0.0).astype(BF16)
    blocks = [slice(s * sub, (s + 1) * sub) for s in range(tm // sub)]

    hs = proj(OFF_SM, SMALL)
    qn = _head_rms(proj(OFF_FQ, W_FOX), fqg_ref[...], ones_pair)
    kn = _head_rms(proj(OFF_FK, W_FOX), fkg_ref[...], ones_pair)
    hv = proj(OFF_FV, W_FOX)
    hg = proj(OFF_FG, W_FOX)
    kn_ref[0] = kn
    kb_ref[0] = kn.astype(BF16)
    qs_ref[0] = (qn * (HEAD ** -0.5 * LOG2E)).astype(BF16)
    v_ref[0] = hv
    if with_vt:
        hvt = hv.T
        ones_row = jnp.where(lax.broadcasted_iota(jnp.int32, (VROWS - HEAD, tm), 0) == 0, 1.0, 0.0)
        vt_and_scratch[0][0, 0] = jnp.concatenate(
            [blk for h in range(N_FOX) for blk in (hvt[HEAD * h:HEAD * (h + 1)], ones_row)], axis=0).astype(BF16)
    gf_ref[0] = _silu(hg).astype(BF16)
    f = hs + bf_ref[...]
    lane = lax.broadcasted_iota(jnp.int32, (tm, SMALL), 1)
    logf = jnp.where(lane < N_FOX, jnp.minimum(f, 0.0) - jnp.log1p(jnp.exp(-jnp.abs(f))), 0.0)
    logf_ref[0] = logf[:, 0:N_FOX]
    if seq is None:
        carry = carry_c[...]
        c_blocks = []
        for blk in blocks:
            cb = _exact_lhs_dot(tri, logf[blk], 3) + carry
            carry = cb[sub - 1:sub, :]
            c_blocks.append(cb)
        carry_c[...] = carry
    else:
        c_blocks = [_exact_lhs_dot(tri, logf[blk], 3) for blk in blocks]
    c = jnp.concatenate(c_blocks, axis=0)
    pieces = _split3(c * LOG2E)
    packed = pieces[0].astype(F32)
    for j in range(1, N_BIAS):
        packed = packed + pltpu.roll(pieces[j].astype(F32), BIAS_STRIDE * j, 1)
    aug = _dot(packed.astype(BF16), pbias_ref[...]) + bias1_ref[...]
    augq_ref[0] = aug[:, :W_FOX].astype(BF16)
    augk_ref[0] = aug[:, W_FOX:].astype(BF16)

    qm = _head_rms(proj(OFF_MQ, W_MEM), mqg_ref[...], ones_pair)
    qms = (qm * (HEAD ** -0.5 * LOG2E)).astype(BF16)

    def mem_attend(q_rows, g):
        s4 = _dot_nt(_stack_heads(q_rows, N_MEM), mk_ref[g].astype(BF16))
        e = jnp.exp2(s4 - jnp.max(s4, axis=1, keepdims=True))
        p4 = (e / jnp.sum(e, axis=1, keepdims=True)).astype(BF16)
        return _unstack_heads(_dot(p4, mv_ref[g].astype(BF16)), N_MEM)

    if seq is None:
        om = mem_attend(qms, 0)
    else:
        om = jnp.concatenate([mem_attend(qms[g * seq:(g + 1) * seq], g) for g in range(tm // seq)], axis=0)
    om_ref[0] = (om * _silu(proj(OFF_MG, W_MEM))).astype(BF16)

    row_in_seq = lax.broadcasted_iota(jnp.int32, (tm, 1), 0) % (tm if seq is None else seq)

    def tshift(cols, off, width):
        if seq is None:
            before_first = carry_shift[:, off:off + width]
            carry_shift[:, off:off + width] = cols[tm - 1:tm, :]
        else:
            before_first = shift0_ref[0, :, off:off + width]
            shift_out_ref[0, :, off:off + width] = cols
        prev = jnp.where(row_in_seq == 0, before_first, pltpu.roll(cols, 1, 0))
        return cols + (prev - cols) * mu_ref[:, off:off + width]

    r = tshift(proj(OFF_RR, W_RWKV), 0, W_RWKV)
    k = tshift(proj(OFF_RK, W_RWKV), W_RWKV, W_RWKV)
    v = tshift(proj(OFF_RV, W_RWKV), 2 * W_RWKV, W_RWKV)
    g = tshift(proj(OFF_RG, W_RWKV), 3 * W_RWKV, W_RWKV)
    sm = tshift(hs, 4 * W_RWKV, SMALL)
    if seq is None:
        shift_out_ref[0] = carry_shift[...]

    w_lin = w0_ref[...] + _dot1(_dot, jnp.tanh(sm), wup_ref[...])
    a = jax.nn.sigmoid(a0_ref[...] + _dot1(_dot, sm, aup_ref[...]))
    lw = -jnp.exp(-_softplus(-w_lin) - 0.5)
    kk = k * kk_ref[...]
    kk = kk * lax.rsqrt(jnp.maximum(_segsum(kk * kk, ones_pair), 1e-24))
    kt = k * (1.0 + (a - 1.0) * ka_ref[...])
    gr = _silu(g)
    gr_ref[0] = gr.astype(BF16)
    bg_ref[0] = (_segsum(r * kt * rk_ref[...], ones_pair) * v * gr).astype(BF16)
    lc = jnp.concatenate([_exact_lhs_dot(tri_chunk, lw[blk], 2) for blk in blocks], axis=0)
    e_neg = jnp.exp(-lc)
    rt_ref[0] = (r * jnp.exp(lc)).astype(BF16)
    at_ref[0] = (-kk * jnp.exp(lc - lw)).astype(BF16)
    kh_ref[0] = (kt * e_neg).astype(BF16)
    bh_ref[0] = (kk * a * e_neg).astype(BF16)
    rv_ref[0] = v.astype(BF16)
    lc_ref[0] = lc


def _proj(x, shift0, mk, mv, wts, tm, chunk, with_vt, seq=None):
    b, t, _ = x.shape
    grid = (b, t // tm)
    tok = lambda w: pl.BlockSpec((1, tm, w), lambda bi, i: (bi, i, 0))
    per_b = lambda s1, s2: pl.BlockSpec((1, s1, s2), lambda bi, i: (bi, 0, 0))
    w_specs = [_full(a.shape) for a in wts]
    bf = lambda w: jax.ShapeDtypeStruct((b, t, w), BF16)
    f32 = lambda w: jax.ShapeDtypeStruct((b, t, w), F32)
    out_shape = [bf(W_FOX)] * 4 + [f32(W_FOX), f32(W_FOX), f32(N_FOX), bf(W_FOX)] \
        + [bf(W_RWKV)] * 5 + [f32(W_RWKV)] + [bf(W_RWKV), bf(W_RWKV), bf(W_MEM),
                                               jax.ShapeDtypeStruct(shift0.shape, F32)]
    shift_spec = per_b(1, SHIFT_W) if seq is None else tok(SHIFT_W)
    mem_spec = pl.BlockSpec((mk.shape[0] // b, N_MEM_TOK, W_MEM), lambda bi, i: (bi, 0, 0))
    out_specs = [tok(W_FOX)] * 4 + [tok(W_FOX), tok(W_FOX), tok(N_FOX), tok(W_FOX)] \
        + [tok(W_RWKV)] * 6 + [tok(W_RWKV), tok(W_RWKV), tok(W_MEM), shift_spec]
    if with_vt:
        out_shape.append(jax.ShapeDtypeStruct((b, t // tm, N_FOX * VROWS, tm), BF16))
        out_specs.append(pl.BlockSpec((1, 1, N_FOX * VROWS, tm), lambda bi, i: (bi, i, 0, 0)))
    return pl.pallas_call(
        functools.partial(_proj_body, chunk, with_vt, seq),
        grid=grid,
        in_specs=[tok(D_MODEL), shift_spec, mem_spec, mem_spec] + w_specs,
        out_specs=out_specs,
        out_shape=out_shape,
        scratch_shapes=[pltpu.VMEM((1, SHIFT_W), F32), pltpu.VMEM((1, SMALL), F32)],
        compiler_params=_params(2),
        name="proj",
    )(x, shift0, mk, mv, *wts)


def _fox_body(qs_ref, aq_ref, kb_ref, ak_ref, vt_ref, o_ref, m_scr, acc_scr, sa_scr, sb_scr, qm_scr):
    qi = pl.program_id(2)
    tq = qs_ref.shape[1]
    slab = vt_ref.shape[3]
    slabs_per_tile = tq // slab
    m_scr[...] = jnp.full_like(m_scr, -1e30)
    acc_scr[...] = jnp.zeros_like(acc_scr)
    q = qs_ref[0]
    aq = aq_ref[0]
    lane_lo = lax.broadcasted_iota(jnp.int32, (tq, PAIR), 1) < HEAD
    zero = jnp.zeros_like(q)
    qm_scr[0] = jnp.concatenate([jnp.where(lane_lo, q, zero), jnp.where(lane_lo, aq, zero)], axis=1)
    qm_scr[1] = jnp.concatenate([jnp.where(lane_lo, zero, q), jnp.where(lane_lo, zero, aq)], axis=1)

    def scores(ki, s_scr):
        rows = pl.ds(pl.multiple_of(ki * tq, tq), tq)
        k2 = jnp.concatenate([kb_ref[0, rows, :], ak_ref[0, rows, :]], axis=1)
        for hh in range(2):
            s_scr[hh] = _dot_nt(k2, qm_scr[hh])

    def consume(ki, s_scr, masked):
        s = [s_scr[hh] for hh in range(2)]
        if masked:
            kpos = lax.broadcasted_iota(jnp.int32, (tq, tq), 0)
            qpos = lax.broadcasted_iota(jnp.int32, (tq, tq), 1)
            s = [jnp.where(kpos <= qpos, sh, -1e30) for sh in s]
        m_prev = [m_scr[hh] for hh in range(2)]
        m_new = [jnp.maximum(m_prev[hh], jnp.max(s[hh], axis=0, keepdims=True)) for hh in range(2)]
        p = [jnp.exp2(s[hh] - m_new[hh]).astype(BF16) for hh in range(2)]
        pv = []
        for hh in range(2):
            vt = jnp.concatenate([vt_ref[0, ki * slabs_per_tile + d, VROWS * hh:VROWS * (hh + 1), :]
                                  for d in range(slabs_per_tile)], axis=1)
            pv.append(_dot(vt, p[hh]))
        for hh in range(2):
            acc_scr[hh] = jnp.exp2(m_prev[hh] - m_new[hh]) * acc_scr[hh] + pv[hh]
            m_scr[hh] = m_new[hh]

    bufs = (sa_scr, sb_scr)
    scores(0, bufs[0])

    def tile_group(j, carry):
        for d in range(FOX_UNROLL):
            scores(FOX_UNROLL * j + d + 1, bufs[(d + 1) % 2])
            consume(FOX_UNROLL * j + d, bufs[d % 2], False)
        return carry

    lax.fori_loop(0, qi // FOX_UNROLL, tile_group, 0)
    base = (qi // FOX_UNROLL) * FOX_UNROLL
    for rem in range(FOX_UNROLL):
        @pl.when(qi - base == rem)
        def _():
            for d in range(rem):
                scores(base + d + 1, bufs[(d + 1) % 2])
                consume(base + d, bufs[d % 2], False)
            consume(qi, bufs[rem % 2], True)

    o_t = jnp.concatenate([acc_scr[hh][0:HEAD] / acc_scr[hh][HEAD:HEAD + 1] for hh in range(2)], axis=0)
    o_ref[0] = o_t.T.astype(BF16)


def _fox(qs, augq, kb, augk, vt, tq):
    b, t, _ = qs.shape
    n_slab, slab = vt.shape[1], vt.shape[3]
    q_blk = pl.BlockSpec((1, tq, PAIR), lambda bi, p, qi: (bi, qi, p))
    k_blk = pl.BlockSpec((1, t, PAIR), lambda bi, p, qi: (bi, 0, p))
    return pl.pallas_call(
        _fox_body,
        grid=(b, N_FOX // 2, t // tq),
        in_specs=[q_blk, q_blk, k_blk, k_blk,
                  pl.BlockSpec((1, n_slab, 2 * VROWS, slab), lambda bi, p, qi: (bi, 0, p, 0))],
        out_specs=q_blk,
        out_shape=jax.ShapeDtypeStruct((b, t, W_FOX), BF16),
        scratch_shapes=[pltpu.VMEM((2, 1, tq), F32), pltpu.VMEM((2, VROWS, tq), F32),
                        pltpu.VMEM((2, tq, tq), F32), pltpu.VMEM((2, tq, tq), F32),
                        pltpu.VMEM((2, tq, 2 * PAIR), BF16)],
        compiler_params=_params(3),
        name="fox",
    )(qs, augq, kb, augk, vt)


def _fox_step_body(qs_ref, aq_ref, kn_ref, ak_ref, vn_ref, kp_ref, vp_ref, lfp_ref, o_ref):
    s_new = qs_ref.shape[1]
    p_len = kp_ref.shape[1]
    blk = min(p_len, STEP_SUFFIX_BLOCK)
    row = lax.broadcasted_iota(jnp.int32, (blk, blk), 0)
    col = lax.broadcasted_iota(jnp.int32, (blk, blk), 1)
    upper = jnp.where(col > row, 1.0, 0.0).astype(BF16)
    later = jnp.zeros((1, SMALL), F32)
    sufs = []
    for bi in reversed(range(p_len // blk)):
        lf = lfp_ref[0, bi * blk:(bi + 1) * blk, :]
        within = _exact_lhs_dot(upper, lf, 3)
        sufs.insert(0, within + later)
        later = later + within[0:1, :] + lf[0:1, :]
    suf_t = (jnp.concatenate(sufs, axis=0) * LOG2E).T
    q6 = _stack_heads(qs_ref[0], N_FOX)
    a6 = _stack_heads(aq_ref[0], N_FOX)
    lane_in_head = lax.broadcasted_iota(jnp.int32, a6.shape, 1) % HEAD
    cq6 = jnp.sum(jnp.where(lane_in_head < N_BIAS, a6.astype(F32), 0.0), axis=1, keepdims=True)
    bias6 = jnp.concatenate([jnp.broadcast_to(suf_t[h:h + 1, :], (s_new, p_len)) for h in range(N_FOX)], axis=0)
    sp = _dot_nt(q6, kp_ref[0].astype(BF16)) + cq6 + bias6
    qrow = lax.broadcasted_iota(jnp.int32, (N_FOX * s_new, s_new), 0) % s_new
    kcol = lax.broadcasted_iota(jnp.int32, (N_FOX * s_new, s_new), 1)
    sn = jnp.where(kcol <= qrow, _dot_nt(q6, kn_ref[0]) + _dot_nt(a6, ak_ref[0]), -1e30)
    m = jnp.maximum(jnp.max(sp, axis=1, keepdims=True), jnp.max(sn, axis=1, keepdims=True))
    pp = jnp.exp2(sp - m)
    pn = jnp.exp2(sn - m)
    l = jnp.sum(pp, axis=1, keepdims=True) + jnp.sum(pn, axis=1, keepdims=True)
    o6 = (_dot((pp / l).astype(BF16), vp_ref[0].astype(BF16))
          + _dot((pn / l).astype(BF16), vn_ref[0].astype(BF16)))
    o_ref[0] = _unstack_heads(o6, N_FOX).astype(BF16)


def _fox_step(qs, augq, kb, augk, v_new, k_past, v_past, logf_past_pad):
    b, s_new, _ = qs.shape
    p_len = k_past.shape[1]
    tok = lambda rows, w: pl.BlockSpec((1, rows, w), lambda i: (i, 0, 0))
    return pl.pallas_call(
        _fox_step_body,
        grid=(b,),
        in_specs=[tok(s_new, W_FOX)] * 5 + [tok(p_len, W_FOX), tok(p_len, W_FOX), tok(p_len, SMALL)],
        out_specs=tok(s_new, W_FOX),
        out_shape=jax.ShapeDtypeStruct((b, s_new, W_FOX), BF16),
        compiler_params=_params(1),
        name="fox_step",
    )(qs, augq, kb, augk, v_new, k_past, v_past, logf_past_pad)


def _rwkv_body(chunk, independent, rt_ref, at_ref, kh_ref, bh_ref, v_ref, lc_ref, s0_ref, y_ref, sout_ref, s_scr):
    j = pl.program_id(1)
    nj = pl.num_programs(1)
    c = chunk
    c2 = 2 * chunk
    n_chunks = rt_ref.shape[1] // c
    n_pairs = N_RWKV // 2

    def pair_state(seq_i, p):
        z = jnp.zeros((HEAD, HEAD), F32)
        return jnp.concatenate([jnp.concatenate([s0_ref[seq_i, 2 * p], z], axis=1),
                                jnp.concatenate([z, s0_ref[seq_i, 2 * p + 1]], axis=1)], axis=0)

    if not independent:
        @pl.when(j == 0)
        def _():
            for p in range(n_pairs):
                s_scr[p] = pair_state(0, p)

    lane_lo = lax.broadcasted_iota(jnp.int32, (c, PAIR), 1) < HEAD
    row = lax.broadcasted_iota(jnp.int32, (c2, c2), 0)
    col = lax.broadcasted_iota(jnp.int32, (c2, c2), 1)
    same = (row >= c) == (col >= c)
    strict = same & (col < row)
    lower = same & (col <= row)
    eye = jnp.where(col == row, 1.0, 0.0)
    srow = lax.broadcasted_iota(jnp.int32, (PAIR, PAIR), 0)
    scol = lax.broadcasted_iota(jnp.int32, (PAIR, PAIR), 1)
    same_state = (srow >= HEAD) == (scol >= HEAD)

    def stack(x):
        zero = jnp.zeros_like(x)
        return jnp.concatenate([jnp.where(lane_lo, x, zero), jnp.where(lane_lo, zero, x)], axis=0)

    def fold(a):
        return a[:c] + a[c:]

    units = [(ci, p) for ci in range(n_chunks) for p in range(n_pairs)]

    def tile(ref, u):
        ci, p = u
        return ref[0, ci * c:(ci + 1) * c, PAIR * p:PAIR * (p + 1)]

    rt = [tile(rt_ref, u) for u in units]
    at2 = [stack(tile(at_ref, u)) for u in units]
    kh = [tile(kh_ref, u) for u in units]
    bh = [tile(bh_ref, u) for u in units]
    v2 = [stack(tile(v_ref, u)) for u in units]
    g = [_dot_nt(jnp.concatenate([at2[i], stack(rt[i])], axis=0),
                 jnp.concatenate([kh[i], kh[i], bh[i], bh[i]], axis=0)) for i in range(len(units))]
    a_ak = [jnp.where(strict, x[:c2, :c2], 0.0) for x in g]
    a_ab = [jnp.where(strict, x[:c2, c2:], 0.0) for x in g]
    a_rk = [fold(jnp.where(lower, x[c2:, :c2], 0.0)) for x in g]
    a_rb = [fold(jnp.where(lower, x[c2:, c2:], 0.0)) for x in g]
    akv2 = [_dot1(_dot, a_ak[i], v2[i]) for i in range(len(units))]
    yv = [_dot1(_dot, a_rk[i], v2[i]) for i in range(len(units))]
    inv = [eye + x for x in a_ab]
    pw = a_ab
    n = 1
    while 2 * n < c:
        pw = [_dot1(_dot, x, x) for x in pw]
        inv = [inv[i] + _dot1(_dot, inv[i], pw[i]) for i in range(len(units))]
        n *= 2
    tw = [_dot(inv[i].astype(BF16), jnp.concatenate([at2[i], akv2[i].astype(BF16)], axis=1))
          for i in range(len(units))]
    w_mat = [fold(x[:, :PAIR]) for x in tw]
    u0 = [fold(x[:, PAIR:]) for x in tw]

    for ci in range(n_chunks):
        idx = [ci * n_pairs + p for p in range(n_pairs)]
        s_old = [pair_state(ci, p) if independent else s_scr[p] for p in range(n_pairs)]
        pc = [jnp.exp(lc_ref[0, (ci + 1) * c - 1:(ci + 1) * c, PAIR * p:PAIR * (p + 1)]) for p in range(n_pairs)]
        ws = [_dot_nt(jnp.concatenate([w_mat[i].astype(BF16), rt[i]], axis=0), s_old[p].astype(BF16))
              for p, i in enumerate(idx)]
        u = [ws[p][:c] + u0[i] for p, i in enumerate(idx)]
        y = [ws[p][c:] + yv[i] + _dot1(_dot, a_rb[i], stack(u[p])) for p, i in enumerate(idx)]
        upd = [_dot_tn(jnp.concatenate([tile(v_ref, units[i]), u[p].astype(BF16)], axis=0),
                       jnp.concatenate([(kh[i] * pc[p]).astype(BF16), (bh[i] * pc[p]).astype(BF16)], axis=0))
               for p, i in enumerate(idx)]
        for p in range(n_pairs):
            s_new = s_old[p] * pc[p] + jnp.where(same_state, upd[p], 0.0)
            y_ref[0, ci * c:(ci + 1) * c, PAIR * p:PAIR * (p + 1)] = y[p]
            if independent:
                sout_ref[ci, 2 * p] = s_new[:HEAD, :HEAD]
                sout_ref[ci, 2 * p + 1] = s_new[HEAD:, HEAD:]
            else:
                s_scr[p] = s_new

    if not independent:
        @pl.when(j == nj - 1)
        def _():
            for p in range(n_pairs):
                s = s_scr[p]
                sout_ref[0, 2 * p] = s[:HEAD, :HEAD]
                sout_ref[0, 2 * p + 1] = s[HEAD:, HEAD:]


def _rwkv(rt, at, kh, bh, v, lc, s0, tc, chunk, independent=False):
    b, t, _ = rt.shape
    tok = pl.BlockSpec((1, tc, W_RWKV), lambda bi, j: (bi, j, 0))
    if independent:
        st = pl.BlockSpec((tc // chunk, N_RWKV, HEAD, HEAD), lambda bi, j: (j, 0, 0, 0))
    else:
        st = pl.BlockSpec((1, N_RWKV, HEAD, HEAD), lambda bi, j: (bi, 0, 0, 0))
    return pl.pallas_call(
        functools.partial(_rwkv_body, chunk, independent),
        grid=(b, t // tc),
        in_specs=[tok] * 6 + [st],
        out_specs=[tok, st],
        out_shape=[jax.ShapeDtypeStruct((b, t, W_RWKV), F32), jax.ShapeDtypeStruct(s0.shape, F32)],
        scratch_shapes=[pltpu.VMEM((N_RWKV // 2, PAIR, PAIR), F32)],
        compiler_params=_params(2),
        name="rwkv",
    )(rt, at, kh, bh, v, lc, s0)


def _out_body(x_ref, of_ref, gf_ref, y_ref, gr_ref, bg_ref, om_ref, gnw_ref, gnb_ref, ones_ref,
              wf_ref, wr_ref, wm_ref, o_ref):
    ones_pair = ones_ref[...]
    y = y_ref[0]
    mu = _segsum(y, ones_pair) * (1.0 / HEAD)
    d = y - mu
    var = _segsum(d * d, ones_pair) * (1.0 / HEAD)
    yn = d * lax.rsqrt(var + GN_EPS) * gnw_ref[...] + gnb_ref[...]
    o_r = yn * gr_ref[0].astype(F32) + bg_ref[0].astype(F32)
    acc = _dot((of_ref[0].astype(F32) * gf_ref[0].astype(F32)).astype(BF16), wf_ref[...])
    acc = acc + _dot(o_r.astype(BF16), wr_ref[...])
    acc = acc + _dot(om_ref[0], wm_ref[...])
    o_ref[0] = x_ref[0] + acc


def _out(x, of, gf, y, gr, bg, om, gnw, gnb, ones_pair, wf, wr, wm, tm):
    b, t, _ = x.shape
    tok = lambda w: pl.BlockSpec((1, tm, w), lambda bi, i: (bi, i, 0))
    return pl.pallas_call(
        _out_body,
        grid=(b, t // tm),
        in_specs=[tok(D_MODEL), tok(W_FOX), tok(W_FOX), tok(W_RWKV), tok(W_RWKV), tok(W_RWKV), tok(W_MEM),
                  _full(gnw.shape), _full(gnb.shape), _full(ones_pair.shape),
                  _full(wf.shape), _full(wr.shape), _full(wm.shape)],
        out_specs=tok(D_MODEL),
        out_shape=jax.ShapeDtypeStruct((b, t, D_MODEL), F32),
        compiler_params=_params(2),
        name="out_proj",
    )(x, of, gf, y, gr, bg, om, gnw, gnb, ones_pair, wf, wr, wm)


def _pad_in_columns(w):
    r0 = FOX_COLS
    m0 = FOX_COLS + RWKV_COLS
    zeros = lambda n: jnp.zeros(w.shape[:-1] + (n,), w.dtype)
    return jnp.concatenate([
        w[..., 0:3 * W_FOX], w[..., 3 * W_FOX + N_FOX:FOX_COLS],
        w[..., r0:r0 + 3 * W_RWKV], w[..., r0 + 3 * W_RWKV + 2 * LORA:r0 + RWKV_COLS],
        w[..., m0:m0 + MEM_COLS],
        w[..., 3 * W_FOX:3 * W_FOX + N_FOX], zeros(SM_WD - N_FOX),
        w[..., r0 + 3 * W_RWKV:r0 + 3 * W_RWKV + 2 * LORA], zeros(SMALL - SM_AD - LORA),
    ], axis=-1)


def _pad_shift_row(a):
    zeros = lambda n: jnp.zeros(a.shape[:-1] + (n,), a.dtype)
    return jnp.concatenate([
        a[..., 0:3 * W_RWKV], a[..., 3 * W_RWKV + 2 * LORA:RWKV_COLS],
        zeros(SM_WD), a[..., 3 * W_RWKV:3 * W_RWKV + 2 * LORA], zeros(SMALL - SM_AD - LORA)], axis=-1)


def _unpad_shift_row(a):
    small = 4 * W_RWKV
    return jnp.concatenate([a[..., 0:3 * W_RWKV], a[..., small + SM_WD:small + SM_AD + LORA],
                            a[..., 3 * W_RWKV:4 * W_RWKV]], axis=-1)


def _bias_placement():
    place = np.zeros((SMALL, 2 * W_FOX), np.float32)
    const = np.zeros((1, 2 * W_FOX), np.float32)
    for h in range(N_FOX):
        for j in range(N_BIAS):
            place[BIAS_STRIDE * j + h, HEAD * h + j] = 1.0
            place[BIAS_STRIDE * j + h, W_FOX + HEAD * h + N_BIAS + j] = -1.0
            const[0, HEAD * h + N_BIAS + j] = 1.0
            const[0, W_FOX + HEAD * h + j] = 1.0
    return place, const


def _block_ones(width):
    h = np.arange(width) // HEAD
    return (h[:, None] == h[None, :]).astype(np.float32)


def _layer(x, shift_prev, s0, mk, mv, fox_past, wts, ones_pair, tm, tq, tc, chunk, flat_out=False):
    (proj_wts, gnw, gnb, wf, wr, wm) = wts
    b, t, _ = x.shape
    if flat_out:
        first = jnp.pad(_pad_shift_row(shift_prev), ((0, 0), (0, t - 1), (0, 0))).reshape(1, b * t, SHIFT_W)
        outs = _proj(x.reshape(1, b * t, D_MODEL), first, mk, mv, proj_wts, b * t, chunk, with_vt=False, seq=t)
        outs = [o.reshape(b, t, o.shape[-1]) for o in outs]
        outs[-1] = outs[-1][:, t - 1:t, :]
    else:
        outs = _proj(x, _pad_shift_row(shift_prev), mk, mv, proj_wts, tm, chunk, with_vt=fox_past is None)
    (qs, kb, augq, augk, kn, v, logf, gf, rt, at, kh, bh, rv, lc, gr, bg, om, shift_out, *vt) = outs
    if fox_past is None:
        of = _fox(qs, augq, kb, augk, vt[0], tq)
    else:
        k_past, v_past, logf_past = fox_past
        lfp = jnp.pad(logf_past.astype(F32), ((0, 0), (0, 0), (0, SMALL - N_FOX)))
        p_len = k_past.shape[1]
        of = _fox_step(qs, augq, kb, augk, v, k_past.reshape(b, p_len, W_FOX).astype(F32),
                       v_past.reshape(b, p_len, W_FOX).astype(F32), lfp)
    rows = lambda a: a.reshape(1, b * t, a.shape[-1]) if flat_out else a
    if flat_out:
        y_r, s_new = _rwkv(rows(rt), rows(at), rows(kh), rows(bh), rows(rv), rows(lc), s0,
                           min(b, STEP_SEQS_PER_TILE) * t, chunk, independent=True)
        y_r = y_r.reshape(b, t, W_RWKV)
    else:
        y_r, s_new = _rwkv(rt, at, kh, bh, rv, lc, s0, tc, chunk)
    y = _out(rows(x), rows(of), rows(gf), rows(y_r), rows(gr), rows(bg), rows(om), gnw, gnb, ones_pair, wf, wr, wm,
             b * t if flat_out else tm).reshape(b, t, D_MODEL)
    return (y, kn.reshape(b, t, N_FOX, HEAD), v.reshape(b, t, N_FOX, HEAD), logf, s_new, _unpad_shift_row(shift_out))


def kernel(x_prompt, x_sample, mem_prompt, cache_fox_k, cache_fox_v, cache_fox_logf, cache_mem_k, cache_mem_v, state_rwkv, state_rwkv_shift, norm_g, w_in, fox_q_g, fox_k_g, fox_b_f, rwkv_mu, rwkv_w0, rwkv_w_up, rwkv_a0, rwkv_a_up, rwkv_k_k, rwkv_k_a, rwkv_r_k, rwkv_gn_w, rwkv_gn_b, mem_norm_g, w_mem_kv, mem_q_g, mem_k_g, w_out):
    depth = w_in.shape[0]
    bp, tp, _ = x_prompt.shape
    bs, s_len, _ = x_sample.shape
    assert x_prompt.shape[2] == D_MODEL and w_in.shape[1:] == (D_MODEL, FOX_COLS + RWKV_COLS + MEM_COLS)
    assert mem_prompt.shape[1:] == (N_MEM_TOK, D_MODEL) and cache_mem_k.shape[2:] == (N_MEM_TOK, N_MEM, HEAD)
    assert tp % PROMPT_TM == 0 and tp % PROMPT_TQ == 0 and tp % PROMPT_TC == 0 and PROMPT_TQ % PROMPT_TM == 0
    assert PROMPT_TM % CUMSUM_BLOCK == 0 and CUMSUM_BLOCK % RWKV_CHUNK == 0 and PROMPT_TC % RWKV_CHUNK == 0
    assert s_len & (s_len - 1) == 0 and BF16_ROWS <= s_len <= RWKV_CHUNK
    assert (bs * s_len) % min(bs * s_len, CUMSUM_BLOCK) == 0 and bs % min(bs, STEP_SEQS_PER_TILE) == 0
    assert cache_fox_k.shape[2] % min(cache_fox_k.shape[2], STEP_SUFFIX_BLOCK) == 0
    place_np, const_np = _bias_placement()
    ones_pair = jnp.asarray(_block_ones(PAIR), BF16)
    pbias = jnp.asarray(place_np, BF16)
    bias1 = jnp.asarray(const_np, F32)

    yp, ys = x_prompt, x_sample
    outs = [[] for _ in range(12)]
    for l in range(depth):
        row = lambda a: a[l].reshape(1, -1).astype(F32)
        tile = lambda a, n: jnp.tile(a[l].reshape(1, -1).astype(F32), (1, n))
        w_pad = _pad_in_columns(w_in[l]).astype(BF16)
        bf_pad = jnp.pad(row(fox_b_f), ((0, 0), (0, SMALL - N_FOX)))
        wup_pad = jnp.pad(rwkv_w_up[l].astype(F32), ((SM_WD, SMALL - SM_WD - LORA), (0, 0)))
        aup_pad = jnp.pad(rwkv_a_up[l].astype(F32), ((SM_AD, SMALL - SM_AD - LORA), (0, 0)))
        proj_wts = (row(norm_g), w_pad, tile(fox_q_g, N_FOX), tile(fox_k_g, N_FOX), bf_pad,
                    _pad_shift_row(row(rwkv_mu)),
                    row(rwkv_w0), row(rwkv_a0), wup_pad, aup_pad, row(rwkv_k_k), row(rwkv_k_a), row(rwkv_r_k),
                    tile(mem_q_g, N_MEM), ones_pair, pbias, bias1)
        wo = w_out[l].astype(BF16)
        wts = (proj_wts, row(rwkv_gn_w), row(rwkv_gn_b), wo[:W_FOX], wo[W_FOX:W_FOX + W_RWKV], wo[W_FOX + W_RWKV:])

        mk2, mv2, mkb, mvb = _mem_kv(mem_prompt, row(mem_norm_g), w_mem_kv[l].astype(BF16), tile(mem_k_g, N_MEM),
                                     ones_pair)
        shift_zero = jnp.zeros((bp, 1, RWKV_COLS), F32)
        s_zero = jnp.zeros((bp, N_RWKV, HEAD, HEAD), F32)
        yp, k, v, lf, s_new, sh_new = _layer(yp, shift_zero, s_zero, mkb, mvb, None, wts, ones_pair,
                                             tm=PROMPT_TM, tq=PROMPT_TQ, tc=PROMPT_TC, chunk=RWKV_CHUNK)
        mk = mk2.reshape(bp, N_MEM_TOK, N_MEM, HEAD)
        mv = mv2.reshape(bp, N_MEM_TOK, N_MEM, HEAD)
        for lst, val in zip(outs[:7], (k, v, lf, mk, mv, s_new, sh_new)):
            lst.append(val)
        bs, s_len = ys.shape[0], ys.shape[1]
        ys, k, v, lf, s_new, sh_new = _layer(
            ys, state_rwkv_shift[l], state_rwkv[l].astype(F32),
            cache_mem_k[l].reshape(bs, N_MEM_TOK, W_MEM).astype(F32),
            cache_mem_v[l].reshape(bs, N_MEM_TOK, W_MEM).astype(F32),
            (cache_fox_k[l], cache_fox_v[l], cache_fox_logf[l]), wts, ones_pair,
            tm=s_len, tq=None, tc=s_len, chunk=s_len, flat_out=True)
        for lst, val in zip(outs[7:], (k, v, lf, s_new, sh_new)):
            lst.append(val)
    return (yp, ys) + tuple(jnp.stack(o) for o in outs)
```

```python
import functools

import numpy as np
import jax
import jax.numpy as jnp
from jax import lax
from jax.experimental import pallas as pl
from jax.experimental.pallas import tpu as pltpu

F32 = jnp.float32
BF16 = jnp.bfloat16

D_MODEL = 1024
HEAD = 64
N_FOX = 6
N_RWKV = 6
N_MEM = 4
W_FOX = N_FOX * HEAD
W_RWKV = N_RWKV * HEAD
W_MEM = N_MEM * HEAD
N_MEM_TOK = 256
LORA = 32
NORM_EPS = 1e-6
GN_EPS = 64e-5
LOG2E = float(np.log2(np.e))
FOX_COLS = 4 * W_FOX + N_FOX
RWKV_COLS = 4 * W_RWKV + 2 * LORA
MEM_COLS = 2 * W_MEM

LANE = 128
PAIR = 2 * HEAD
SMALL = LANE
SM_WD = 32
SM_AD = 64
OFF_FQ, OFF_FK, OFF_FV, OFF_FG = 0, 384, 768, 1152
OFF_RR, OFF_RK, OFF_RV, OFF_RG = 1536, 1920, 2304, 2688
OFF_MQ, OFF_MG = 3072, 3328
OFF_SM = 3584
N_PAD = OFF_SM + SMALL
SHIFT_W = 4 * W_RWKV + SMALL
N_BIAS = 3
BIAS_STRIDE = 8
BF16_ROWS = 16
VROWS = HEAD + BF16_ROWS
RWKV_CHUNK = 64
VMEM_LIMIT = 56 * 1024 * 1024
PROMPT_TM = 512
PROMPT_TQ = 512
PROMPT_TC = 512
FOX_UNROLL = 4
CUMSUM_BLOCK = 256
STEP_SUFFIX_BLOCK = 256
STEP_SEQS_PER_TILE = 8


def _dot(a, b):
    return jnp.dot(a, b, preferred_element_type=F32)


def _dot_nt(a, b):
    return lax.dot_general(a, b, (((1,), (1,)), ((), ())), preferred_element_type=F32)


def _dot_tn(a, b):
    return lax.dot_general(a, b, (((0,), (0,)), ((), ())), preferred_element_type=F32)


def _split2(x):
    hi = x.astype(BF16)
    lo = (x - hi.astype(F32)).astype(BF16)
    return hi, lo


def _split3(x):
    hi = x.astype(BF16)
    r1 = x - hi.astype(F32)
    mid = r1.astype(BF16)
    lo = (r1 - mid.astype(F32)).astype(BF16)
    return hi, mid, lo


def _dot1(fn, a, b):
    return fn(a.astype(BF16), b.astype(BF16))


def _exact_lhs_dot(m_bf16, x, parts):
    pieces = _split3(x) if parts == 3 else _split2(x)
    acc = _dot(m_bf16, pieces[0])
    for p in pieces[1:]:
        acc = acc + _dot(m_bf16, p)
    return acc


def _segsum(x, ones_pair):
    xb = x.astype(BF16)
    return jnp.concatenate([_dot(xb[:, PAIR * p:PAIR * (p + 1)], ones_pair) for p in range(x.shape[1] // PAIR)],
                           axis=1)


def _head_rms(t, gain, ones_pair):
    msq = _segsum(t * t, ones_pair) * (1.0 / HEAD)
    return t * lax.rsqrt(msq + NORM_EPS) * gain


def _silu(x):
    return x * jax.nn.sigmoid(x)


def _softplus(z):
    return jnp.maximum(z, 0.0) + jnp.log1p(jnp.exp(-jnp.abs(z)))


def _stack_heads(x, n_heads):
    head_of_lane = lax.broadcasted_iota(jnp.int32, x.shape, 1) // HEAD
    zero = jnp.zeros_like(x)
    return jnp.concatenate([jnp.where(head_of_lane == h, x, zero) for h in range(n_heads)], axis=0)


def _unstack_heads(x6, n_heads):
    rows = x6.shape[0] // n_heads
    head_of_lane = lax.broadcasted_iota(jnp.int32, (rows, x6.shape[1]), 1) // HEAD
    out = jnp.zeros((rows, x6.shape[1]), x6.dtype)
    for h in range(n_heads):
        out = jnp.where(head_of_lane == h, x6[h * rows:(h + 1) * rows], out)
    return out


def _full(shape):
    n = len(shape)
    return pl.BlockSpec(shape, lambda *_: (0,) * n)


def _params(n_axes):
    return pltpu.CompilerParams(dimension_semantics=("arbitrary",) * n_axes, vmem_limit_bytes=VMEM_LIMIT)


def _mem_kv_body(mem_ref, g_ref, w_ref, kg_ref, ones_ref, k_ref, v_ref, kb_ref, vb_ref):
    x = mem_ref[0]
    ms = jnp.mean(x * x, axis=-1, keepdims=True)
    xn = (x * lax.rsqrt(ms + NORM_EPS) * g_ref[...]).astype(BF16)
    kv = _dot(xn, w_ref[...])
    k = _head_rms(kv[:, :W_MEM], kg_ref[...], ones_ref[...])
    v = kv[:, W_MEM:]
    k_ref[0] = k
    v_ref[0] = v
    kb_ref[0] = k.astype(BF16)
    vb_ref[0] = v.astype(BF16)


def _mem_kv(mem, g, w_bf16, kg4, ones_pair):
    b = mem.shape[0]
    blk = pl.BlockSpec((1, N_MEM_TOK, W_MEM), lambda i: (i, 0, 0))
    return pl.pallas_call(
        _mem_kv_body,
        grid=(b,),
        in_specs=[pl.BlockSpec((1, N_MEM_TOK, D_MODEL), lambda i: (i, 0, 0)),
                  _full((1, D_MODEL)), _full((D_MODEL, 2 * W_MEM)), _full((1, W_MEM)), _full((PAIR, PAIR))],
        out_specs=[blk] * 4,
        out_shape=[jax.ShapeDtypeStruct((b, N_MEM_TOK, W_MEM), F32)] * 2
        + [jax.ShapeDtypeStruct((b, N_MEM_TOK, W_MEM), BF16)] * 2,
        compiler_params=_params(1),
        name="mem_kv",
    )(mem, g, w_bf16, kg4, ones_pair)


def _proj_body(chunk, with_vt, seq,
               x_ref, shift0_ref, mk_ref, mv_ref, ng_ref, w_ref,
               fqg_ref, fkg_ref, bf_ref, mu_ref, w0_ref, a0_ref, wup_ref, aup_ref,
               kk_ref, ka_ref, rk_ref, mqg_ref, ones_ref, pbias_ref, bias1_ref,
               qs_ref, kb_ref, augq_ref, augk_ref, kn_ref, v_ref, logf_ref, gf_ref,
               rt_ref, at_ref, kh_ref, bh_ref, rv_ref, lc_ref, gr_ref, bg_ref, om_ref, shift_out_ref,
               *vt_and_scratch):
    carry_shift, carry_c = vt_and_scratch[-2:]
    i = pl.program_id(1)
    tm = x_ref.shape[1]
    sub = min(tm, CUMSUM_BLOCK)

    if seq is None:
        @pl.when(i == 0)
        def _():
            carry_shift[...] = shift0_ref[0]
            carry_c[...] = jnp.zeros_like(carry_c)

    x = x_ref[0]
    ms = jnp.mean(x * x, axis=-1, keepdims=True)
    xn = (x * lax.rsqrt(ms + NORM_EPS) * ng_ref[...]).astype(BF16)

    def proj(off, width):
        return _dot(xn, w_ref[:, off:off + width])

    ones_pair = ones_ref[...]
    row = lax.broadcasted_iota(jnp.int32, (sub, sub), 0)
    col = lax.broadcasted_iota(jnp.int32, (sub, sub), 1)
    tri = jnp.where((col <= row) & ((row ^ col) < (sub if seq is None else seq)), 1.0, 0.0).astype(BF16)
    tri_chunk = jnp.where((col <= row) & ((row ^ col) < chunk), 1.0, 0.0).astype(BF16)
    blocks = [slice(s * sub, (s + 1) * sub) for s in range(tm // sub)]

    hs = proj(OFF_SM, SMALL)
    hq = proj(OFF_FQ, W_FOX)
    hk = proj(OFF_FK, W_FOX)
    hv = proj(OFF_FV, W_FOX)
    hg = proj(OFF_FG, W_FOX)
    hmq = proj(OFF_MQ, W_MEM)
    hmg = proj(OFF_MG, W_MEM)
    hrr = proj(OFF_RR, W_RWKV)
    hrk = proj(OFF_RK, W_RWKV)
    hrv = proj(OFF_RV, W_RWKV)
    hrg = proj(OFF_RG, W_RWKV)
    qn = _head_rms(hq, fqg_ref[...], ones_pair)
    kn = _head_rms(hk, fkg_ref[...], ones_pair)
    qm = _head_rms(hmq, mqg_ref[...], ones_pair)

    qms = (qm * (HEAD ** -0.5 * LOG2E)).astype(BF16)
    mem_rows = [(qms, 0)] if seq is None else [(qms[g * seq:(g + 1) * seq], g) for g in range(tm // seq)]

    def mem_probs():
        probs = []
        for q_rows, g in mem_rows:
            s4 = _dot_nt(_stack_heads(q_rows, N_MEM), mk_ref[g].astype(BF16))
            e = jnp.exp2(s4 - jnp.max(s4, axis=1, keepdims=True))
            probs.append((e / jnp.sum(e, axis=1, keepdims=True)).astype(BF16))
        return probs

    def mem_values(probs):
        return jnp.concatenate([_unstack_heads(_dot(p4, mv_ref[g].astype(BF16)), N_MEM)
                                for p4, (_, g) in zip(probs, mem_rows)], axis=0)

    if seq is not None:
        mem_p = mem_probs()
    kn_ref[0] = kn
    kb_ref[0] = kn.astype(BF16)
    qs_ref[0] = (qn * (HEAD ** -0.5 * LOG2E)).astype(BF16)
    v_ref[0] = hv
    if with_vt:
        hvt = hv.T
        ones_row = jnp.where(lax.broadcasted_iota(jnp.int32, (VROWS - HEAD, tm), 0) == 0, 1.0, 0.0)
        vt_and_scratch[0][0, 0] = jnp.concatenate(
            [blk for h in range(N_FOX) for blk in (hvt[HEAD * h:HEAD * (h + 1)], ones_row)], axis=0).astype(BF16)
    gf_ref[0] = _silu(hg).astype(BF16)
    f = hs + bf_ref[...]
    lane = lax.broadcasted_iota(jnp.int32, (tm, SMALL), 1)
    logf = jnp.where(lane < N_FOX, jnp.minimum(f, 0.0) - jnp.log1p(jnp.exp(-jnp.abs(f))), 0.0)
    logf_ref[0] = logf[:, 0:N_FOX]
    if seq is None:
        carry = carry_c[...]
        c_blocks = []
        for blk in blocks:
            cb = _exact_lhs_dot(tri, logf[blk], 3) + carry
            carry = cb[sub - 1:sub, :]
            c_blocks.append(cb)
        carry_c[...] = carry
    else:
        c_blocks = [_exact_lhs_dot(tri, logf[blk], 3) for blk in blocks]
    c = jnp.concatenate(c_blocks, axis=0)
    pieces = _split3(c * LOG2E)
    packed = pieces[0].astype(F32)
    for j in range(1, N_BIAS):
        packed = packed + pltpu.roll(pieces[j].astype(F32), BIAS_STRIDE * j, 1)
    aug = _dot(packed.astype(BF16), pbias_ref[...]) + bias1_ref[...]
    augq_ref[0] = aug[:, :W_FOX].astype(BF16)
    augk_ref[0] = aug[:, W_FOX:].astype(BF16)

    if seq is None:
        om_ref[0] = (mem_values(mem_probs()) * _silu(hmg)).astype(BF16)

    row_in_seq = lax.broadcasted_iota(jnp.int32, (tm, 1), 0) % (tm if seq is None else seq)

    def tshift(cols, off, width):
        if seq is None:
            before_first = carry_shift[:, off:off + width]
            carry_shift[:, off:off + width] = cols[tm - 1:tm, :]
        else:
            before_first = shift0_ref[0, :, off:off + width]
            shift_out_ref[0, :, off:off + width] = cols
        prev = jnp.where(row_in_seq == 0, before_first, pltpu.roll(cols, 1, 0))
        return cols + (prev - cols) * mu_ref[:, off:off + width]

    r = tshift(hrr, 0, W_RWKV)
    k = tshift(hrk, W_RWKV, W_RWKV)
    v = tshift(hrv, 2 * W_RWKV, W_RWKV)
    g = tshift(hrg, 3 * W_RWKV, W_RWKV)
    sm = tshift(hs, 4 * W_RWKV, SMALL)
    if seq is None:
        shift_out_ref[0] = carry_shift[...]

    w_lin = w0_ref[...] + _dot1(_dot, jnp.tanh(sm), wup_ref[...])
    a = jax.nn.sigmoid(a0_ref[...] + _dot1(_dot, sm, aup_ref[...]))
    lw = -jnp.exp(-_softplus(-w_lin) - 0.5)
    kk = k * kk_ref[...]
    kk = kk * lax.rsqrt(jnp.maximum(_segsum(kk * kk, ones_pair), 1e-24))
    kt = k * (1.0 + (a - 1.0) * ka_ref[...])
    gr = _silu(g)
    gr_ref[0] = gr.astype(BF16)
    bg_ref[0] = (_segsum(r * kt * rk_ref[...], ones_pair) * v * gr).astype(BF16)
    lc = jnp.concatenate([_exact_lhs_dot(tri_chunk, lw[blk], 2) for blk in blocks], axis=0)
    e_neg = jnp.exp(-lc)
    rt_ref[0] = (r * jnp.exp(lc)).astype(BF16)
    at_ref[0] = (-kk * jnp.exp(lc - lw)).astype(BF16)
    kh_ref[0] = (kt * e_neg).astype(BF16)
    bh_ref[0] = (kk * a * e_neg).astype(BF16)
    rv_ref[0] = v.astype(BF16)
    lc_ref[0] = lc
    if seq is not None:
        om_ref[0] = (mem_values(mem_p) * _silu(hmg)).astype(BF16)


def _proj(x, shift0, mk, mv, wts, tm, chunk, with_vt, seq=None):
    b, t, _ = x.shape
    grid = (b, t // tm)
    tok = lambda w: pl.BlockSpec((1, tm, w), lambda bi, i: (bi, i, 0))
    per_b = lambda s1, s2: pl.BlockSpec((1, s1, s2), lambda bi, i: (bi, 0, 0))
    w_specs = [_full(a.shape) for a in wts]
    bf = lambda w: jax.ShapeDtypeStruct((b, t, w), BF16)
    f32 = lambda w: jax.ShapeDtypeStruct((b, t, w), F32)
    out_shape = [bf(W_FOX)] * 4 + [f32(W_FOX), f32(W_FOX), f32(N_FOX), bf(W_FOX)] \
        + [bf(W_RWKV)] * 5 + [f32(W_RWKV)] + [bf(W_RWKV), bf(W_RWKV), bf(W_MEM),
                                               jax.ShapeDtypeStruct(shift0.shape, F32)]
    shift_spec = per_b(1, SHIFT_W) if seq is None else tok(SHIFT_W)
    mem_spec = pl.BlockSpec((mk.shape[0] // b, N_MEM_TOK, W_MEM), lambda bi, i: (bi, 0, 0))
    out_specs = [tok(W_FOX)] * 4 + [tok(W_FOX), tok(W_FOX), tok(N_FOX), tok(W_FOX)] \
        + [tok(W_RWKV)] * 6 + [tok(W_RWKV), tok(W_RWKV), tok(W_MEM), shift_spec]
    if with_vt:
        out_shape.append(jax.ShapeDtypeStruct((b, t // tm, N_FOX * VROWS, tm), BF16))
        out_specs.append(pl.BlockSpec((1, 1, N_FOX * VROWS, tm), lambda bi, i: (bi, i, 0, 0)))
    return pl.pallas_call(
        functools.partial(_proj_body, chunk, with_vt, seq),
        grid=grid,
        in_specs=[tok(D_MODEL), shift_spec, mem_spec, mem_spec] + w_specs,
        out_specs=out_specs,
        out_shape=out_shape,
        scratch_shapes=[pltpu.VMEM((1, SHIFT_W), F32), pltpu.VMEM((1, SMALL), F32)],
        compiler_params=_params(2),
        name="proj",
    )(x, shift0, mk, mv, *wts)


def _fox_body(qs_ref, aq_ref, kb_ref, ak_ref, vt_ref, o_ref, m_scr, acc_scr, sa_scr, sb_scr, qm_scr):
    qi = pl.program_id(2)
    tq = qs_ref.shape[1]
    slab = vt_ref.shape[3]
    slabs_per_tile = tq // slab
    m_scr[...] = jnp.full_like(m_scr, -1e30)
    acc_scr[...] = jnp.zeros_like(acc_scr)
    q = qs_ref[0]
    aq = aq_ref[0]
    lane_lo = lax.broadcasted_iota(jnp.int32, (tq, PAIR), 1) < HEAD
    zero = jnp.zeros_like(q)
    qm_scr[0] = jnp.concatenate([jnp.where(lane_lo, q, zero), jnp.where(lane_lo, aq, zero)], axis=1)
    qm_scr[1] = jnp.concatenate([jnp.where(lane_lo, zero, q), jnp.where(lane_lo, zero, aq)], axis=1)

    def scores(ki, s_scr):
        rows = pl.ds(pl.multiple_of(ki * tq, tq), tq)
        k2 = jnp.concatenate([kb_ref[0, rows, :], ak_ref[0, rows, :]], axis=1)
        for hh in range(2):
            s_scr[hh] = _dot_nt(k2, qm_scr[hh])

    def consume(ki, s_scr, masked):
        s = [s_scr[hh] for hh in range(2)]
        if masked:
            kpos = lax.broadcasted_iota(jnp.int32, (tq, tq), 0)
            qpos = lax.broadcasted_iota(jnp.int32, (tq, tq), 1)
            s = [jnp.where(kpos <= qpos, sh, -1e30) for sh in s]
        m_prev = [m_scr[hh] for hh in range(2)]
        m_new = [jnp.maximum(m_prev[hh], jnp.max(s[hh], axis=0, keepdims=True)) for hh in range(2)]
        p = [jnp.exp2(s[hh] - m_new[hh]).astype(BF16) for hh in range(2)]
        pv = []
        for hh in range(2):
            vt = jnp.concatenate([vt_ref[0, ki * slabs_per_tile + d, VROWS * hh:VROWS * (hh + 1), :]
                                  for d in range(slabs_per_tile)], axis=1)
            pv.append(_dot(vt, p[hh]))
        for hh in range(2):
            acc_scr[hh] = jnp.exp2(m_prev[hh] - m_new[hh]) * acc_scr[hh] + pv[hh]
            m_scr[hh] = m_new[hh]

    bufs = (sa_scr, sb_scr)
    scores(0, bufs[0])

    def tile_group(j, carry):
        for d in range(FOX_UNROLL):
            scores(FOX_UNROLL * j + d + 1, bufs[(d + 1) % 2])
            consume(FOX_UNROLL * j + d, bufs[d % 2], False)
        return carry

    lax.fori_loop(0, qi // FOX_UNROLL, tile_group, 0)
    base = (qi // FOX_UNROLL) * FOX_UNROLL
    for rem in range(FOX_UNROLL):
        @pl.when(qi - base == rem)
        def _():
            for d in range(rem):
                scores(base + d + 1, bufs[(d + 1) % 2])
                consume(base + d, bufs[d % 2], False)
            consume(qi, bufs[rem % 2], True)

    o_t = jnp.concatenate([acc_scr[hh][0:HEAD] / acc_scr[hh][HEAD:HEAD + 1] for hh in range(2)], axis=0)
    o_ref[0] = o_t.T.astype(BF16)


def _fox(qs, augq, kb, augk, vt, tq):
    b, t, _ = qs.shape
    n_slab, slab = vt.shape[1], vt.shape[3]
    q_blk = pl.BlockSpec((1, tq, PAIR), lambda bi, p, qi: (bi, qi, p))
    k_blk = pl.BlockSpec((1, t, PAIR), lambda bi, p, qi: (bi, 0, p))
    return pl.pallas_call(
        _fox_body,
        grid=(b, N_FOX // 2, t // tq),
        in_specs=[q_blk, q_blk, k_blk, k_blk,
                  pl.BlockSpec((1, n_slab, 2 * VROWS, slab), lambda bi, p, qi: (bi, 0, p, 0))],
        out_specs=q_blk,
        out_shape=jax.ShapeDtypeStruct((b, t, W_FOX), BF16),
        scratch_shapes=[pltpu.VMEM((2, 1, tq), F32), pltpu.VMEM((2, VROWS, tq), F32),
                        pltpu.VMEM((2, tq, tq), F32), pltpu.VMEM((2, tq, tq), F32),
                        pltpu.VMEM((2, tq, 2 * PAIR), BF16)],
        compiler_params=_params(3),
        name="fox",
    )(qs, augq, kb, augk, vt)


def _fox_step_body(qs_ref, aq_ref, kn_ref, ak_ref, vn_ref, kp_ref, vp_ref, lfp_ref, o_ref):
    s_new = qs_ref.shape[1]
    p_len = kp_ref.shape[1]
    blk = min(p_len, STEP_SUFFIX_BLOCK)
    row = lax.broadcasted_iota(jnp.int32, (blk, blk), 0)
    col = lax.broadcasted_iota(jnp.int32, (blk, blk), 1)
    upper = jnp.where(col > row, 1.0, 0.0).astype(BF16)
    later = jnp.zeros((1, SMALL), F32)
    sufs = []
    for bi in reversed(range(p_len // blk)):
        lf = lfp_ref[0, bi * blk:(bi + 1) * blk, :]
        within = _exact_lhs_dot(upper, lf, 3)
        sufs.insert(0, within + later)
        later = later + within[0:1, :] + lf[0:1, :]
    suf_t = (jnp.concatenate(sufs, axis=0) * LOG2E).T
    q6 = _stack_heads(qs_ref[0], N_FOX)
    a6 = _stack_heads(aq_ref[0], N_FOX)
    lane_in_head = lax.broadcasted_iota(jnp.int32, a6.shape, 1) % HEAD
    cq6 = jnp.sum(jnp.where(lane_in_head < N_BIAS, a6.astype(F32), 0.0), axis=1, keepdims=True)
    bias6 = jnp.concatenate([jnp.broadcast_to(suf_t[h:h + 1, :], (s_new, p_len)) for h in range(N_FOX)], axis=0)
    sp = _dot_nt(q6, kp_ref[0].astype(BF16)) + cq6 + bias6
    qrow = lax.broadcasted_iota(jnp.int32, (N_FOX * s_new, s_new), 0) % s_new
    kcol = lax.broadcasted_iota(jnp.int32, (N_FOX * s_new, s_new), 1)
    sn = jnp.where(kcol <= qrow, _dot_nt(q6, kn_ref[0]) + _dot_nt(a6, ak_ref[0]), -1e30)
    m = jnp.maximum(jnp.max(sp, axis=1, keepdims=True), jnp.max(sn, axis=1, keepdims=True))
    pp = jnp.exp2(sp - m)
    pn = jnp.exp2(sn - m)
    l = jnp.sum(pp, axis=1, keepdims=True) + jnp.sum(pn, axis=1, keepdims=True)
    o6 = (_dot((pp / l).astype(BF16), vp_ref[0].astype(BF16))
          + _dot((pn / l).astype(BF16), vn_ref[0].astype(BF16)))
    o_ref[0] = _unstack_heads(o6, N_FOX).astype(BF16)


def _fox_step(qs, augq, kb, augk, v_new, k_past, v_past, logf_past_pad):
    b, s_new, _ = qs.shape
    p_len = k_past.shape[1]
    tok = lambda rows, w: pl.BlockSpec((1, rows, w), lambda i: (i, 0, 0))
    return pl.pallas_call(
        _fox_step_body,
        grid=(b,),
        in_specs=[tok(s_new, W_FOX)] * 5 + [tok(p_len, W_FOX), tok(p_len, W_FOX), tok(p_len, SMALL)],
        out_specs=tok(s_new, W_FOX),
        out_shape=jax.ShapeDtypeStruct((b, s_new, W_FOX), BF16),
        compiler_params=_params(1),
        name="fox_step",
    )(qs, augq, kb, augk, v_new, k_past, v_past, logf_past_pad)


def _rwkv_body(chunk, independent, rt_ref, at_ref, kh_ref, bh_ref, v_ref, lc_ref, s0_ref, y_ref, sout_ref, s_scr):
    j = pl.program_id(1)
    nj = pl.num_programs(1)
    c = chunk
    c2 = 2 * chunk
    n_chunks = rt_ref.shape[1] // c
    n_pairs = N_RWKV // 2

    def pair_state(seq_i, p):
        z = jnp.zeros((HEAD, HEAD), F32)
        return jnp.concatenate([jnp.concatenate([s0_ref[seq_i, 2 * p], z], axis=1),
                                jnp.concatenate([z, s0_ref[seq_i, 2 * p + 1]], axis=1)], axis=0)

    if not independent:
        @pl.when(j == 0)
        def _():
            for p in range(n_pairs):
                s_scr[p] = pair_state(0, p)

    lane_lo = lax.broadcasted_iota(jnp.int32, (c, PAIR), 1) < HEAD
    row = lax.broadcasted_iota(jnp.int32, (c2, c2), 0)
    col = lax.broadcasted_iota(jnp.int32, (c2, c2), 1)
    same = (row >= c) == (col >= c)
    strict = same & (col < row)
    lower = same & (col <= row)
    eye = jnp.where(col == row, 1.0, 0.0)
    srow = lax.broadcasted_iota(jnp.int32, (PAIR, PAIR), 0)
    scol = lax.broadcasted_iota(jnp.int32, (PAIR, PAIR), 1)
    same_state = (srow >= HEAD) == (scol >= HEAD)

    def stack(x):
        zero = jnp.zeros_like(x)
        return jnp.concatenate([jnp.where(lane_lo, x, zero), jnp.where(lane_lo, zero, x)], axis=0)

    def fold(a):
        return a[:c] + a[c:]

    units = [(ci, p) for ci in range(n_chunks) for p in range(n_pairs)]

    def tile(ref, u):
        ci, p = u
        return ref[0, ci * c:(ci + 1) * c, PAIR * p:PAIR * (p + 1)]

    rt = [tile(rt_ref, u) for u in units]
    at2 = [stack(tile(at_ref, u)) for u in units]
    kh = [tile(kh_ref, u) for u in units]
    bh = [tile(bh_ref, u) for u in units]
    v2 = [stack(tile(v_ref, u)) for u in units]
    g = [_dot_nt(jnp.concatenate([at2[i], stack(rt[i])], axis=0),
                 jnp.concatenate([kh[i], kh[i], bh[i], bh[i]], axis=0)) for i in range(len(units))]
    a_ak = [jnp.where(strict, x[:c2, :c2], 0.0) for x in g]
    a_ab = [jnp.where(strict, x[:c2, c2:], 0.0) for x in g]
    a_rk = [fold(jnp.where(lower, x[c2:, :c2], 0.0)) for x in g]
    a_rb = [fold(jnp.where(lower, x[c2:, c2:], 0.0)) for x in g]
    akv2 = [_dot1(_dot, a_ak[i], v2[i]) for i in range(len(units))]
    yv = [_dot1(_dot, a_rk[i], v2[i]) for i in range(len(units))]
    inv = [eye + x for x in a_ab]
    pw = a_ab
    n = 1
    while 2 * n < c:
        pw = [_dot1(_dot, x, x) for x in pw]
        inv = [inv[i] + _dot1(_dot, inv[i], pw[i]) for i in range(len(units))]
        n *= 2
    tw = [_dot(inv[i].astype(BF16), jnp.concatenate([at2[i], akv2[i].astype(BF16)], axis=1))
          for i in range(len(units))]
    w_mat = [fold(x[:, :PAIR]) for x in tw]
    u0 = [fold(x[:, PAIR:]) for x in tw]

    for ci in range(n_chunks):
        idx = [ci * n_pairs + p for p in range(n_pairs)]
        s_old = [pair_state(ci, p) if independent else s_scr[p] for p in range(n_pairs)]
        pc = [jnp.exp(lc_ref[0, (ci + 1) * c - 1:(ci + 1) * c, PAIR * p:PAIR * (p + 1)]) for p in range(n_pairs)]
        ws = [_dot_nt(jnp.concatenate([w_mat[i].astype(BF16), rt[i]], axis=0), s_old[p].astype(BF16))
              for p, i in enumerate(idx)]
        u = [ws[p][:c] + u0[i] for p, i in enumerate(idx)]
        y = [ws[p][c:] + yv[i] + _dot1(_dot, a_rb[i], stack(u[p])) for p, i in enumerate(idx)]
        upd = [_dot_tn(jnp.concatenate([tile(v_ref, units[i]), u[p].astype(BF16)], axis=0),
                       jnp.concatenate([(kh[i] * pc[p]).astype(BF16), (bh[i] * pc[p]).astype(BF16)], axis=0))
               for p, i in enumerate(idx)]
        for p in range(n_pairs):
            s_new = s_old[p] * pc[p] + jnp.where(same_state, upd[p], 0.0)
            y_ref[0, ci * c:(ci + 1) * c, PAIR * p:PAIR * (p + 1)] = y[p]
            if independent:
                sout_ref[ci, 2 * p] = s_new[:HEAD, :HEAD]
                sout_ref[ci, 2 * p + 1] = s_new[HEAD:, HEAD:]
            else:
                s_scr[p] = s_new

    if not independent:
        @pl.when(j == nj - 1)
        def _():
            for p in range(n_pairs):
                s = s_scr[p]
                sout_ref[0, 2 * p] = s[:HEAD, :HEAD]
                sout_ref[0, 2 * p + 1] = s[HEAD:, HEAD:]


def _rwkv(rt, at, kh, bh, v, lc, s0, tc, chunk, independent=False):
    b, t, _ = rt.shape
    tok = pl.BlockSpec((1, tc, W_RWKV), lambda bi, j: (bi, j, 0))
    if independent:
        st = pl.BlockSpec((tc // chunk, N_RWKV, HEAD, HEAD), lambda bi, j: (j, 0, 0, 0))
    else:
        st = pl.BlockSpec((1, N_RWKV, HEAD, HEAD), lambda bi, j: (bi, 0, 0, 0))
    return pl.pallas_call(
        functools.partial(_rwkv_body, chunk, independent),
        grid=(b, t // tc),
        in_specs=[tok] * 6 + [st],
        out_specs=[tok, st],
        out_shape=[jax.ShapeDtypeStruct((b, t, W_RWKV), F32), jax.ShapeDtypeStruct(s0.shape, F32)],
        scratch_shapes=[pltpu.VMEM((N_RWKV // 2, PAIR, PAIR), F32)],
        compiler_params=_params(2),
        name="rwkv",
    )(rt, at, kh, bh, v, lc, s0)


def _out_body(x_ref, of_ref, gf_ref, y_ref, gr_ref, bg_ref, om_ref, gnw_ref, gnb_ref, ones_ref,
              wf_ref, wr_ref, wm_ref, o_ref):
    ones_pair = ones_ref[...]
    acc = _dot((of_ref[0].astype(F32) * gf_ref[0].astype(F32)).astype(BF16), wf_ref[...])
    acc = acc + _dot(om_ref[0], wm_ref[...])
    y = y_ref[0]
    mu = _segsum(y, ones_pair) * (1.0 / HEAD)
    d = y - mu
    var = _segsum(d * d, ones_pair) * (1.0 / HEAD)
    yn = d * lax.rsqrt(var + GN_EPS) * gnw_ref[...] + gnb_ref[...]
    o_r = yn * gr_ref[0].astype(F32) + bg_ref[0].astype(F32)
    acc = acc + _dot(o_r.astype(BF16), wr_ref[...])
    o_ref[0] = x_ref[0] + acc


def _out(x, of, gf, y, gr, bg, om, gnw, gnb, ones_pair, wf, wr, wm, tm):
    b, t, _ = x.shape
    tok = lambda w: pl.BlockSpec((1, tm, w), lambda bi, i: (bi, i, 0))
    return pl.pallas_call(
        _out_body,
        grid=(b, t // tm),
        in_specs=[tok(D_MODEL), tok(W_FOX), tok(W_FOX), tok(W_RWKV), tok(W_RWKV), tok(W_RWKV), tok(W_MEM),
                  _full(gnw.shape), _full(gnb.shape), _full(ones_pair.shape),
                  _full(wf.shape), _full(wr.shape), _full(wm.shape)],
        out_specs=tok(D_MODEL),
        out_shape=jax.ShapeDtypeStruct((b, t, D_MODEL), F32),
        compiler_params=_params(2),
        name="out_proj",
    )(x, of, gf, y, gr, bg, om, gnw, gnb, ones_pair, wf, wr, wm)


def _pad_in_columns(w):
    r0 = FOX_COLS
    m0 = FOX_COLS + RWKV_COLS
    zeros = lambda n: jnp.zeros(w.shape[:-1] + (n,), w.dtype)
    return jnp.concatenate([
        w[..., 0:3 * W_FOX], w[..., 3 * W_FOX + N_FOX:FOX_COLS],
        w[..., r0:r0 + 3 * W_RWKV], w[..., r0 + 3 * W_RWKV + 2 * LORA:r0 + RWKV_COLS],
        w[..., m0:m0 + MEM_COLS],
        w[..., 3 * W_FOX:3 * W_FOX + N_FOX], zeros(SM_WD - N_FOX),
        w[..., r0 + 3 * W_RWKV:r0 + 3 * W_RWKV + 2 * LORA], zeros(SMALL - SM_AD - LORA),
    ], axis=-1)


def _pad_shift_row(a):
    zeros = lambda n: jnp.zeros(a.shape[:-1] + (n,), a.dtype)
    return jnp.concatenate([
        a[..., 0:3 * W_RWKV], a[..., 3 * W_RWKV + 2 * LORA:RWKV_COLS],
        zeros(SM_WD), a[..., 3 * W_RWKV:3 * W_RWKV + 2 * LORA], zeros(SMALL - SM_AD - LORA)], axis=-1)


def _unpad_shift_row(a):
    small = 4 * W_RWKV
    return jnp.concatenate([a[..., 0:3 * W_RWKV], a[..., small + SM_WD:small + SM_AD + LORA],
                            a[..., 3 * W_RWKV:4 * W_RWKV]], axis=-1)


def _bias_placement():
    place = np.zeros((SMALL, 2 * W_FOX), np.float32)
    const = np.zeros((1, 2 * W_FOX), np.float32)
    for h in range(N_FOX):
        for j in range(N_BIAS):
            place[BIAS_STRIDE * j + h, HEAD * h + j] = 1.0
            place[BIAS_STRIDE * j + h, W_FOX + HEAD * h + N_BIAS + j] = -1.0
            const[0, HEAD * h + N_BIAS + j] = 1.0
            const[0, W_FOX + HEAD * h + j] = 1.0
    return place, const


def _block_ones(width):
    h = np.arange(width) // HEAD
    return (h[:, None] == h[None, :]).astype(np.float32)


def _layer(x, shift_prev, s0, mk, mv, fox_past, wts, ones_pair, tm, tq, tc, chunk, flat_out=False):
    (proj_wts, gnw, gnb, wf, wr, wm) = wts
    b, t, _ = x.shape
    if flat_out:
        first = jnp.pad(_pad_shift_row(shift_prev), ((0, 0), (0, t - 1), (0, 0))).reshape(1, b * t, SHIFT_W)
        outs = _proj(x.reshape(1, b * t, D_MODEL), first, mk, mv, proj_wts, b * t, chunk, with_vt=False, seq=t)
        outs = [o.reshape(b, t, o.shape[-1]) for o in outs]
        outs[-1] = outs[-1][:, t - 1:t, :]
    else:
        outs = _proj(x, _pad_shift_row(shift_prev), mk, mv, proj_wts, tm, chunk, with_vt=fox_past is None)
    (qs, kb, augq, augk, kn, v, logf, gf, rt, at, kh, bh, rv, lc, gr, bg, om, shift_out, *vt) = outs
    if fox_past is None:
        of = _fox(qs, augq, kb, augk, vt[0], tq)
    else:
        k_past, v_past, logf_past = fox_past
        lfp = jnp.pad(logf_past.astype(F32), ((0, 0), (0, 0), (0, SMALL - N_FOX)))
        p_len = k_past.shape[1]
        of = _fox_step(qs, augq, kb, augk, v, k_past.reshape(b, p_len, W_FOX).astype(F32),
                       v_past.reshape(b, p_len, W_FOX).astype(F32), lfp)
    rows = lambda a: a.reshape(1, b * t, a.shape[-1]) if flat_out else a
    if flat_out:
        y_r, s_new = _rwkv(rows(rt), rows(at), rows(kh), rows(bh), rows(rv), rows(lc), s0,
                           min(b, STEP_SEQS_PER_TILE) * t, chunk, independent=True)
        y_r = y_r.reshape(b, t, W_RWKV)
    else:
        y_r, s_new = _rwkv(rt, at, kh, bh, rv, lc, s0, tc, chunk)
    y = _out(rows(x), rows(of), rows(gf), rows(y_r), rows(gr), rows(bg), rows(om), gnw, gnb, ones_pair, wf, wr, wm,
             b * t if flat_out else tm).reshape(b, t, D_MODEL)
    return (y, kn.reshape(b, t, N_FOX, HEAD), v.reshape(b, t, N_FOX, HEAD), logf, s_new, _unpad_shift_row(shift_out))


def kernel(x_prompt, x_sample, mem_prompt, cache_fox_k, cache_fox_v, cache_fox_logf, cache_mem_k, cache_mem_v, state_rwkv, state_rwkv_shift, norm_g, w_in, fox_q_g, fox_k_g, fox_b_f, rwkv_mu, rwkv_w0, rwkv_w_up, rwkv_a0, rwkv_a_up, rwkv_k_k, rwkv_k_a, rwkv_r_k, rwkv_gn_w, rwkv_gn_b, mem_norm_g, w_mem_kv, mem_q_g, mem_k_g, w_out):
    depth = w_in.shape[0]
    bp, tp, _ = x_prompt.shape
    bs, s_len, _ = x_sample.shape
    assert x_prompt.shape[2] == D_MODEL and w_in.shape[1:] == (D_MODEL, FOX_COLS + RWKV_COLS + MEM_COLS)
    assert mem_prompt.shape[1:] == (N_MEM_TOK, D_MODEL) and cache_mem_k.shape[2:] == (N_MEM_TOK, N_MEM, HEAD)
    assert tp % PROMPT_TM == 0 and tp % PROMPT_TQ == 0 and tp % PROMPT_TC == 0 and PROMPT_TQ % PROMPT_TM == 0
    assert PROMPT_TM % CUMSUM_BLOCK == 0 and CUMSUM_BLOCK % RWKV_CHUNK == 0 and PROMPT_TC % RWKV_CHUNK == 0
    assert s_len & (s_len - 1) == 0 and BF16_ROWS <= s_len <= RWKV_CHUNK
    assert (bs * s_len) % min(bs * s_len, CUMSUM_BLOCK) == 0 and bs % min(bs, STEP_SEQS_PER_TILE) == 0
    assert cache_fox_k.shape[2] % min(cache_fox_k.shape[2], STEP_SUFFIX_BLOCK) == 0
    place_np, const_np = _bias_placement()
    ones_pair = jnp.asarray(_block_ones(PAIR), BF16)
    pbias = jnp.asarray(place_np, BF16)
    bias1 = jnp.asarray(const_np, F32)

    yp, ys = x_prompt, x_sample
    outs = [[] for _ in range(12)]
    for l in range(depth):
        row = lambda a: a[l].reshape(1, -1).astype(F32)
        tile = lambda a, n: jnp.tile(a[l].reshape(1, -1).astype(F32), (1, n))
        w_pad = _pad_in_columns(w_in[l]).astype(BF16)
        bf_pad = jnp.pad(row(fox_b_f), ((0, 0), (0, SMALL - N_FOX)))
        wup_pad = jnp.pad(rwkv_w_up[l].astype(F32), ((SM_WD, SMALL - SM_WD - LORA), (0, 0)))
        aup_pad = jnp.pad(rwkv_a_up[l].astype(F32), ((SM_AD, SMALL - SM_AD - LORA), (0, 0)))
        proj_wts = (row(norm_g), w_pad, tile(fox_q_g, N_FOX), tile(fox_k_g, N_FOX), bf_pad,
                    _pad_shift_row(row(rwkv_mu)),
                    row(rwkv_w0), row(rwkv_a0), wup_pad, aup_pad, row(rwkv_k_k), row(rwkv_k_a), row(rwkv_r_k),
                    tile(mem_q_g, N_MEM), ones_pair, pbias, bias1)
        wo = w_out[l].astype(BF16)
        wts = (proj_wts, row(rwkv_gn_w), row(rwkv_gn_b), wo[:W_FOX], wo[W_FOX:W_FOX + W_RWKV], wo[W_FOX + W_RWKV:])

        mk2, mv2, mkb, mvb = _mem_kv(mem_prompt, row(mem_norm_g), w_mem_kv[l].astype(BF16), tile(mem_k_g, N_MEM),
                                     ones_pair)
        shift_zero = jnp.zeros((bp, 1, RWKV_COLS), F32)
        s_zero = jnp.zeros((bp, N_RWKV, HEAD, HEAD), F32)
        yp, k, v, lf, s_new, sh_new = _layer(yp, shift_zero, s_zero, mkb, mvb, None, wts, ones_pair,
                                             tm=PROMPT_TM, tq=PROMPT_TQ, tc=PROMPT_TC, chunk=RWKV_CHUNK)
        mk = mk2.reshape(bp, N_MEM_TOK, N_MEM, HEAD)
        mv = mv2.reshape(bp, N_MEM_TOK, N_MEM, HEAD)
        for lst, val in zip(outs[:7], (k, v, lf, mk, mv, s_new, sh_new)):
            lst.append(val)
        bs, s_len = ys.shape[0], ys.shape[1]
        ys, k, v, lf, s_new, sh_new = _layer(
            ys, state_rwkv_shift[l], state_rwkv[l].astype(F32),
            cache_mem_k[l].reshape(bs, N_MEM_TOK, W_MEM).astype(F32),
            cache_mem_v[l].reshape(bs, N_MEM_TOK, W_MEM).astype(F32),
            (cache_fox_k[l], cache_fox_v[l], cache_fox_logf[l]), wts, ones_pair,
            tm=s_len, tq=None, tc=s_len, chunk=s_len, flat_out=True)
        for lst, val in zip(outs[7:], (k, v, lf, s_new, sh_new)):
            lst.append(val)
    return (yp, ys) + tuple(jnp.stack(o) for o in outs)
```

```python
import functools

import numpy as np
import jax
import jax.numpy as jnp
from jax import lax
from jax.experimental import pallas as pl
from jax.experimental.pallas import tpu as pltpu

F32 = jnp.float32
BF16 = jnp.bfloat16

D_MODEL = 1024
HEAD = 64
N_FOX = 6
N_RWKV = 6
N_MEM = 4
W_FOX = N_FOX * HEAD
W_RWKV = N_RWKV * HEAD
W_MEM = N_MEM * HEAD
N_MEM_TOK = 256
LORA = 32
NORM_EPS = 1e-6
GN_EPS = 64e-5
LOG2E = float(np.log2(np.e))
FOX_COLS = 4 * W_FOX + N_FOX
RWKV_COLS = 4 * W_RWKV + 2 * LORA
MEM_COLS = 2 * W_MEM

LANE = 128
PAIR = 2 * HEAD
SMALL = LANE
SM_WD = 32
SM_AD = 64
OFF_FQ, OFF_FK, OFF_FV, OFF_FG = 0, 384, 768, 1152
OFF_RR, OFF_RK, OFF_RV, OFF_RG = 1536, 1920, 2304, 2688
OFF_MQ, OFF_MG = 3072, 3328
OFF_SM = 3584
N_PAD = OFF_SM + SMALL
SHIFT_W = 4 * W_RWKV + SMALL
N_BIAS = 3
BIAS_STRIDE = 8
BF16_ROWS = 16
VROWS = HEAD + BF16_ROWS
RWKV_CHUNK = 64
VMEM_LIMIT = 56 * 1024 * 1024
PROMPT_TM = 512
PROMPT_TQ = 512
PROMPT_TC = 512
FOX_UNROLL = 4
CUMSUM_BLOCK = 256
STEP_SUFFIX_BLOCK = 256
STEP_SEQS_PER_TILE = 8


def _dot(a, b):
    return jnp.dot(a, b, preferred_element_type=F32)


def _dot_nt(a, b):
    return lax.dot_general(a, b, (((1,), (1,)), ((), ())), preferred_element_type=F32)


def _dot_tn(a, b):
    return lax.dot_general(a, b, (((0,), (0,)), ((), ())), preferred_element_type=F32)


def _split2(x):
    hi = x.astype(BF16)
    lo = (x - hi.astype(F32)).astype(BF16)
    return hi, lo


def _split3(x):
    hi = x.astype(BF16)
    r1 = x - hi.astype(F32)
    mid = r1.astype(BF16)
    lo = (r1 - mid.astype(F32)).astype(BF16)
    return hi, mid, lo


def _dot1(fn, a, b):
    return fn(a.astype(BF16), b.astype(BF16))


def _exact_lhs_dot(m_bf16, x, parts):
    pieces = _split3(x) if parts == 3 else _split2(x)
    acc = _dot(m_bf16, pieces[0])
    for p in pieces[1:]:
        acc = acc + _dot(m_bf16, p)
    return acc


def _segsum(x, ones_pair):
    xb = x.astype(BF16)
    return jnp.concatenate([_dot(xb[:, PAIR * p:PAIR * (p + 1)], ones_pair) for p in range(x.shape[1] // PAIR)],
                           axis=1)


def _head_rms(t, gain, ones_pair):
    msq = _segsum(t * t, ones_pair) * (1.0 / HEAD)
    return t * lax.rsqrt(msq + NORM_EPS) * gain


def _silu(x):
    return x * jax.nn.sigmoid(x)


def _softplus(z):
    return jnp.maximum(z, 0.0) + jnp.log1p(jnp.exp(-jnp.abs(z)))


def _stack_heads(x, n_heads):
    head_of_lane = lax.broadcasted_iota(jnp.int32, x.shape, 1) // HEAD
    zero = jnp.zeros_like(x)
    return jnp.concatenate([jnp.where(head_of_lane == h, x, zero) for h in range(n_heads)], axis=0)


def _unstack_heads(x6, n_heads):
    rows = x6.shape[0] // n_heads
    head_of_lane = lax.broadcasted_iota(jnp.int32, (rows, x6.shape[1]), 1) // HEAD
    out = jnp.zeros((rows, x6.shape[1]), x6.dtype)
    for h in range(n_heads):
        out = jnp.where(head_of_lane == h, x6[h * rows:(h + 1) * rows], out)
    return out


def _full(shape):
    n = len(shape)
    return pl.BlockSpec(shape, lambda *_: (0,) * n)


def _params(n_axes):
    return pltpu.CompilerParams(dimension_semantics=("arbitrary",) * n_axes, vmem_limit_bytes=VMEM_LIMIT)


def _mem_kv_body(mem_ref, g_ref, w_ref, kg_ref, ones_ref, k_ref, v_ref, kb_ref, vb_ref):
    x = mem_ref[0]
    ms = jnp.mean(x * x, axis=-1, keepdims=True)
    xn = (x * lax.rsqrt(ms + NORM_EPS) * g_ref[...]).astype(BF16)
    kv = _dot(xn, w_ref[...])
    k = _head_rms(kv[:, :W_MEM], kg_ref[...], ones_ref[...])
    v = kv[:, W_MEM:]
    k_ref[0] = k
    v_ref[0] = v
    kb_ref[0] = k.astype(BF16)
    vb_ref[0] = v.astype(BF16)


def _mem_kv(mem, g, w_bf16, kg4, ones_pair):
    b = mem.shape[0]
    blk = pl.BlockSpec((1, N_MEM_TOK, W_MEM), lambda i: (i, 0, 0))
    return pl.pallas_call(
        _mem_kv_body,
        grid=(b,),
        in_specs=[pl.BlockSpec((1, N_MEM_TOK, D_MODEL), lambda i: (i, 0, 0)),
                  _full((1, D_MODEL)), _full((D_MODEL, 2 * W_MEM)), _full((1, W_MEM)), _full((PAIR, PAIR))],
        out_specs=[blk] * 4,
        out_shape=[jax.ShapeDtypeStruct((b, N_MEM_TOK, W_MEM), F32)] * 2
        + [jax.ShapeDtypeStruct((b, N_MEM_TOK, W_MEM), BF16)] * 2,
        compiler_params=_params(1),
        name="mem_kv",
    )(mem, g, w_bf16, kg4, ones_pair)


def _proj_body(chunk, with_vt, seq,
               x_ref, shift0_ref, mk_ref, mv_ref, ng_ref, w_ref,
               fqg_ref, fkg_ref, bf_ref, mu_ref, w0_ref, a0_ref, wup_ref, aup_ref,
               kk_ref, ka_ref, rk_ref, mqg_ref, ones_ref, pbias_ref, bias1_ref,
               qs_ref, kb_ref, augq_ref, augk_ref, kn_ref, v_ref, logf_ref, gf_ref,
               rt_ref, at_ref, kh_ref, bh_ref, rv_ref, lc_ref, gr_ref, bg_ref, om_ref, shift_out_ref,
               *vt_and_scratch):
    carry_shift, carry_c = vt_and_scratch[-2:]
    i = pl.program_id(1)
    tm = x_ref.shape[1]
    sub = min(tm, CUMSUM_BLOCK)

    if seq is None:
        @pl.when(i == 0)
        def _():
            carry_shift[...] = shift0_ref[0]
            carry_c[...] = jnp.zeros_like(carry_c)

    x = x_ref[0]
    ms = jnp.mean(x * x, axis=-1, keepdims=True)
    xn = (x * lax.rsqrt(ms + NORM_EPS) * ng_ref[...]).astype(BF16)

    def proj(off, width):
        return _dot(xn, w_ref[:, off:off + width])

    ones_pair = ones_ref[...]
    row = lax.broadcasted_iota(jnp.int32, (sub, sub), 0)
    col = lax.broadcasted_iota(jnp.int32, (sub, sub), 1)
    tri = jnp.where((col <= row) & ((row ^ col) < (sub if seq is None else seq)), 1.0, 0.0).astype(BF16)
    tri_chunk = jnp.where((col <= row) & ((row ^ col) < chunk), 1.0, 0.0).astype(BF16)
    blocks = [slice(s * sub, (s + 1) * sub) for s in range(tm // sub)]

    hs = proj(OFF_SM, SMALL)
    hq = proj(OFF_FQ, W_FOX)
    hk = proj(OFF_FK, W_FOX)
    hv = proj(OFF_FV, W_FOX)
    hg = proj(OFF_FG, W_FOX)
    hmq = proj(OFF_MQ, W_MEM)
    hmg = proj(OFF_MG, W_MEM)
    hrr = proj(OFF_RR, W_RWKV)
    hrk = proj(OFF_RK, W_RWKV)
    hrv = proj(OFF_RV, W_RWKV)
    hrg = proj(OFF_RG, W_RWKV)
    qn = _head_rms(hq, fqg_ref[...], ones_pair)
    kn = _head_rms(hk, fkg_ref[...], ones_pair)
    qm = _head_rms(hmq, mqg_ref[...], ones_pair)

    qms = (qm * (HEAD ** -0.5 * LOG2E)).astype(BF16)
    mem_rows = [(qms, 0)] if seq is None else [(qms[g * seq:(g + 1) * seq], g) for g in range(tm // seq)]

    def mem_probs():
        probs = []
        for q_rows, g in mem_rows:
            s4 = _dot_nt(_stack_heads(q_rows, N_MEM), mk_ref[g].astype(BF16))
            e = jnp.exp2(s4 - jnp.max(s4, axis=1, keepdims=True))
            probs.append((e / jnp.sum(e, axis=1, keepdims=True)).astype(BF16))
        return probs

    def mem_values(probs):
        return jnp.concatenate([_unstack_heads(_dot(p4, mv_ref[g].astype(BF16)), N_MEM)
                                for p4, (_, g) in zip(probs, mem_rows)], axis=0)

    if seq is not None:
        mem_p = mem_probs()
    kn_ref[0] = kn
    kb_ref[0] = kn.astype(BF16)
    qs_ref[0] = (qn * (HEAD ** -0.5 * LOG2E)).astype(BF16)
    v_ref[0] = hv
    if with_vt:
        hvt = hv.T
        ones_row = jnp.where(lax.broadcasted_iota(jnp.int32, (VROWS - HEAD, tm), 0) == 0, 1.0, 0.0)
        vt_and_scratch[0][0, 0] = jnp.concatenate(
            [blk for h in range(N_FOX) for blk in (hvt[HEAD * h:HEAD * (h + 1)], ones_row)], axis=0).astype(BF16)
    gf_ref[0] = _silu(hg).astype(BF16)
    f = hs + bf_ref[...]
    lane = lax.broadcasted_iota(jnp.int32, (tm, SMALL), 1)
    logf = jnp.where(lane < N_FOX, jnp.minimum(f, 0.0) - jnp.log1p(jnp.exp(-jnp.abs(f))), 0.0)
    logf_ref[0] = logf[:, 0:N_FOX]
    if seq is None:
        carry = carry_c[...]
        c_blocks = []
        for blk in blocks:
            cb = _exact_lhs_dot(tri, logf[blk], 3) + carry
            carry = cb[sub - 1:sub, :]
            c_blocks.append(cb)
        carry_c[...] = carry
    else:
        c_blocks = [_exact_lhs_dot(tri, logf[blk], 3) for blk in blocks]
    c = jnp.concatenate(c_blocks, axis=0)
    pieces = _split3(c * LOG2E)
    packed = pieces[0].astype(F32)
    for j in range(1, N_BIAS):
        packed = packed + pltpu.roll(pieces[j].astype(F32), BIAS_STRIDE * j, 1)
    aug = _dot(packed.astype(BF16), pbias_ref[...]) + bias1_ref[...]
    augq_ref[0] = aug[:, :W_FOX].astype(BF16)
    augk_ref[0] = aug[:, W_FOX:].astype(BF16)

    if seq is None:
        om_ref[0] = (mem_values(mem_probs()) * _silu(hmg)).astype(BF16)

    row_in_seq = lax.broadcasted_iota(jnp.int32, (tm, 1), 0) % (tm if seq is None else seq)

    def tshift(cols, off, width):
        if seq is None:
            before_first = carry_shift[:, off:off + width]
            carry_shift[:, off:off + width] = cols[tm - 1:tm, :]
        else:
            before_first = shift0_ref[0, :, off:off + width]
            shift_out_ref[0, :, off:off + width] = cols
        prev = jnp.where(row_in_seq == 0, before_first, pltpu.roll(cols, 1, 0))
        return cols + (prev - cols) * mu_ref[:, off:off + width]

    r = tshift(hrr, 0, W_RWKV)
    k = tshift(hrk, W_RWKV, W_RWKV)
    v = tshift(hrv, 2 * W_RWKV, W_RWKV)
    g = tshift(hrg, 3 * W_RWKV, W_RWKV)
    sm = tshift(hs, 4 * W_RWKV, SMALL)
    if seq is None:
        shift_out_ref[0] = carry_shift[...]

    w_lin = w0_ref[...] + _dot1(_dot, jnp.tanh(sm), wup_ref[...])
    a = jax.nn.sigmoid(a0_ref[...] + _dot1(_dot, sm, aup_ref[...]))
    lw = -jnp.exp(-_softplus(-w_lin) - 0.5)
    kk = k * kk_ref[...]
    kk = kk * lax.rsqrt(jnp.maximum(_segsum(kk * kk, ones_pair), 1e-24))
    kt = k * (1.0 + (a - 1.0) * ka_ref[...])
    gr = _silu(g)
    gr_ref[0] = gr.astype(BF16)
    bg_ref[0] = (_segsum(r * kt * rk_ref[...], ones_pair) * v * gr).astype(BF16)
    lc = jnp.concatenate([_exact_lhs_dot(tri_chunk, lw[blk], 2) for blk in blocks], axis=0)
    e_neg = jnp.exp(-lc)
    rt_ref[0] = (r * jnp.exp(lc)).astype(BF16)
    at_ref[0] = (-kk * jnp.exp(lc - lw)).astype(BF16)
    kh_ref[0] = (kt * e_neg).astype(BF16)
    bh_ref[0] = (kk * a * e_neg).astype(BF16)
    rv_ref[0] = v.astype(BF16)
    lc_ref[0] = lc
    if seq is not None:
        om_ref[0] = (mem_values(mem_p) * _silu(hmg)).astype(BF16)


def _proj(x, shift0, mk, mv, wts, tm, chunk, with_vt, seq=None):
    b, t, _ = x.shape
    grid = (b, t // tm)
    tok = lambda w: pl.BlockSpec((1, tm, w), lambda bi, i: (bi, i, 0))
    per_b = lambda s1, s2: pl.BlockSpec((1, s1, s2), lambda bi, i: (bi, 0, 0))
    w_specs = [_full(a.shape) for a in wts]
    bf = lambda w: jax.ShapeDtypeStruct((b, t, w), BF16)
    f32 = lambda w: jax.ShapeDtypeStruct((b, t, w), F32)
    out_shape = [bf(W_FOX)] * 4 + [f32(W_FOX), f32(W_FOX), f32(N_FOX), bf(W_FOX)] \
        + [bf(W_RWKV)] * 5 + [f32(W_RWKV)] + [bf(W_RWKV), bf(W_RWKV), bf(W_MEM),
                                               jax.ShapeDtypeStruct(shift0.shape, F32)]
    shift_spec = per_b(1, SHIFT_W) if seq is None else tok(SHIFT_W)
    mem_spec = pl.BlockSpec((mk.shape[0] // b, N_MEM_TOK, W_MEM), lambda bi, i: (bi, 0, 0))
    out_specs = [tok(W_FOX)] * 4 + [tok(W_FOX), tok(W_FOX), tok(N_FOX), tok(W_FOX)] \
        + [tok(W_RWKV)] * 6 + [tok(W_RWKV), tok(W_RWKV), tok(W_MEM), shift_spec]
    if with_vt:
        out_shape.append(jax.ShapeDtypeStruct((b, t // tm, N_FOX * VROWS, tm), BF16))
        out_specs.append(pl.BlockSpec((1, 1, N_FOX * VROWS, tm), lambda bi, i: (bi, i, 0, 0)))
    return pl.pallas_call(
        functools.partial(_proj_body, chunk, with_vt, seq),
        grid=grid,
        in_specs=[tok(D_MODEL), shift_spec, mem_spec, mem_spec] + w_specs,
        out_specs=out_specs,
        out_shape=out_shape,
        scratch_shapes=[pltpu.VMEM((1, SHIFT_W), F32), pltpu.VMEM((1, SMALL), F32)],
        compiler_params=_params(2),
        name="proj",
    )(x, shift0, mk, mv, *wts)


def _fox_body(qs_ref, aq_ref, kb_ref, ak_ref, vt_ref, o_ref, m_scr, acc_scr, sa_scr, sb_scr, qm_scr):
    qi = pl.program_id(2)
    tq = qs_ref.shape[1]
    slab = vt_ref.shape[3]
    slabs_per_tile = tq // slab
    m_scr[...] = jnp.full_like(m_scr, -1e30)
    acc_scr[...] = jnp.zeros_like(acc_scr)
    q = qs_ref[0]
    aq = aq_ref[0]
    lane_lo = lax.broadcasted_iota(jnp.int32, (tq, PAIR), 1) < HEAD
    zero = jnp.zeros_like(q)
    qm_scr[0] = jnp.concatenate([jnp.where(lane_lo, q, zero), jnp.where(lane_lo, aq, zero)], axis=1)
    qm_scr[1] = jnp.concatenate([jnp.where(lane_lo, zero, q), jnp.where(lane_lo, zero, aq)], axis=1)

    def scores(ki, s_scr):
        rows = pl.ds(pl.multiple_of(ki * tq, tq), tq)
        k2 = jnp.concatenate([kb_ref[0, rows, :], ak_ref[0, rows, :]], axis=1)
        for hh in range(2):
            s_scr[hh] = _dot_nt(k2, qm_scr[hh])

    def consume(ki, s_scr, masked):
        s = [s_scr[hh] for hh in range(2)]
        if masked:
            kpos = lax.broadcasted_iota(jnp.int32, (tq, tq), 0)
            qpos = lax.broadcasted_iota(jnp.int32, (tq, tq), 1)
            s = [jnp.where(kpos <= qpos, sh, -1e30) for sh in s]
        m_prev = [m_scr[hh] for hh in range(2)]
        m_new = [jnp.maximum(m_prev[hh], jnp.max(s[hh], axis=0, keepdims=True)) for hh in range(2)]
        p = [jnp.exp2(s[hh] - m_new[hh]).astype(BF16) for hh in range(2)]
        pv = []
        for hh in range(2):
            vt = jnp.concatenate([vt_ref[0, ki * slabs_per_tile + d, VROWS * hh:VROWS * (hh + 1), :]
                                  for d in range(slabs_per_tile)], axis=1)
            pv.append(_dot(vt, p[hh]))
        for hh in range(2):
            acc_scr[hh] = jnp.exp2(m_prev[hh] - m_new[hh]) * acc_scr[hh] + pv[hh]
            m_scr[hh] = m_new[hh]

    bufs = (sa_scr, sb_scr)
    scores(0, bufs[0])

    def tile_group(j, carry):
        for d in range(FOX_UNROLL):
            scores(FOX_UNROLL * j + d + 1, bufs[(d + 1) % 2])
            consume(FOX_UNROLL * j + d, bufs[d % 2], False)
        return carry

    lax.fori_loop(0, qi // FOX_UNROLL, tile_group, 0)
    base = (qi // FOX_UNROLL) * FOX_UNROLL
    for rem in range(FOX_UNROLL):
        @pl.when(qi - base == rem)
        def _():
            for d in range(rem):
                scores(base + d + 1, bufs[(d + 1) % 2])
                consume(base + d, bufs[d % 2], False)
            consume(qi, bufs[rem % 2], True)

    o_t = jnp.concatenate([acc_scr[hh][0:HEAD] / acc_scr[hh][HEAD:HEAD + 1] for hh in range(2)], axis=0)
    o_ref[0] = o_t.T.astype(BF16)


def _fox(qs, augq, kb, augk, vt, tq):
    b, t, _ = qs.shape
    n_slab, slab = vt.shape[1], vt.shape[3]
    q_blk = pl.BlockSpec((1, tq, PAIR), lambda bi, p, qi: (bi, qi, p))
    k_blk = pl.BlockSpec((1, t, PAIR), lambda bi, p, qi: (bi, 0, p))
    return pl.pallas_call(
        _fox_body,
        grid=(b, N_FOX // 2, t // tq),
        in_specs=[q_blk, q_blk, k_blk, k_blk,
                  pl.BlockSpec((1, n_slab, 2 * VROWS, slab), lambda bi, p, qi: (bi, 0, p, 0))],
        out_specs=q_blk,
        out_shape=jax.ShapeDtypeStruct((b, t, W_FOX), BF16),
        scratch_shapes=[pltpu.VMEM((2, 1, tq), F32), pltpu.VMEM((2, VROWS, tq), F32),
                        pltpu.VMEM((2, tq, tq), F32), pltpu.VMEM((2, tq, tq), F32),
                        pltpu.VMEM((2, tq, 2 * PAIR), BF16)],
        compiler_params=_params(3),
        name="fox",
    )(qs, augq, kb, augk, vt)


def _fox_step_body(qs_ref, aq_ref, kn_ref, ak_ref, vn_ref, kp_ref, vp_ref, lfp_ref, o_ref):
    s_new = qs_ref.shape[1]
    p_len = kp_ref.shape[1]
    blk = min(p_len, STEP_SUFFIX_BLOCK)
    row = lax.broadcasted_iota(jnp.int32, (blk, blk), 0)
    col = lax.broadcasted_iota(jnp.int32, (blk, blk), 1)
    upper = jnp.where(col > row, 1.0, 0.0).astype(BF16)
    later = jnp.zeros((1, SMALL), F32)
    sufs = []
    for bi in reversed(range(p_len // blk)):
        lf = lfp_ref[0, bi * blk:(bi + 1) * blk, :]
        within = _exact_lhs_dot(upper, lf, 3)
        sufs.insert(0, within + later)
        later = later + within[0:1, :] + lf[0:1, :]
    suf_t = (jnp.concatenate(sufs, axis=0) * LOG2E).T
    q6 = _stack_heads(qs_ref[0], N_FOX)
    a6 = _stack_heads(aq_ref[0], N_FOX)
    lane_in_head = lax.broadcasted_iota(jnp.int32, a6.shape, 1) % HEAD
    cq6 = jnp.sum(jnp.where(lane_in_head < N_BIAS, a6.astype(F32), 0.0), axis=1, keepdims=True)
    bias6 = jnp.concatenate([jnp.broadcast_to(suf_t[h:h + 1, :], (s_new, p_len)) for h in range(N_FOX)], axis=0)
    sp = _dot_nt(q6, kp_ref[0].astype(BF16)) + cq6 + bias6
    qrow = lax.broadcasted_iota(jnp.int32, (N_FOX * s_new, s_new), 0) % s_new
    kcol = lax.broadcasted_iota(jnp.int32, (N_FOX * s_new, s_new), 1)
    sn = jnp.where(kcol <= qrow, _dot_nt(q6, kn_ref[0]) + _dot_nt(a6, ak_ref[0]), -1e30)
    m = jnp.maximum(jnp.max(sp, axis=1, keepdims=True), jnp.max(sn, axis=1, keepdims=True))
    pp = jnp.exp2(sp - m)
    pn = jnp.exp2(sn - m)
    l = jnp.sum(pp, axis=1, keepdims=True) + jnp.sum(pn, axis=1, keepdims=True)
    o6 = (_dot((pp / l).astype(BF16), vp_ref[0].astype(BF16))
          + _dot((pn / l).astype(BF16), vn_ref[0].astype(BF16)))
    o_ref[0] = _unstack_heads(o6, N_FOX).astype(BF16)


def _fox_step(qs, augq, kb, augk, v_new, k_past, v_past, logf_past_pad):
    b, s_new, _ = qs.shape
    p_len = k_past.shape[1]
    tok = lambda rows, w: pl.BlockSpec((1, rows, w), lambda i: (i, 0, 0))
    return pl.pallas_call(
        _fox_step_body,
        grid=(b,),
        in_specs=[tok(s_new, W_FOX)] * 5 + [tok(p_len, W_FOX), tok(p_len, W_FOX), tok(p_len, SMALL)],
        out_specs=tok(s_new, W_FOX),
        out_shape=jax.ShapeDtypeStruct((b, s_new, W_FOX), BF16),
        compiler_params=_params(1),
        name="fox_step",
    )(qs, augq, kb, augk, v_new, k_past, v_past, logf_past_pad)


def _rwkv_body(chunk, independent, rt_ref, at_ref, kh_ref, bh_ref, v_ref, lc_ref, s0_ref, y_ref, sout_ref, s_scr):
    j = pl.program_id(1)
    nj = pl.num_programs(1)
    c = chunk
    c2 = 2 * chunk
    n_chunks = rt_ref.shape[1] // c
    n_pairs = N_RWKV // 2

    def pair_state(seq_i, p):
        z = jnp.zeros((HEAD, HEAD), F32)
        return jnp.concatenate([jnp.concatenate([s0_ref[seq_i, 2 * p], z], axis=1),
                                jnp.concatenate([z, s0_ref[seq_i, 2 * p + 1]], axis=1)], axis=0)

    if not independent:
        @pl.when(j == 0)
        def _():
            for p in range(n_pairs):
                s_scr[p] = pair_state(0, p)

    lane_lo = lax.broadcasted_iota(jnp.int32, (c, PAIR), 1) < HEAD
    row = lax.broadcasted_iota(jnp.int32, (c2, c2), 0)
    col = lax.broadcasted_iota(jnp.int32, (c2, c2), 1)
    same = (row >= c) == (col >= c)
    strict = same & (col < row)
    lower = same & (col <= row)
    eye = jnp.where(col == row, 1.0, 0.0)
    srow = lax.broadcasted_iota(jnp.int32, (PAIR, PAIR), 0)
    scol = lax.broadcasted_iota(jnp.int32, (PAIR, PAIR), 1)
    same_state = (srow >= HEAD) == (scol >= HEAD)

    def stack(x):
        zero = jnp.zeros_like(x)
        return jnp.concatenate([jnp.where(lane_lo, x, zero), jnp.where(lane_lo, zero, x)], axis=0)

    def fold(a):
        return a[:c] + a[c:]

    units = [(ci, p) for ci in range(n_chunks) for p in range(n_pairs)]

    def tile(ref, u):
        ci, p = u
        return ref[0, ci * c:(ci + 1) * c, PAIR * p:PAIR * (p + 1)]

    rt = [tile(rt_ref, u) for u in units]
    at2 = [stack(tile(at_ref, u)) for u in units]
    kh = [tile(kh_ref, u) for u in units]
    bh = [tile(bh_ref, u) for u in units]
    v2 = [stack(tile(v_ref, u)) for u in units]
    g = [_dot_nt(jnp.concatenate([at2[i], stack(rt[i])], axis=0),
                 jnp.concatenate([kh[i], kh[i], bh[i], bh[i]], axis=0)) for i in range(len(units))]
    a_ak = [jnp.where(strict, x[:c2, :c2], 0.0) for x in g]
    a_ab = [jnp.where(strict, x[:c2, c2:], 0.0) for x in g]
    a_rk = [fold(jnp.where(lower, x[c2:, :c2], 0.0)) for x in g]
    a_rb = [fold(jnp.where(lower, x[c2:, c2:], 0.0)) for x in g]
    akv2 = [_dot1(_dot, a_ak[i], v2[i]) for i in range(len(units))]
    yv = [_dot1(_dot, a_rk[i], v2[i]) for i in range(len(units))]
    inv = [eye + x for x in a_ab]
    pw = a_ab
    n = 1
    while 2 * n < c:
        pw = [_dot1(_dot, x, x) for x in pw]
        inv = [inv[i] + _dot1(_dot, inv[i], pw[i]) for i in range(len(units))]
        n *= 2
    tw = [_dot(inv[i].astype(BF16), jnp.concatenate([at2[i], akv2[i].astype(BF16)], axis=1))
          for i in range(len(units))]
    w_mat = [fold(x[:, :PAIR]) for x in tw]
    u0 = [fold(x[:, PAIR:]) for x in tw]
    pc = [jnp.exp(lc_ref[0, (ci + 1) * c - 1:(ci + 1) * c, PAIR * p:PAIR * (p + 1)]) for ci, p in units]
    kc = [(kh[i] * pc[i]).astype(BF16) for i in range(len(units))]
    bc = [(bh[i] * pc[i]).astype(BF16) for i in range(len(units))]
    s_mix = [jnp.where(same_state, _dot_tn(w_mat[i].astype(BF16), bc[i]), 0.0).astype(BF16)
             for i in range(len(units))]
    s_add = [jnp.where(same_state, _dot_tn(jnp.concatenate([tile(v_ref, units[i]), u0[i].astype(BF16)], axis=0),
                                           jnp.concatenate([kc[i], bc[i]], axis=0)), 0.0)
             for i in range(len(units))]
    rb = [_dot1(_dot, a_rb[i], jnp.concatenate([stack(w_mat[i]), stack(u0[i])], axis=1)) for i in range(len(units))]
    r_eff = [(rt[i].astype(F32) + rb[i][:, :PAIR]).astype(BF16) for i in range(len(units))]
    y_add = [yv[i] + rb[i][:, PAIR:] for i in range(len(units))]

    for ci in range(n_chunks):
        idx = [ci * n_pairs + p for p in range(n_pairs)]
        s_old = [pair_state(ci, p) if independent else s_scr[p] for p in range(n_pairs)]
        s_bf = [s.astype(BF16) for s in s_old]
        upd = [_dot(s_bf[p], s_mix[i]) for p, i in enumerate(idx)]
        y = [_dot_nt(r_eff[i], s_bf[p]) + y_add[i] for p, i in enumerate(idx)]
        for p, i in enumerate(idx):
            s_new = s_old[p] * pc[i] + (upd[p] + s_add[i])
            y_ref[0, ci * c:(ci + 1) * c, PAIR * p:PAIR * (p + 1)] = y[p]
            if independent:
                sout_ref[ci, 2 * p] = s_new[:HEAD, :HEAD]
                sout_ref[ci, 2 * p + 1] = s_new[HEAD:, HEAD:]
            else:
                s_scr[p] = s_new

    if not independent:
        @pl.when(j == nj - 1)
        def _():
            for p in range(n_pairs):
                s = s_scr[p]
                sout_ref[0, 2 * p] = s[:HEAD, :HEAD]
                sout_ref[0, 2 * p + 1] = s[HEAD:, HEAD:]


def _rwkv(rt, at, kh, bh, v, lc, s0, tc, chunk, independent=False):
    b, t, _ = rt.shape
    tok = pl.BlockSpec((1, tc, W_RWKV), lambda bi, j: (bi, j, 0))
    if independent:
        st = pl.BlockSpec((tc // chunk, N_RWKV, HEAD, HEAD), lambda bi, j: (j, 0, 0, 0))
    else:
        st = pl.BlockSpec((1, N_RWKV, HEAD, HEAD), lambda bi, j: (bi, 0, 0, 0))
    return pl.pallas_call(
        functools.partial(_rwkv_body, chunk, independent),
        grid=(b, t // tc),
        in_specs=[tok] * 6 + [st],
        out_specs=[tok, st],
        out_shape=[jax.ShapeDtypeStruct((b, t, W_RWKV), F32), jax.ShapeDtypeStruct(s0.shape, F32)],
        scratch_shapes=[pltpu.VMEM((N_RWKV // 2, PAIR, PAIR), F32)],
        compiler_params=_params(2),
        name="rwkv",
    )(rt, at, kh, bh, v, lc, s0)


def _out_body(x_ref, of_ref, gf_ref, y_ref, gr_ref, bg_ref, om_ref, gnw_ref, gnb_ref, ones_ref,
              wf_ref, wr_ref, wm_ref, o_ref):
    ones_pair = ones_ref[...]
    acc = _dot((of_ref[0].astype(F32) * gf_ref[0].astype(F32)).astype(BF16), wf_ref[...])
    acc = acc + _dot(om_ref[0], wm_ref[...])
    y = y_ref[0]
    mu = _segsum(y, ones_pair) * (1.0 / HEAD)
    d = y - mu
    var = _segsum(d * d, ones_pair) * (1.0 / HEAD)
    yn = d * lax.rsqrt(var + GN_EPS) * gnw_ref[...] + gnb_ref[...]
    o_r = yn * gr_ref[0].astype(F32) + bg_ref[0].astype(F32)
    acc = acc + _dot(o_r.astype(BF16), wr_ref[...])
    o_ref[0] = x_ref[0] + acc


def _out(x, of, gf, y, gr, bg, om, gnw, gnb, ones_pair, wf, wr, wm, tm):
    b, t, _ = x.shape
    tok = lambda w: pl.BlockSpec((1, tm, w), lambda bi, i: (bi, i, 0))
    return pl.pallas_call(
        _out_body,
        grid=(b, t // tm),
        in_specs=[tok(D_MODEL), tok(W_FOX), tok(W_FOX), tok(W_RWKV), tok(W_RWKV), tok(W_RWKV), tok(W_MEM),
                  _full(gnw.shape), _full(gnb.shape), _full(ones_pair.shape),
                  _full(wf.shape), _full(wr.shape), _full(wm.shape)],
        out_specs=tok(D_MODEL),
        out_shape=jax.ShapeDtypeStruct((b, t, D_MODEL), F32),
        compiler_params=_params(2),
        name="out_proj",
    )(x, of, gf, y, gr, bg, om, gnw, gnb, ones_pair, wf, wr, wm)


def _pad_in_columns(w):
    r0 = FOX_COLS
    m0 = FOX_COLS + RWKV_COLS
    zeros = lambda n: jnp.zeros(w.shape[:-1] + (n,), w.dtype)
    return jnp.concatenate([
        w[..., 0:3 * W_FOX], w[..., 3 * W_FOX + N_FOX:FOX_COLS],
        w[..., r0:r0 + 3 * W_RWKV], w[..., r0 + 3 * W_RWKV + 2 * LORA:r0 + RWKV_COLS],
        w[..., m0:m0 + MEM_COLS],
        w[..., 3 * W_FOX:3 * W_FOX + N_FOX], zeros(SM_WD - N_FOX),
        w[..., r0 + 3 * W_RWKV:r0 + 3 * W_RWKV + 2 * LORA], zeros(SMALL - SM_AD - LORA),
    ], axis=-1)


def _pad_shift_row(a):
    zeros = lambda n: jnp.zeros(a.shape[:-1] + (n,), a.dtype)
    return jnp.concatenate([
        a[..., 0:3 * W_RWKV], a[..., 3 * W_RWKV + 2 * LORA:RWKV_COLS],
        zeros(SM_WD), a[..., 3 * W_RWKV:3 * W_RWKV + 2 * LORA], zeros(SMALL - SM_AD - LORA)], axis=-1)


def _unpad_shift_row(a):
    small = 4 * W_RWKV
    return jnp.concatenate([a[..., 0:3 * W_RWKV], a[..., small + SM_WD:small + SM_AD + LORA],
                            a[..., 3 * W_RWKV:4 * W_RWKV]], axis=-1)


def _bias_placement():
    place = np.zeros((SMALL, 2 * W_FOX), np.float32)
    const = np.zeros((1, 2 * W_FOX), np.float32)
    for h in range(N_FOX):
        for j in range(N_BIAS):
            place[BIAS_STRIDE * j + h, HEAD * h + j] = 1.0
            place[BIAS_STRIDE * j + h, W_FOX + HEAD * h + N_BIAS + j] = -1.0
            const[0, HEAD * h + N_BIAS + j] = 1.0
            const[0, W_FOX + HEAD * h + j] = 1.0
    return place, const


def _block_ones(width):
    h = np.arange(width) // HEAD
    return (h[:, None] == h[None, :]).astype(np.float32)


def _layer(x, shift_prev, s0, mk, mv, fox_past, wts, ones_pair, tm, tq, tc, chunk, flat_out=False):
    (proj_wts, gnw, gnb, wf, wr, wm) = wts
    b, t, _ = x.shape
    if flat_out:
        first = jnp.pad(_pad_shift_row(shift_prev), ((0, 0), (0, t - 1), (0, 0))).reshape(1, b * t, SHIFT_W)
        outs = _proj(x.reshape(1, b * t, D_MODEL), first, mk, mv, proj_wts, b * t, chunk, with_vt=False, seq=t)
        outs = [o.reshape(b, t, o.shape[-1]) for o in outs]
        outs[-1] = outs[-1][:, t - 1:t, :]
    else:
        outs = _proj(x, _pad_shift_row(shift_prev), mk, mv, proj_wts, tm, chunk, with_vt=fox_past is None)
    (qs, kb, augq, augk, kn, v, logf, gf, rt, at, kh, bh, rv, lc, gr, bg, om, shift_out, *vt) = outs
    if fox_past is None:
        of = _fox(qs, augq, kb, augk, vt[0], tq)
    else:
        k_past, v_past, logf_past = fox_past
        lfp = jnp.pad(logf_past.astype(F32), ((0, 0), (0, 0), (0, SMALL - N_FOX)))
        p_len = k_past.shape[1]
        of = _fox_step(qs, augq, kb, augk, v, k_past.reshape(b, p_len, W_FOX).astype(F32),
                       v_past.reshape(b, p_len, W_FOX).astype(F32), lfp)
    rows = lambda a: a.reshape(1, b * t, a.shape[-1]) if flat_out else a
    if flat_out:
        y_r, s_new = _rwkv(rows(rt), rows(at), rows(kh), rows(bh), rows(rv), rows(lc), s0,
                           min(b, STEP_SEQS_PER_TILE) * t, chunk, independent=True)
        y_r = y_r.reshape(b, t, W_RWKV)
    else:
        y_r, s_new = _rwkv(rt, at, kh, bh, rv, lc, s0, tc, chunk)
    y = _out(rows(x), rows(of), rows(gf), rows(y_r), rows(gr), rows(bg), rows(om), gnw, gnb, ones_pair, wf, wr, wm,
             b * t if flat_out else tm).reshape(b, t, D_MODEL)
    return (y, kn.reshape(b, t, N_FOX, HEAD), v.reshape(b, t, N_FOX, HEAD), logf, s_new, _unpad_shift_row(shift_out))


def kernel(x_prompt, x_sample, mem_prompt, cache_fox_k, cache_fox_v, cache_fox_logf, cache_mem_k, cache_mem_v, state_rwkv, state_rwkv_shift, norm_g, w_in, fox_q_g, fox_k_g, fox_b_f, rwkv_mu, rwkv_w0, rwkv_w_up, rwkv_a0, rwkv_a_up, rwkv_k_k, rwkv_k_a, rwkv_r_k, rwkv_gn_w, rwkv_gn_b, mem_norm_g, w_mem_kv, mem_q_g, mem_k_g, w_out):
    depth = w_in.shape[0]
    bp, tp, _ = x_prompt.shape
    bs, s_len, _ = x_sample.shape
    assert x_prompt.shape[2] == D_MODEL and w_in.shape[1:] == (D_MODEL, FOX_COLS + RWKV_COLS + MEM_COLS)
    assert mem_prompt.shape[1:] == (N_MEM_TOK, D_MODEL) and cache_mem_k.shape[2:] == (N_MEM_TOK, N_MEM, HEAD)
    assert tp % PROMPT_TM == 0 and tp % PROMPT_TQ == 0 and tp % PROMPT_TC == 0 and PROMPT_TQ % PROMPT_TM == 0
    assert PROMPT_TM % CUMSUM_BLOCK == 0 and CUMSUM_BLOCK % RWKV_CHUNK == 0 and PROMPT_TC % RWKV_CHUNK == 0
    assert s_len & (s_len - 1) == 0 and BF16_ROWS <= s_len <= RWKV_CHUNK
    assert (bs * s_len) % min(bs * s_len, CUMSUM_BLOCK) == 0 and bs % min(bs, STEP_SEQS_PER_TILE) == 0
    assert cache_fox_k.shape[2] % min(cache_fox_k.shape[2], STEP_SUFFIX_BLOCK) == 0
    place_np, const_np = _bias_placement()
    ones_pair = jnp.asarray(_block_ones(PAIR), BF16)
    pbias = jnp.asarray(place_np, BF16)
    bias1 = jnp.asarray(const_np, F32)

    yp, ys = x_prompt, x_sample
    outs = [[] for _ in range(12)]
    for l in range(depth):
        row = lambda a: a[l].reshape(1, -1).astype(F32)
        tile = lambda a, n: jnp.tile(a[l].reshape(1, -1).astype(F32), (1, n))
        w_pad = _pad_in_columns(w_in[l]).astype(BF16)
        bf_pad = jnp.pad(row(fox_b_f), ((0, 0), (0, SMALL - N_FOX)))
        wup_pad = jnp.pad(rwkv_w_up[l].astype(F32), ((SM_WD, SMALL - SM_WD - LORA), (0, 0)))
        aup_pad = jnp.pad(rwkv_a_up[l].astype(F32), ((SM_AD, SMALL - SM_AD - LORA), (0, 0)))
        proj_wts = (row(norm_g), w_pad, tile(fox_q_g, N_FOX), tile(fox_k_g, N_FOX), bf_pad,
                    _pad_shift_row(row(rwkv_mu)),
                    row(rwkv_w0), row(rwkv_a0), wup_pad, aup_pad, row(rwkv_k_k), row(rwkv_k_a), row(rwkv_r_k),
                    tile(mem_q_g, N_MEM), ones_pair, pbias, bias1)
        wo = w_out[l].astype(BF16)
        wts = (proj_wts, row(rwkv_gn_w), row(rwkv_gn_b), wo[:W_FOX], wo[W_FOX:W_FOX + W_RWKV], wo[W_FOX + W_RWKV:])

        mk2, mv2, mkb, mvb = _mem_kv(mem_prompt, row(mem_norm_g), w_mem_kv[l].astype(BF16), tile(mem_k_g, N_MEM),
                                     ones_pair)
        shift_zero = jnp.zeros((bp, 1, RWKV_COLS), F32)
        s_zero = jnp.zeros((bp, N_RWKV, HEAD, HEAD), F32)
        yp, k, v, lf, s_new, sh_new = _layer(yp, shift_zero, s_zero, mkb, mvb, None, wts, ones_pair,
                                             tm=PROMPT_TM, tq=PROMPT_TQ, tc=PROMPT_TC, chunk=RWKV_CHUNK)
        mk = mk2.reshape(bp, N_MEM_TOK, N_MEM, HEAD)
        mv = mv2.reshape(bp, N_MEM_TOK, N_MEM, HEAD)
        for lst, val in zip(outs[:7], (k, v, lf, mk, mv, s_new, sh_new)):
            lst.append(val)
        bs, s_len = ys.shape[0], ys.shape[1]
        ys, k, v, lf, s_new, sh_new = _layer(
            ys, state_rwkv_shift[l], state_rwkv[l].astype(F32),
            cache_mem_k[l].reshape(bs, N_MEM_TOK, W_MEM).astype(F32),
            cache_mem_v[l].reshape(bs, N_MEM_TOK, W_MEM).astype(F32),
            (cache_fox_k[l], cache_fox_v[l], cache_fox_logf[l]), wts, ones_pair,
            tm=s_len, tq=None, tc=s_len, chunk=s_len, flat_out=True)
        for lst, val in zip(outs[7:], (k, v, lf, s_new, sh_new)):
            lst.append(val)
    return (yp, ys) + tuple(jnp.stack(o) for o in outs)
```

```python
import functools

import numpy as np
import jax
import jax.numpy as jnp
from jax import lax
from jax.experimental import pallas as pl
from jax.experimental.pallas import tpu as pltpu

F32 = jnp.float32
BF16 = jnp.bfloat16

D_MODEL = 1024
HEAD = 64
N_FOX = 6
N_RWKV = 6
N_MEM = 4
W_FOX = N_FOX * HEAD
W_RWKV = N_RWKV * HEAD
W_MEM = N_MEM * HEAD
N_MEM_TOK = 256
LORA = 32
NORM_EPS = 1e-6
GN_EPS = 64e-5
LOG2E = float(np.log2(np.e))
FOX_COLS = 4 * W_FOX + N_FOX
RWKV_COLS = 4 * W_RWKV + 2 * LORA
MEM_COLS = 2 * W_MEM

LANE = 128
PAIR = 2 * HEAD
SMALL = LANE
SM_WD = 32
SM_AD = 64
OFF_FQ, OFF_FK, OFF_FV, OFF_FG = 0, 384, 768, 1152
OFF_RR, OFF_RK, OFF_RV, OFF_RG = 1536, 1920, 2304, 2688
OFF_MQ, OFF_MG = 3072, 3328
OFF_SM = 3584
N_PAD = OFF_SM + SMALL
SHIFT_W = 4 * W_RWKV + SMALL
N_BIAS = 3
BIAS_STRIDE = 8
BF16_ROWS = 16
VROWS = HEAD + BF16_ROWS
RWKV_CHUNK = 64
VMEM_LIMIT = 56 * 1024 * 1024
PROMPT_TM = 512
PROMPT_TQ = 512
PROMPT_TC = 512
FOX_UNROLL = 4
CUMSUM_BLOCK = 256
STEP_SUFFIX_BLOCK = 256
STEP_SEQS_PER_TILE = 8


def _dot(a, b):
    return jnp.dot(a, b, preferred_element_type=F32)


def _dot_nt(a, b):
    return lax.dot_general(a, b, (((1,), (1,)), ((), ())), preferred_element_type=F32)


def _dot_tn(a, b):
    return lax.dot_general(a, b, (((0,), (0,)), ((), ())), preferred_element_type=F32)


def _split2(x):
    hi = x.astype(BF16)
    lo = (x - hi.astype(F32)).astype(BF16)
    return hi, lo


def _split3(x):
    hi = x.astype(BF16)
    r1 = x - hi.astype(F32)
    mid = r1.astype(BF16)
    lo = (r1 - mid.astype(F32)).astype(BF16)
    return hi, mid, lo


def _dot1(fn, a, b):
    return fn(a.astype(BF16), b.astype(BF16))


def _exact_lhs_dot(m_bf16, x, parts):
    pieces = _split3(x) if parts == 3 else _split2(x)
    acc = _dot(m_bf16, pieces[0])
    for p in pieces[1:]:
        acc = acc + _dot(m_bf16, p)
    return acc


def _segsum(x, ones_pair):
    xb = x.astype(BF16)
    return jnp.concatenate([_dot(xb[:, PAIR * p:PAIR * (p + 1)], ones_pair) for p in range(x.shape[1] // PAIR)],
                           axis=1)


def _head_rms(t, gain, ones_pair):
    msq = _segsum(t * t, ones_pair) * (1.0 / HEAD)
    return t * lax.rsqrt(msq + NORM_EPS) * gain


def _silu(x):
    return x * jax.nn.sigmoid(x)


def _softplus(z):
    return jnp.maximum(z, 0.0) + jnp.log1p(jnp.exp(-jnp.abs(z)))


def _stack_heads(x, n_heads):
    head_of_lane = lax.broadcasted_iota(jnp.int32, x.shape, 1) // HEAD
    zero = jnp.zeros_like(x)
    return jnp.concatenate([jnp.where(head_of_lane == h, x, zero) for h in range(n_heads)], axis=0)


def _unstack_heads(x6, n_heads):
    rows = x6.shape[0] // n_heads
    head_of_lane = lax.broadcasted_iota(jnp.int32, (rows, x6.shape[1]), 1) // HEAD
    out = jnp.zeros((rows, x6.shape[1]), x6.dtype)
    for h in range(n_heads):
        out = jnp.where(head_of_lane == h, x6[h * rows:(h + 1) * rows], out)
    return out


def _full(shape):
    n = len(shape)
    return pl.BlockSpec(shape, lambda *_: (0,) * n)


def _params(n_axes):
    return pltpu.CompilerParams(dimension_semantics=("arbitrary",) * n_axes, vmem_limit_bytes=VMEM_LIMIT)


def _mem_kv_body(mem_ref, g_ref, w_ref, kg_ref, ones_ref, k_ref, v_ref, kb_ref, vb_ref):
    x = mem_ref[0]
    ms = jnp.mean(x * x, axis=-1, keepdims=True)
    xn = (x * lax.rsqrt(ms + NORM_EPS) * g_ref[...]).astype(BF16)
    kv = _dot(xn, w_ref[...])
    k = _head_rms(kv[:, :W_MEM], kg_ref[...], ones_ref[...])
    v = kv[:, W_MEM:]
    k_ref[0] = k
    v_ref[0] = v
    kb_ref[0] = k.astype(BF16)
    vb_ref[0] = v.astype(BF16)


def _mem_kv(mem, g, w_bf16, kg4, ones_pair):
    b = mem.shape[0]
    blk = pl.BlockSpec((1, N_MEM_TOK, W_MEM), lambda i: (i, 0, 0))
    return pl.pallas_call(
        _mem_kv_body,
        grid=(b,),
        in_specs=[pl.BlockSpec((1, N_MEM_TOK, D_MODEL), lambda i: (i, 0, 0)),
                  _full((1, D_MODEL)), _full((D_MODEL, 2 * W_MEM)), _full((1, W_MEM)), _full((PAIR, PAIR))],
        out_specs=[blk] * 4,
        out_shape=[jax.ShapeDtypeStruct((b, N_MEM_TOK, W_MEM), F32)] * 2
        + [jax.ShapeDtypeStruct((b, N_MEM_TOK, W_MEM), BF16)] * 2,
        compiler_params=_params(1),
        name="mem_kv",
    )(mem, g, w_bf16, kg4, ones_pair)


def _proj_body(chunk, with_vt, seq,
               x_ref, shift0_ref, mk_ref, mv_ref, ng_ref, w_ref,
               fqg_ref, fkg_ref, bf_ref, mu_ref, w0_ref, a0_ref, wup_ref, aup_ref,
               kk_ref, ka_ref, rk_ref, mqg_ref, ones_ref, pbias_ref, bias1_ref,
               qs_ref, kb_ref, augq_ref, augk_ref, kn_ref, v_ref, logf_ref, gf_ref,
               rt_ref, at_ref, kh_ref, bh_ref, rv_ref, lc_ref, gr_ref, bg_ref, om_ref, shift_out_ref,
               *vt_and_scratch):
    carry_shift, carry_c = vt_and_scratch[-2:]
    i = pl.program_id(1)
    tm = x_ref.shape[1]
    sub = min(tm, CUMSUM_BLOCK)

    if seq is None:
        @pl.when(i == 0)
        def _():
            carry_shift[...] = shift0_ref[0]
            carry_c[...] = jnp.zeros_like(carry_c)

    x = x_ref[0]
    ms = jnp.mean(x * x, axis=-1, keepdims=True)
    xn = (x * lax.rsqrt(ms + NORM_EPS) * ng_ref[...]).astype(BF16)

    def proj(off, width):
        return _dot(xn, w_ref[:, off:off + width])

    ones_pair = ones_ref[...]
    row = lax.broadcasted_iota(jnp.int32, (sub, sub), 0)
    col = lax.broadcasted_iota(jnp.int32, (sub, sub), 1)
    tri = jnp.where((col <= row) & ((row ^ col) < (sub if seq is None else seq)), 1.0, 0.0).astype(BF16)
    tri_chunk = jnp.where((col <= row) & ((row ^ col) < chunk), 1.0, 0.0).astype(BF16)
    blocks = [slice(s * sub, (s + 1) * sub) for s in range(tm // sub)]

    hs = proj(OFF_SM, SMALL)
    hq = proj(OFF_FQ, W_FOX)
    hk = proj(OFF_FK, W_FOX)
    hv = proj(OFF_FV, W_FOX)
    hg = proj(OFF_FG, W_FOX)
    hmq = proj(OFF_MQ, W_MEM)
    hmg = proj(OFF_MG, W_MEM)
    hrr = proj(OFF_RR, W_RWKV)
    hrk = proj(OFF_RK, W_RWKV)
    hrv = proj(OFF_RV, W_RWKV)
    hrg = proj(OFF_RG, W_RWKV)
    qn = _head_rms(hq, fqg_ref[...], ones_pair)
    kn = _head_rms(hk, fkg_ref[...], ones_pair)
    qm = _head_rms(hmq, mqg_ref[...], ones_pair)

    qms = (qm * (HEAD ** -0.5 * LOG2E)).astype(BF16)
    mem_rows = [(qms, 0)] if seq is None else [(qms[g * seq:(g + 1) * seq], g) for g in range(tm // seq)]

    def mem_probs():
        probs = []
        for q_rows, g in mem_rows:
            s4 = _dot_nt(_stack_heads(q_rows, N_MEM), mk_ref[g].astype(BF16))
            e = jnp.exp2(s4 - jnp.max(s4, axis=1, keepdims=True))
            probs.append((e / jnp.sum(e, axis=1, keepdims=True)).astype(BF16))
        return probs

    def mem_values(probs):
        return jnp.concatenate([_unstack_heads(_dot(p4, mv_ref[g].astype(BF16)), N_MEM)
                                for p4, (_, g) in zip(probs, mem_rows)], axis=0)

    if seq is not None:
        mem_p = mem_probs()
    kn_ref[0] = kn
    kb_ref[0] = kn.astype(BF16)
    qs_ref[0] = (qn * (HEAD ** -0.5 * LOG2E)).astype(BF16)
    v_ref[0] = hv
    if with_vt:
        hvt = hv.T
        ones_row = jnp.where(lax.broadcasted_iota(jnp.int32, (VROWS - HEAD, tm), 0) == 0, 1.0, 0.0)
        vt_and_scratch[0][0, 0] = jnp.concatenate(
            [blk for h in range(N_FOX) for blk in (hvt[HEAD * h:HEAD * (h + 1)], ones_row)], axis=0).astype(BF16)
    gf_ref[0] = _silu(hg).astype(BF16)
    f = hs + bf_ref[...]
    lane = lax.broadcasted_iota(jnp.int32, (tm, SMALL), 1)
    logf = jnp.where(lane < N_FOX, jnp.minimum(f, 0.0) - jnp.log1p(jnp.exp(-jnp.abs(f))), 0.0)
    logf_ref[0] = logf[:, 0:N_FOX]
    if seq is None:
        carry = carry_c[...]
        c_blocks = []
        for blk in blocks:
            cb = _exact_lhs_dot(tri, logf[blk], 3) + carry
            carry = cb[sub - 1:sub, :]
            c_blocks.append(cb)
        carry_c[...] = carry
    else:
        c_blocks = [_exact_lhs_dot(tri, logf[blk], 3) for blk in blocks]
    c = jnp.concatenate(c_blocks, axis=0)
    pieces = _split3(c * LOG2E)
    packed = pieces[0].astype(F32)
    for j in range(1, N_BIAS):
        packed = packed + pltpu.roll(pieces[j].astype(F32), BIAS_STRIDE * j, 1)
    aug = _dot(packed.astype(BF16), pbias_ref[...]) + bias1_ref[...]
    augq_ref[0] = aug[:, :W_FOX].astype(BF16)
    augk_ref[0] = aug[:, W_FOX:].astype(BF16)

    if seq is None:
        om_ref[0] = (mem_values(mem_probs()) * _silu(hmg)).astype(BF16)

    row_in_seq = lax.broadcasted_iota(jnp.int32, (tm, 1), 0) % (tm if seq is None else seq)

    def tshift(cols, off, width):
        if seq is None:
            before_first = carry_shift[:, off:off + width]
            carry_shift[:, off:off + width] = cols[tm - 1:tm, :]
        else:
            before_first = shift0_ref[0, :, off:off + width]
            shift_out_ref[0, :, off:off + width] = cols
        prev = jnp.where(row_in_seq == 0, before_first, pltpu.roll(cols, 1, 0))
        return cols + (prev - cols) * mu_ref[:, off:off + width]

    r = tshift(hrr, 0, W_RWKV)
    k = tshift(hrk, W_RWKV, W_RWKV)
    v = tshift(hrv, 2 * W_RWKV, W_RWKV)
    g = tshift(hrg, 3 * W_RWKV, W_RWKV)
    sm = tshift(hs, 4 * W_RWKV, SMALL)
    if seq is None:
        shift_out_ref[0] = carry_shift[...]

    w_lin = w0_ref[...] + _dot1(_dot, jnp.tanh(sm), wup_ref[...])
    a = jax.nn.sigmoid(a0_ref[...] + _dot1(_dot, sm, aup_ref[...]))
    lw = -jnp.exp(-_softplus(-w_lin) - 0.5)
    kk = k * kk_ref[...]
    kk = kk * lax.rsqrt(jnp.maximum(_segsum(kk * kk, ones_pair), 1e-24))
    kt = k * (1.0 + (a - 1.0) * ka_ref[...])
    gr = _silu(g)
    gr_ref[0] = gr.astype(BF16)
    bg_ref[0] = (_segsum(r * kt * rk_ref[...], ones_pair) * v * gr).astype(BF16)
    lc = jnp.concatenate([_exact_lhs_dot(tri_chunk, lw[blk], 2) for blk in blocks], axis=0)
    e_neg = jnp.exp(-lc)
    rt_ref[0] = (r * jnp.exp(lc)).astype(BF16)
    at_ref[0] = (-kk * jnp.exp(lc - lw)).astype(BF16)
    kh_ref[0] = (kt * e_neg).astype(BF16)
    bh_ref[0] = (kk * a * e_neg).astype(BF16)
    rv_ref[0] = v.astype(BF16)
    lc_ref[0] = lc
    if seq is not None:
        om_ref[0] = (mem_values(mem_p) * _silu(hmg)).astype(BF16)


def _proj(x, shift0, mk, mv, wts, tm, chunk, with_vt, seq=None):
    b, t, _ = x.shape
    grid = (b, t // tm)
    tok = lambda w: pl.BlockSpec((1, tm, w), lambda bi, i: (bi, i, 0))
    per_b = lambda s1, s2: pl.BlockSpec((1, s1, s2), lambda bi, i: (bi, 0, 0))
    w_specs = [_full(a.shape) for a in wts]
    bf = lambda w: jax.ShapeDtypeStruct((b, t, w), BF16)
    f32 = lambda w: jax.ShapeDtypeStruct((b, t, w), F32)
    out_shape = [bf(W_FOX)] * 4 + [f32(W_FOX), f32(W_FOX), f32(N_FOX), bf(W_FOX)] \
        + [bf(W_RWKV)] * 5 + [f32(W_RWKV)] + [bf(W_RWKV), bf(W_RWKV), bf(W_MEM),
                                               jax.ShapeDtypeStruct(shift0.shape, F32)]
    shift_spec = per_b(1, SHIFT_W) if seq is None else tok(SHIFT_W)
    mem_spec = pl.BlockSpec((mk.shape[0] // b, N_MEM_TOK, W_MEM), lambda bi, i: (bi, 0, 0))
    out_specs = [tok(W_FOX)] * 4 + [tok(W_FOX), tok(W_FOX), tok(N_FOX), tok(W_FOX)] \
        + [tok(W_RWKV)] * 6 + [tok(W_RWKV), tok(W_RWKV), tok(W_MEM), shift_spec]
    if with_vt:
        out_shape.append(jax.ShapeDtypeStruct((b, t // tm, N_FOX * VROWS, tm), BF16))
        out_specs.append(pl.BlockSpec((1, 1, N_FOX * VROWS, tm), lambda bi, i: (bi, i, 0, 0)))
    return pl.pallas_call(
        functools.partial(_proj_body, chunk, with_vt, seq),
        grid=grid,
        in_specs=[tok(D_MODEL), shift_spec, mem_spec, mem_spec] + w_specs,
        out_specs=out_specs,
        out_shape=out_shape,
        scratch_shapes=[pltpu.VMEM((1, SHIFT_W), F32), pltpu.VMEM((1, SMALL), F32)],
        compiler_params=_params(2),
        name="proj",
    )(x, shift0, mk, mv, *wts)


def _fox_body(qs_ref, aq_ref, kb_ref, ak_ref, vt_ref, o_ref, m_scr, acc_scr, sa_scr, sb_scr, qm_scr):
    qi = pl.program_id(2)
    tq = qs_ref.shape[1]
    slab = vt_ref.shape[3]
    slabs_per_tile = tq // slab
    m_scr[...] = jnp.full_like(m_scr, -1e30)
    acc_scr[...] = jnp.zeros_like(acc_scr)
    q = qs_ref[0]
    aq = aq_ref[0]
    lane_lo = lax.broadcasted_iota(jnp.int32, (tq, PAIR), 1) < HEAD
    zero = jnp.zeros_like(q)
    qm_scr[0] = jnp.concatenate([jnp.where(lane_lo, q, zero), jnp.where(lane_lo, aq, zero)], axis=1)
    qm_scr[1] = jnp.concatenate([jnp.where(lane_lo, zero, q), jnp.where(lane_lo, zero, aq)], axis=1)

    def scores(ki, s_scr):
        rows = pl.ds(pl.multiple_of(ki * tq, tq), tq)
        k2 = jnp.concatenate([kb_ref[0, rows, :], ak_ref[0, rows, :]], axis=1)
        for hh in range(2):
            s_scr[hh] = _dot_nt(k2, qm_scr[hh])

    def consume(ki, s_scr, masked):
        s = [s_scr[hh] for hh in range(2)]
        if masked:
            kpos = lax.broadcasted_iota(jnp.int32, (tq, tq), 0)
            qpos = lax.broadcasted_iota(jnp.int32, (tq, tq), 1)
            s = [jnp.where(kpos <= qpos, sh, -1e30) for sh in s]
        m_prev = [m_scr[hh] for hh in range(2)]
        m_new = [jnp.maximum(m_prev[hh], jnp.max(s[hh], axis=0, keepdims=True)) for hh in range(2)]
        p = [jnp.exp2(s[hh] - m_new[hh]).astype(BF16) for hh in range(2)]
        pv = []
        for hh in range(2):
            vt = jnp.concatenate([vt_ref[0, ki * slabs_per_tile + d, VROWS * hh:VROWS * (hh + 1), :]
                                  for d in range(slabs_per_tile)], axis=1)
            pv.append(_dot(vt, p[hh]))
        for hh in range(2):
            acc_scr[hh] = jnp.exp2(m_prev[hh] - m_new[hh]) * acc_scr[hh] + pv[hh]
            m_scr[hh] = m_new[hh]

    bufs = (sa_scr, sb_scr)
    scores(0, bufs[0])

    def tile_group(j, carry):
        for d in range(FOX_UNROLL):
            scores(FOX_UNROLL * j + d + 1, bufs[(d + 1) % 2])
            consume(FOX_UNROLL * j + d, bufs[d % 2], False)
        return carry

    lax.fori_loop(0, qi // FOX_UNROLL, tile_group, 0)
    base = (qi // FOX_UNROLL) * FOX_UNROLL
    for rem in range(FOX_UNROLL):
        @pl.when(qi - base == rem)
        def _():
            for d in range(rem):
                scores(base + d + 1, bufs[(d + 1) % 2])
                consume(base + d, bufs[d % 2], False)
            consume(qi, bufs[rem % 2], True)

    o_t = jnp.concatenate([acc_scr[hh][0:HEAD] / acc_scr[hh][HEAD:HEAD + 1] for hh in range(2)], axis=0)
    o_ref[0] = o_t.T.astype(BF16)


def _fox(qs, augq, kb, augk, vt, tq):
    b, t, _ = qs.shape
    n_slab, slab = vt.shape[1], vt.shape[3]
    q_blk = pl.BlockSpec((1, tq, PAIR), lambda bi, p, qi: (bi, qi, p))
    k_blk = pl.BlockSpec((1, t, PAIR), lambda bi, p, qi: (bi, 0, p))
    return pl.pallas_call(
        _fox_body,
        grid=(b, N_FOX // 2, t // tq),
        in_specs=[q_blk, q_blk, k_blk, k_blk,
                  pl.BlockSpec((1, n_slab, 2 * VROWS, slab), lambda bi, p, qi: (bi, 0, p, 0))],
        out_specs=q_blk,
        out_shape=jax.ShapeDtypeStruct((b, t, W_FOX), BF16),
        scratch_shapes=[pltpu.VMEM((2, 1, tq), F32), pltpu.VMEM((2, VROWS, tq), F32),
                        pltpu.VMEM((2, tq, tq), F32), pltpu.VMEM((2, tq, tq), F32),
                        pltpu.VMEM((2, tq, 2 * PAIR), BF16)],
        compiler_params=_params(3),
        name="fox",
    )(qs, augq, kb, augk, vt)


def _fox_step_body(qs_ref, aq_ref, kn_ref, ak_ref, vn_ref, kp_ref, vp_ref, lfp_ref, o_ref):
    s_new = qs_ref.shape[1]
    p_len = kp_ref.shape[1]
    blk = min(p_len, STEP_SUFFIX_BLOCK)
    row = lax.broadcasted_iota(jnp.int32, (blk, blk), 0)
    col = lax.broadcasted_iota(jnp.int32, (blk, blk), 1)
    upper = jnp.where(col > row, 1.0, 0.0).astype(BF16)
    later = jnp.zeros((1, SMALL), F32)
    sufs = []
    for bi in reversed(range(p_len // blk)):
        lf = jnp.concatenate([lfp_ref[0, bi * blk:(bi + 1) * blk, :], jnp.zeros((blk, SMALL - N_FOX), F32)], axis=1)
        within = _exact_lhs_dot(upper, lf, 3)
        sufs.insert(0, within + later)
        later = later + within[0:1, :] + lf[0:1, :]
    suf_t = (jnp.concatenate(sufs, axis=0) * LOG2E).T
    q6 = _stack_heads(qs_ref[0], N_FOX)
    a6 = _stack_heads(aq_ref[0], N_FOX)
    lane_in_head = lax.broadcasted_iota(jnp.int32, a6.shape, 1) % HEAD
    cq6 = jnp.sum(jnp.where(lane_in_head < N_BIAS, a6.astype(F32), 0.0), axis=1, keepdims=True)
    bias6 = jnp.concatenate([jnp.broadcast_to(suf_t[h:h + 1, :], (s_new, p_len)) for h in range(N_FOX)], axis=0)
    sp = _dot_nt(q6, kp_ref[0].astype(BF16)) + cq6 + bias6
    qrow = lax.broadcasted_iota(jnp.int32, (N_FOX * s_new, s_new), 0) % s_new
    kcol = lax.broadcasted_iota(jnp.int32, (N_FOX * s_new, s_new), 1)
    sn = jnp.where(kcol <= qrow, _dot_nt(q6, kn_ref[0]) + _dot_nt(a6, ak_ref[0]), -1e30)
    m = jnp.maximum(jnp.max(sp, axis=1, keepdims=True), jnp.max(sn, axis=1, keepdims=True))
    pp = jnp.exp2(sp - m)
    pn = jnp.exp2(sn - m)
    l = jnp.sum(pp, axis=1, keepdims=True) + jnp.sum(pn, axis=1, keepdims=True)
    o6 = (_dot((pp / l).astype(BF16), vp_ref[0].astype(BF16))
          + _dot((pn / l).astype(BF16), vn_ref[0].astype(BF16)))
    o_ref[0] = _unstack_heads(o6, N_FOX).astype(BF16)


def _fox_step(qs, augq, kb, augk, v_new, k_past, v_past, logf_past):
    b, s_new, _ = qs.shape
    p_len = k_past.shape[1]
    tok = lambda rows, w: pl.BlockSpec((1, rows, w), lambda i: (i, 0, 0))
    return pl.pallas_call(
        _fox_step_body,
        grid=(b,),
        in_specs=[tok(s_new, W_FOX)] * 5 + [tok(p_len, W_FOX), tok(p_len, W_FOX), tok(p_len, N_FOX)],
        out_specs=tok(s_new, W_FOX),
        out_shape=jax.ShapeDtypeStruct((b, s_new, W_FOX), BF16),
        compiler_params=_params(1),
        name="fox_step",
    )(qs, augq, kb, augk, v_new, k_past, v_past, logf_past)


def _rwkv_body(chunk, independent, rt_ref, at_ref, kh_ref, bh_ref, v_ref, lc_ref, s0_ref, y_ref, sout_ref, s_scr):
    j = pl.program_id(1)
    nj = pl.num_programs(1)
    c = chunk
    c2 = 2 * chunk
    n_chunks = rt_ref.shape[1] // c
    n_pairs = N_RWKV // 2

    def pair_state(seq_i, p):
        z = jnp.zeros((HEAD, HEAD), F32)
        return jnp.concatenate([jnp.concatenate([s0_ref[seq_i, 2 * p], z], axis=1),
                                jnp.concatenate([z, s0_ref[seq_i, 2 * p + 1]], axis=1)], axis=0)

    if not independent:
        @pl.when(j == 0)
        def _():
            for p in range(n_pairs):
                s_scr[p] = pair_state(0, p)

    lane_lo = lax.broadcasted_iota(jnp.int32, (c, PAIR), 1) < HEAD
    row = lax.broadcasted_iota(jnp.int32, (c2, c2), 0)
    col = lax.broadcasted_iota(jnp.int32, (c2, c2), 1)
    same = (row >= c) == (col >= c)
    strict = same & (col < row)
    lower = same & (col <= row)
    eye = jnp.where(col == row, 1.0, 0.0)
    srow = lax.broadcasted_iota(jnp.int32, (PAIR, PAIR), 0)
    scol = lax.broadcasted_iota(jnp.int32, (PAIR, PAIR), 1)
    same_state = (srow >= HEAD) == (scol >= HEAD)

    def stack(x):
        zero = jnp.zeros_like(x)
        return jnp.concatenate([jnp.where(lane_lo, x, zero), jnp.where(lane_lo, zero, x)], axis=0)

    def fold(a):
        return a[:c] + a[c:]

    units = [(ci, p) for ci in range(n_chunks) for p in range(n_pairs)]

    def tile(ref, u):
        ci, p = u
        return ref[0, ci * c:(ci + 1) * c, PAIR * p:PAIR * (p + 1)]

    rt = [tile(rt_ref, u) for u in units]
    at2 = [stack(tile(at_ref, u)) for u in units]
    kh = [tile(kh_ref, u) for u in units]
    bh = [tile(bh_ref, u) for u in units]
    v2 = [stack(tile(v_ref, u)) for u in units]
    g = [_dot_nt(jnp.concatenate([at2[i], stack(rt[i])], axis=0),
                 jnp.concatenate([kh[i], kh[i], bh[i], bh[i]], axis=0)) for i in range(len(units))]
    a_ak = [jnp.where(strict, x[:c2, :c2], 0.0) for x in g]
    a_ab = [jnp.where(strict, x[:c2, c2:], 0.0) for x in g]
    a_rk = [fold(jnp.where(lower, x[c2:, :c2], 0.0)) for x in g]
    a_rb = [fold(jnp.where(lower, x[c2:, c2:], 0.0)) for x in g]
    akv2 = [_dot1(_dot, a_ak[i], v2[i]) for i in range(len(units))]
    yv = [_dot1(_dot, a_rk[i], v2[i]) for i in range(len(units))]
    inv = [eye + x for x in a_ab]
    pw = a_ab
    n = 1
    while 2 * n < c:
        pw = [_dot1(_dot, x, x) for x in pw]
        inv = [inv[i] + _dot1(_dot, inv[i], pw[i]) for i in range(len(units))]
        n *= 2
    tw = [_dot(inv[i].astype(BF16), jnp.concatenate([at2[i], akv2[i].astype(BF16)], axis=1))
          for i in range(len(units))]
    w_mat = [fold(x[:, :PAIR]) for x in tw]
    u0 = [fold(x[:, PAIR:]) for x in tw]
    pc = [jnp.exp(lc_ref[0, (ci + 1) * c - 1:(ci + 1) * c, PAIR * p:PAIR * (p + 1)]) for ci, p in units]
    kc = [(kh[i] * pc[i]).astype(BF16) for i in range(len(units))]
    bc = [(bh[i] * pc[i]).astype(BF16) for i in range(len(units))]
    s_mix = [jnp.where(same_state, _dot_tn(w_mat[i].astype(BF16), bc[i]), 0.0).astype(BF16)
             for i in range(len(units))]
    s_add = [jnp.where(same_state, _dot_tn(jnp.concatenate([tile(v_ref, units[i]), u0[i].astype(BF16)], axis=0),
                                           jnp.concatenate([kc[i], bc[i]], axis=0)), 0.0)
             for i in range(len(units))]
    rb = [_dot1(_dot, a_rb[i], jnp.concatenate([stack(w_mat[i]), stack(u0[i])], axis=1)) for i in range(len(units))]
    r_eff = [(rt[i].astype(F32) + rb[i][:, :PAIR]).astype(BF16) for i in range(len(units))]
    y_add = [yv[i] + rb[i][:, PAIR:] for i in range(len(units))]

    for ci in range(n_chunks):
        idx = [ci * n_pairs + p for p in range(n_pairs)]
        s_old = [pair_state(ci, p) if independent else s_scr[p] for p in range(n_pairs)]
        s_bf = [s.astype(BF16) for s in s_old]
        upd = [_dot(s_bf[p], s_mix[i]) for p, i in enumerate(idx)]
        y = [_dot_nt(r_eff[i], s_bf[p]) + y_add[i] for p, i in enumerate(idx)]
        for p, i in enumerate(idx):
            s_new = s_old[p] * pc[i] + (upd[p] + s_add[i])
            y_ref[0, ci * c:(ci + 1) * c, PAIR * p:PAIR * (p + 1)] = y[p]
            if independent:
                sout_ref[ci, 2 * p] = s_new[:HEAD, :HEAD]
                sout_ref[ci, 2 * p + 1] = s_new[HEAD:, HEAD:]
            else:
                s_scr[p] = s_new

    if not independent:
        @pl.when(j == nj - 1)
        def _():
            for p in range(n_pairs):
                s = s_scr[p]
                sout_ref[0, 2 * p] = s[:HEAD, :HEAD]
                sout_ref[0, 2 * p + 1] = s[HEAD:, HEAD:]


def _rwkv(rt, at, kh, bh, v, lc, s0, tc, chunk, independent=False):
    b, t, _ = rt.shape
    tok = pl.BlockSpec((1, tc, W_RWKV), lambda bi, j: (bi, j, 0))
    if independent:
        st = pl.BlockSpec((tc // chunk, N_RWKV, HEAD, HEAD), lambda bi, j: (j, 0, 0, 0))
    else:
        st = pl.BlockSpec((1, N_RWKV, HEAD, HEAD), lambda bi, j: (bi, 0, 0, 0))
    return pl.pallas_call(
        functools.partial(_rwkv_body, chunk, independent),
        grid=(b, t // tc),
        in_specs=[tok] * 6 + [st],
        out_specs=[tok, st],
        out_shape=[jax.ShapeDtypeStruct((b, t, W_RWKV), F32), jax.ShapeDtypeStruct(s0.shape, F32)],
        scratch_shapes=[pltpu.VMEM((N_RWKV // 2, PAIR, PAIR), F32)],
        compiler_params=_params(2),
        name="rwkv",
    )(rt, at, kh, bh, v, lc, s0)


def _out_body(x_ref, of_ref, gf_ref, y_ref, gr_ref, bg_ref, om_ref, gnw_ref, gnb_ref, ones_ref,
              wf_ref, wr_ref, wm_ref, o_ref):
    ones_pair = ones_ref[...]
    acc = _dot((of_ref[0].astype(F32) * gf_ref[0].astype(F32)).astype(BF16), wf_ref[...])
    acc = acc + _dot(om_ref[0], wm_ref[...])
    y = y_ref[0]
    mu = _segsum(y, ones_pair) * (1.0 / HEAD)
    d = y - mu
    var = _segsum(d * d, ones_pair) * (1.0 / HEAD)
    yn = d * lax.rsqrt(var + GN_EPS) * gnw_ref[...] + gnb_ref[...]
    o_r = yn * gr_ref[0].astype(F32) + bg_ref[0].astype(F32)
    acc = acc + _dot(o_r.astype(BF16), wr_ref[...])
    o_ref[0] = x_ref[0] + acc


def _out(x, of, gf, y, gr, bg, om, gnw, gnb, ones_pair, wf, wr, wm, tm):
    b, t, _ = x.shape
    tok = lambda w: pl.BlockSpec((1, tm, w), lambda bi, i: (bi, i, 0))
    return pl.pallas_call(
        _out_body,
        grid=(b, t // tm),
        in_specs=[tok(D_MODEL), tok(W_FOX), tok(W_FOX), tok(W_RWKV), tok(W_RWKV), tok(W_RWKV), tok(W_MEM),
                  _full(gnw.shape), _full(gnb.shape), _full(ones_pair.shape),
                  _full(wf.shape), _full(wr.shape), _full(wm.shape)],
        out_specs=tok(D_MODEL),
        out_shape=jax.ShapeDtypeStruct((b, t, D_MODEL), F32),
        compiler_params=_params(2),
        name="out_proj",
    )(x, of, gf, y, gr, bg, om, gnw, gnb, ones_pair, wf, wr, wm)


def _pad_in_columns(w):
    r0 = FOX_COLS
    m0 = FOX_COLS + RWKV_COLS
    zeros = lambda n: jnp.zeros(w.shape[:-1] + (n,), w.dtype)
    return jnp.concatenate([
        w[..., 0:3 * W_FOX], w[..., 3 * W_FOX + N_FOX:FOX_COLS],
        w[..., r0:r0 + 3 * W_RWKV], w[..., r0 + 3 * W_RWKV + 2 * LORA:r0 + RWKV_COLS],
        w[..., m0:m0 + MEM_COLS],
        w[..., 3 * W_FOX:3 * W_FOX + N_FOX], zeros(SM_WD - N_FOX),
        w[..., r0 + 3 * W_RWKV:r0 + 3 * W_RWKV + 2 * LORA], zeros(SMALL - SM_AD - LORA),
    ], axis=-1)


def _pad_w_body(w_ref, o_ref):
    o_ref[...] = _pad_in_columns(w_ref[...]).astype(BF16)


def _pad_w(w):
    rows = w.shape[0] // 4
    return pl.pallas_call(
        _pad_w_body,
        grid=(w.shape[0] // rows,),
        in_specs=[pl.BlockSpec((rows, w.shape[1]), lambda i: (i, 0))],
        out_specs=pl.BlockSpec((rows, N_PAD), lambda i: (i, 0)),
        out_shape=jax.ShapeDtypeStruct((w.shape[0], N_PAD), BF16),
        compiler_params=_params(1),
        name="pad_w",
    )(w)


def _pad_shift_row(a):
    zeros = lambda n: jnp.zeros(a.shape[:-1] + (n,), a.dtype)
    return jnp.concatenate([
        a[..., 0:3 * W_RWKV], a[..., 3 * W_RWKV + 2 * LORA:RWKV_COLS],
        zeros(SM_WD), a[..., 3 * W_RWKV:3 * W_RWKV + 2 * LORA], zeros(SMALL - SM_AD - LORA)], axis=-1)


def _unpad_shift_row(a):
    small = 4 * W_RWKV
    return jnp.concatenate([a[..., 0:3 * W_RWKV], a[..., small + SM_WD:small + SM_AD + LORA],
                            a[..., 3 * W_RWKV:4 * W_RWKV]], axis=-1)


def _bias_placement():
    place = np.zeros((SMALL, 2 * W_FOX), np.float32)
    const = np.zeros((1, 2 * W_FOX), np.float32)
    for h in range(N_FOX):
        for j in range(N_BIAS):
            place[BIAS_STRIDE * j + h, HEAD * h + j] = 1.0
            place[BIAS_STRIDE * j + h, W_FOX + HEAD * h + N_BIAS + j] = -1.0
            const[0, HEAD * h + N_BIAS + j] = 1.0
            const[0, W_FOX + HEAD * h + j] = 1.0
    return place, const


def _block_ones(width):
    h = np.arange(width) // HEAD
    return (h[:, None] == h[None, :]).astype(np.float32)


def _layer(x, shift_prev, s0, mk, mv, fox_past, wts, ones_pair, tm, tq, tc, chunk, flat_out=False):
    (proj_wts, gnw, gnb, wf, wr, wm) = wts
    b, t, _ = x.shape
    if flat_out:
        first = jnp.pad(_pad_shift_row(shift_prev), ((0, 0), (0, t - 1), (0, 0))).reshape(1, b * t, SHIFT_W)
        outs = _proj(x.reshape(1, b * t, D_MODEL), first, mk, mv, proj_wts, b * t, chunk, with_vt=False, seq=t)
        outs = [o.reshape(b, t, o.shape[-1]) for o in outs]
        outs[-1] = outs[-1][:, t - 1:t, :]
    else:
        outs = _proj(x, _pad_shift_row(shift_prev), mk, mv, proj_wts, tm, chunk, with_vt=fox_past is None)
    (qs, kb, augq, augk, kn, v, logf, gf, rt, at, kh, bh, rv, lc, gr, bg, om, shift_out, *vt) = outs
    if fox_past is None:
        of = _fox(qs, augq, kb, augk, vt[0], tq)
    else:
        k_past, v_past, logf_past = fox_past
        lfp = logf_past.astype(F32)
        p_len = k_past.shape[1]
        of = _fox_step(qs, augq, kb, augk, v, k_past.reshape(b, p_len, W_FOX).astype(F32),
                       v_past.reshape(b, p_len, W_FOX).astype(F32), lfp)
    rows = lambda a: a.reshape(1, b * t, a.shape[-1]) if flat_out else a
    if flat_out:
        y_r, s_new = _rwkv(rows(rt), rows(at), rows(kh), rows(bh), rows(rv), rows(lc), s0,
                           min(b, STEP_SEQS_PER_TILE) * t, chunk, independent=True)
        y_r = y_r.reshape(b, t, W_RWKV)
    else:
        y_r, s_new = _rwkv(rt, at, kh, bh, rv, lc, s0, tc, chunk)
    y = _out(rows(x), rows(of), rows(gf), rows(y_r), rows(gr), rows(bg), rows(om), gnw, gnb, ones_pair, wf, wr, wm,
             b * t if flat_out else tm).reshape(b, t, D_MODEL)
    return (y, kn.reshape(b, t, N_FOX, HEAD), v.reshape(b, t, N_FOX, HEAD), logf, s_new, _unpad_shift_row(shift_out))


def kernel(x_prompt, x_sample, mem_prompt, cache_fox_k, cache_fox_v, cache_fox_logf, cache_mem_k, cache_mem_v, state_rwkv, state_rwkv_shift, norm_g, w_in, fox_q_g, fox_k_g, fox_b_f, rwkv_mu, rwkv_w0, rwkv_w_up, rwkv_a0, rwkv_a_up, rwkv_k_k, rwkv_k_a, rwkv_r_k, rwkv_gn_w, rwkv_gn_b, mem_norm_g, w_mem_kv, mem_q_g, mem_k_g, w_out):
    depth = w_in.shape[0]
    bp, tp, _ = x_prompt.shape
    bs, s_len, _ = x_sample.shape
    assert x_prompt.shape[2] == D_MODEL and w_in.shape[1:] == (D_MODEL, FOX_COLS + RWKV_COLS + MEM_COLS)
    assert mem_prompt.shape[1:] == (N_MEM_TOK, D_MODEL) and cache_mem_k.shape[2:] == (N_MEM_TOK, N_MEM, HEAD)
    assert tp % PROMPT_TM == 0 and tp % PROMPT_TQ == 0 and tp % PROMPT_TC == 0 and PROMPT_TQ % PROMPT_TM == 0
    assert PROMPT_TM % CUMSUM_BLOCK == 0 and CUMSUM_BLOCK % RWKV_CHUNK == 0 and PROMPT_TC % RWKV_CHUNK == 0
    assert s_len & (s_len - 1) == 0 and BF16_ROWS <= s_len <= RWKV_CHUNK
    assert (bs * s_len) % min(bs * s_len, CUMSUM_BLOCK) == 0 and bs % min(bs, STEP_SEQS_PER_TILE) == 0
    assert cache_fox_k.shape[2] % min(cache_fox_k.shape[2], STEP_SUFFIX_BLOCK) == 0
    place_np, const_np = _bias_placement()
    ones_pair = jnp.asarray(_block_ones(PAIR), BF16)
    pbias = jnp.asarray(place_np, BF16)
    bias1 = jnp.asarray(const_np, F32)

    yp, ys = x_prompt, x_sample
    outs = [[] for _ in range(12)]
    for l in range(depth):
        row = lambda a: a[l].reshape(1, -1).astype(F32)
        tile = lambda a, n: jnp.tile(a[l].reshape(1, -1).astype(F32), (1, n))
        w_pad = _pad_w(w_in[l].astype(F32))
        bf_pad = jnp.pad(row(fox_b_f), ((0, 0), (0, SMALL - N_FOX)))
        wup_pad = jnp.pad(rwkv_w_up[l].astype(F32), ((SM_WD, SMALL - SM_WD - LORA), (0, 0)))
        aup_pad = jnp.pad(rwkv_a_up[l].astype(F32), ((SM_AD, SMALL - SM_AD - LORA), (0, 0)))
        proj_wts = (row(norm_g), w_pad, tile(fox_q_g, N_FOX), tile(fox_k_g, N_FOX), bf_pad,
                    _pad_shift_row(row(rwkv_mu)),
                    row(rwkv_w0), row(rwkv_a0), wup_pad, aup_pad, row(rwkv_k_k), row(rwkv_k_a), row(rwkv_r_k),
                    tile(mem_q_g, N_MEM), ones_pair, pbias, bias1)
        wo = w_out[l].astype(BF16)
        wts = (proj_wts, row(rwkv_gn_w), row(rwkv_gn_b), wo[:W_FOX], wo[W_FOX:W_FOX + W_RWKV], wo[W_FOX + W_RWKV:])

        mk2, mv2, mkb, mvb = _mem_kv(mem_prompt, row(mem_norm_g), w_mem_kv[l].astype(BF16), tile(mem_k_g, N_MEM),
                                     ones_pair)
        shift_zero = jnp.zeros((bp, 1, RWKV_COLS), F32)
        s_zero = jnp.zeros((bp, N_RWKV, HEAD, HEAD), F32)
        yp, k, v, lf, s_new, sh_new = _layer(yp, shift_zero, s_zero, mkb, mvb, None, wts, ones_pair,
                                             tm=PROMPT_TM, tq=PROMPT_TQ, tc=PROMPT_TC, chunk=RWKV_CHUNK)
        mk = mk2.reshape(bp, N_MEM_TOK, N_MEM, HEAD)
        mv = mv2.reshape(bp, N_MEM_TOK, N_MEM, HEAD)
        for lst, val in zip(outs[:7], (k, v, lf, mk, mv, s_new, sh_new)):
            lst.append(val)
        bs, s_len = ys.shape[0], ys.shape[1]
        ys, k, v, lf, s_new, sh_new = _layer(
            ys, state_rwkv_shift[l], state_rwkv[l].astype(F32),
            cache_mem_k[l].reshape(bs, N_MEM_TOK, W_MEM).astype(F32),
            cache_mem_v[l].reshape(bs, N_MEM_TOK, W_MEM).astype(F32),
            (cache_fox_k[l], cache_fox_v[l], cache_fox_logf[l]), wts, ones_pair,
            tm=s_len, tq=None, tc=s_len, chunk=s_len, flat_out=True)
        for lst, val in zip(outs[7:], (k, v, lf, s_new, sh_new)):
            lst.append(val)
    return (yp, ys) + tuple(jnp.stack(o) for o in outs)
```

```python
import functools

import numpy as np
import jax
import jax.numpy as jnp
from jax import lax
from jax.experimental import pallas as pl
from jax.experimental.pallas import tpu as pltpu

F32 = jnp.float32
BF16 = jnp.bfloat16

D_MODEL = 1024
HEAD = 64
N_FOX = 6
N_RWKV = 6
N_MEM = 4
W_FOX = N_FOX * HEAD
W_RWKV = N_RWKV * HEAD
W_MEM = N_MEM * HEAD
N_MEM_TOK = 256
LORA = 32
NORM_EPS = 1e-6
GN_EPS = 64e-5
LOG2E = float(np.log2(np.e))
FOX_COLS = 4 * W_FOX + N_FOX
RWKV_COLS = 4 * W_RWKV + 2 * LORA
MEM_COLS = 2 * W_MEM

LANE = 128
PAIR = 2 * HEAD
SMALL = LANE
SM_WD = 32
SM_AD = 64
OFF_FQ, OFF_FK, OFF_FV, OFF_FG = 0, 384, 768, 1152
OFF_RR, OFF_RK, OFF_RV, OFF_RG = 1536, 1920, 2304, 2688
OFF_MQ, OFF_MG = 3072, 3328
OFF_SM = 3584
N_PAD = OFF_SM + SMALL
SHIFT_W = 4 * W_RWKV + SMALL
N_BIAS = 3
BIAS_STRIDE = 8
BF16_ROWS = 16
VROWS = HEAD + BF16_ROWS
RWKV_CHUNK = 64
VMEM_LIMIT = 56 * 1024 * 1024
PROMPT_TM = 512
PROMPT_TQ = 512
PROMPT_TC = 512
FOX_UNROLL = 4
CUMSUM_BLOCK = 256
STEP_SUFFIX_BLOCK = 256
STEP_SEQS_PER_TILE = 8


def _dot(a, b):
    return jnp.dot(a, b, preferred_element_type=F32)


def _dot_nt(a, b):
    return lax.dot_general(a, b, (((1,), (1,)), ((), ())), preferred_element_type=F32)


def _dot_tn(a, b):
    return lax.dot_general(a, b, (((0,), (0,)), ((), ())), preferred_element_type=F32)


def _split2(x):
    hi = x.astype(BF16)
    lo = (x - hi.astype(F32)).astype(BF16)
    return hi, lo


def _split3(x):
    hi = x.astype(BF16)
    r1 = x - hi.astype(F32)
    mid = r1.astype(BF16)
    lo = (r1 - mid.astype(F32)).astype(BF16)
    return hi, mid, lo


def _dot1(fn, a, b):
    return fn(a.astype(BF16), b.astype(BF16))


def _exact_lhs_dot(m_bf16, x, parts):
    pieces = _split3(x) if parts == 3 else _split2(x)
    acc = _dot(m_bf16, pieces[0])
    for p in pieces[1:]:
        acc = acc + _dot(m_bf16, p)
    return acc


def _segsum(x, ones_pair):
    xb = x.astype(BF16)
    return jnp.concatenate([_dot(xb[:, PAIR * p:PAIR * (p + 1)], ones_pair) for p in range(x.shape[1] // PAIR)],
                           axis=1)


def _head_rms(t, gain, ones_pair):
    msq = _segsum(t * t, ones_pair) * (1.0 / HEAD)
    return t * lax.rsqrt(msq + NORM_EPS) * gain


def _silu(x):
    return x * jax.nn.sigmoid(x)


def _softplus(z):
    return jnp.maximum(z, 0.0) + jnp.log1p(jnp.exp(-jnp.abs(z)))


def _stack_heads(x, n_heads):
    head_of_lane = lax.broadcasted_iota(jnp.int32, x.shape, 1) // HEAD
    zero = jnp.zeros_like(x)
    return jnp.concatenate([jnp.where(head_of_lane == h, x, zero) for h in range(n_heads)], axis=0)


def _unstack_heads(x6, n_heads):
    rows = x6.shape[0] // n_heads
    head_of_lane = lax.broadcasted_iota(jnp.int32, (rows, x6.shape[1]), 1) // HEAD
    out = jnp.zeros((rows, x6.shape[1]), x6.dtype)
    for h in range(n_heads):
        out = jnp.where(head_of_lane == h, x6[h * rows:(h + 1) * rows], out)
    return out


def _full(shape):
    n = len(shape)
    return pl.BlockSpec(shape, lambda *_: (0,) * n)


def _params(n_axes):
    return pltpu.CompilerParams(dimension_semantics=("arbitrary",) * n_axes, vmem_limit_bytes=VMEM_LIMIT)


def _mem_kv_body(mem_ref, g_ref, w_ref, kg_ref, ones_ref, k_ref, v_ref, kb_ref, vb_ref):
    x = mem_ref[0]
    ms = jnp.mean(x * x, axis=-1, keepdims=True)
    xn = (x * lax.rsqrt(ms + NORM_EPS) * g_ref[...]).astype(BF16)
    kv = _dot(xn, w_ref[...])
    k = _head_rms(kv[:, :W_MEM], kg_ref[...], ones_ref[...])
    v = kv[:, W_MEM:]
    k_ref[0] = k
    v_ref[0] = v
    kb_ref[0] = k.astype(BF16)
    vb_ref[0] = v.astype(BF16)


def _mem_kv(mem, g, w_bf16, kg4, ones_pair):
    b = mem.shape[0]
    blk = pl.BlockSpec((1, N_MEM_TOK, W_MEM), lambda i: (i, 0, 0))
    return pl.pallas_call(
        _mem_kv_body,
        grid=(b,),
        in_specs=[pl.BlockSpec((1, N_MEM_TOK, D_MODEL), lambda i: (i, 0, 0)),
                  _full((1, D_MODEL)), _full((D_MODEL, 2 * W_MEM)), _full((1, W_MEM)), _full((PAIR, PAIR))],
        out_specs=[blk] * 4,
        out_shape=[jax.ShapeDtypeStruct((b, N_MEM_TOK, W_MEM), F32)] * 2
        + [jax.ShapeDtypeStruct((b, N_MEM_TOK, W_MEM), BF16)] * 2,
        compiler_params=_params(1),
        name="mem_kv",
    )(mem, g, w_bf16, kg4, ones_pair)


def _proj_body(chunk, with_vt, seq,
               x_ref, shift0_ref, mk_ref, mv_ref, ng_ref, w_ref,
               fqg_ref, fkg_ref, bf_ref, mu_ref, w0_ref, a0_ref, wup_ref, aup_ref,
               kk_ref, ka_ref, rk_ref, mqg_ref, ones_ref, pbias_ref, bias1_ref,
               qs_ref, kb_ref, augq_ref, augk_ref, kn_ref, v_ref, logf_ref, gf_ref,
               rt_ref, at_ref, kh_ref, bh_ref, rv_ref, lc_ref, gr_ref, bg_ref, om_ref, shift_out_ref,
               *vt_and_scratch):
    carry_shift, carry_c = vt_and_scratch[-2:]
    i = pl.program_id(1)
    tm = x_ref.shape[1]
    sub = min(tm, CUMSUM_BLOCK)

    if seq is None:
        @pl.when(i == 0)
        def _():
            carry_shift[...] = shift0_ref[0]
            carry_c[...] = jnp.zeros_like(carry_c)

    x = x_ref[0]
    ms = jnp.mean(x * x, axis=-1, keepdims=True)
    xn = (x * lax.rsqrt(ms + NORM_EPS) * ng_ref[...]).astype(BF16)

    def proj(off, width):
        return _dot(xn, w_ref[:, off:off + width])

    ones_pair = ones_ref[...]
    row = lax.broadcasted_iota(jnp.int32, (sub, sub), 0)
    col = lax.broadcasted_iota(jnp.int32, (sub, sub), 1)
    tri = jnp.where((col <= row) & ((row ^ col) < (sub if seq is None else seq)), 1.0, 0.0).astype(BF16)
    tri_chunk = jnp.where((col <= row) & ((row ^ col) < chunk), 1.0, 0.0).astype(BF16)
    blocks = [slice(s * sub, (s + 1) * sub) for s in range(tm // sub)]

    hs = proj(OFF_SM, SMALL)
    hq = proj(OFF_FQ, W_FOX)
    hk = proj(OFF_FK, W_FOX)
    hv = proj(OFF_FV, W_FOX)
    hg = proj(OFF_FG, W_FOX)
    hmq = proj(OFF_MQ, W_MEM)
    hmg = proj(OFF_MG, W_MEM)
    hrr = proj(OFF_RR, W_RWKV)
    hrk = proj(OFF_RK, W_RWKV)
    hrv = proj(OFF_RV, W_RWKV)
    hrg = proj(OFF_RG, W_RWKV)
    qn = _head_rms(hq, fqg_ref[...], ones_pair)
    kn = _head_rms(hk, fkg_ref[...], ones_pair)
    qm = _head_rms(hmq, mqg_ref[...], ones_pair)

    qms = (qm * (HEAD ** -0.5 * LOG2E)).astype(BF16)
    mem_rows = [(qms, 0)] if seq is None else [(qms[g * seq:(g + 1) * seq], g) for g in range(tm // seq)]

    def mem_probs():
        probs = []
        for q_rows, g in mem_rows:
            s4 = _dot_nt(_stack_heads(q_rows, N_MEM), mk_ref[g].astype(BF16))
            e = jnp.exp2(s4 - jnp.max(s4, axis=1, keepdims=True))
            probs.append((e / jnp.sum(e, axis=1, keepdims=True)).astype(BF16))
        return probs

    def mem_values(probs):
        return jnp.concatenate([_unstack_heads(_dot(p4, mv_ref[g].astype(BF16)), N_MEM)
                                for p4, (_, g) in zip(probs, mem_rows)], axis=0)

    if seq is not None:
        mem_p = mem_probs()
    kn_ref[0] = kn
    kb_ref[0] = kn.astype(BF16)
    qs_ref[0] = (qn * (HEAD ** -0.5 * LOG2E)).astype(BF16)
    v_ref[0] = hv
    if with_vt:
        hvt = hv.T
        ones_row = jnp.where(lax.broadcasted_iota(jnp.int32, (VROWS - HEAD, tm), 0) == 0, 1.0, 0.0)
        vt_and_scratch[0][0, 0] = jnp.concatenate(
            [blk for h in range(N_FOX) for blk in (hvt[HEAD * h:HEAD * (h + 1)], ones_row)], axis=0).astype(BF16)
    gf_ref[0] = _silu(hg).astype(BF16)
    f = hs + bf_ref[...]
    lane = lax.broadcasted_iota(jnp.int32, (tm, SMALL), 1)
    logf = jnp.where(lane < N_FOX, jnp.minimum(f, 0.0) - jnp.log1p(jnp.exp(-jnp.abs(f))), 0.0)
    logf_ref[0] = logf[:, 0:N_FOX]
    if seq is None:
        carry = carry_c[...]
        c_blocks = []
        for blk in blocks:
            cb = _exact_lhs_dot(tri, logf[blk], 3) + carry
            carry = cb[sub - 1:sub, :]
            c_blocks.append(cb)
        carry_c[...] = carry
    else:
        c_blocks = [_exact_lhs_dot(tri, logf[blk], 3) for blk in blocks]
    c = jnp.concatenate(c_blocks, axis=0)
    pieces = _split3(c * LOG2E)
    packed = pieces[0].astype(F32)
    for j in range(1, N_BIAS):
        packed = packed + pltpu.roll(pieces[j].astype(F32), BIAS_STRIDE * j, 1)
    aug = _dot(packed.astype(BF16), pbias_ref[...]) + bias1_ref[...]
    augq_ref[0] = aug[:, :W_FOX].astype(BF16)
    augk_ref[0] = aug[:, W_FOX:].astype(BF16)

    if seq is None:
        om_ref[0] = (mem_values(mem_probs()) * _silu(hmg)).astype(BF16)

    row_in_seq = lax.broadcasted_iota(jnp.int32, (tm, 1), 0) % (tm if seq is None else seq)

    def tshift(cols, off, width):
        if seq is None:
            before_first = carry_shift[:, off:off + width]
            carry_shift[:, off:off + width] = cols[tm - 1:tm, :]
        else:
            before_first = shift0_ref[0, :, off:off + width]
            shift_out_ref[0, :, off:off + width] = cols
        prev = jnp.where(row_in_seq == 0, before_first, pltpu.roll(cols, 1, 0))
        return cols + (prev - cols) * mu_ref[:, off:off + width]

    r = tshift(hrr, 0, W_RWKV)
    k = tshift(hrk, W_RWKV, W_RWKV)
    v = tshift(hrv, 2 * W_RWKV, W_RWKV)
    g = tshift(hrg, 3 * W_RWKV, W_RWKV)
    sm = tshift(hs, 4 * W_RWKV, SMALL)
    if seq is None:
        shift_out_ref[0] = carry_shift[...]

    w_lin = w0_ref[...] + _dot1(_dot, jnp.tanh(sm), wup_ref[...])
    a = jax.nn.sigmoid(a0_ref[...] + _dot1(_dot, sm, aup_ref[...]))
    lw = -jnp.exp(-_softplus(-w_lin) - 0.5)
    kk = k * kk_ref[...]
    kk = kk * lax.rsqrt(jnp.maximum(_segsum(kk * kk, ones_pair), 1e-24))
    kt = k * (1.0 + (a - 1.0) * ka_ref[...])
    gr = _silu(g)
    gr_ref[0] = gr.astype(BF16)
    bg_ref[0] = (_segsum(r * kt * rk_ref[...], ones_pair) * v * gr).astype(BF16)
    lc = jnp.concatenate([_exact_lhs_dot(tri_chunk, lw[blk], 2) for blk in blocks], axis=0)
    e_neg = jnp.exp(-lc)
    rt_ref[0] = (r * jnp.exp(lc)).astype(BF16)
    at_ref[0] = (-kk * jnp.exp(lc - lw)).astype(BF16)
    kh_ref[0] = (kt * e_neg).astype(BF16)
    bh_ref[0] = (kk * a * e_neg).astype(BF16)
    rv_ref[0] = v.astype(BF16)
    lc_ref[0] = lc
    if seq is not None:
        om_ref[0] = (mem_values(mem_p) * _silu(hmg)).astype(BF16)


def _proj(x, shift0, mk, mv, wts, tm, chunk, with_vt, seq=None):
    b, t, _ = x.shape
    grid = (b, t // tm)
    tok = lambda w: pl.BlockSpec((1, tm, w), lambda bi, i: (bi, i, 0))
    per_b = lambda s1, s2: pl.BlockSpec((1, s1, s2), lambda bi, i: (bi, 0, 0))
    w_specs = [_full(a.shape) for a in wts]
    bf = lambda w: jax.ShapeDtypeStruct((b, t, w), BF16)
    f32 = lambda w: jax.ShapeDtypeStruct((b, t, w), F32)
    out_shape = [bf(W_FOX)] * 4 + [f32(W_FOX), f32(W_FOX), f32(N_FOX), bf(W_FOX)] \
        + [bf(W_RWKV)] * 5 + [f32(W_RWKV)] + [bf(W_RWKV), bf(W_RWKV), bf(W_MEM),
                                               jax.ShapeDtypeStruct(shift0.shape, F32)]
    shift_spec = per_b(1, SHIFT_W) if seq is None else tok(SHIFT_W)
    mem_spec = pl.BlockSpec((mk.shape[0] // b, N_MEM_TOK, W_MEM), lambda bi, i: (bi, 0, 0))
    out_specs = [tok(W_FOX)] * 4 + [tok(W_FOX), tok(W_FOX), tok(N_FOX), tok(W_FOX)] \
        + [tok(W_RWKV)] * 6 + [tok(W_RWKV), tok(W_RWKV), tok(W_MEM), shift_spec]
    if with_vt:
        out_shape.append(jax.ShapeDtypeStruct((b, t // tm, N_FOX * VROWS, tm), BF16))
        out_specs.append(pl.BlockSpec((1, 1, N_FOX * VROWS, tm), lambda bi, i: (bi, i, 0, 0)))
    return pl.pallas_call(
        functools.partial(_proj_body, chunk, with_vt, seq),
        grid=grid,
        in_specs=[tok(D_MODEL), shift_spec, mem_spec, mem_spec] + w_specs,
        out_specs=out_specs,
        out_shape=out_shape,
        scratch_shapes=[pltpu.VMEM((1, SHIFT_W), F32), pltpu.VMEM((1, SMALL), F32)],
        compiler_params=_params(2),
        name="proj",
    )(x, shift0, mk, mv, *wts)


def _fox_body(qs_ref, aq_ref, kb_ref, ak_ref, vt_ref, o_ref, m_scr, acc_scr, sa_scr, sb_scr, qm_scr):
    qi = pl.program_id(2)
    tq = qs_ref.shape[1]
    slab = vt_ref.shape[3]
    slabs_per_tile = tq // slab
    m_scr[...] = jnp.full_like(m_scr, -1e30)
    acc_scr[...] = jnp.zeros_like(acc_scr)
    q = qs_ref[0]
    aq = aq_ref[0]
    lane_lo = lax.broadcasted_iota(jnp.int32, (tq, PAIR), 1) < HEAD
    zero = jnp.zeros_like(q)
    qm_scr[0] = jnp.concatenate([jnp.where(lane_lo, q, zero), jnp.where(lane_lo, aq, zero)], axis=1)
    qm_scr[1] = jnp.concatenate([jnp.where(lane_lo, zero, q), jnp.where(lane_lo, zero, aq)], axis=1)

    def scores(ki, s_scr):
        rows = pl.ds(pl.multiple_of(ki * tq, tq), tq)
        k2 = jnp.concatenate([kb_ref[0, rows, :], ak_ref[0, rows, :]], axis=1)
        for hh in range(2):
            s_scr[hh] = _dot_nt(k2, qm_scr[hh])

    def consume(ki, s_scr, masked):
        s = [s_scr[hh] for hh in range(2)]
        if masked:
            kpos = lax.broadcasted_iota(jnp.int32, (tq, tq), 0)
            qpos = lax.broadcasted_iota(jnp.int32, (tq, tq), 1)
            s = [jnp.where(kpos <= qpos, sh, -1e30) for sh in s]
        m_prev = [m_scr[hh] for hh in range(2)]
        m_new = [jnp.maximum(m_prev[hh], jnp.max(s[hh], axis=0, keepdims=True)) for hh in range(2)]
        p = [jnp.exp2(s[hh] - m_new[hh]).astype(BF16) for hh in range(2)]
        pv = []
        for hh in range(2):
            vt = jnp.concatenate([vt_ref[0, ki * slabs_per_tile + d, VROWS * hh:VROWS * (hh + 1), :]
                                  for d in range(slabs_per_tile)], axis=1)
            pv.append(_dot(vt, p[hh]))
        for hh in range(2):
            acc_scr[hh] = jnp.exp2(m_prev[hh] - m_new[hh]) * acc_scr[hh] + pv[hh]
            m_scr[hh] = m_new[hh]

    bufs = (sa_scr, sb_scr)
    scores(0, bufs[0])

    def tile_group(j, carry):
        for d in range(FOX_UNROLL):
            scores(FOX_UNROLL * j + d + 1, bufs[(d + 1) % 2])
            consume(FOX_UNROLL * j + d, bufs[d % 2], False)
        return carry

    lax.fori_loop(0, qi // FOX_UNROLL, tile_group, 0)
    base = (qi // FOX_UNROLL) * FOX_UNROLL
    for rem in range(FOX_UNROLL):
        @pl.when(qi - base == rem)
        def _():
            for d in range(rem):
                scores(base + d + 1, bufs[(d + 1) % 2])
                consume(base + d, bufs[d % 2], False)
            consume(qi, bufs[rem % 2], True)

    o_t = jnp.concatenate([acc_scr[hh][0:HEAD] / acc_scr[hh][HEAD:HEAD + 1] for hh in range(2)], axis=0)
    o_ref[0] = o_t.T.astype(BF16)


def _fox(qs, augq, kb, augk, vt, tq):
    b, t, _ = qs.shape
    n_slab, slab = vt.shape[1], vt.shape[3]
    q_blk = pl.BlockSpec((1, tq, PAIR), lambda bi, p, qi: (bi, qi, p))
    k_blk = pl.BlockSpec((1, t, PAIR), lambda bi, p, qi: (bi, 0, p))
    return pl.pallas_call(
        _fox_body,
        grid=(b, N_FOX // 2, t // tq),
        in_specs=[q_blk, q_blk, k_blk, k_blk,
                  pl.BlockSpec((1, n_slab, 2 * VROWS, slab), lambda bi, p, qi: (bi, 0, p, 0))],
        out_specs=q_blk,
        out_shape=jax.ShapeDtypeStruct((b, t, W_FOX), BF16),
        scratch_shapes=[pltpu.VMEM((2, 1, tq), F32), pltpu.VMEM((2, VROWS, tq), F32),
                        pltpu.VMEM((2, tq, tq), F32), pltpu.VMEM((2, tq, tq), F32),
                        pltpu.VMEM((2, tq, 2 * PAIR), BF16)],
        compiler_params=_params(3),
        name="fox",
    )(qs, augq, kb, augk, vt)


def _fox_step_body(qs_ref, aq_ref, kn_ref, ak_ref, vn_ref, kp_ref, vp_ref, lfp_ref, o_ref):
    s_new = qs_ref.shape[1]
    p_len = kp_ref.shape[1]
    blk = min(p_len, STEP_SUFFIX_BLOCK)
    row = lax.broadcasted_iota(jnp.int32, (blk, blk), 0)
    col = lax.broadcasted_iota(jnp.int32, (blk, blk), 1)
    upper = jnp.where(col > row, 1.0, 0.0).astype(BF16)
    later = jnp.zeros((1, SMALL), F32)
    sufs = []
    for bi in reversed(range(p_len // blk)):
        lf = jnp.concatenate([lfp_ref[0, bi * blk:(bi + 1) * blk, :], jnp.zeros((blk, SMALL - N_FOX), F32)], axis=1)
        within = _exact_lhs_dot(upper, lf, 3)
        sufs.insert(0, within + later)
        later = later + within[0:1, :] + lf[0:1, :]
    suf_t = (jnp.concatenate(sufs, axis=0) * LOG2E).T
    q6 = _stack_heads(qs_ref[0], N_FOX)
    a6 = _stack_heads(aq_ref[0], N_FOX)
    lane_in_head = lax.broadcasted_iota(jnp.int32, a6.shape, 1) % HEAD
    cq6 = jnp.sum(jnp.where(lane_in_head < N_BIAS, a6.astype(F32), 0.0), axis=1, keepdims=True)
    bias6 = jnp.concatenate([jnp.broadcast_to(suf_t[h:h + 1, :], (s_new, p_len)) for h in range(N_FOX)], axis=0)
    sp = _dot_nt(q6, kp_ref[0].astype(BF16)) + cq6 + bias6
    qrow = lax.broadcasted_iota(jnp.int32, (N_FOX * s_new, s_new), 0) % s_new
    kcol = lax.broadcasted_iota(jnp.int32, (N_FOX * s_new, s_new), 1)
    sn = jnp.where(kcol <= qrow, _dot_nt(q6, kn_ref[0]) + _dot_nt(a6, ak_ref[0]), -1e30)
    m = jnp.maximum(jnp.max(sp, axis=1, keepdims=True), jnp.max(sn, axis=1, keepdims=True))
    pp = jnp.exp2(sp - m)
    pn = jnp.exp2(sn - m)
    l = jnp.sum(pp, axis=1, keepdims=True) + jnp.sum(pn, axis=1, keepdims=True)
    o6 = (_dot((pp / l).astype(BF16), vp_ref[0].astype(BF16))
          + _dot((pn / l).astype(BF16), vn_ref[0].astype(BF16)))
    o_ref[0] = _unstack_heads(o6, N_FOX).astype(BF16)


def _fox_step(qs, augq, kb, augk, v_new, k_past, v_past, logf_past):
    b, s_new, _ = qs.shape
    p_len = k_past.shape[1]
    tok = lambda rows, w: pl.BlockSpec((1, rows, w), lambda i: (i, 0, 0))
    return pl.pallas_call(
        _fox_step_body,
        grid=(b,),
        in_specs=[tok(s_new, W_FOX)] * 5 + [tok(p_len, W_FOX), tok(p_len, W_FOX), tok(p_len, N_FOX)],
        out_specs=tok(s_new, W_FOX),
        out_shape=jax.ShapeDtypeStruct((b, s_new, W_FOX), BF16),
        compiler_params=_params(1),
        name="fox_step",
    )(qs, augq, kb, augk, v_new, k_past, v_past, logf_past)


def _rwkv_body(chunk, independent, rt_ref, at_ref, kh_ref, bh_ref, v_ref, lc_ref, s0_ref, y_ref, sout_ref, s_scr):
    j = pl.program_id(1)
    nj = pl.num_programs(1)
    c = chunk
    c2 = 2 * chunk
    n_chunks = rt_ref.shape[1] // c
    n_pairs = N_RWKV // 2

    def pair_state(seq_i, p):
        z = jnp.zeros((HEAD, HEAD), F32)
        return jnp.concatenate([jnp.concatenate([s0_ref[seq_i, 2 * p], z], axis=1),
                                jnp.concatenate([z, s0_ref[seq_i, 2 * p + 1]], axis=1)], axis=0)

    if not independent:
        @pl.when(j == 0)
        def _():
            for p in range(n_pairs):
                s_scr[p] = pair_state(0, p)

    lane_lo = lax.broadcasted_iota(jnp.int32, (c, PAIR), 1) < HEAD
    row = lax.broadcasted_iota(jnp.int32, (c2, c2), 0)
    col = lax.broadcasted_iota(jnp.int32, (c2, c2), 1)
    same = (row >= c) == (col >= c)
    strict = same & (col < row)
    lower = same & (col <= row)
    eye = jnp.where(col == row, 1.0, 0.0)
    srow = lax.broadcasted_iota(jnp.int32, (PAIR, PAIR), 0)
    scol = lax.broadcasted_iota(jnp.int32, (PAIR, PAIR), 1)
    same_state = (srow >= HEAD) == (scol >= HEAD)

    def stack(x):
        zero = jnp.zeros_like(x)
        return jnp.concatenate([jnp.where(lane_lo, x, zero), jnp.where(lane_lo, zero, x)], axis=0)

    def fold(a):
        return a[:c] + a[c:]

    units = [(ci, p) for ci in range(n_chunks) for p in range(n_pairs)]

    def tile(ref, u):
        ci, p = u
        return ref[0, ci * c:(ci + 1) * c, PAIR * p:PAIR * (p + 1)]

    rt = [tile(rt_ref, u) for u in units]
    at2 = [stack(tile(at_ref, u)) for u in units]
    kh = [tile(kh_ref, u) for u in units]
    bh = [tile(bh_ref, u) for u in units]
    v2 = [stack(tile(v_ref, u)) for u in units]
    g = [_dot_nt(jnp.concatenate([at2[i], stack(rt[i])], axis=0),
                 jnp.concatenate([kh[i], kh[i], bh[i], bh[i]], axis=0)) for i in range(len(units))]
    a_ak = [jnp.where(strict, x[:c2, :c2], 0.0) for x in g]
    a_ab = [jnp.where(strict, x[:c2, c2:], 0.0) for x in g]
    a_rk = [fold(jnp.where(lower, x[c2:, :c2], 0.0)) for x in g]
    a_rb = [fold(jnp.where(lower, x[c2:, c2:], 0.0)) for x in g]
    akv2 = [_dot1(_dot, a_ak[i], v2[i]) for i in range(len(units))]
    yv = [_dot1(_dot, a_rk[i], v2[i]) for i in range(len(units))]
    inv = [eye + x for x in a_ab]
    pw = a_ab
    n = 1
    while 2 * n < c:
        pw = [_dot1(_dot, x, x) for x in pw]
        inv = [inv[i] + _dot1(_dot, inv[i], pw[i]) for i in range(len(units))]
        n *= 2
    tw = [_dot(inv[i].astype(BF16), jnp.concatenate([at2[i], akv2[i].astype(BF16)], axis=1))
          for i in range(len(units))]
    w_mat = [fold(x[:, :PAIR]) for x in tw]
    u0 = [fold(x[:, PAIR:]) for x in tw]
    pc = [jnp.exp(lc_ref[0, (ci + 1) * c - 1:(ci + 1) * c, PAIR * p:PAIR * (p + 1)]) for ci, p in units]
    kc = [(kh[i] * pc[i]).astype(BF16) for i in range(len(units))]
    bc = [(bh[i] * pc[i]).astype(BF16) for i in range(len(units))]
    s_mix = [jnp.where(same_state, _dot_tn(w_mat[i].astype(BF16), bc[i]), 0.0).astype(BF16)
             for i in range(len(units))]
    s_add = [jnp.where(same_state, _dot_tn(jnp.concatenate([tile(v_ref, units[i]), u0[i].astype(BF16)], axis=0),
                                           jnp.concatenate([kc[i], bc[i]], axis=0)), 0.0)
             for i in range(len(units))]
    rb = [_dot1(_dot, a_rb[i], jnp.concatenate([stack(w_mat[i]), stack(u0[i])], axis=1)) for i in range(len(units))]
    r_eff = [(rt[i].astype(F32) + rb[i][:, :PAIR]).astype(BF16) for i in range(len(units))]
    y_add = [yv[i] + rb[i][:, PAIR:] for i in range(len(units))]

    for ci in range(n_chunks):
        idx = [ci * n_pairs + p for p in range(n_pairs)]
        s_old = [pair_state(ci, p) if independent else s_scr[p] for p in range(n_pairs)]
        s_bf = [s.astype(BF16) for s in s_old]
        upd = [_dot(s_bf[p], s_mix[i]) for p, i in enumerate(idx)]
        y = [_dot_nt(r_eff[i], s_bf[p]) + y_add[i] for p, i in enumerate(idx)]
        for p, i in enumerate(idx):
            s_new = s_old[p] * pc[i] + (upd[p] + s_add[i])
            y_ref[0, ci * c:(ci + 1) * c, PAIR * p:PAIR * (p + 1)] = y[p]
            if independent:
                sout_ref[ci, 2 * p] = s_new[:HEAD, :HEAD]
                sout_ref[ci, 2 * p + 1] = s_new[HEAD:, HEAD:]
            else:
                s_scr[p] = s_new

    if not independent:
        @pl.when(j == nj - 1)
        def _():
            for p in range(n_pairs):
                s = s_scr[p]
                sout_ref[0, 2 * p] = s[:HEAD, :HEAD]
                sout_ref[0, 2 * p + 1] = s[HEAD:, HEAD:]


def _rwkv(rt, at, kh, bh, v, lc, s0, tc, chunk, independent=False):
    b, t, _ = rt.shape
    tok = pl.BlockSpec((1, tc, W_RWKV), lambda bi, j: (bi, j, 0))
    if independent:
        st = pl.BlockSpec((tc // chunk, N_RWKV, HEAD, HEAD), lambda bi, j: (j, 0, 0, 0))
    else:
        st = pl.BlockSpec((1, N_RWKV, HEAD, HEAD), lambda bi, j: (bi, 0, 0, 0))
    return pl.pallas_call(
        functools.partial(_rwkv_body, chunk, independent),
        grid=(b, t // tc),
        in_specs=[tok] * 6 + [st],
        out_specs=[tok, st],
        out_shape=[jax.ShapeDtypeStruct((b, t, W_RWKV), F32), jax.ShapeDtypeStruct(s0.shape, F32)],
        scratch_shapes=[pltpu.VMEM((N_RWKV // 2, PAIR, PAIR), F32)],
        compiler_params=_params(2),
        name="rwkv",
    )(rt, at, kh, bh, v, lc, s0)


def _out_body(x_ref, of_ref, gf_ref, y_ref, gr_ref, bg_ref, om_ref, gnw_ref, gnb_ref, ones_ref,
              wf_ref, wr_ref, wm_ref, o_ref):
    ones_pair = ones_ref[...]
    acc = _dot((of_ref[0].astype(F32) * gf_ref[0].astype(F32)).astype(BF16), wf_ref[...])
    acc = acc + _dot(om_ref[0], wm_ref[...])
    y = y_ref[0]
    mu = _segsum(y, ones_pair) * (1.0 / HEAD)
    d = y - mu
    var = _segsum(d * d, ones_pair) * (1.0 / HEAD)
    yn = d * lax.rsqrt(var + GN_EPS) * gnw_ref[...] + gnb_ref[...]
    o_r = yn * gr_ref[0].astype(F32) + bg_ref[0].astype(F32)
    acc = acc + _dot(o_r.astype(BF16), wr_ref[...])
    o_ref[0] = x_ref[0] + acc


def _out(x, of, gf, y, gr, bg, om, gnw, gnb, ones_pair, wf, wr, wm, tm):
    b, t, _ = x.shape
    tok = lambda w: pl.BlockSpec((1, tm, w), lambda bi, i: (bi, i, 0))
    return pl.pallas_call(
        _out_body,
        grid=(b, t // tm),
        in_specs=[tok(D_MODEL), tok(W_FOX), tok(W_FOX), tok(W_RWKV), tok(W_RWKV), tok(W_RWKV), tok(W_MEM),
                  _full(gnw.shape), _full(gnb.shape), _full(ones_pair.shape),
                  _full(wf.shape), _full(wr.shape), _full(wm.shape)],
        out_specs=tok(D_MODEL),
        out_shape=jax.ShapeDtypeStruct((b, t, D_MODEL), F32),
        compiler_params=_params(2),
        name="out_proj",
    )(x, of, gf, y, gr, bg, om, gnw, gnb, ones_pair, wf, wr, wm)


def _pad_in_columns(w):
    r0 = FOX_COLS
    m0 = FOX_COLS + RWKV_COLS
    zeros = lambda n: jnp.zeros(w.shape[:-1] + (n,), w.dtype)
    return jnp.concatenate([
        w[..., 0:3 * W_FOX], w[..., 3 * W_FOX + N_FOX:FOX_COLS],
        w[..., r0:r0 + 3 * W_RWKV], w[..., r0 + 3 * W_RWKV + 2 * LORA:r0 + RWKV_COLS],
        w[..., m0:m0 + MEM_COLS],
        w[..., 3 * W_FOX:3 * W_FOX + N_FOX], zeros(SM_WD - N_FOX),
        w[..., r0 + 3 * W_RWKV:r0 + 3 * W_RWKV + 2 * LORA], zeros(SMALL - SM_AD - LORA),
    ], axis=-1)


def _pad_w_body(w_ref, o_ref):
    o_ref[...] = _pad_in_columns(w_ref[0].astype(F32)).astype(BF16)


def _pad_w(w_all, layer):
    _, d_in, n_in = w_all.shape
    rows = d_in // 4
    return pl.pallas_call(
        _pad_w_body,
        grid=(d_in // rows,),
        in_specs=[pl.BlockSpec((1, rows, n_in), lambda i: (layer, i, 0))],
        out_specs=pl.BlockSpec((rows, N_PAD), lambda i: (i, 0)),
        out_shape=jax.ShapeDtypeStruct((d_in, N_PAD), BF16),
        compiler_params=_params(1),
        name="pad_w",
    )(w_all)


def _pad_shift_row(a):
    zeros = lambda n: jnp.zeros(a.shape[:-1] + (n,), a.dtype)
    return jnp.concatenate([
        a[..., 0:3 * W_RWKV], a[..., 3 * W_RWKV + 2 * LORA:RWKV_COLS],
        zeros(SM_WD), a[..., 3 * W_RWKV:3 * W_RWKV + 2 * LORA], zeros(SMALL - SM_AD - LORA)], axis=-1)


def _unpad_shift_row(a):
    small = 4 * W_RWKV
    return jnp.concatenate([a[..., 0:3 * W_RWKV], a[..., small + SM_WD:small + SM_AD + LORA],
                            a[..., 3 * W_RWKV:4 * W_RWKV]], axis=-1)


def _bias_placement():
    place = np.zeros((SMALL, 2 * W_FOX), np.float32)
    const = np.zeros((1, 2 * W_FOX), np.float32)
    for h in range(N_FOX):
        for j in range(N_BIAS):
            place[BIAS_STRIDE * j + h, HEAD * h + j] = 1.0
            place[BIAS_STRIDE * j + h, W_FOX + HEAD * h + N_BIAS + j] = -1.0
            const[0, HEAD * h + N_BIAS + j] = 1.0
            const[0, W_FOX + HEAD * h + j] = 1.0
    return place, const


def _block_ones(width):
    h = np.arange(width) // HEAD
    return (h[:, None] == h[None, :]).astype(np.float32)


def _layer(x, shift_prev, s0, mk, mv, fox_past, wts, ones_pair, tm, tq, tc, chunk, flat_out=False):
    (proj_wts, gnw, gnb, wf, wr, wm) = wts
    b, t, _ = x.shape
    if flat_out:
        first = jnp.pad(_pad_shift_row(shift_prev), ((0, 0), (0, t - 1), (0, 0))).reshape(1, b * t, SHIFT_W)
        outs = _proj(x.reshape(1, b * t, D_MODEL), first, mk, mv, proj_wts, b * t, chunk, with_vt=False, seq=t)
        outs = [o.reshape(b, t, o.shape[-1]) for o in outs]
        outs[-1] = outs[-1][:, t - 1:t, :]
    else:
        outs = _proj(x, _pad_shift_row(shift_prev), mk, mv, proj_wts, tm, chunk, with_vt=fox_past is None)
    (qs, kb, augq, augk, kn, v, logf, gf, rt, at, kh, bh, rv, lc, gr, bg, om, shift_out, *vt) = outs
    if fox_past is None:
        of = _fox(qs, augq, kb, augk, vt[0], tq)
    else:
        k_past, v_past, logf_past = fox_past
        lfp = logf_past.astype(F32)
        p_len = k_past.shape[1]
        of = _fox_step(qs, augq, kb, augk, v, k_past.reshape(b, p_len, W_FOX).astype(F32),
                       v_past.reshape(b, p_len, W_FOX).astype(F32), lfp)
    rows = lambda a: a.reshape(1, b * t, a.shape[-1]) if flat_out else a
    if flat_out:
        y_r, s_new = _rwkv(rows(rt), rows(at), rows(kh), rows(bh), rows(rv), rows(lc), s0,
                           min(b, STEP_SEQS_PER_TILE) * t, chunk, independent=True)
        y_r = y_r.reshape(b, t, W_RWKV)
    else:
        y_r, s_new = _rwkv(rt, at, kh, bh, rv, lc, s0, tc, chunk)
    y = _out(rows(x), rows(of), rows(gf), rows(y_r), rows(gr), rows(bg), rows(om), gnw, gnb, ones_pair, wf, wr, wm,
             b * t if flat_out else tm).reshape(b, t, D_MODEL)
    return (y, kn.reshape(b, t, N_FOX, HEAD), v.reshape(b, t, N_FOX, HEAD), logf, s_new, _unpad_shift_row(shift_out))


def kernel(x_prompt, x_sample, mem_prompt, cache_fox_k, cache_fox_v, cache_fox_logf, cache_mem_k, cache_mem_v, state_rwkv, state_rwkv_shift, norm_g, w_in, fox_q_g, fox_k_g, fox_b_f, rwkv_mu, rwkv_w0, rwkv_w_up, rwkv_a0, rwkv_a_up, rwkv_k_k, rwkv_k_a, rwkv_r_k, rwkv_gn_w, rwkv_gn_b, mem_norm_g, w_mem_kv, mem_q_g, mem_k_g, w_out):
    depth = w_in.shape[0]
    bp, tp, _ = x_prompt.shape
    bs, s_len, _ = x_sample.shape
    assert x_prompt.shape[2] == D_MODEL and w_in.shape[1:] == (D_MODEL, FOX_COLS + RWKV_COLS + MEM_COLS)
    assert mem_prompt.shape[1:] == (N_MEM_TOK, D_MODEL) and cache_mem_k.shape[2:] == (N_MEM_TOK, N_MEM, HEAD)
    assert tp % PROMPT_TM == 0 and tp % PROMPT_TQ == 0 and tp % PROMPT_TC == 0 and PROMPT_TQ % PROMPT_TM == 0
    assert PROMPT_TM % CUMSUM_BLOCK == 0 and CUMSUM_BLOCK % RWKV_CHUNK == 0 and PROMPT_TC % RWKV_CHUNK == 0
    assert s_len & (s_len - 1) == 0 and BF16_ROWS <= s_len <= RWKV_CHUNK
    assert (bs * s_len) % min(bs * s_len, CUMSUM_BLOCK) == 0 and bs % min(bs, STEP_SEQS_PER_TILE) == 0
    assert cache_fox_k.shape[2] % min(cache_fox_k.shape[2], STEP_SUFFIX_BLOCK) == 0
    place_np, const_np = _bias_placement()
    ones_pair = jnp.asarray(_block_ones(PAIR), BF16)
    pbias = jnp.asarray(place_np, BF16)
    bias1 = jnp.asarray(const_np, F32)

    yp, ys = x_prompt, x_sample
    outs = [[] for _ in range(12)]
    for l in range(depth):
        row = lambda a: a[l].reshape(1, -1).astype(F32)
        tile = lambda a, n: jnp.tile(a[l].reshape(1, -1).astype(F32), (1, n))
        w_pad = _pad_w(w_in, l)
        bf_pad = jnp.pad(row(fox_b_f), ((0, 0), (0, SMALL - N_FOX)))
        wup_pad = jnp.pad(rwkv_w_up[l].astype(F32), ((SM_WD, SMALL - SM_WD - LORA), (0, 0)))
        aup_pad = jnp.pad(rwkv_a_up[l].astype(F32), ((SM_AD, SMALL - SM_AD - LORA), (0, 0)))
        proj_wts = (row(norm_g), w_pad, tile(fox_q_g, N_FOX), tile(fox_k_g, N_FOX), bf_pad,
                    _pad_shift_row(row(rwkv_mu)),
                    row(rwkv_w0), row(rwkv_a0), wup_pad, aup_pad, row(rwkv_k_k), row(rwkv_k_a), row(rwkv_r_k),
                    tile(mem_q_g, N_MEM), ones_pair, pbias, bias1)
        wo = w_out[l].astype(BF16)
        wts = (proj_wts, row(rwkv_gn_w), row(rwkv_gn_b), wo[:W_FOX], wo[W_FOX:W_FOX + W_RWKV], wo[W_FOX + W_RWKV:])

        mk2, mv2, mkb, mvb = _mem_kv(mem_prompt, row(mem_norm_g), w_mem_kv[l].astype(BF16), tile(mem_k_g, N_MEM),
                                     ones_pair)
        shift_zero = jnp.zeros((bp, 1, RWKV_COLS), F32)
        s_zero = jnp.zeros((bp, N_RWKV, HEAD, HEAD), F32)
        yp, k, v, lf, s_new, sh_new = _layer(yp, shift_zero, s_zero, mkb, mvb, None, wts, ones_pair,
                                             tm=PROMPT_TM, tq=PROMPT_TQ, tc=PROMPT_TC, chunk=RWKV_CHUNK)
        mk = mk2.reshape(bp, N_MEM_TOK, N_MEM, HEAD)
        mv = mv2.reshape(bp, N_MEM_TOK, N_MEM, HEAD)
        for lst, val in zip(outs[:7], (k, v, lf, mk, mv, s_new, sh_new)):
            lst.append(val)
        bs, s_len = ys.shape[0], ys.shape[1]
        ys, k, v, lf, s_new, sh_new = _layer(
            ys, state_rwkv_shift[l], state_rwkv[l].astype(F32),
            cache_mem_k[l].reshape(bs, N_MEM_TOK, W_MEM).astype(F32),
            cache_mem_v[l].reshape(bs, N_MEM_TOK, W_MEM).astype(F32),
            (cache_fox_k[l], cache_fox_v[l], cache_fox_logf[l]), wts, ones_pair,
            tm=s_len, tq=None, tc=s_len, chunk=s_len, flat_out=True)
        for lst, val in zip(outs[7:], (k, v, lf, s_new, sh_new)):
            lst.append(val)
    return (yp, ys) + tuple(jnp.stack(o) for o in outs)
```

```python
import functools

import numpy as np
import jax
import jax.numpy as jnp
from jax import lax
from jax.experimental import pallas as pl
from jax.experimental.pallas import tpu as pltpu

F32 = jnp.float32
BF16 = jnp.bfloat16

D_MODEL = 1024
HEAD = 64
N_FOX = 6
N_RWKV = 6
N_MEM = 4
W_FOX = N_FOX * HEAD
W_RWKV = N_RWKV * HEAD
W_MEM = N_MEM * HEAD
N_MEM_TOK = 256
LORA = 32
NORM_EPS = 1e-6
GN_EPS = 64e-5
LOG2E = float(np.log2(np.e))
FOX_COLS = 4 * W_FOX + N_FOX
RWKV_COLS = 4 * W_RWKV + 2 * LORA
MEM_COLS = 2 * W_MEM

LANE = 128
PAIR = 2 * HEAD
SMALL = LANE
SM_WD = 32
SM_AD = 64
OFF_FQ, OFF_FK, OFF_FV, OFF_FG = 0, 384, 768, 1152
OFF_RR, OFF_RK, OFF_RV, OFF_RG = 1536, 1920, 2304, 2688
OFF_MQ, OFF_MG = 3072, 3328
OFF_SM = 3584
N_PAD = OFF_SM + SMALL
SHIFT_W = 4 * W_RWKV + SMALL
N_BIAS = 3
BIAS_STRIDE = 8
BF16_ROWS = 16
VROWS = HEAD + BF16_ROWS
RWKV_CHUNK = 64
VMEM_LIMIT = 56 * 1024 * 1024
PROMPT_TM = 512
PROMPT_TQ = 512
PROMPT_TC = 512
PROMPT_TO = 1024
FOX_UNROLL = 4
CUMSUM_BLOCK = 256
STEP_SUFFIX_BLOCK = 256
STEP_SEQS_PER_TILE = 8


def _dot(a, b):
    return jnp.dot(a, b, preferred_element_type=F32)


def _dot_nt(a, b):
    return lax.dot_general(a, b, (((1,), (1,)), ((), ())), preferred_element_type=F32)


def _dot_tn(a, b):
    return lax.dot_general(a, b, (((0,), (0,)), ((), ())), preferred_element_type=F32)


def _split2(x):
    hi = x.astype(BF16)
    lo = (x - hi.astype(F32)).astype(BF16)
    return hi, lo


def _split3(x):
    hi = x.astype(BF16)
    r1 = x - hi.astype(F32)
    mid = r1.astype(BF16)
    lo = (r1 - mid.astype(F32)).astype(BF16)
    return hi, mid, lo


def _dot1(fn, a, b):
    return fn(a.astype(BF16), b.astype(BF16))


def _exact_lhs_dot(m_bf16, x, parts):
    pieces = _split3(x) if parts == 3 else _split2(x)
    acc = _dot(m_bf16, pieces[0])
    for p in pieces[1:]:
        acc = acc + _dot(m_bf16, p)
    return acc


def _segsum(x, ones_pair):
    xb = x.astype(BF16)
    return jnp.concatenate([_dot(xb[:, PAIR * p:PAIR * (p + 1)], ones_pair) for p in range(x.shape[1] // PAIR)],
                           axis=1)


def _head_rms(t, gain, ones_pair):
    msq = _segsum(t * t, ones_pair) * (1.0 / HEAD)
    return t * lax.rsqrt(msq + NORM_EPS) * gain


def _silu(x):
    return x * jax.nn.sigmoid(x)


def _softplus(z):
    return jnp.maximum(z, 0.0) + jnp.log1p(jnp.exp(-jnp.abs(z)))


def _stack_heads(x, n_heads):
    head_of_lane = lax.broadcasted_iota(jnp.int32, x.shape, 1) // HEAD
    zero = jnp.zeros_like(x)
    return jnp.concatenate([jnp.where(head_of_lane == h, x, zero) for h in range(n_heads)], axis=0)


def _unstack_heads(x6, n_heads):
    rows = x6.shape[0] // n_heads
    head_of_lane = lax.broadcasted_iota(jnp.int32, (rows, x6.shape[1]), 1) // HEAD
    out = jnp.zeros((rows, x6.shape[1]), x6.dtype)
    for h in range(n_heads):
        out = jnp.where(head_of_lane == h, x6[h * rows:(h + 1) * rows], out)
    return out


def _full(shape):
    n = len(shape)
    return pl.BlockSpec(shape, lambda *_: (0,) * n)


def _params(n_axes):
    return pltpu.CompilerParams(dimension_semantics=("arbitrary",) * n_axes, vmem_limit_bytes=VMEM_LIMIT)


def _mem_kv_body(mem_ref, g_ref, w_ref, kg_ref, ones_ref, k_ref, v_ref, kb_ref, vb_ref):
    x = mem_ref[0]
    ms = jnp.mean(x * x, axis=-1, keepdims=True)
    xn = (x * lax.rsqrt(ms + NORM_EPS) * g_ref[...]).astype(BF16)
    kv = _dot(xn, w_ref[...])
    k = _head_rms(kv[:, :W_MEM], kg_ref[...], ones_ref[...])
    v = kv[:, W_MEM:]
    k_ref[0] = k
    v_ref[0] = v
    kb_ref[0] = k.astype(BF16)
    vb_ref[0] = v.astype(BF16)


def _mem_kv(mem, g, w_bf16, kg4, ones_pair):
    b = mem.shape[0]
    blk = pl.BlockSpec((1, N_MEM_TOK, W_MEM), lambda i: (i, 0, 0))
    return pl.pallas_call(
        _mem_kv_body,
        grid=(b,),
        in_specs=[pl.BlockSpec((1, N_MEM_TOK, D_MODEL), lambda i: (i, 0, 0)),
                  _full((1, D_MODEL)), _full((D_MODEL, 2 * W_MEM)), _full((1, W_MEM)), _full((PAIR, PAIR))],
        out_specs=[blk] * 4,
        out_shape=[jax.ShapeDtypeStruct((b, N_MEM_TOK, W_MEM), F32)] * 2
        + [jax.ShapeDtypeStruct((b, N_MEM_TOK, W_MEM), BF16)] * 2,
        compiler_params=_params(1),
        name="mem_kv",
    )(mem, g, w_bf16, kg4, ones_pair)


def _proj_body(chunk, with_vt, seq,
               x_ref, shift0_ref, mk_ref, mv_ref, ng_ref, w_ref,
               fqg_ref, fkg_ref, bf_ref, mu_ref, w0_ref, a0_ref, wup_ref, aup_ref,
               kk_ref, ka_ref, rk_ref, mqg_ref, ones_ref, pbias_ref, bias1_ref,
               qs_ref, kb_ref, augq_ref, augk_ref, kn_ref, v_ref, logf_ref, gf_ref,
               rt_ref, at_ref, kh_ref, bh_ref, rv_ref, lc_ref, gr_ref, bg_ref, om_ref, shift_out_ref,
               *vt_and_scratch):
    carry_shift, carry_c = vt_and_scratch[-2:]
    i = pl.program_id(1)
    tm = x_ref.shape[1]
    sub = min(tm, CUMSUM_BLOCK)

    if seq is None:
        @pl.when(i == 0)
        def _():
            carry_shift[...] = shift0_ref[0]
            carry_c[...] = jnp.zeros_like(carry_c)

    x = x_ref[0]
    ms = jnp.mean(x * x, axis=-1, keepdims=True)
    xn = (x * lax.rsqrt(ms + NORM_EPS) * ng_ref[...]).astype(BF16)

    def proj(off, width):
        return _dot(xn, w_ref[:, off:off + width])

    ones_pair = ones_ref[...]
    row = lax.broadcasted_iota(jnp.int32, (sub, sub), 0)
    col = lax.broadcasted_iota(jnp.int32, (sub, sub), 1)
    tri = jnp.where((col <= row) & ((row ^ col) < (sub if seq is None else seq)), 1.0, 0.0).astype(BF16)
    tri_chunk = jnp.where((col <= row) & ((row ^ col) < chunk), 1.0, 0.0).astype(BF16)
    blocks = [slice(s * sub, (s + 1) * sub) for s in range(tm // sub)]

    hs = proj(OFF_SM, SMALL)
    hq = proj(OFF_FQ, W_FOX)
    hk = proj(OFF_FK, W_FOX)
    hv = proj(OFF_FV, W_FOX)
    hg = proj(OFF_FG, W_FOX)
    hmq = proj(OFF_MQ, W_MEM)
    hmg = proj(OFF_MG, W_MEM)
    hrr = proj(OFF_RR, W_RWKV)
    hrk = proj(OFF_RK, W_RWKV)
    hrv = proj(OFF_RV, W_RWKV)
    hrg = proj(OFF_RG, W_RWKV)
    qn = _head_rms(hq, fqg_ref[...], ones_pair)
    kn = _head_rms(hk, fkg_ref[...], ones_pair)
    qm = _head_rms(hmq, mqg_ref[...], ones_pair)

    qms = (qm * (HEAD ** -0.5 * LOG2E)).astype(BF16)
    mem_rows = [(qms, 0)] if seq is None else [(qms[g * seq:(g + 1) * seq], g) for g in range(tm // seq)]

    def mem_probs():
        probs = []
        for q_rows, g in mem_rows:
            s4 = _dot_nt(_stack_heads(q_rows, N_MEM), mk_ref[g].astype(BF16))
            e = jnp.exp2(s4 - jnp.max(s4, axis=1, keepdims=True))
            probs.append((e / jnp.sum(e, axis=1, keepdims=True)).astype(BF16))
        return probs

    def mem_values(probs):
        return jnp.concatenate([_unstack_heads(_dot(p4, mv_ref[g].astype(BF16)), N_MEM)
                                for p4, (_, g) in zip(probs, mem_rows)], axis=0)

    if seq is not None:
        mem_p = mem_probs()
    kn_ref[0] = kn
    kb_ref[0] = kn.astype(BF16)
    qs_ref[0] = (qn * (HEAD ** -0.5 * LOG2E)).astype(BF16)
    v_ref[0] = hv
    if with_vt:
        hvt = hv.T
        ones_row = jnp.where(lax.broadcasted_iota(jnp.int32, (VROWS - HEAD, tm), 0) == 0, 1.0, 0.0)
        vt_and_scratch[0][0, 0] = jnp.concatenate(
            [blk for h in range(N_FOX) for blk in (hvt[HEAD * h:HEAD * (h + 1)], ones_row)], axis=0).astype(BF16)
    gf_ref[0] = _silu(hg).astype(BF16)
    f = hs + bf_ref[...]
    lane = lax.broadcasted_iota(jnp.int32, (tm, SMALL), 1)
    logf = jnp.where(lane < N_FOX, jnp.minimum(f, 0.0) - jnp.log1p(jnp.exp(-jnp.abs(f))), 0.0)
    logf_ref[0] = logf[:, 0:N_FOX]
    if seq is None:
        carry = carry_c[...]
        c_blocks = []
        for blk in blocks:
            cb = _exact_lhs_dot(tri, logf[blk], 3) + carry
            carry = cb[sub - 1:sub, :]
            c_blocks.append(cb)
        carry_c[...] = carry
    else:
        c_blocks = [_exact_lhs_dot(tri, logf[blk], 3) for blk in blocks]
    c = jnp.concatenate(c_blocks, axis=0)
    pieces = _split3(c * LOG2E)
    packed = pieces[0].astype(F32)
    for j in range(1, N_BIAS):
        packed = packed + pltpu.roll(pieces[j].astype(F32), BIAS_STRIDE * j, 1)
    aug = _dot(packed.astype(BF16), pbias_ref[...]) + bias1_ref[...]
    augq_ref[0] = aug[:, :W_FOX].astype(BF16)
    augk_ref[0] = aug[:, W_FOX:].astype(BF16)

    if seq is None:
        om_ref[0] = (mem_values(mem_probs()) * _silu(hmg)).astype(BF16)

    row_in_seq = lax.broadcasted_iota(jnp.int32, (tm, 1), 0) % (tm if seq is None else seq)

    def tshift(cols, off, width):
        if seq is None:
            before_first = carry_shift[:, off:off + width]
            carry_shift[:, off:off + width] = cols[tm - 1:tm, :]
        else:
            before_first = shift0_ref[0, :, off:off + width]
            shift_out_ref[0, :, off:off + width] = cols
        prev = jnp.where(row_in_seq == 0, before_first, pltpu.roll(cols, 1, 0))
        return cols + (prev - cols) * mu_ref[:, off:off + width]

    r = tshift(hrr, 0, W_RWKV)
    k = tshift(hrk, W_RWKV, W_RWKV)
    v = tshift(hrv, 2 * W_RWKV, W_RWKV)
    g = tshift(hrg, 3 * W_RWKV, W_RWKV)
    sm = tshift(hs, 4 * W_RWKV, SMALL)
    if seq is None:
        shift_out_ref[0] = carry_shift[...]

    w_lin = w0_ref[...] + _dot1(_dot, jnp.tanh(sm), wup_ref[...])
    a = jax.nn.sigmoid(a0_ref[...] + _dot1(_dot, sm, aup_ref[...]))
    lw = -jnp.exp(-_softplus(-w_lin) - 0.5)
    kk = k * kk_ref[...]
    kk = kk * lax.rsqrt(jnp.maximum(_segsum(kk * kk, ones_pair), 1e-24))
    kt = k * (1.0 + (a - 1.0) * ka_ref[...])
    gr = _silu(g)
    gr_ref[0] = gr.astype(BF16)
    bg_ref[0] = (_segsum(r * kt * rk_ref[...], ones_pair) * v * gr).astype(BF16)
    lc = jnp.concatenate([_exact_lhs_dot(tri_chunk, lw[blk], 2) for blk in blocks], axis=0)
    e_neg = jnp.exp(-lc)
    rt_ref[0] = (r * jnp.exp(lc)).astype(BF16)
    at_ref[0] = (-kk * jnp.exp(lc - lw)).astype(BF16)
    kh_ref[0] = (kt * e_neg).astype(BF16)
    bh_ref[0] = (kk * a * e_neg).astype(BF16)
    rv_ref[0] = v.astype(BF16)
    lc_ref[0] = lc
    if seq is not None:
        om_ref[0] = (mem_values(mem_p) * _silu(hmg)).astype(BF16)


def _proj(x, shift0, mk, mv, wts, tm, chunk, with_vt, seq=None):
    b, t, _ = x.shape
    grid = (b, t // tm)
    tok = lambda w: pl.BlockSpec((1, tm, w), lambda bi, i: (bi, i, 0))
    per_b = lambda s1, s2: pl.BlockSpec((1, s1, s2), lambda bi, i: (bi, 0, 0))
    w_specs = [_full(a.shape) for a in wts]
    bf = lambda w: jax.ShapeDtypeStruct((b, t, w), BF16)
    f32 = lambda w: jax.ShapeDtypeStruct((b, t, w), F32)
    out_shape = [bf(W_FOX)] * 4 + [f32(W_FOX), f32(W_FOX), f32(N_FOX), bf(W_FOX)] \
        + [bf(W_RWKV)] * 5 + [f32(W_RWKV)] + [bf(W_RWKV), bf(W_RWKV), bf(W_MEM),
                                               jax.ShapeDtypeStruct(shift0.shape, F32)]
    shift_spec = per_b(1, SHIFT_W) if seq is None else tok(SHIFT_W)
    mem_spec = pl.BlockSpec((mk.shape[0] // b, N_MEM_TOK, W_MEM), lambda bi, i: (bi, 0, 0))
    out_specs = [tok(W_FOX)] * 4 + [tok(W_FOX), tok(W_FOX), tok(N_FOX), tok(W_FOX)] \
        + [tok(W_RWKV)] * 6 + [tok(W_RWKV), tok(W_RWKV), tok(W_MEM), shift_spec]
    if with_vt:
        out_shape.append(jax.ShapeDtypeStruct((b, t // tm, N_FOX * VROWS, tm), BF16))
        out_specs.append(pl.BlockSpec((1, 1, N_FOX * VROWS, tm), lambda bi, i: (bi, i, 0, 0)))
    return pl.pallas_call(
        functools.partial(_proj_body, chunk, with_vt, seq),
        grid=grid,
        in_specs=[tok(D_MODEL), shift_spec, mem_spec, mem_spec] + w_specs,
        out_specs=out_specs,
        out_shape=out_shape,
        scratch_shapes=[pltpu.VMEM((1, SHIFT_W), F32), pltpu.VMEM((1, SMALL), F32)],
        compiler_params=_params(2),
        name="proj",
    )(x, shift0, mk, mv, *wts)


def _fox_body(qs_ref, aq_ref, kb_ref, ak_ref, vt_ref, o_ref, m_scr, acc_scr, sa_scr, sb_scr, qm_scr):
    qi = pl.program_id(2)
    tq = qs_ref.shape[1]
    slab = vt_ref.shape[3]
    slabs_per_tile = tq // slab
    m_scr[...] = jnp.full_like(m_scr, -1e30)
    acc_scr[...] = jnp.zeros_like(acc_scr)
    q = qs_ref[0]
    aq = aq_ref[0]
    lane_lo = lax.broadcasted_iota(jnp.int32, (tq, PAIR), 1) < HEAD
    zero = jnp.zeros_like(q)
    qm_scr[0] = jnp.concatenate([jnp.where(lane_lo, q, zero), jnp.where(lane_lo, aq, zero)], axis=1)
    qm_scr[1] = jnp.concatenate([jnp.where(lane_lo, zero, q), jnp.where(lane_lo, zero, aq)], axis=1)

    def scores(ki, s_scr):
        rows = pl.ds(pl.multiple_of(ki * tq, tq), tq)
        k2 = jnp.concatenate([kb_ref[0, rows, :], ak_ref[0, rows, :]], axis=1)
        for hh in range(2):
            s_scr[hh] = _dot_nt(k2, qm_scr[hh])

    def consume(ki, s_scr, masked):
        s = [s_scr[hh] for hh in range(2)]
        if masked:
            kpos = lax.broadcasted_iota(jnp.int32, (tq, tq), 0)
            qpos = lax.broadcasted_iota(jnp.int32, (tq, tq), 1)
            s = [jnp.where(kpos <= qpos, sh, -1e30) for sh in s]
        m_prev = [m_scr[hh] for hh in range(2)]
        m_new = [jnp.maximum(m_prev[hh], jnp.max(s[hh], axis=0, keepdims=True)) for hh in range(2)]
        p = [jnp.exp2(s[hh] - m_new[hh]).astype(BF16) for hh in range(2)]
        pv = []
        for hh in range(2):
            vt = jnp.concatenate([vt_ref[0, ki * slabs_per_tile + d, VROWS * hh:VROWS * (hh + 1), :]
                                  for d in range(slabs_per_tile)], axis=1)
            pv.append(_dot(vt, p[hh]))
        for hh in range(2):
            acc_scr[hh] = jnp.exp2(m_prev[hh] - m_new[hh]) * acc_scr[hh] + pv[hh]
            m_scr[hh] = m_new[hh]

    bufs = (sa_scr, sb_scr)
    scores(0, bufs[0])

    def tile_group(j, carry):
        for d in range(FOX_UNROLL):
            scores(FOX_UNROLL * j + d + 1, bufs[(d + 1) % 2])
            consume(FOX_UNROLL * j + d, bufs[d % 2], False)
        return carry

    lax.fori_loop(0, qi // FOX_UNROLL, tile_group, 0)
    base = (qi // FOX_UNROLL) * FOX_UNROLL
    for rem in range(FOX_UNROLL):
        @pl.when(qi - base == rem)
        def _():
            for d in range(rem):
                scores(base + d + 1, bufs[(d + 1) % 2])
                consume(base + d, bufs[d % 2], False)
            consume(qi, bufs[rem % 2], True)

    o_t = jnp.concatenate([acc_scr[hh][0:HEAD] / acc_scr[hh][HEAD:HEAD + 1] for hh in range(2)], axis=0)
    o_ref[0] = o_t.T.astype(BF16)


def _fox(qs, augq, kb, augk, vt, tq):
    b, t, _ = qs.shape
    n_slab, slab = vt.shape[1], vt.shape[3]
    q_blk = pl.BlockSpec((1, tq, PAIR), lambda bi, p, qi: (bi, qi, p))
    k_blk = pl.BlockSpec((1, t, PAIR), lambda bi, p, qi: (bi, 0, p))
    return pl.pallas_call(
        _fox_body,
        grid=(b, N_FOX // 2, t // tq),
        in_specs=[q_blk, q_blk, k_blk, k_blk,
                  pl.BlockSpec((1, n_slab, 2 * VROWS, slab), lambda bi, p, qi: (bi, 0, p, 0))],
        out_specs=q_blk,
        out_shape=jax.ShapeDtypeStruct((b, t, W_FOX), BF16),
        scratch_shapes=[pltpu.VMEM((2, 1, tq), F32), pltpu.VMEM((2, VROWS, tq), F32),
                        pltpu.VMEM((2, tq, tq), F32), pltpu.VMEM((2, tq, tq), F32),
                        pltpu.VMEM((2, tq, 2 * PAIR), BF16)],
        compiler_params=_params(3),
        name="fox",
    )(qs, augq, kb, augk, vt)


def _fox_step_body(qs_ref, aq_ref, kn_ref, ak_ref, vn_ref, kp_ref, vp_ref, lfp_ref, o_ref):
    s_new = qs_ref.shape[1]
    p_len = kp_ref.shape[1]
    blk = min(p_len, STEP_SUFFIX_BLOCK)
    row = lax.broadcasted_iota(jnp.int32, (blk, blk), 0)
    col = lax.broadcasted_iota(jnp.int32, (blk, blk), 1)
    upper = jnp.where(col > row, 1.0, 0.0).astype(BF16)
    later = jnp.zeros((1, SMALL), F32)
    sufs = []
    for bi in reversed(range(p_len // blk)):
        lf = jnp.concatenate([lfp_ref[0, bi * blk:(bi + 1) * blk, :], jnp.zeros((blk, SMALL - N_FOX), F32)], axis=1)
        within = _exact_lhs_dot(upper, lf, 3)
        sufs.insert(0, within + later)
        later = later + within[0:1, :] + lf[0:1, :]
    suf_t = (jnp.concatenate(sufs, axis=0) * LOG2E).T
    q6 = _stack_heads(qs_ref[0], N_FOX)
    a6 = _stack_heads(aq_ref[0], N_FOX)
    lane_in_head = lax.broadcasted_iota(jnp.int32, a6.shape, 1) % HEAD
    cq6 = jnp.sum(jnp.where(lane_in_head < N_BIAS, a6.astype(F32), 0.0), axis=1, keepdims=True)
    bias6 = jnp.concatenate([jnp.broadcast_to(suf_t[h:h + 1, :], (s_new, p_len)) for h in range(N_FOX)], axis=0)
    sp = _dot_nt(q6, kp_ref[0].astype(BF16)) + cq6 + bias6
    qrow = lax.broadcasted_iota(jnp.int32, (N_FOX * s_new, s_new), 0) % s_new
    kcol = lax.broadcasted_iota(jnp.int32, (N_FOX * s_new, s_new), 1)
    sn = jnp.where(kcol <= qrow, _dot_nt(q6, kn_ref[0]) + _dot_nt(a6, ak_ref[0]), -1e30)
    m = jnp.maximum(jnp.max(sp, axis=1, keepdims=True), jnp.max(sn, axis=1, keepdims=True))
    pp = jnp.exp2(sp - m)
    pn = jnp.exp2(sn - m)
    l = jnp.sum(pp, axis=1, keepdims=True) + jnp.sum(pn, axis=1, keepdims=True)
    o6 = (_dot((pp / l).astype(BF16), vp_ref[0].astype(BF16))
          + _dot((pn / l).astype(BF16), vn_ref[0].astype(BF16)))
    o_ref[0] = _unstack_heads(o6, N_FOX).astype(BF16)


def _fox_step(qs, augq, kb, augk, v_new, k_past, v_past, logf_past):
    b, s_new, _ = qs.shape
    p_len = k_past.shape[1]
    tok = lambda rows, w: pl.BlockSpec((1, rows, w), lambda i: (i, 0, 0))
    return pl.pallas_call(
        _fox_step_body,
        grid=(b,),
        in_specs=[tok(s_new, W_FOX)] * 5 + [tok(p_len, W_FOX), tok(p_len, W_FOX), tok(p_len, N_FOX)],
        out_specs=tok(s_new, W_FOX),
        out_shape=jax.ShapeDtypeStruct((b, s_new, W_FOX), BF16),
        compiler_params=_params(1),
        name="fox_step",
    )(qs, augq, kb, augk, v_new, k_past, v_past, logf_past)


def _rwkv_body(chunk, independent, rt_ref, at_ref, kh_ref, bh_ref, v_ref, lc_ref, s0_ref, y_ref, sout_ref, s_scr):
    j = pl.program_id(1)
    nj = pl.num_programs(1)
    c = chunk
    c2 = 2 * chunk
    n_chunks = rt_ref.shape[1] // c
    n_pairs = N_RWKV // 2

    def pair_state(seq_i, p):
        z = jnp.zeros((HEAD, HEAD), F32)
        return jnp.concatenate([jnp.concatenate([s0_ref[seq_i, 2 * p], z], axis=1),
                                jnp.concatenate([z, s0_ref[seq_i, 2 * p + 1]], axis=1)], axis=0)

    if not independent:
        @pl.when(j == 0)
        def _():
            for p in range(n_pairs):
                s_scr[p] = pair_state(0, p)

    lane_lo = lax.broadcasted_iota(jnp.int32, (c, PAIR), 1) < HEAD
    row = lax.broadcasted_iota(jnp.int32, (c2, c2), 0)
    col = lax.broadcasted_iota(jnp.int32, (c2, c2), 1)
    same = (row >= c) == (col >= c)
    strict = same & (col < row)
    lower = same & (col <= row)
    eye = jnp.where(col == row, 1.0, 0.0)
    srow = lax.broadcasted_iota(jnp.int32, (PAIR, PAIR), 0)
    scol = lax.broadcasted_iota(jnp.int32, (PAIR, PAIR), 1)
    same_state = (srow >= HEAD) == (scol >= HEAD)

    def stack(x):
        zero = jnp.zeros_like(x)
        return jnp.concatenate([jnp.where(lane_lo, x, zero), jnp.where(lane_lo, zero, x)], axis=0)

    def fold(a):
        return a[:c] + a[c:]

    units = [(ci, p) for ci in range(n_chunks) for p in range(n_pairs)]

    def tile(ref, u):
        ci, p = u
        return ref[0, ci * c:(ci + 1) * c, PAIR * p:PAIR * (p + 1)]

    rt = [tile(rt_ref, u) for u in units]
    at2 = [stack(tile(at_ref, u)) for u in units]
    kh = [tile(kh_ref, u) for u in units]
    bh = [tile(bh_ref, u) for u in units]
    v2 = [stack(tile(v_ref, u)) for u in units]
    g = [_dot_nt(jnp.concatenate([at2[i], stack(rt[i])], axis=0),
                 jnp.concatenate([kh[i], kh[i], bh[i], bh[i]], axis=0)) for i in range(len(units))]
    a_ak = [jnp.where(strict, x[:c2, :c2], 0.0) for x in g]
    a_ab = [jnp.where(strict, x[:c2, c2:], 0.0) for x in g]
    a_rk = [fold(jnp.where(lower, x[c2:, :c2], 0.0)) for x in g]
    a_rb = [fold(jnp.where(lower, x[c2:, c2:], 0.0)) for x in g]
    akv2 = [_dot1(_dot, a_ak[i], v2[i]) for i in range(len(units))]
    yv = [_dot1(_dot, a_rk[i], v2[i]) for i in range(len(units))]
    inv = [eye + x for x in a_ab]
    pw = a_ab
    n = 1
    while 2 * n < c:
        pw = [_dot1(_dot, x, x) for x in pw]
        inv = [inv[i] + _dot1(_dot, inv[i], pw[i]) for i in range(len(units))]
        n *= 2
    tw = [_dot(inv[i].astype(BF16), jnp.concatenate([at2[i], akv2[i].astype(BF16)], axis=1))
          for i in range(len(units))]
    w_mat = [fold(x[:, :PAIR]) for x in tw]
    u0 = [fold(x[:, PAIR:]) for x in tw]
    pc = [jnp.exp(lc_ref[0, (ci + 1) * c - 1:(ci + 1) * c, PAIR * p:PAIR * (p + 1)]) for ci, p in units]
    kc = [(kh[i] * pc[i]).astype(BF16) for i in range(len(units))]
    bc = [(bh[i] * pc[i]).astype(BF16) for i in range(len(units))]
    s_mix = [jnp.where(same_state, _dot_tn(w_mat[i].astype(BF16), bc[i]), 0.0).astype(BF16)
             for i in range(len(units))]
    s_add = [jnp.where(same_state, _dot_tn(jnp.concatenate([tile(v_ref, units[i]), u0[i].astype(BF16)], axis=0),
                                           jnp.concatenate([kc[i], bc[i]], axis=0)), 0.0)
             for i in range(len(units))]
    rb = [_dot1(_dot, a_rb[i], jnp.concatenate([stack(w_mat[i]), stack(u0[i])], axis=1)) for i in range(len(units))]
    r_eff = [(rt[i].astype(F32) + rb[i][:, :PAIR]).astype(BF16) for i in range(len(units))]
    y_add = [yv[i] + rb[i][:, PAIR:] for i in range(len(units))]

    for ci in range(n_chunks):
        idx = [ci * n_pairs + p for p in range(n_pairs)]
        s_old = [pair_state(ci, p) if independent else s_scr[p] for p in range(n_pairs)]
        s_bf = [s.astype(BF16) for s in s_old]
        upd = [_dot(s_bf[p], s_mix[i]) for p, i in enumerate(idx)]
        y = [_dot_nt(r_eff[i], s_bf[p]) + y_add[i] for p, i in enumerate(idx)]
        for p, i in enumerate(idx):
            s_new = s_old[p] * pc[i] + (upd[p] + s_add[i])
            y_ref[0, ci * c:(ci + 1) * c, PAIR * p:PAIR * (p + 1)] = y[p]
            if independent:
                sout_ref[ci, 2 * p] = s_new[:HEAD, :HEAD]
                sout_ref[ci, 2 * p + 1] = s_new[HEAD:, HEAD:]
            else:
                s_scr[p] = s_new

    if not independent:
        @pl.when(j == nj - 1)
        def _():
            for p in range(n_pairs):
                s = s_scr[p]
                sout_ref[0, 2 * p] = s[:HEAD, :HEAD]
                sout_ref[0, 2 * p + 1] = s[HEAD:, HEAD:]


def _rwkv(rt, at, kh, bh, v, lc, s0, tc, chunk, independent=False):
    b, t, _ = rt.shape
    tok = pl.BlockSpec((1, tc, W_RWKV), lambda bi, j: (bi, j, 0))
    if independent:
        st = pl.BlockSpec((tc // chunk, N_RWKV, HEAD, HEAD), lambda bi, j: (j, 0, 0, 0))
    else:
        st = pl.BlockSpec((1, N_RWKV, HEAD, HEAD), lambda bi, j: (bi, 0, 0, 0))
    return pl.pallas_call(
        functools.partial(_rwkv_body, chunk, independent),
        grid=(b, t // tc),
        in_specs=[tok] * 6 + [st],
        out_specs=[tok, st],
        out_shape=[jax.ShapeDtypeStruct((b, t, W_RWKV), F32), jax.ShapeDtypeStruct(s0.shape, F32)],
        scratch_shapes=[pltpu.VMEM((N_RWKV // 2, PAIR, PAIR), F32)],
        compiler_params=_params(2),
        name="rwkv",
    )(rt, at, kh, bh, v, lc, s0)


def _out_body(x_ref, of_ref, gf_ref, y_ref, gr_ref, bg_ref, om_ref, gnw_ref, gnb_ref, ones_ref,
              wf_ref, wr_ref, wm_ref, o_ref):
    ones_pair = ones_ref[...]
    acc = _dot((of_ref[0].astype(F32) * gf_ref[0].astype(F32)).astype(BF16), wf_ref[...])
    acc = acc + _dot(om_ref[0], wm_ref[...])
    y = y_ref[0]
    mu = _segsum(y, ones_pair) * (1.0 / HEAD)
    d = y - mu
    var = _segsum(d * d, ones_pair) * (1.0 / HEAD)
    yn = d * lax.rsqrt(var + GN_EPS) * gnw_ref[...] + gnb_ref[...]
    o_r = yn * gr_ref[0].astype(F32) + bg_ref[0].astype(F32)
    acc = acc + _dot(o_r.astype(BF16), wr_ref[...])
    o_ref[0] = x_ref[0] + acc


def _out(x, of, gf, y, gr, bg, om, gnw, gnb, ones_pair, wf, wr, wm, tm):
    b, t, _ = x.shape
    tok = lambda w: pl.BlockSpec((1, tm, w), lambda bi, i: (bi, i, 0))
    return pl.pallas_call(
        _out_body,
        grid=(b, t // tm),
        in_specs=[tok(D_MODEL), tok(W_FOX), tok(W_FOX), tok(W_RWKV), tok(W_RWKV), tok(W_RWKV), tok(W_MEM),
                  _full(gnw.shape), _full(gnb.shape), _full(ones_pair.shape),
                  _full(wf.shape), _full(wr.shape), _full(wm.shape)],
        out_specs=tok(D_MODEL),
        out_shape=jax.ShapeDtypeStruct((b, t, D_MODEL), F32),
        compiler_params=_params(2),
        name="out_proj",
    )(x, of, gf, y, gr, bg, om, gnw, gnb, ones_pair, wf, wr, wm)


def _pad_in_columns(w):
    r0 = FOX_COLS
    m0 = FOX_COLS + RWKV_COLS
    zeros = lambda n: jnp.zeros(w.shape[:-1] + (n,), w.dtype)
    return jnp.concatenate([
        w[..., 0:3 * W_FOX], w[..., 3 * W_FOX + N_FOX:FOX_COLS],
        w[..., r0:r0 + 3 * W_RWKV], w[..., r0 + 3 * W_RWKV + 2 * LORA:r0 + RWKV_COLS],
        w[..., m0:m0 + MEM_COLS],
        w[..., 3 * W_FOX:3 * W_FOX + N_FOX], zeros(SM_WD - N_FOX),
        w[..., r0 + 3 * W_RWKV:r0 + 3 * W_RWKV + 2 * LORA], zeros(SMALL - SM_AD - LORA),
    ], axis=-1)


def _pad_w_body(w_ref, o_ref):
    o_ref[...] = _pad_in_columns(w_ref[...]).astype(BF16)


def _pad_w(w):
    rows = w.shape[0] // 4
    return pl.pallas_call(
        _pad_w_body,
        grid=(w.shape[0] // rows,),
        in_specs=[pl.BlockSpec((rows, w.shape[1]), lambda i: (i, 0))],
        out_specs=pl.BlockSpec((rows, N_PAD), lambda i: (i, 0)),
        out_shape=jax.ShapeDtypeStruct((w.shape[0], N_PAD), BF16),
        compiler_params=_params(1),
        name="pad_w",
    )(w)


def _pad_shift_row(a):
    zeros = lambda n: jnp.zeros(a.shape[:-1] + (n,), a.dtype)
    return jnp.concatenate([
        a[..., 0:3 * W_RWKV], a[..., 3 * W_RWKV + 2 * LORA:RWKV_COLS],
        zeros(SM_WD), a[..., 3 * W_RWKV:3 * W_RWKV + 2 * LORA], zeros(SMALL - SM_AD - LORA)], axis=-1)


def _unpad_shift_row(a):
    small = 4 * W_RWKV
    return jnp.concatenate([a[..., 0:3 * W_RWKV], a[..., small + SM_WD:small + SM_AD + LORA],
                            a[..., 3 * W_RWKV:4 * W_RWKV]], axis=-1)


def _bias_placement():
    place = np.zeros((SMALL, 2 * W_FOX), np.float32)
    const = np.zeros((1, 2 * W_FOX), np.float32)
    for h in range(N_FOX):
        for j in range(N_BIAS):
            place[BIAS_STRIDE * j + h, HEAD * h + j] = 1.0
            place[BIAS_STRIDE * j + h, W_FOX + HEAD * h + N_BIAS + j] = -1.0
            const[0, HEAD * h + N_BIAS + j] = 1.0
            const[0, W_FOX + HEAD * h + j] = 1.0
    return place, const


def _block_ones(width):
    h = np.arange(width) // HEAD
    return (h[:, None] == h[None, :]).astype(np.float32)


def _layer(x, shift_prev, s0, mk, mv, fox_past, wts, ones_pair, tm, tq, tc, chunk, flat_out=False):
    (proj_wts, gnw, gnb, wf, wr, wm) = wts
    b, t, _ = x.shape
    if flat_out:
        first = jnp.pad(_pad_shift_row(shift_prev), ((0, 0), (0, t - 1), (0, 0))).reshape(1, b * t, SHIFT_W)
        outs = _proj(x.reshape(1, b * t, D_MODEL), first, mk, mv, proj_wts, b * t, chunk, with_vt=False, seq=t)
        outs = [o.reshape(b, t, o.shape[-1]) for o in outs]
        outs[-1] = outs[-1][:, t - 1:t, :]
    else:
        outs = _proj(x, _pad_shift_row(shift_prev), mk, mv, proj_wts, tm, chunk, with_vt=fox_past is None)
    (qs, kb, augq, augk, kn, v, logf, gf, rt, at, kh, bh, rv, lc, gr, bg, om, shift_out, *vt) = outs
    if fox_past is None:
        of = _fox(qs, augq, kb, augk, vt[0], tq)
    else:
        k_past, v_past, logf_past = fox_past
        lfp = logf_past.astype(F32)
        p_len = k_past.shape[1]
        of = _fox_step(qs, augq, kb, augk, v, k_past.reshape(b, p_len, W_FOX).astype(F32),
                       v_past.reshape(b, p_len, W_FOX).astype(F32), lfp)
    rows = lambda a: a.reshape(1, b * t, a.shape[-1]) if flat_out else a
    if flat_out:
        y_r, s_new = _rwkv(rows(rt), rows(at), rows(kh), rows(bh), rows(rv), rows(lc), s0,
                           min(b, STEP_SEQS_PER_TILE) * t, chunk, independent=True)
        y_r = y_r.reshape(b, t, W_RWKV)
    else:
        y_r, s_new = _rwkv(rt, at, kh, bh, rv, lc, s0, tc, chunk)
    y = _out(rows(x), rows(of), rows(gf), rows(y_r), rows(gr), rows(bg), rows(om), gnw, gnb, ones_pair, wf, wr, wm,
             b * t if flat_out else PROMPT_TO).reshape(b, t, D_MODEL)
    return (y, kn.reshape(b, t, N_FOX, HEAD), v.reshape(b, t, N_FOX, HEAD), logf, s_new, _unpad_shift_row(shift_out))


def kernel(x_prompt, x_sample, mem_prompt, cache_fox_k, cache_fox_v, cache_fox_logf, cache_mem_k, cache_mem_v, state_rwkv, state_rwkv_shift, norm_g, w_in, fox_q_g, fox_k_g, fox_b_f, rwkv_mu, rwkv_w0, rwkv_w_up, rwkv_a0, rwkv_a_up, rwkv_k_k, rwkv_k_a, rwkv_r_k, rwkv_gn_w, rwkv_gn_b, mem_norm_g, w_mem_kv, mem_q_g, mem_k_g, w_out):
    depth = w_in.shape[0]
    bp, tp, _ = x_prompt.shape
    bs, s_len, _ = x_sample.shape
    assert x_prompt.shape[2] == D_MODEL and w_in.shape[1:] == (D_MODEL, FOX_COLS + RWKV_COLS + MEM_COLS)
    assert mem_prompt.shape[1:] == (N_MEM_TOK, D_MODEL) and cache_mem_k.shape[2:] == (N_MEM_TOK, N_MEM, HEAD)
    assert tp % PROMPT_TM == 0 and tp % PROMPT_TQ == 0 and tp % PROMPT_TC == 0 and PROMPT_TQ % PROMPT_TM == 0
    assert tp % PROMPT_TO == 0
    assert PROMPT_TM % CUMSUM_BLOCK == 0 and CUMSUM_BLOCK % RWKV_CHUNK == 0 and PROMPT_TC % RWKV_CHUNK == 0
    assert s_len & (s_len - 1) == 0 and BF16_ROWS <= s_len <= RWKV_CHUNK
    assert (bs * s_len) % min(bs * s_len, CUMSUM_BLOCK) == 0 and bs % min(bs, STEP_SEQS_PER_TILE) == 0
    assert cache_fox_k.shape[2] % min(cache_fox_k.shape[2], STEP_SUFFIX_BLOCK) == 0
    place_np, const_np = _bias_placement()
    ones_pair = jnp.asarray(_block_ones(PAIR), BF16)
    pbias = jnp.asarray(place_np, BF16)
    bias1 = jnp.asarray(const_np, F32)

    yp, ys = x_prompt, x_sample
    outs = [[] for _ in range(12)]
    for l in range(depth):
        row = lambda a: a[l].reshape(1, -1).astype(F32)
        tile = lambda a, n: jnp.tile(a[l].reshape(1, -1).astype(F32), (1, n))
        w_pad = _pad_w(w_in[l].astype(F32))
        bf_pad = jnp.pad(row(fox_b_f), ((0, 0), (0, SMALL - N_FOX)))
        wup_pad = jnp.pad(rwkv_w_up[l].astype(F32), ((SM_WD, SMALL - SM_WD - LORA), (0, 0)))
        aup_pad = jnp.pad(rwkv_a_up[l].astype(F32), ((SM_AD, SMALL - SM_AD - LORA), (0, 0)))
        proj_wts = (row(norm_g), w_pad, tile(fox_q_g, N_FOX), tile(fox_k_g, N_FOX), bf_pad,
                    _pad_shift_row(row(rwkv_mu)),
                    row(rwkv_w0), row(rwkv_a0), wup_pad, aup_pad, row(rwkv_k_k), row(rwkv_k_a), row(rwkv_r_k),
                    tile(mem_q_g, N_MEM), ones_pair, pbias, bias1)
        wo = w_out[l].astype(BF16)
        wts = (proj_wts, row(rwkv_gn_w), row(rwkv_gn_b), wo[:W_FOX], wo[W_FOX:W_FOX + W_RWKV], wo[W_FOX + W_RWKV:])

        mk2, mv2, mkb, mvb = _mem_kv(mem_prompt, row(mem_norm_g), w_mem_kv[l].astype(BF16), tile(mem_k_g, N_MEM),
                                     ones_pair)
        shift_zero = jnp.zeros((bp, 1, RWKV_COLS), F32)
        s_zero = jnp.zeros((bp, N_RWKV, HEAD, HEAD), F32)
        yp, k, v, lf, s_new, sh_new = _layer(yp, shift_zero, s_zero, mkb, mvb, None, wts, ones_pair,
                                             tm=PROMPT_TM, tq=PROMPT_TQ, tc=PROMPT_TC, chunk=RWKV_CHUNK)
        mk = mk2.reshape(bp, N_MEM_TOK, N_MEM, HEAD)
        mv = mv2.reshape(bp, N_MEM_TOK, N_MEM, HEAD)
        for lst, val in zip(outs[:7], (k, v, lf, mk, mv, s_new, sh_new)):
            lst.append(val)
        bs, s_len = ys.shape[0], ys.shape[1]
        ys, k, v, lf, s_new, sh_new = _layer(
            ys, state_rwkv_shift[l], state_rwkv[l].astype(F32),
            cache_mem_k[l].reshape(bs, N_MEM_TOK, W_MEM).astype(F32),
            cache_mem_v[l].reshape(bs, N_MEM_TOK, W_MEM).astype(F32),
            (cache_fox_k[l], cache_fox_v[l], cache_fox_logf[l]), wts, ones_pair,
            tm=s_len, tq=None, tc=s_len, chunk=s_len, flat_out=True)
        for lst, val in zip(outs[7:], (k, v, lf, s_new, sh_new)):
            lst.append(val)
    return (yp, ys) + tuple(jnp.stack(o) for o in outs)
```
